```python
import math
import jax
import jax.numpy as jnp
from jax import lax
import numpy as np

D_MODEL = 1024
BATCH = 1
SEQ = 16384
DEPTH = 2
DEC_BATCH = 32
DEC_SEQ = 4
PAST_LEN = 16384
PAGE_SIZE = 128

N_HEADS = 8
HEAD_DIM = 64
KV_HEADS = 2
HPG = N_HEADS // KV_HEADS
CMP_STRIDE = 16
CMP_BLOCK = 32
CMP_RATIO = CMP_BLOCK // CMP_STRIDE
CMP_HIDDEN = HEAD_DIM
SEL_BLOCK = 64
SEL_TOPN = 16
WINDOW = 512
Q_BLOCK = 128
ATT_SCALE = HEAD_DIM ** -0.5
N_BUCKETS = 32
MAX_DISTANCE = 128
HG_HEADS = 4
HG_DK = 128
HG_DV = 128
HG_CHUNK = 64
PEER_HEADS = 8
PEER_NKEYS = 128
PEER_NEXP = PEER_NKEYS * PEER_NKEYS
PEER_DQ = 256
PEER_DH = PEER_DQ // 2
PEER_TOPK = 16
PEER_BLOCK = 128
ATT_W = N_HEADS * HEAD_DIM
KVB_W = 2 * KV_HEADS * HEAD_DIM
HGK_W = HG_HEADS * HG_DK
HGV_W = HG_HEADS * HG_DV
PROJ_SPLITS = (ATT_W, KVB_W, KVB_W, KVB_W, 3 * N_HEADS, HGK_W, HGK_W, HGV_W, HGV_W, D_MODEL, D_MODEL)
N_IN = ATT_W + 3 * KVB_W + 3 * N_HEADS + 2 * HGK_W + 2 * HGV_W + 2 * D_MODEL
EPS = 1e-6
NEG = -1e30
BIG = 1e6
TINY = 1e-30
F32 = jnp.float32

kernel_name = "nsa_hgrn2_peer_hybrid_step"


def rmsnorm(x, g):
    xf = x.astype(F32)
    y = xf * lax.rsqrt(jnp.mean(xf * xf, axis=-1, keepdims=True) + EPS)
    return (y * g.astype(F32)).astype(x.dtype)


def t5_bucket(dist):
    n = jnp.maximum(dist, 0)
    exact = N_BUCKETS // 2
    nf = jnp.maximum(n, 1).astype(F32)
    large = exact + (jnp.log(nf / exact) / math.log(MAX_DISTANCE / exact) * (N_BUCKETS - exact)).astype(jnp.int32)
    return jnp.where(n < exact, n, jnp.minimum(large, N_BUCKETS - 1))


def split_proj(h):
    B, T = h.shape[:2]
    parts = []
    o = 0
    for w in PROJ_SPLITS:
        parts.append(h[..., o:o + w])
        o += w
    q, kvc, kvs, kvw, ng, hq, hf, hi, hg, ga, gh = parts
    kvr = lambda a: a.reshape(B, T, 2, KV_HEADS, HEAD_DIM)
    return (q.reshape(B, T, N_HEADS, HEAD_DIM), kvr(kvc), kvr(kvs), kvr(kvw),
            jax.nn.sigmoid(ng.astype(F32)).reshape(B, T, 3, N_HEADS),
            hq.reshape(B, T, HG_HEADS, HG_DK), hf.reshape(B, T, HG_HEADS, HG_DK),
            hi.reshape(B, T, HG_HEADS, HG_DV), hg, ga, gh)


def cmp_chunk_proj(rows, pe, w1):
    B, L = rows.shape[:2]
    ch = rows.reshape(B, L // CMP_STRIDE, CMP_STRIDE, 2, KV_HEADS, HEAD_DIM)
    w1r = w1.reshape(2, CMP_RATIO, CMP_STRIDE, HEAD_DIM, CMP_HIDDEN)
    per = pe.reshape(2, CMP_RATIO, CMP_STRIDE, HEAD_DIM)
    pe_term = jnp.einsum('crsd,crsdh->rch', per, w1r)
    return jnp.einsum('bnscgd,crsdh->brncgh', ch, w1r) + pe_term[None, :, None, :, None, :]


def cmp_blocks(proj, w2):
    nc = proj.shape[2] - CMP_RATIO + 1
    hsum = proj[:, 0, 0:nc]
    for r in range(1, CMP_RATIO):
        hsum = hsum + proj[:, r, r:r + nc]
    return jnp.einsum('bncgh,chd->bncgd', jax.nn.gelu(hsum, approximate=False), w2)


def nsa_core(q, gates, t_pos, ckv, c_end, sel_fn, n_sel, wkv, w_pos, rel_bias):
    B, T = q.shape[:2]
    dt = q.dtype
    qg = q.reshape(B, T, KV_HEADS, HPG, HEAD_DIM) * ATT_SCALE
    rb = rel_bias.astype(F32)

    def head_bias(dist):
        return jnp.transpose(rb[t5_bucket(dist)].reshape(dist.shape + (KV_HEADS, HPG)), (0, 2, 3, 1))

    dist_c = t_pos[:, None] - c_end[None, :]
    valid_c = (dist_c >= 0)[:, None, None, :]
    lc = jnp.einsum('btghd,bcgd->btghc', qg, ckv[:, :, 0]).astype(F32) + head_bias(dist_c)
    p_c = jax.nn.softmax(jnp.where(valid_c, lc, NEG), axis=-1) * valid_c
    o_c = jnp.einsum('btghc,bcgd->btghd', p_c.astype(dt), ckv[:, :, 1])
    nc = c_end.shape[0]
    ci = jnp.arange(nc)[:, None] * CMP_STRIDE
    sj = jnp.arange(n_sel)[None, :] * SEL_BLOCK
    cover = ((ci < sj + SEL_BLOCK) & (ci + CMP_BLOCK > sj)).astype(F32)
    imp = jnp.einsum('btghc,cj->btgj', p_c, cover)
    blk = jnp.arange(n_sel)[None, :]
    cur = (t_pos // SEL_BLOCK)[:, None]
    forced = (blk == 0) | (blk == cur) | (blk == cur - 1)
    imp = jnp.where(forced[None, :, None, :], BIG, jnp.where((blk <= cur)[None, :, None, :], imp, -BIG))
    _, sel = lax.top_k(imp, min(SEL_TOPN, n_sel))
    ks, vs = sel_fn(sel)
    pos_s = (sel[..., None] * SEL_BLOCK + jnp.arange(SEL_BLOCK)).reshape(B, T, KV_HEADS, -1)
    dist_s = t_pos[None, :, None, None] - pos_s
    bias_g = jnp.transpose(rb.reshape(N_BUCKETS, KV_HEADS, HPG), (1, 0, 2))
    bias_s = bias_g[jnp.arange(KV_HEADS)[None, None, :, None], t5_bucket(dist_s)]
    ls = jnp.einsum('btghd,btgkd->btghk', qg, ks).astype(F32) + jnp.swapaxes(bias_s, -1, -2)
    p_s = jax.nn.softmax(jnp.where((dist_s >= 0)[:, :, :, None, :], ls, NEG), axis=-1)
    o_s = jnp.einsum('btghk,btgkd->btghd', p_s.astype(dt), vs)
    dist_w = t_pos[:, None] - w_pos[None, :]
    valid_w = ((dist_w >= 0) & (dist_w <= WINDOW) & (w_pos >= 0)[None, :])[:, None, None, :]
    lw = jnp.einsum('btghd,bkgd->btghk', qg, wkv[:, :, 0]).astype(F32) + head_bias(dist_w)
    p_w = jax.nn.softmax(jnp.where(valid_w, lw, NEG), axis=-1)
    o_w = jnp.einsum('btghk,bkgd->btghd', p_w.astype(dt), wkv[:, :, 1])
    g = gates.reshape(B, T, 3, KV_HEADS, HPG, 1).astype(dt)
    out = g[:, :, 0] * o_c + g[:, :, 1] * o_s + g[:, :, 2] * o_w
    return out.reshape(B, T, ATT_W)


def nsa_prompt(q, gates, kvc, kvs, kvw, pe, w1, w2, rel_bias):
    B, S = q.shape[:2]
    ckv = cmp_blocks(cmp_chunk_proj(kvc, pe, w1), w2)
    c_end = jnp.arange(ckv.shape[1]) * CMP_STRIDE + CMP_BLOCK - 1
    ks, vs = kvs[:, :, 0], kvs[:, :, 1]
    bi = jnp.arange(B)[:, None, None, None, None]
    gidx = jnp.arange(KV_HEADS)[None, None, :, None, None]

    def sel_fn(sel):
        pos = sel[..., None] * SEL_BLOCK + jnp.arange(SEL_BLOCK)
        shp = sel.shape[:3] + (-1, HEAD_DIM)
        return ks[bi, pos, gidx].reshape(shp), vs[bi, pos, gidx].reshape(shp)

    wpad = jnp.pad(kvw, ((0, 0), (WINDOW, 0), (0, 0), (0, 0), (0, 0)))
    nqb = S // Q_BLOCK
    qb = jnp.swapaxes(q.reshape(B, nqb, Q_BLOCK, N_HEADS, HEAD_DIM), 0, 1)
    gb = jnp.swapaxes(gates.reshape(B, nqb, Q_BLOCK, 3, N_HEADS), 0, 1)

    def block(args):
        qi, gi, i = args
        q0 = i * Q_BLOCK
        t_pos = q0 + jnp.arange(Q_BLOCK)
        wkv = lax.dynamic_slice_in_dim(wpad, q0, Q_BLOCK + WINDOW, axis=1)
        w_pos = q0 - WINDOW + jnp.arange(Q_BLOCK + WINDOW)
        return nsa_core(qi, gi, t_pos, ckv, c_end, sel_fn, S // SEL_BLOCK, wkv, w_pos, rel_bias)

    out = lax.map(block, (qb, gb, jnp.arange(nqb)))
    return jnp.swapaxes(out, 0, 1).reshape(B, S, ATT_W)


def nsa_sample(l, q, gates, kvc, kvs, kvw, cache_kv, win_buf, page_table, pe, w1, w2, rel_bias):
    Bd, T = q.shape[:2]
    past = page_table.shape[1] * PAGE_SIZE
    past_cmp = cache_kv[l, page_table, :, :2].reshape(Bd, past, 2, KV_HEADS, HEAD_DIM)
    proj = cmp_chunk_proj(past_cmp, pe, w1)
    n_new = T // CMP_STRIDE
    if n_new > 0:
        proj = jnp.concatenate([proj, cmp_chunk_proj(kvc[:, :n_new * CMP_STRIDE], pe, w1)], axis=2)
    ckv = cmp_blocks(proj, w2)
    c_end = jnp.arange(ckv.shape[1]) * CMP_STRIDE + CMP_BLOCK - 1
    t_pos = past + jnp.arange(T)
    ks_new, vs_new = kvs[:, :, 0], kvs[:, :, 1]
    bi = jnp.arange(Bd)[:, None, None, None, None]
    gidx = jnp.arange(KV_HEADS)[None, None, :, None, None]

    def sel_fn(sel):
        pos = sel[..., None] * SEL_BLOCK + jnp.arange(SEL_BLOCK)
        in_past = (pos < past)[..., None]
        pc = jnp.clip(pos, 0, past - 1)
        phys = page_table[bi, pc // PAGE_SIZE]
        off = pc % PAGE_SIZE
        pn = jnp.clip(pos - past, 0, T - 1)
        k = jnp.where(in_past, cache_kv[l, phys, off, 2, gidx], ks_new[bi, pn, gidx])
        v = jnp.where(in_past, cache_kv[l, phys, off, 3, gidx], vs_new[bi, pn, gidx])
        shp = sel.shape[:3] + (-1, HEAD_DIM)
        return k.reshape(shp), v.reshape(shp)

    w_len = win_buf.shape[1]
    wcat = jnp.concatenate([win_buf, kvw], axis=1)
    w_pos = past - w_len + jnp.arange(w_len + T)
    n_sel = -(-(past + T) // SEL_BLOCK)
    out = nsa_core(q, gates, t_pos, ckv, c_end, sel_fn, n_sel, wcat, w_pos, rel_bias)
    return out, wcat[:, T:]


def hgrn2_inputs(hq, hf, hi, lb):
    f = hf.astype(F32)
    lb = lb.reshape(HG_HEADS, HG_DK)
    sig = jax.nn.sigmoid(f)
    forget = lb + (1.0 - lb) * sig
    logf = jnp.log(jnp.maximum(forget, TINY))
    k = (1.0 - lb) * (1.0 - sig)
    q = jax.nn.silu(hq.astype(F32))
    return q, k, hi.astype(F32), logf


def hgrn2_chunk(S, inp):
    q, k, v, logf = inp
    C = q.shape[1]
    A = jnp.cumsum(logf, axis=1)
    o_inter = jnp.einsum('bthk,bhkv->bthv', q * jnp.exp(A), S)
    causal = (jnp.arange(C)[:, None] >= jnp.arange(C)[None, :])[None, :, :, None, None]
    decay = jnp.exp(jnp.where(causal, A[:, :, None] - A[:, None, :], 0.0)) * causal
    scores = jnp.einsum('bthk,btshk,bshk->btsh', q, decay, k)
    o_intra = jnp.einsum('btsh,bshv->bthv', scores, v)
    A_last = A[:, -1]
    S_new = jnp.exp(A_last)[..., None] * S + jnp.einsum('bshk,bshv->bhkv', k * jnp.exp(A_last[:, None] - A), v)
    return S_new, o_inter + o_intra


def hgrn2_prompt(q, k, v, logf):
    B, S = q.shape[:2]
    nch = S // HG_CHUNK
    ch = lambda a: jnp.swapaxes(a.reshape((B, nch, HG_CHUNK) + a.shape[2:]), 0, 1)
    S0 = jnp.zeros((B, HG_HEADS, HG_DK, HG_DV), F32)
    S_fin, o = lax.scan(hgrn2_chunk, S0, (ch(q), ch(k), ch(v), ch(logf)))
    return S_fin, jnp.swapaxes(o, 0, 1).reshape(B, S, HG_HEADS, HG_DV)


def hgrn2_readout(o, gate, norm_g, dt):
    B, T = o.shape[:2]
    o = o * lax.rsqrt(jnp.mean(o * o, axis=-1, keepdims=True) + EPS)
    o = o.reshape(B, T, HGV_W) * norm_g.astype(F32) * jax.nn.silu(gate.astype(F32))
    return o.astype(dt)


def merge(att, hg, ga, gh, w_a, w_h, w_o):
    m = jax.nn.sigmoid(ga) * (att @ w_a) + jax.nn.sigmoid(gh) * (hg @ w_h)
    return m @ w_o


def peer(x, wq, keys, u, v):
    shp = x.shape
    xf = x.reshape(-1, D_MODEL)
    n = xf.shape[0]
    pad = (-n) % PEER_BLOCK
    xb = jnp.pad(xf, ((0, pad), (0, 0))).reshape(-1, PEER_BLOCK, D_MODEL)

    def block(xt):
        q = (xt @ wq).reshape(PEER_BLOCK, PEER_HEADS, 2, PEER_DH)
        s1 = jnp.einsum('thd,kd->thk', q[:, :, 0], keys[0]).astype(F32)
        s2 = jnp.einsum('thd,kd->thk', q[:, :, 1], keys[1]).astype(F32)
        v1, i1 = lax.top_k(s1, PEER_TOPK)
        v2, i2 = lax.top_k(s2, PEER_TOPK)
        cand = (v1[..., :, None] + v2[..., None, :]).reshape(PEER_BLOCK, PEER_HEADS, -1)
        cidx = (i1[..., :, None] * PEER_NKEYS + i2[..., None, :]).reshape(PEER_BLOCK, PEER_HEADS, -1)
        sv, sp = lax.top_k(cand, PEER_TOPK)
        e = jnp.take_along_axis(cidx, sp, axis=-1)
        g = jax.nn.softmax(sv, axis=-1)
        h = jax.nn.gelu(jnp.einsum('thkd,td->thk', u[e], xt).astype(F32), approximate=False)
        return jnp.einsum('thk,thkd->td', (g * h).astype(xt.dtype), v[e])

    y = lax.map(block, xb).reshape(-1, D_MODEL)[:n]
    return y.reshape(shp)


def setup_inputs(seed: int = 0) -> dict:
    key = jax.random.key(seed)
    ks = jax.random.split(key, 24)
    n_pages = PAST_LEN // PAGE_SIZE
    n_used = DEC_BATCH * n_pages
    n_pool = n_used + max(n_used // 4, 1)
    w_buf = min(WINDOW, PAST_LEN)
    nrm = lambda k, shape, s: jax.random.normal(k, shape, F32) * s
    gain = lambda k, shape: 1.0 + 0.02 * jax.random.normal(k, shape, F32)
    page_table = jax.random.permutation(ks[5], n_pool)[:n_used].reshape(DEC_BATCH, n_pages).astype(jnp.int32)
    return {
        "x_prompt": nrm(ks[0], (BATCH, SEQ, D_MODEL), 1.0),
        "x_sample": nrm(ks[1], (DEC_BATCH, DEC_SEQ, D_MODEL), 1.0),
        "cache_kv": nrm(ks[2], (DEPTH, n_pool, PAGE_SIZE, 4, KV_HEADS, HEAD_DIM), 1.0),
        "cache_win": nrm(ks[3], (DEPTH, DEC_BATCH, w_buf, 2, KV_HEADS, HEAD_DIM), 1.0),
        "state_hgrn": nrm(ks[4], (DEPTH, DEC_BATCH, HG_HEADS, HG_DK, HG_DV), 0.5),
        "page_table": page_table,
        "norm_attn_g": gain(ks[6], (DEPTH, D_MODEL)),
        "norm_ffn_g": gain(ks[7], (DEPTH, D_MODEL)),
        "final_norm_g": gain(ks[8], (D_MODEL,)),
        "w_in": nrm(ks[9], (DEPTH, D_MODEL, N_IN), D_MODEL ** -0.5),
        "cmp_pe": nrm(ks[10], (DEPTH, 2, CMP_BLOCK, HEAD_DIM), 0.1),
        "cmp_w1": nrm(ks[11], (DEPTH, 2, CMP_BLOCK * HEAD_DIM, CMP_HIDDEN), (CMP_BLOCK * HEAD_DIM) ** -0.5),
        "cmp_w2": nrm(ks[12], (DEPTH, 2, CMP_HIDDEN, HEAD_DIM), 1.5 * CMP_HIDDEN ** -0.5),
        "rel_bias": nrm(ks[13], (N_BUCKETS, N_HEADS), 0.5),
        "hg_lb_logits": nrm(ks[14], (DEPTH, HGK_W), 0.5),
        "hg_norm_g": gain(ks[15], (DEPTH, HGV_W)),
        "w_br_attn": nrm(ks[16], (DEPTH, ATT_W, D_MODEL), ATT_W ** -0.5),
        "w_br_hg": nrm(ks[17], (DEPTH, HGV_W, D_MODEL), HGV_W ** -0.5),
        "w_out": nrm(ks[18], (DEPTH, D_MODEL, D_MODEL), D_MODEL ** -0.5),
        "peer_wq": nrm(ks[19], (DEPTH, D_MODEL, PEER_HEADS * PEER_DQ), D_MODEL ** -0.5),
        "peer_keys": nrm(ks[20], (DEPTH, 2, PEER_NKEYS, PEER_DH), PEER_DH ** -0.5),
        "peer_u": nrm(ks[21], (DEPTH, PEER_NEXP, D_MODEL), D_MODEL ** -0.5),
        "peer_v": nrm(ks[22], (DEPTH, PEER_NEXP, D_MODEL), 0.2),
    }


def reference(x_prompt, x_sample, cache_kv, cache_win, state_hgrn, page_table, norm_attn_g, norm_ffn_g,
              final_norm_g, w_in, cmp_pe, cmp_w1, cmp_w2, rel_bias, hg_lb_logits, hg_norm_g,
              w_br_attn, w_br_hg, w_out, peer_wq, peer_keys, peer_u, peer_v):
    p_lb = jax.nn.softmax(hg_lb_logits.astype(F32), axis=0)
    lower_bounds = jnp.cumsum(p_lb, axis=0) - p_lb[0]
    xp, xs = x_prompt, x_sample
    kv_p, win_p, hs_p, kv_s, win_s, hs_s = [], [], [], [], [], []
    for l in range(DEPTH):
        (q, kvc, kvs, kvw, ng, hq, hf, hi, hg, ga, gh) = split_proj(rmsnorm(xp, norm_attn_g[l]) @ w_in[l])
        att = nsa_prompt(q, ng, kvc, kvs, kvw, cmp_pe[l], cmp_w1[l], cmp_w2[l], rel_bias)
        s_fin, o = hgrn2_prompt(*hgrn2_inputs(hq, hf, hi, lower_bounds[l]))
        hgo = hgrn2_readout(o, hg, hg_norm_g[l], xp.dtype)
        xp = xp + merge(att, hgo, ga, gh, w_br_attn[l], w_br_hg[l], w_out[l])
        xp = xp + peer(rmsnorm(xp, norm_ffn_g[l]), peer_wq[l], peer_keys[l], peer_u[l], peer_v[l])
        kv_p.append(jnp.concatenate([kvc, kvs], axis=2))
        win_p.append(kvw[:, kvw.shape[1] - min(WINDOW, kvw.shape[1]):])
        hs_p.append(s_fin.astype(state_hgrn.dtype))
        (q, kvc, kvs, kvw, ng, hq, hf, hi, hg, ga, gh) = split_proj(rmsnorm(xs, norm_attn_g[l]) @ w_in[l])
        att, win_new = nsa_sample(l, q, ng, kvc, kvs, kvw, cache_kv, cache_win[l], page_table,
                                  cmp_pe[l], cmp_w1[l], cmp_w2[l], rel_bias)
        s_new, o = hgrn2_chunk(state_hgrn[l].astype(F32), hgrn2_inputs(hq, hf, hi, lower_bounds[l]))
        hgo = hgrn2_readout(o, hg, hg_norm_g[l], xs.dtype)
        xs = xs + merge(att, hgo, ga, gh, w_br_attn[l], w_br_hg[l], w_out[l])
        xs = xs + peer(rmsnorm(xs, norm_ffn_g[l]), peer_wq[l], peer_keys[l], peer_u[l], peer_v[l])
        kv_s.append(jnp.concatenate([kvc, kvs], axis=2))
        win_s.append(win_new)
        hs_s.append(s_new.astype(state_hgrn.dtype))
    y_prompt = rmsnorm(xp, final_norm_g)
    y_sample = rmsnorm(xs, final_norm_g)
    return (y_prompt, y_sample, jnp.stack(kv_p), jnp.stack(win_p), jnp.stack(hs_p),
            jnp.stack(kv_s), jnp.stack(win_s), jnp.stack(hs_s))
```

```python
import functools
import math

import jax
import jax.numpy as jnp
import numpy as np
from jax import lax
from jax.experimental import pallas as pl
from jax.experimental.pallas import tpu as pltpu

F32 = jnp.float32
BF16 = jnp.bfloat16

D_MODEL = 1024
PAGE_SIZE = 128
N_HEADS = 8
HEAD_DIM = 64
KV_HEADS = 2
HPG = N_HEADS // KV_HEADS
CMP_STRIDE = 16
CMP_BLOCK = 32
SEL_BLOCK = 64
SEL_TOPN = 16
WINDOW = 512
ATT_SCALE = HEAD_DIM ** -0.5
N_BUCKETS = 32
MAX_DISTANCE = 128
HG_HEADS = 4
HG_DK = 128
HG_DV = 128
PEER_HEADS = 8
PEER_NKEYS = 128
PEER_DH = 128
PEER_TOPK = 16
ATT_W = N_HEADS * HEAD_DIM
KVB_W = 2 * KV_HEADS * HEAD_DIM
HGK_W = HG_HEADS * HG_DK
N_GATE = 3 * N_HEADS
EPS = 1e-6
NEG = -1e30
BIG = 1e6
TINY = 1e-30
NEG_INF = float("-inf")

LANE = 128
VMEM_LIMIT = 56 * 1024 * 1024

C_Q, C_HQ, C_GA, C_GH, C_HF, C_HI, C_HG = 0, 512, 1024, 2048, 3072, 3584, 4096
C_KVC, C_KVS, C_KVW, C_NG, N_PROJ = 4608, 4864, 5120, 5376, 5632
SUB = 16
TQ = 128
TK = 256


def _cparams(sem, vmem=VMEM_LIMIT):
    return pltpu.CompilerParams(dimension_semantics=sem, vmem_limit_bytes=vmem)


def _dot(a, b):
    return jnp.dot(a, b, preferred_element_type=F32)


def _dot_nt(a, b):
    return lax.dot_general(a, b, (((1,), (1,)), ((), ())), preferred_element_type=F32)


def _dot_tn(a, b):
    return lax.dot_general(a, b, (((0,), (0,)), ((), ())), preferred_element_type=F32)


def _sigmoid(x):
    return 1.0 / (1.0 + jnp.exp(-x))


def _gelu(x):
    return 0.5 * x * (1.0 + lax.erf(x * (2.0 ** -0.5)))


def _norm_matmul_kernel(x_ref, g_ref, w_ref, o_ref, xn_ref):
    @pl.when(pl.program_id(1) == 0)
    def _():
        x = x_ref[...]
        y = x * lax.rsqrt(jnp.mean(x * x, axis=-1, keepdims=True) + EPS)
        xn_ref[...] = (y * g_ref[...]).astype(BF16)

    o_ref[...] = _dot(xn_ref[...], w_ref[...])


def norm_matmul(x, g, w, tm, tn):
    m, k = x.shape
    n = w.shape[1]
    return pl.pallas_call(
        _norm_matmul_kernel,
        grid=(m // tm, n // tn),
        in_specs=[pl.BlockSpec((tm, k), lambda i, j: (i, 0)),
                  pl.BlockSpec((1, k), lambda i, j: (0, 0)),
                  pl.BlockSpec((k, tn), lambda i, j: (0, j))],
        out_specs=pl.BlockSpec((tm, tn), lambda i, j: (i, j)),
        out_shape=jax.ShapeDtypeStruct((m, n), F32),
        scratch_shapes=[pltpu.VMEM((tm, k), BF16)],
        compiler_params=_cparams(("parallel", "arbitrary")),
        name="norm_matmul",
    )(x, g.reshape(1, k), w)


def _matmul_kernel(x_ref, w_ref, o_ref):
    o_ref[...] = _dot(x_ref[...], w_ref[...])


def matmul(x, w, tm, tn):
    m, k = x.shape
    n = w.shape[1]
    return pl.pallas_call(
        _matmul_kernel,
        grid=(m // tm, n // tn),
        in_specs=[pl.BlockSpec((tm, k), lambda i, j: (i, 0)),
                  pl.BlockSpec((k, tn), lambda i, j: (0, j))],
        out_specs=pl.BlockSpec((tm, tn), lambda i, j: (i, j)),
        out_shape=jax.ShapeDtypeStruct((m, n), F32),
        compiler_params=_cparams(("parallel", "arbitrary")),
        name="matmul",
    )(x, w)


def _rmsnorm_kernel(x_ref, g_ref, o_ref):
    x = x_ref[...]
    o_ref[...] = x * lax.rsqrt(jnp.mean(x * x, axis=-1, keepdims=True) + EPS) * g_ref[...]


def rmsnorm(x, g, tm):
    m, k = x.shape
    return pl.pallas_call(
        _rmsnorm_kernel,
        grid=(m // tm,),
        in_specs=[pl.BlockSpec((tm, k), lambda i: (i, 0)), pl.BlockSpec((1, k), lambda i: (0, 0))],
        out_specs=pl.BlockSpec((tm, k), lambda i: (i, 0)),
        out_shape=jax.ShapeDtypeStruct((m, k), F32),
        compiler_params=_cparams(("parallel",)),
        name="rmsnorm",
    )(x, g.reshape(1, k))


def _cproj_from_pieces(piece_fn, w_ref, y_ref):
    acc = jnp.zeros(y_ref.shape, F32)
    for s in range(CMP_STRIDE):
        acc = acc + _dot(piece_fn(s).astype(BF16), w_ref[s])
    y_ref[...] = acc


def _cproj_rows_kernel(r0_ref, r1_ref, w_ref, y_ref):
    nch = y_ref.shape[0]

    def piece(s):
        return jnp.concatenate([r[pl.ds(s, nch, stride=CMP_STRIDE), :] for r in (r0_ref, r1_ref)], axis=1)

    _cproj_from_pieces(piece, w_ref, y_ref)


def cproj_rows(rows_arr, col_block, wbig, rb):
    t = rows_arr.shape[0]
    rb = min(rb, t)
    half = lambda k: pl.BlockSpec((rb, LANE), lambda i: (i, 2 * col_block + k))
    return pl.pallas_call(
        _cproj_rows_kernel,
        grid=(t // rb,),
        in_specs=[half(0), half(1), pl.BlockSpec(wbig.shape, lambda i: (0, 0, 0))],
        out_specs=pl.BlockSpec((rb // CMP_STRIDE, 2 * KVB_W), lambda i: (i, 0)),
        out_shape=jax.ShapeDtypeStruct((t // CMP_STRIDE, 2 * KVB_W), F32),
        compiler_params=_cparams(("parallel",)),
        name="cproj_rows",
    )(rows_arr, rows_arr, wbig)


def _cproj_pages_kernel(pt_ref, *refs):
    del pt_ref
    pages, w_ref, y_ref = refs[:-2], refs[-2], refs[-1]
    per = PAGE_SIZE // CMP_STRIDE
    npg = len(pages) // 2

    def piece(s):
        halves = [jnp.concatenate([pages[2 * p + k][pl.ds(s, per, stride=CMP_STRIDE), :] for p in range(npg)], axis=0)
                  for k in range(2)]
        return jnp.concatenate(halves, axis=1)

    _cproj_from_pieces(piece, w_ref, y_ref)


def cproj_pages(cache4, layer, page_table, wbig, pg):
    nb, npages = page_table.shape
    per = PAGE_SIZE // CMP_STRIDE

    def page_spec(p, k):
        return pl.BlockSpec((None, None, PAGE_SIZE, LANE),
                            lambda b, j, pt: (layer, pt[b, j * pg + p], 0, k))

    grid_spec = pltpu.PrefetchScalarGridSpec(
        num_scalar_prefetch=1,
        grid=(nb, npages // pg),
        in_specs=[page_spec(p, k) for p in range(pg) for k in range(2)]
        + [pl.BlockSpec(wbig.shape, lambda b, j, pt: (0, 0, 0))],
        out_specs=pl.BlockSpec((None, pg * per, 2 * KVB_W), lambda b, j, pt: (b, j, 0)),
    )
    return pl.pallas_call(
        _cproj_pages_kernel,
        grid_spec=grid_spec,
        out_shape=jax.ShapeDtypeStruct((nb, npages * per, 2 * KVB_W), F32),
        compiler_params=_cparams(("parallel", "arbitrary")),
        name="cproj_pages",
    )(page_table, *([cache4] * (2 * pg)), wbig)


def _cmp_mlp_kernel(y_ref, pe_ref, w2_ref, o_ref):
    y = y_ref[...]
    n = y.shape[0]
    nxt = pltpu.roll(y[:, KVB_W:], n - 1, 0)
    hsum = y[:, :KVB_W] + nxt + pe_ref[...]
    o_ref[...] = _dot(_gelu(hsum).astype(BF16), w2_ref[...]).astype(o_ref.dtype)


def cmp_mlp(y, pe_term, w2big):
    b, ncp, _ = y.shape
    return pl.pallas_call(
        _cmp_mlp_kernel,
        grid=(b,),
        in_specs=[pl.BlockSpec((None, ncp, 2 * KVB_W), lambda i: (i, 0, 0)),
                  pl.BlockSpec((1, KVB_W), lambda i: (0, 0)),
                  pl.BlockSpec((KVB_W, KVB_W), lambda i: (0, 0))],
        out_specs=pl.BlockSpec((None, ncp, KVB_W), lambda i: (i, 0, 0)),
        out_shape=jax.ShapeDtypeStruct((b, ncp, KVB_W), BF16),
        compiler_params=_cparams(("parallel",)),
        name="cmp_mlp",
    )(y, pe_term, w2big)


def compress_weights(pe, w1, w2):
    w1r = w1.reshape(2, 2, CMP_STRIDE, HEAD_DIM, HEAD_DIM)
    eye = jnp.eye(2, dtype=F32)
    wbig = jnp.einsum('crsdh,cC,gG->scgdrCGh', w1r, eye, eye).reshape(CMP_STRIDE, KVB_W, 2 * KVB_W)
    w2big = jnp.einsum('chd,cC,gG->cghCGd', w2, eye, eye).reshape(KVB_W, KVB_W)
    per = pe.reshape(2, 2, CMP_STRIDE, HEAD_DIM)
    pe_rows = jnp.broadcast_to(jnp.transpose(per, (1, 2, 0, 3))[:, :, :, None, :],
                               (2, CMP_STRIDE, 2, KV_HEADS, HEAD_DIM)).reshape(2 * CMP_STRIDE, KVB_W)
    pe_rows = jnp.pad(pe_rows, ((0, 8 * CMP_STRIDE - 2 * CMP_STRIDE), (0, 0)))
    return wbig.astype(BF16), w2big.astype(BF16), pe_rows


def pe_term_from(pe_rows, wbig):
    ype = cproj_rows(pe_rows, 0, wbig, pe_rows.shape[0])
    return ype[0:1, :KVB_W] + ype[1:2, KVB_W:]


def bias_table(rel_bias):
    n = np.arange(LANE)
    exact = N_BUCKETS // 2
    nf = np.maximum(n, 1).astype(np.float32)
    large = exact + (np.log(nf / np.float32(exact)) / np.float32(math.log(MAX_DISTANCE / exact))
                     * np.float32(N_BUCKETS - exact)).astype(np.int32)
    bucket = np.where(n < exact, n, np.minimum(large, N_BUCKETS - 1))
    assert bucket[LANE - 1] == N_BUCKETS - 1
    return jnp.transpose(rel_bias.astype(F32)[bucket], (1, 0))


def _bias_lookup(tab_row, dist_clipped):
    r, w = dist_clipped.shape
    src = jnp.broadcast_to(tab_row, (r, LANE))
    parts = [jnp.take_along_axis(src, dist_clipped[:, j * LANE:(j + 1) * LANE], axis=1)
             for j in range(w // LANE)]
    return parts[0] if len(parts) == 1 else jnp.concatenate(parts, axis=1)


def _topk_mask(x, k):
    r, n = x.shape
    col = lax.broadcasted_iota(jnp.int32, (r, n), 1)
    sel = jnp.zeros((r, n), F32)
    for _ in range(k):
        m = jnp.max(x, axis=-1, keepdims=True)
        idx = jnp.min(jnp.where(x == m, col, n), axis=-1, keepdims=True)
        hit = col == idx
        sel = jnp.where(hit, 1.0, sel)
        x = jnp.where(hit, NEG_INF, x)
    return sel


def _topk_idx(x, k):
    r, n = x.shape
    col = lax.broadcasted_iota(jnp.int32, (r, n), 1)
    lane = lax.broadcasted_iota(jnp.int32, (r, LANE), 1)
    out = jnp.zeros((r, LANE), jnp.int32)
    for it in range(k):
        m = jnp.max(x, axis=-1, keepdims=True)
        idx = jnp.min(jnp.where(x == m, col, n), axis=-1, keepdims=True)
        out = jnp.where(lane == it, idx, out)
        x = jnp.where(col == idx, NEG_INF, x)
    return out


def _cmp_branch(q, t_pos, ckv_ref, tab_ref, cover_ref, nc):
    r = q.shape[0]
    ncp = ckv_ref.shape[0]
    n_sel = cover_ref.shape[1]
    c_idx = lax.broadcasted_iota(jnp.int32, (1, ncp), 1)
    dist = t_pos - (c_idx * CMP_STRIDE + CMP_BLOCK - 1)
    valid = (dist >= 0) & (c_idx < nc)
    validf = valid.astype(F32)
    dcl = jnp.clip(dist, 0, LANE - 1)
    blk = lax.broadcasted_iota(jnp.int32, (1, n_sel), 1)
    cur = t_pos // SEL_BLOCK
    forced = (blk == 0) | (blk == cur) | (blk == cur - 1)
    outs, imps = [], []
    for g in range(KV_HEADS):
        kc = ckv_ref[:, g * HEAD_DIM:(g + 1) * HEAD_DIM]
        vc = ckv_ref[:, (KV_HEADS + g) * HEAD_DIM:(KV_HEADS + g + 1) * HEAD_DIM]
        psum = jnp.zeros((r, ncp), F32)
        for h in range(HPG):
            head = g * HPG + h
            qh = (q[:, head * HEAD_DIM:(head + 1) * HEAD_DIM] * ATT_SCALE).astype(BF16)
            s = _dot_nt(qh, kc) + _bias_lookup(tab_ref[head:head + 1, :], dcl)
            s = jnp.where(valid, s, NEG)
            e = jnp.exp(s - jnp.max(s, axis=-1, keepdims=True))
            p = e / jnp.sum(e, axis=-1, keepdims=True) * validf
            outs.append(_dot(p.astype(BF16), vc))
            psum = psum + p
        p_hi = psum.astype(BF16)
        p_lo = (psum - p_hi.astype(F32)).astype(BF16)
        imp = _dot(p_hi, cover_ref[...]) + _dot(p_lo, cover_ref[...])
        imps.append(jnp.where(forced, BIG, jnp.where(blk <= cur, imp, -BIG)))
    return outs, imps


def _nsa_cmp_kernel(q_ref, ckv_ref, tab_ref, cover_ref, oc_ref, mask_ref, *, nc, n_sel):
    t_pos = pl.program_id(0) * TQ + lax.broadcasted_iota(jnp.int32, (TQ, 1), 0)
    outs, imps = _cmp_branch(q_ref[...], t_pos, ckv_ref, tab_ref, cover_ref, nc)
    for g in range(KV_HEADS):
        mask_ref[g] = _topk_mask(imps[g], min(SEL_TOPN, n_sel))
    oc_ref[...] = jnp.concatenate(outs, axis=1)


def cover_matrix(ncp, n_sel):
    ci = np.arange(ncp)[:, None] * CMP_STRIDE
    sj = np.arange(n_sel)[None, :] * SEL_BLOCK
    return jnp.asarray(((ci < sj + SEL_BLOCK) & (ci + CMP_BLOCK > sj)).astype(np.float32), dtype=BF16)


def nsa_cmp_prompt(hproj, ckv, tab, nc):
    t = hproj.shape[0]
    ncp = ckv.shape[0]
    n_sel = t // SEL_BLOCK
    cover = cover_matrix(ncp, n_sel)
    return pl.pallas_call(
        functools.partial(_nsa_cmp_kernel, nc=nc, n_sel=n_sel),
        grid=(t // TQ,),
        in_specs=[pl.BlockSpec((TQ, ATT_W), lambda i: (i, C_Q // ATT_W)),
                  pl.BlockSpec((ncp, KVB_W), lambda i: (0, 0)),
                  pl.BlockSpec((N_HEADS, LANE), lambda i: (0, 0)),
                  pl.BlockSpec((ncp, n_sel), lambda i: (0, 0))],
        out_specs=[pl.BlockSpec((TQ, ATT_W), lambda i: (i, 0)),
                   pl.BlockSpec((KV_HEADS, TQ, n_sel), lambda i: (0, i, 0))],
        out_shape=[jax.ShapeDtypeStruct((t, ATT_W), F32),
                   jax.ShapeDtypeStruct((KV_HEADS, t, n_sel), F32)],
        compiler_params=_cparams(("parallel",)),
        name="nsa_cmp_prompt",
    )(hproj, ckv, tab, cover)


def _softmax_step(carry, s3, ok, v):
    m, l, acc = carry
    okb = ok[None]
    s3 = jnp.where(okb, s3, NEG)
    m_new = jnp.maximum(m, jnp.max(s3, axis=-1, keepdims=True))
    p = jnp.where(okb, jnp.exp(s3 - m_new), 0.0)
    alpha = jnp.exp(m - m_new)
    l = alpha * l + jnp.sum(p, axis=-1, keepdims=True)
    pv = _dot(p.reshape(HPG * TQ, p.shape[-1]).astype(BF16), v).reshape(HPG, TQ, HEAD_DIM)
    return m_new, l, alpha * acc + pv


def _head_bias(tab_ref, g, dist, near):
    rows = []
    for h in range(HPG):
        head = g * HPG + h
        if near:
            rows.append(_bias_lookup(tab_ref[head:head + 1, :], jnp.clip(dist, 0, LANE - 1))[None])
        else:
            rows.append(jnp.broadcast_to(tab_ref[head:head + 1, LANE - 1:LANE], dist.shape)[None])
    return jnp.concatenate(rows, axis=0)


def _nsa_sel_kernel(q_ref, ng_ref, oc_ref, mask_ref, kv_ref, tab_ref, o_ref, *, n_sel):
    i = pl.program_id(0)
    t_pos = i * TQ + lax.broadcasted_iota(jnp.int32, (TQ, 1), 0)
    gates = _sigmoid(ng_ref[...])
    n_kt = (i * TQ + TQ - 1) // TK + 1
    init = (jnp.full((HPG, TQ, 1), NEG, F32), jnp.zeros((HPG, TQ, 1), F32), jnp.zeros((HPG, TQ, HEAD_DIM), F32))
    blk_r = lax.broadcasted_iota(jnp.int32, (n_sel, TK), 0)
    key_c = lax.broadcasted_iota(jnp.int32, (n_sel, TK), 1) // SEL_BLOCK
    heads_out = []
    for g in range(KV_HEADS):
        qs = jnp.concatenate(
            [(q_ref[:, (g * HPG + h) * HEAD_DIM:(g * HPG + h + 1) * HEAD_DIM] * ATT_SCALE).astype(BF16)
             for h in range(HPG)], axis=0)
        mask_g = mask_ref[g].astype(BF16)

        def sel_tile(kt, carry, near, g=g, qs=qs, mask_g=mask_g):
            k0 = pl.multiple_of(kt * TK, TK)
            k = kv_ref[pl.ds(k0, TK), g * HEAD_DIM:(g + 1) * HEAD_DIM]
            v = kv_ref[pl.ds(k0, TK), (KV_HEADS + g) * HEAD_DIM:(KV_HEADS + g + 1) * HEAD_DIM]
            s3 = _dot_nt(qs, k).reshape(HPG, TQ, TK)
            expand = (blk_r == key_c + kt * (TK // SEL_BLOCK)).astype(BF16)
            mexp = _dot(mask_g, expand)
            dist = t_pos - (k0 + lax.broadcasted_iota(jnp.int32, (1, TK), 1))
            ok = (mexp > 0.5) & (dist >= 0)
            return _softmax_step(carry, s3 + _head_bias(tab_ref, g, dist, near), ok, v)

        def sel_body(kt, carry):
            is_near = kt * TK + TK - 1 > i * TQ - LANE
            return lax.cond(is_near, lambda c: sel_tile(kt, c, True), lambda c: sel_tile(kt, c, False), carry)

        m, l, acc = lax.fori_loop(0, n_kt, sel_body, init)
        o_s = acc / l

        carry = init
        for c in range(WINDOW // TQ + 1):
            blk_i = i - WINDOW // TQ + c

            def win_tile(carry, c=c, blk_i=blk_i, g=g, qs=qs):
                k0 = pl.multiple_of(jnp.maximum(blk_i, 0) * TQ, TQ)
                k = kv_ref[pl.ds(k0, TQ), (2 * KV_HEADS + g) * HEAD_DIM:(2 * KV_HEADS + g + 1) * HEAD_DIM]
                v = kv_ref[pl.ds(k0, TQ), (3 * KV_HEADS + g) * HEAD_DIM:(3 * KV_HEADS + g + 1) * HEAD_DIM]
                s3 = _dot_nt(qs, k).reshape(HPG, TQ, TQ)
                dist = t_pos - (k0 + lax.broadcasted_iota(jnp.int32, (1, TQ), 1))
                ok = (dist >= 0) & (dist <= WINDOW)
                near = c >= WINDOW // TQ - 1
                return _softmax_step(carry, s3 + _head_bias(tab_ref, g, dist, near), ok, v)

            carry = lax.cond(blk_i >= 0, win_tile, lambda c_: c_, carry)
        o_w = carry[2] / carry[1]

        for h in range(HPG):
            head = g * HPG + h
            o_c = oc_ref[:, head * HEAD_DIM:(head + 1) * HEAD_DIM]
            heads_out.append(gates[:, head:head + 1] * o_c
                             + gates[:, N_HEADS + head:N_HEADS + head + 1] * o_s[h]
                             + gates[:, 2 * N_HEADS + head:2 * N_HEADS + head + 1] * o_w[h])
    o_ref[...] = jnp.concatenate(heads_out, axis=1)


def nsa_sel_prompt(hproj, o_c, mask, kvb, tab):
    t = hproj.shape[0]
    n_sel = mask.shape[-1]
    return pl.pallas_call(
        functools.partial(_nsa_sel_kernel, n_sel=n_sel),
        grid=(t // TQ,),
        in_specs=[pl.BlockSpec((TQ, ATT_W), lambda i: (i, C_Q // ATT_W)),
                  pl.BlockSpec((TQ, LANE), lambda i: (i, C_NG // LANE)),
                  pl.BlockSpec((TQ, ATT_W), lambda i: (i, 0)),
                  pl.BlockSpec((KV_HEADS, TQ, n_sel), lambda i: (0, i, 0)),
                  pl.BlockSpec(kvb.shape, lambda i: (0, 0), pipeline_mode=pl.Buffered(1)),
                  pl.BlockSpec((N_HEADS, LANE), lambda i: (0, 0))],
        out_specs=pl.BlockSpec((TQ, ATT_W), lambda i: (i, 0)),
        out_shape=jax.ShapeDtypeStruct((t, ATT_W), F32),
        compiler_params=_cparams(("parallel",)),
        name="nsa_sel_prompt",
    )(hproj, hproj, o_c, mask, kvb, tab)


def _hgrn_gates(hq, hf, lb):
    sig = _sigmoid(hf)
    forget = lb + (1.0 - lb) * sig
    logf = jnp.log(jnp.maximum(forget, TINY))
    k = (1.0 - lb) * (1.0 - sig)
    q = hq * _sigmoid(hq)
    return q, k, logf


def _hgrn_intra(q, k, v, a, sub):
    n = q.shape[0]
    row = lax.broadcasted_iota(jnp.int32, (n, 1), 0) % sub
    outs = [jnp.zeros((n, HG_DV), F32) for _ in range(HG_HEADS)]
    for d in range(sub):
        ks = k if d == 0 else pltpu.roll(k, d, 0)
        a_s = a if d == 0 else pltpu.roll(a, d, 0)
        vs = v if d == 0 else pltpu.roll(v, d, 0)
        msk = row >= d
        w = q * ks * jnp.exp(jnp.where(msk, a - a_s, 0.0)) * msk.astype(F32)
        for h in range(HG_HEADS):
            r = jnp.sum(w[:, h * HG_DK:(h + 1) * HG_DK], axis=-1, keepdims=True)
            outs[h] = outs[h] + r * vs[:, h * HG_DV:(h + 1) * HG_DV]
    return outs


def _hgrn_prompt_kernel(hq_ref, hf_ref, hi_ref, lb_ref, tri_ref, o_ref, st_out_ref, st_ref):
    step = pl.program_id(0)

    @pl.when(step == 0)
    def _():
        st_ref[...] = jnp.zeros(st_ref.shape, F32)

    n = hq_ref.shape[0]
    q, k, logf = _hgrn_gates(hq_ref[...], hf_ref[...], lb_ref[...])
    v = hi_ref[...]
    a = jnp.dot(tri_ref[...], logf, preferred_element_type=F32, precision=lax.Precision.HIGHEST)
    outs = _hgrn_intra(q, k, v, a, SUB)
    rows_out = [[] for _ in range(HG_HEADS)]
    for c in range(n // SUB):
        sl = slice(c * SUB, (c + 1) * SUB)
        a_c = a[sl]
        a_last = a_c[SUB - 1:SUB]
        qe = (q[sl] * jnp.exp(a_c)).astype(BF16)
        kd = (k[sl] * jnp.exp(a_last - a_c)).astype(BF16)
        dec = jnp.exp(a_last)
        for h in range(HG_HEADS):
            hs = slice(h * HG_DK, (h + 1) * HG_DK)
            st = st_ref[h]
            rows_out[h].append(outs[h][sl] + _dot_nt(qe[:, hs], st.astype(BF16)))
            st_ref[h] = dec[:, hs] * st + _dot_tn(v[sl, hs].astype(BF16), kd[:, hs])
    o_ref[...] = jnp.concatenate([jnp.concatenate(r, axis=0) for r in rows_out], axis=1)
    st_out_ref[...] = st_ref[...]


def hgrn_prompt(hproj, lb, tc):
    t = hproj.shape[0]
    tc = min(tc, t)
    r = np.arange(tc)
    tri = jnp.asarray(((r[:, None] >= r[None, :]) & (r[:, None] // SUB == r[None, :] // SUB)).astype(np.float32))
    return pl.pallas_call(
        _hgrn_prompt_kernel,
        grid=(t // tc,),
        in_specs=[pl.BlockSpec((tc, HGK_W), lambda i: (i, C_HQ // HGK_W)),
                  pl.BlockSpec((tc, HGK_W), lambda i: (i, C_HF // HGK_W)),
                  pl.BlockSpec((tc, HGK_W), lambda i: (i, C_HI // HGK_W)),
                  pl.BlockSpec((1, HGK_W), lambda i: (0, 0)),
                  pl.BlockSpec((tc, tc), lambda i: (0, 0))],
        out_specs=[pl.BlockSpec((tc, HGK_W), lambda i: (i, 0)),
                   pl.BlockSpec((HG_HEADS, HG_DV, HG_DK), lambda i: (0, 0, 0))],
        out_shape=[jax.ShapeDtypeStruct((t, HGK_W), F32),
                   jax.ShapeDtypeStruct((HG_HEADS, HG_DV, HG_DK), F32)],
        scratch_shapes=[pltpu.VMEM((HG_HEADS, HG_DV, HG_DK), F32)],
        compiler_params=_cparams(("arbitrary",)),
        name="hgrn_prompt",
    )(hproj, hproj, hproj, lb.reshape(1, HGK_W), tri)


def _merge_kernel(x_ref, att_ref, o_ref, hg_ref, ga_ref, gh_ref, ng_ref, fg_ref, wa_ref, wh_ref, wo_ref,
                  xo_ref, xn_ref):
    o = o_ref[...]
    parts = []
    for h in range(HG_HEADS):
        oh = o[:, h * HG_DV:(h + 1) * HG_DV]
        parts.append(oh * lax.rsqrt(jnp.mean(oh * oh, axis=-1, keepdims=True) + EPS))
    hg = hg_ref[...]
    hgo = jnp.concatenate(parts, axis=1) * ng_ref[...] * (hg * _sigmoid(hg))
    m = (_sigmoid(ga_ref[...]) * _dot(att_ref[...].astype(BF16), wa_ref[...])
         + _sigmoid(gh_ref[...]) * _dot(hgo.astype(BF16), wh_ref[...]))
    x = x_ref[...] + _dot(m.astype(BF16), wo_ref[...])
    xo_ref[...] = x
    y = x * lax.rsqrt(jnp.mean(x * x, axis=-1, keepdims=True) + EPS)
    xn_ref[...] = (y * fg_ref[...]).astype(BF16)


def merge(x, att, o, hproj, hg_norm_g, ffn_g, wa, wh, wo, tm):
    t = x.shape[0]
    tm = min(tm, t)
    row = lambda w, cb: pl.BlockSpec((tm, w), lambda i: (i, cb))
    full = lambda a: pl.BlockSpec(a.shape, lambda i: (0, 0))
    hg_norm_g = hg_norm_g.reshape(1, HGK_W)
    ffn_g = ffn_g.reshape(1, D_MODEL)
    return pl.pallas_call(
        _merge_kernel,
        grid=(t // tm,),
        in_specs=[row(D_MODEL, 0), row(ATT_W, 0), row(HGK_W, 0), row(HGK_W, C_HG // HGK_W),
                  row(D_MODEL, C_GA // D_MODEL), row(D_MODEL, C_GH // D_MODEL),
                  full(hg_norm_g), full(ffn_g), full(wa), full(wh), full(wo)],
        out_specs=[row(D_MODEL, 0), row(D_MODEL, 0)],
        out_shape=[jax.ShapeDtypeStruct((t, D_MODEL), F32), jax.ShapeDtypeStruct((t, D_MODEL), BF16)],
        compiler_params=_cparams(("parallel",)),
        name="merge",
    )(x, att, o, hproj, hproj, hproj, hg_norm_g, ffn_g, wa, wh, wo)


def _topk_cols(x, k, payload=None):
    n, t = x.shape
    row = lax.broadcasted_iota(jnp.int32, (n, t), 0)
    vals, ids, pay = [], [], []
    for _ in range(k):
        m = jnp.max(x, axis=0, keepdims=True)
        idx = jnp.min(jnp.where(x == m, row, n), axis=0, keepdims=True)
        hit = row == idx
        vals.append(m)
        ids.append(idx)
        if payload is not None:
            pay.append(jnp.max(jnp.where(hit, payload, -1), axis=0, keepdims=True))
        x = jnp.where(hit, NEG_INF, x)
    cat = lambda xs: jnp.concatenate(xs, axis=0)
    return cat(vals), cat(ids), (cat(pay) if payload is not None else None)


def _peer_topk_kernel(q_ref, keys_ref, i1_ref, i2_ref, g_ref):
    tb = q_ref.shape[0]
    o1, o2, og = [], [], []
    for h in range(PEER_HEADS):
        top = []
        for side in range(2):
            c0 = (h * 2 + side) * PEER_DH
            s = _dot_nt(keys_ref[side], q_ref[:, c0:c0 + PEER_DH].astype(BF16))
            top.append(_topk_cols(s, PEER_TOPK)[:2])
        (v1, i1), (v2, i2) = top
        cand = jnp.concatenate([v1[a:a + 1] + v2 for a in range(PEER_TOPK)], axis=0)
        cidx = jnp.concatenate([i1[a:a + 1] * PEER_NKEYS + i2 for a in range(PEER_TOPK)], axis=0)
        sv, _, e = _topk_cols(cand, PEER_TOPK, cidx)
        ex = jnp.exp(sv - sv[0:1])
        o1.append(e // PEER_NKEYS)
        o2.append(e % PEER_NKEYS)
        og.append(ex / jnp.sum(ex, axis=0, keepdims=True))
    i1_ref[...] = jnp.concatenate(o1, axis=0).astype(F32).T.astype(jnp.int32)
    i2_ref[...] = jnp.concatenate(o2, axis=0).astype(F32).T.astype(jnp.int32)
    g_ref[...] = jnp.concatenate(og, axis=0).T
    del tb


def peer_topk(q, keys_bf):
    t = q.shape[0]
    tb = LANE
    nsel = PEER_HEADS * PEER_TOPK
    out = lambda dt: jax.ShapeDtypeStruct((t, nsel), dt)
    return pl.pallas_call(
        _peer_topk_kernel,
        grid=(t // tb,),
        in_specs=[pl.BlockSpec((tb, q.shape[1]), lambda i: (i, 0)),
                  pl.BlockSpec(keys_bf.shape, lambda i: (0, 0, 0))],
        out_specs=[pl.BlockSpec((tb, nsel), lambda i: (i, 0))] * 3,
        out_shape=[out(jnp.int32), out(jnp.int32), out(F32)],
        compiler_params=_cparams(("parallel",)),
        name="peer_topk",
    )(q, keys_bf)


def _peer_gate_kernel(i1_ref, i2_ref, g_ref, o_ref, tmp_ref):
    tb, nsel = i1_ref.shape
    row = lax.broadcasted_iota(jnp.int32, (PEER_NKEYS, nsel), 0)

    def body(t, _):
        a = i1_ref[pl.ds(t, 1), :]
        b = i2_ref[pl.ds(t, 1), :]
        g = g_ref[pl.ds(t, 1), :]
        pa = jnp.where(row == a, 1.0, 0.0).astype(BF16)
        gb = jnp.where(row == b, g, 0.0).astype(BF16)
        tmp_ref[pl.ds(pl.multiple_of(t * PEER_NKEYS, PEER_NKEYS), PEER_NKEYS), :] = _dot_nt(pa, gb)
        return 0

    lax.fori_loop(0, tb, body, 0)
    for a in range(PEER_NKEYS):
        o_ref[a] = tmp_ref[pl.ds(a, tb, stride=PEER_NKEYS), :]


def peer_gate(i1, i2, g, tb):
    t, nsel = i1.shape
    tb = min(tb, t)
    spec = pl.BlockSpec((tb, nsel), lambda i: (i, 0))
    return pl.pallas_call(
        _peer_gate_kernel,
        grid=(t // tb,),
        in_specs=[spec, spec, spec],
        out_specs=pl.BlockSpec((PEER_NKEYS, tb, PEER_NKEYS), lambda i: (0, i, 0)),
        out_shape=jax.ShapeDtypeStruct((PEER_NKEYS, t, PEER_NKEYS), F32),
        scratch_shapes=[pltpu.VMEM((tb * PEER_NKEYS, PEER_NKEYS), F32)],
        compiler_params=_cparams(("parallel",)),
        name="peer_gate",
    )(i1, i2, g)


def _peer_dense_kernel(*refs, n_a):
    xn_ref, x_ref, u_ref, v_ref = refs[:4]
    gate_refs = refs[4:4 + n_a]
    o_ref, acc_ref = refs[4 + n_a], refs[5 + n_a]
    j = pl.program_id(1)

    @pl.when(j == 0)
    def _():
        acc_ref[...] = jnp.zeros(acc_ref.shape, F32)

    s = _dot_nt(xn_ref[...], u_ref[...])
    gate = jnp.concatenate([r[...] for r in gate_refs], axis=1)
    acc_ref[...] += _dot((gate * _gelu(s)).astype(BF16), v_ref[...])

    @pl.when(j == pl.num_programs(1) - 1)
    def _():
        o_ref[...] = x_ref[...] + acc_ref[...]


def peer_dense(xn, x, u_bf, v_bf, gate, tm, n_a):
    t = x.shape[0]
    tm = min(tm, t)
    te = n_a * PEER_NKEYS
    gate_spec = lambda k: pl.BlockSpec((None, tm, PEER_NKEYS), lambda i, j: (j * n_a + k, i, 0))
    return pl.pallas_call(
        functools.partial(_peer_dense_kernel, n_a=n_a),
        grid=(t // tm, PEER_NKEYS // n_a),
        in_specs=[pl.BlockSpec((tm, D_MODEL), lambda i, j: (i, 0)),
                  pl.BlockSpec((tm, D_MODEL), lambda i, j: (i, 0)),
                  pl.BlockSpec((te, D_MODEL), lambda i, j: (j, 0)),
                  pl.BlockSpec((te, D_MODEL), lambda i, j: (j, 0))] + [gate_spec(k) for k in range(n_a)],
        out_specs=pl.BlockSpec((tm, D_MODEL), lambda i, j: (i, 0)),
        out_shape=jax.ShapeDtypeStruct((t, D_MODEL), F32),
        scratch_shapes=[pltpu.VMEM((tm, D_MODEL), F32)],
        compiler_params=_cparams(("parallel", "arbitrary")),
        name="peer_dense",
    )(xn, x, u_bf, v_bf, *([gate] * n_a))


def peer(xn_bf, x, wq_bf, keys_bf, u_bf, v_bf, tm):
    t = x.shape[0]
    q = matmul(xn_bf, wq_bf, min(512, t), 1024)
    i1, i2, g = peer_topk(q, keys_bf)
    gate = peer_gate(i1, i2, g, 128)
    return peer_dense(xn_bf, x, u_bf, v_bf, gate, tm, 8)


ROWS = 8


def _pad_rows(x, rows=ROWS):
    return jnp.concatenate([x, jnp.zeros((rows - x.shape[0], x.shape[1]), x.dtype)], axis=0)


def _nsa_cmp_sample_kernel(q_ref, ckv_ref, tab_ref, cover_ref, oc_ref, sel_ref, *, nc, past, n_sel):
    nt = q_ref.shape[0]
    t_pos = past + lax.broadcasted_iota(jnp.int32, (ROWS, 1), 0)
    outs, imps = _cmp_branch(_pad_rows(q_ref[...]), t_pos, ckv_ref, tab_ref, cover_ref, nc)
    for g in range(KV_HEADS):
        sel_ref[g] = _topk_idx(imps[g], min(SEL_TOPN, n_sel))
    oc_ref[...] = jnp.concatenate(outs, axis=1)[:nt]


def nsa_cmp_sample(hproj3, ckv, tab, nc, past):
    nb, nt, _ = hproj3.shape
    ncp = ckv.shape[1]
    n_sel = -(-(past + nt) // SEL_BLOCK)
    n_sel_pad = -(-n_sel // LANE) * LANE
    cover = cover_matrix(ncp, n_sel_pad)
    return pl.pallas_call(
        functools.partial(_nsa_cmp_sample_kernel, nc=nc, past=past, n_sel=n_sel),
        grid=(nb,),
        in_specs=[pl.BlockSpec((None, nt, ATT_W), lambda b: (b, 0, C_Q // ATT_W)),
                  pl.BlockSpec((None, ncp, KVB_W), lambda b: (b, 0, 0)),
                  pl.BlockSpec((N_HEADS, LANE), lambda b: (0, 0)),
                  pl.BlockSpec((ncp, n_sel_pad), lambda b: (0, 0))],
        out_specs=[pl.BlockSpec((None, nt, ATT_W), lambda b: (b, 0, 0)),
                   pl.BlockSpec((None, KV_HEADS, ROWS, LANE), lambda b: (b, 0, 0, 0))],
        out_shape=[jax.ShapeDtypeStruct((nb, nt, ATT_W), F32),
                   jax.ShapeDtypeStruct((nb, KV_HEADS, ROWS, LANE), jnp.int32)],
        compiler_params=_cparams(("parallel",)),
        name="nsa_cmp_sample",
    )(hproj3, ckv, tab, cover)


def _nsa_sel_sample_kernel(sel_ref, pt_ref, q_ref, ng_ref, oc_ref, kvs_ref, kvw_ref, win_ref, tab_ref, cache_ref,
                           o_ref, buf_ref, sem, *, layer, past, w_len):
    b = pl.program_id(0)
    nt = q_ref.shape[0]
    nslot = SEL_TOPN
    per_page = PAGE_SIZE // SEL_BLOCK

    def slot_copy(t, g, slot):
        blk = sel_ref[((b * nt + t) * KV_HEADS + g) * nslot + slot]
        in_past = blk * SEL_BLOCK < past
        blk_c = jnp.minimum(blk, past // SEL_BLOCK - 1)
        page = pt_ref[b, blk_c // per_page]
        src = cache_ref.at[layer, page, pl.ds((blk_c % per_page) * SEL_BLOCK, SEL_BLOCK), pl.ds(KVB_W, KVB_W)]
        dst = buf_ref.at[t * KV_HEADS + g, pl.ds(slot * SEL_BLOCK, SEL_BLOCK), :]
        return blk, in_past, pltpu.make_async_copy(src, dst, sem)

    for t in range(nt):
        for g in range(KV_HEADS):
            for slot in range(nslot):
                _, in_past, cp = slot_copy(t, g, slot)

                @pl.when(in_past)
                def _():
                    cp.start()

                @pl.when(jnp.logical_not(in_past))
                def _():
                    buf_ref[t * KV_HEADS + g, pl.ds(slot * SEL_BLOCK, SEL_BLOCK), :] = jnp.zeros(
                        (SEL_BLOCK, KVB_W), F32)
    for t in range(nt):
        for g in range(KV_HEADS):
            for slot in range(nslot):
                _, in_past, cp = slot_copy(t, g, slot)

                @pl.when(in_past)
                def _():
                    cp.wait()

    gates = _sigmoid(ng_ref[...])
    nk = nslot * SEL_BLOCK
    lane_slot = lax.broadcasted_iota(jnp.int32, (1, nk), 1) // SEL_BLOCK
    lane_off = lax.broadcasted_iota(jnp.int32, (1, nk), 1) % SEL_BLOCK
    j_new = lax.broadcasted_iota(jnp.int32, (1, LANE), 1)
    kvn = jnp.concatenate([kvs_ref[...], kvw_ref[...]], axis=1)
    kvn_pad = _pad_rows(kvn, LANE).astype(BF16)
    win = win_ref[...].astype(BF16)
    w_idx = lax.broadcasted_iota(jnp.int32, (1, w_len + LANE), 1)
    rows_out = []
    for t in range(nt):
        t_pos = past + t
        heads_out = []
        for g in range(KV_HEADS):
            q4 = jnp.concatenate(
                [q_ref[t:t + 1, (g * HPG + h) * HEAD_DIM:(g * HPG + h + 1) * HEAD_DIM] for h in range(HPG)],
                axis=0)
            q4 = (q4 * ATT_SCALE).astype(BF16)

            def bias_of(dist):
                d = jnp.broadcast_to(jnp.clip(dist, 0, LANE - 1), (ROWS, dist.shape[1]))
                return jnp.concatenate(
                    [_bias_lookup(tab_ref[g * HPG + h:g * HPG + h + 1, :], d)[0:1] for h in range(HPG)], axis=0)

            pos = jnp.zeros((1, nk), jnp.int32)
            has_new = jnp.zeros((1, 1), jnp.int32)
            for slot in range(nslot):
                blk = sel_ref[((b * nt + t) * KV_HEADS + g) * nslot + slot]
                pos = jnp.where(lane_slot == slot, blk * SEL_BLOCK, pos)
                has_new = jnp.maximum(has_new, (blk * SEL_BLOCK >= past).astype(jnp.int32))
            pos = pos + lane_off
            kv_g = buf_ref[t * KV_HEADS + g].astype(BF16)
            k_all = jnp.concatenate([kv_g[:, g * HEAD_DIM:(g + 1) * HEAD_DIM],
                                     kvn_pad[:, g * HEAD_DIM:(g + 1) * HEAD_DIM]], axis=0)
            v_all = jnp.concatenate([kv_g[:, (KV_HEADS + g) * HEAD_DIM:(KV_HEADS + g + 1) * HEAD_DIM],
                                     kvn_pad[:, (KV_HEADS + g) * HEAD_DIM:(KV_HEADS + g + 1) * HEAD_DIM]], axis=0)
            dist = jnp.concatenate([t_pos - pos, t - j_new], axis=1)
            ok = jnp.concatenate([pos < past, (j_new <= t) & (has_new > 0)], axis=1)
            s = jnp.where(ok, _dot_nt(q4, k_all) + bias_of(dist), NEG)
            e = jnp.where(ok, jnp.exp(s - jnp.max(s, axis=-1, keepdims=True)), 0.0)
            o_s = _dot((e / jnp.sum(e, axis=-1, keepdims=True)).astype(BF16), v_all)

            kw = jnp.concatenate([win[:, g * HEAD_DIM:(g + 1) * HEAD_DIM],
                                  kvn_pad[:, (2 * KV_HEADS + g) * HEAD_DIM:(2 * KV_HEADS + g + 1) * HEAD_DIM]], axis=0)
            vw = jnp.concatenate([win[:, (KV_HEADS + g) * HEAD_DIM:(KV_HEADS + g + 1) * HEAD_DIM],
                                  kvn_pad[:, (3 * KV_HEADS + g) * HEAD_DIM:(3 * KV_HEADS + g + 1) * HEAD_DIM]], axis=0)
            dist_w = t_pos - (past - w_len + w_idx)
            ok_w = (dist_w >= 0) & (dist_w <= WINDOW) & (w_idx < w_len + nt)
            s = jnp.where(ok_w, _dot_nt(q4, kw) + bias_of(dist_w), NEG)
            e = jnp.where(ok_w, jnp.exp(s - jnp.max(s, axis=-1, keepdims=True)), 0.0)
            o_w = _dot((e / jnp.sum(e, axis=-1, keepdims=True)).astype(BF16), vw)

            for h in range(HPG):
                head = g * HPG + h
                o_c = oc_ref[t:t + 1, head * HEAD_DIM:(head + 1) * HEAD_DIM]
                heads_out.append(gates[t:t + 1, head:head + 1] * o_c
                                 + gates[t:t + 1, N_HEADS + head:N_HEADS + head + 1] * o_s[h:h + 1]
                                 + gates[t:t + 1, 2 * N_HEADS + head:2 * N_HEADS + head + 1] * o_w[h:h + 1])
        rows_out.append(jnp.concatenate(heads_out, axis=1))
    o_ref[...] = jnp.concatenate(rows_out, axis=0)


def nsa_sel_sample(hproj3, o_c, sel_flat, page_table, cache4, win, tab, layer, past):
    nb, nt, _ = hproj3.shape
    w_len = win.shape[1]
    assert past % SEL_BLOCK == 0 and PAGE_SIZE % SEL_BLOCK == 0 and w_len % LANE == 0 and past >= w_len
    grid_spec = pltpu.PrefetchScalarGridSpec(
        num_scalar_prefetch=2,
        grid=(nb,),
        in_specs=[pl.BlockSpec((None, nt, ATT_W), lambda b, s, p: (b, 0, C_Q // ATT_W)),
                  pl.BlockSpec((None, nt, LANE), lambda b, s, p: (b, 0, C_NG // LANE)),
                  pl.BlockSpec((None, nt, ATT_W), lambda b, s, p: (b, 0, 0)),
                  pl.BlockSpec((None, nt, KVB_W), lambda b, s, p: (b, 0, C_KVS // KVB_W)),
                  pl.BlockSpec((None, nt, KVB_W), lambda b, s, p: (b, 0, C_KVW // KVB_W)),
                  pl.BlockSpec((None, w_len, KVB_W), lambda b, s, p: (b, 0, 0)),
                  pl.BlockSpec((N_HEADS, LANE), lambda b, s, p: (0, 0)),
                  pl.BlockSpec(memory_space=pl.ANY)],
        out_specs=pl.BlockSpec((None, nt, ATT_W), lambda b, s, p: (b, 0, 0)),
        scratch_shapes=[pltpu.VMEM((nt * KV_HEADS, SEL_TOPN * SEL_BLOCK, KVB_W), F32),
                        pltpu.SemaphoreType.DMA(())],
    )
    return pl.pallas_call(
        functools.partial(_nsa_sel_sample_kernel, layer=layer, past=past, w_len=w_len),
        grid_spec=grid_spec,
        out_shape=jax.ShapeDtypeStruct((nb, nt, ATT_W), F32),
        compiler_params=_cparams(("arbitrary",)),
        name="nsa_sel_sample",
    )(sel_flat, page_table, hproj3, hproj3, o_c, hproj3, hproj3, win, tab, cache4)


def _hgrn_sample_kernel(hq_ref, hf_ref, hi_ref, lb_ref, s_ref, o_ref, so_ref):
    nt = hq_ref.shape[0]
    live = (lax.broadcasted_iota(jnp.int32, (ROWS, 1), 0) < nt).astype(F32)
    q, k, logf = _hgrn_gates(_pad_rows(hq_ref[...]), _pad_rows(hf_ref[...]), lb_ref[...])
    q, k, logf = q * live, k * live, logf * live
    v = _pad_rows(hi_ref[...])
    rows = [logf[0:1]]
    for i in range(1, ROWS):
        rows.append(rows[-1] + logf[i:i + 1])
    a = jnp.concatenate(rows, axis=0)
    outs = _hgrn_intra(q, k, v, a, ROWS)
    a_last = a[ROWS - 1:ROWS]
    qe = (q * jnp.exp(a)).astype(BF16)
    kd = (k * jnp.exp(a_last - a)).astype(BF16)
    dec = jnp.exp(a_last)
    o_parts = []
    for h in range(HG_HEADS):
        hs = slice(h * HG_DK, (h + 1) * HG_DK)
        st = s_ref[h].T
        o_parts.append(outs[h] + _dot_nt(qe[:, hs], st.astype(BF16)))
        so_ref[h] = (dec[:, hs] * st + _dot_tn(v[:, hs].astype(BF16), kd[:, hs])).T
    o_ref[...] = jnp.concatenate(o_parts, axis=1)[:nt]


def hgrn_sample(hproj3, lb, state):
    nb, nt, _ = hproj3.shape
    col = lambda c: pl.BlockSpec((None, nt, HGK_W), lambda b: (b, 0, c // HGK_W))
    st_spec = pl.BlockSpec((None, HG_HEADS, HG_DK, HG_DV), lambda b: (b, 0, 0, 0))
    return pl.pallas_call(
        _hgrn_sample_kernel,
        grid=(nb,),
        in_specs=[col(C_HQ), col(C_HF), col(C_HI), pl.BlockSpec((1, HGK_W), lambda b: (0, 0)), st_spec],
        out_specs=[pl.BlockSpec((None, nt, HGK_W), lambda b: (b, 0, 0)), st_spec],
        out_shape=[jax.ShapeDtypeStruct((nb, nt, HGK_W), F32),
                   jax.ShapeDtypeStruct(state.shape, F32)],
        compiler_params=_cparams(("parallel",)),
        name="hgrn_sample",
    )(hproj3, hproj3, hproj3, lb.reshape(1, HGK_W), state)


def reorder_w_in(w):
    o = np.cumsum([0, ATT_W, KVB_W, KVB_W, KVB_W, N_GATE, HGK_W, HGK_W, HGK_W, HGK_W, D_MODEL, D_MODEL])
    q, kvc, kvs, kvw, ng, hq, hf, hi, hg, ga, gh = [w[:, o[i]:o[i + 1]] for i in range(11)]
    pad = jnp.zeros((w.shape[0], N_PROJ - C_NG - N_GATE), w.dtype)
    return jnp.concatenate([q, hq, ga, gh, hf, hi, hg, kvc, kvs, kvw, ng, pad], axis=1).astype(BF16)


def prompt_layer(x, lp, tab):
    t = x.shape[0]
    hproj = norm_matmul(x, lp["attn_g"], lp["w_in"], min(512, t), 1408)
    y = cproj_rows(hproj, C_KVC // KVB_W, lp["wbig"], 2048)
    ckv = cmp_mlp(y[None], lp["pe_term"], lp["w2big"])[0]
    o_c, mask = nsa_cmp_prompt(hproj, ckv, tab, t // CMP_STRIDE - 1)
    kvb = hproj[:, C_KVS:C_KVS + 2 * KVB_W].astype(BF16)
    att = nsa_sel_prompt(hproj, o_c, mask, kvb, tab)
    o_hg, st = hgrn_prompt(hproj, lp["lb"], 128)
    x, xn = merge(x, att, o_hg, hproj, lp["hg_norm_g"], lp["ffn_g"], lp["wa"], lp["wh"], lp["wo"], 256)
    x = peer(xn, x, lp["peer_wq"], lp["peer_keys"], lp["peer_u"], lp["peer_v"], 512)
    return x, hproj[:, C_KVC:C_KVC + 2 * KVB_W], hproj[:, C_KVW:C_KVW + KVB_W], jnp.swapaxes(st, 1, 2)


def layer_params(l, norm_attn_g, norm_ffn_g, w_in, cmp_pe, cmp_w1, cmp_w2, lower_bounds, hg_norm_g,
                 w_br_attn, w_br_hg, w_out, peer_wq, peer_keys, peer_u, peer_v):
    wbig, w2big, pe_rows = compress_weights(cmp_pe[l], cmp_w1[l], cmp_w2[l])
    return dict(attn_g=norm_attn_g[l], ffn_g=norm_ffn_g[l], w_in=reorder_w_in(w_in[l]),
                wbig=wbig, w2big=w2big, pe_term=pe_term_from(pe_rows, wbig),
                lb=lower_bounds[l], hg_norm_g=hg_norm_g[l],
                wa=w_br_attn[l].astype(BF16), wh=w_br_hg[l].astype(BF16), wo=w_out[l].astype(BF16),
                peer_wq=peer_wq[l].astype(BF16), peer_keys=peer_keys[l].astype(BF16),
                peer_u=peer_u[l].astype(BF16), peer_v=peer_v[l].astype(BF16))


def sample_layer(x, lp, tab, layer, cache4, win, state, page_table, nb, nt):
    past = page_table.shape[1] * PAGE_SIZE
    assert nt < CMP_STRIDE, "new tokens never complete a compression chunk"
    hproj = norm_matmul(x, lp["attn_g"], lp["w_in"], x.shape[0], 1408)
    hproj3 = hproj.reshape(nb, nt, N_PROJ)
    y = cproj_pages(cache4, layer, page_table, lp["wbig"], 16)
    ckv = cmp_mlp(y, lp["pe_term"], lp["w2big"])
    o_c, sel = nsa_cmp_sample(hproj3, ckv, tab, past // CMP_STRIDE - 1, past)
    sel_flat = jnp.transpose(sel[:, :, :nt, :SEL_TOPN], (0, 2, 1, 3)).reshape(-1)
    att = nsa_sel_sample(hproj3, o_c, sel_flat, page_table, cache4, win, tab, layer, past)
    o_hg, st = hgrn_sample(hproj3, lp["lb"], state)
    x, xn = merge(x, att.reshape(nb * nt, ATT_W), o_hg.reshape(nb * nt, HGK_W), hproj, lp["hg_norm_g"],
                  lp["ffn_g"], lp["wa"], lp["wh"], lp["wo"], 256)
    x = peer(xn, x, lp["peer_wq"], lp["peer_keys"], lp["peer_u"], lp["peer_v"], 512)
    return x, hproj[:, C_KVC:C_KVC + 2 * KVB_W], hproj[:, C_KVW:C_KVW + KVB_W], st


def kernel(x_prompt, x_sample, cache_kv, cache_win, state_hgrn, page_table, norm_attn_g, norm_ffn_g, final_norm_g, w_in, cmp_pe, cmp_w1, cmp_w2, rel_bias, hg_lb_logits, hg_norm_g, w_br_attn, w_br_hg, w_out, peer_wq, peer_keys, peer_u, peer_v):
    depth = w_in.shape[0]
    bp, seq, _ = x_prompt.shape
    nb, nt, _ = x_sample.shape
    assert bp == 1
    p_lb = jax.nn.softmax(hg_lb_logits.astype(F32), axis=0)
    lower_bounds = jnp.cumsum(p_lb, axis=0) - p_lb[0]
    tab = bias_table(rel_bias)
    n_pool = cache_kv.shape[1]
    cache4 = cache_kv.reshape(depth, n_pool, PAGE_SIZE, 2 * KVB_W)
    w_buf = cache_win.shape[2]
    xp = x_prompt.reshape(seq, D_MODEL)
    xs = x_sample.reshape(nb * nt, D_MODEL)
    kv_p, win_p, hs_p, kv_s, win_s, hs_s = [], [], [], [], [], []
    for l in range(depth):
        lp = layer_params(l, norm_attn_g, norm_ffn_g, w_in, cmp_pe, cmp_w1, cmp_w2, lower_bounds, hg_norm_g,
                          w_br_attn, w_br_hg, w_out, peer_wq, peer_keys, peer_u, peer_v)
        xp, kv, kw, st = prompt_layer(xp, lp, tab)
        kv_p.append(kv.reshape(bp, seq, 4, KV_HEADS, HEAD_DIM))
        win_p.append(kw[seq - min(WINDOW, seq):].reshape(bp, -1, 2, KV_HEADS, HEAD_DIM))
        hs_p.append(st[None])
        win = cache_win[l].reshape(nb, w_buf, KVB_W)
        xs, kv, kw, st = sample_layer(xs, lp, tab, l, cache4, win, state_hgrn[l], page_table, nb, nt)
        kv_s.append(kv.reshape(nb, nt, 4, KV_HEADS, HEAD_DIM))
        win_s.append(jnp.concatenate([cache_win[l], kw.reshape(nb, nt, 2, KV_HEADS, HEAD_DIM)], axis=1)[:, nt:])
        hs_s.append(st)
    y_prompt = rmsnorm(xp, final_norm_g, min(512, seq)).reshape(bp, seq, D_MODEL)
    y_sample = rmsnorm(xs, final_norm_g, nb * nt).reshape(nb, nt, D_MODEL)
    return (y_prompt, y_sample, jnp.stack(kv_p), jnp.stack(win_p), jnp.stack(hs_p),
            jnp.stack(kv_s), jnp.stack(win_s), jnp.stack(hs_s))
```

```python
import functools
import math

import jax
import jax.numpy as jnp
import numpy as np
from jax import lax
from jax.experimental import pallas as pl
from jax.experimental.pallas import tpu as pltpu

F32 = jnp.float32
BF16 = jnp.bfloat16

D_MODEL = 1024
PAGE_SIZE = 128
N_HEADS = 8
HEAD_DIM = 64
KV_HEADS = 2
HPG = N_HEADS // KV_HEADS
CMP_STRIDE = 16
CMP_BLOCK = 32
SEL_BLOCK = 64
SEL_TOPN = 16
WINDOW = 512
ATT_SCALE = HEAD_DIM ** -0.5
N_BUCKETS = 32
MAX_DISTANCE = 128
HG_HEADS = 4
HG_DK = 128
HG_DV = 128
PEER_HEADS = 8
PEER_NKEYS = 128
PEER_DH = 128
PEER_TOPK = 16
ATT_W = N_HEADS * HEAD_DIM
KVB_W = 2 * KV_HEADS * HEAD_DIM
HGK_W = HG_HEADS * HG_DK
N_GATE = 3 * N_HEADS
EPS = 1e-6
NEG = -1e30
BIG = 1e6
TINY = 1e-30
NEG_INF = float("-inf")

LANE = 128
VMEM_LIMIT = 56 * 1024 * 1024

C_Q, C_HQ, C_GA, C_GH, C_HF, C_HI, C_HG = 0, 512, 1024, 2048, 3072, 3584, 4096
C_KVC, C_KVS, C_KVW, C_NG, N_PROJ = 4608, 4864, 5120, 5376, 5632
SUB = 16
TQ = 128
TK = 512


def _cparams(sem, vmem=VMEM_LIMIT):
    return pltpu.CompilerParams(dimension_semantics=sem, vmem_limit_bytes=vmem)


def _dot(a, b):
    return jnp.dot(a, b, preferred_element_type=F32)


def _dot_nt(a, b):
    return lax.dot_general(a, b, (((1,), (1,)), ((), ())), preferred_element_type=F32)


def _dot_tn(a, b):
    return lax.dot_general(a, b, (((0,), (0,)), ((), ())), preferred_element_type=F32)


def _sigmoid(x):
    return 1.0 / (1.0 + jnp.exp(-x))


def _gelu(x):
    return 0.5 * x * (1.0 + lax.erf(x * (2.0 ** -0.5)))


def _norm_matmul_kernel(x_ref, g_ref, w_ref, o_ref, xn_ref):
    @pl.when(pl.program_id(1) == 0)
    def _():
        x = x_ref[...]
        y = x * lax.rsqrt(jnp.mean(x * x, axis=-1, keepdims=True) + EPS)
        xn_ref[...] = (y * g_ref[...]).astype(BF16)

    o_ref[...] = _dot(xn_ref[...], w_ref[...])


def norm_matmul(x, g, w, tm, tn):
    m, k = x.shape
    n = w.shape[1]
    return pl.pallas_call(
        _norm_matmul_kernel,
        grid=(m // tm, n // tn),
        in_specs=[pl.BlockSpec((tm, k), lambda i, j: (i, 0)),
                  pl.BlockSpec((1, k), lambda i, j: (0, 0)),
                  pl.BlockSpec((k, tn), lambda i, j: (0, j))],
        out_specs=pl.BlockSpec((tm, tn), lambda i, j: (i, j)),
        out_shape=jax.ShapeDtypeStruct((m, n), F32),
        scratch_shapes=[pltpu.VMEM((tm, k), BF16)],
        compiler_params=_cparams(("parallel", "arbitrary")),
        name="norm_matmul",
    )(x, g.reshape(1, k), w)


def _matmul_kernel(x_ref, w_ref, o_ref):
    o_ref[...] = _dot(x_ref[...], w_ref[...])


def matmul(x, w, tm, tn):
    m, k = x.shape
    n = w.shape[1]
    return pl.pallas_call(
        _matmul_kernel,
        grid=(m // tm, n // tn),
        in_specs=[pl.BlockSpec((tm, k), lambda i, j: (i, 0)),
                  pl.BlockSpec((k, tn), lambda i, j: (0, j))],
        out_specs=pl.BlockSpec((tm, tn), lambda i, j: (i, j)),
        out_shape=jax.ShapeDtypeStruct((m, n), F32),
        compiler_params=_cparams(("parallel", "arbitrary")),
        name="matmul",
    )(x, w)


def _rmsnorm_kernel(x_ref, g_ref, o_ref):
    x = x_ref[...]
    o_ref[...] = x * lax.rsqrt(jnp.mean(x * x, axis=-1, keepdims=True) + EPS) * g_ref[...]


def rmsnorm(x, g, tm):
    m, k = x.shape
    return pl.pallas_call(
        _rmsnorm_kernel,
        grid=(m // tm,),
        in_specs=[pl.BlockSpec((tm, k), lambda i: (i, 0)), pl.BlockSpec((1, k), lambda i: (0, 0))],
        out_specs=pl.BlockSpec((tm, k), lambda i: (i, 0)),
        out_shape=jax.ShapeDtypeStruct((m, k), F32),
        compiler_params=_cparams(("parallel",)),
        name="rmsnorm",
    )(x, g.reshape(1, k))


def _cproj_from_pieces(piece_fn, w_ref, y_ref):
    acc = jnp.zeros(y_ref.shape, F32)
    for s in range(CMP_STRIDE):
        acc = acc + _dot(piece_fn(s).astype(BF16), w_ref[s])
    y_ref[...] = acc


def _cproj_rows_kernel(r0_ref, r1_ref, w_ref, y_ref):
    nch = y_ref.shape[0]

    def piece(s):
        return jnp.concatenate([r[pl.ds(s, nch, stride=CMP_STRIDE), :] for r in (r0_ref, r1_ref)], axis=1)

    _cproj_from_pieces(piece, w_ref, y_ref)


def cproj_rows(rows_arr, col_block, wbig, rb):
    t = rows_arr.shape[0]
    rb = min(rb, t)
    half = lambda k: pl.BlockSpec((rb, LANE), lambda i: (i, 2 * col_block + k))
    return pl.pallas_call(
        _cproj_rows_kernel,
        grid=(t // rb,),
        in_specs=[half(0), half(1), pl.BlockSpec(wbig.shape, lambda i: (0, 0, 0))],
        out_specs=pl.BlockSpec((rb // CMP_STRIDE, 2 * KVB_W), lambda i: (i, 0)),
        out_shape=jax.ShapeDtypeStruct((t // CMP_STRIDE, 2 * KVB_W), F32),
        compiler_params=_cparams(("parallel",)),
        name="cproj_rows",
    )(rows_arr, rows_arr, wbig)


def _cproj_pages_kernel(pt_ref, *refs):
    del pt_ref
    pages, w_ref, y_ref = refs[:-2], refs[-2], refs[-1]
    per = PAGE_SIZE // CMP_STRIDE
    npg = len(pages) // 2

    def piece(s):
        halves = [jnp.concatenate([pages[2 * p + k][pl.ds(s, per, stride=CMP_STRIDE), :] for p in range(npg)], axis=0)
                  for k in range(2)]
        return jnp.concatenate(halves, axis=1)

    _cproj_from_pieces(piece, w_ref, y_ref)


def cproj_pages(cache4, layer, page_table, wbig, pg):
    nb, npages = page_table.shape
    per = PAGE_SIZE // CMP_STRIDE

    def page_spec(p, k):
        return pl.BlockSpec((None, None, PAGE_SIZE, LANE),
                            lambda b, j, pt: (layer, pt[b, j * pg + p], 0, k))

    grid_spec = pltpu.PrefetchScalarGridSpec(
        num_scalar_prefetch=1,
        grid=(nb, npages // pg),
        in_specs=[page_spec(p, k) for p in range(pg) for k in range(2)]
        + [pl.BlockSpec(wbig.shape, lambda b, j, pt: (0, 0, 0))],
        out_specs=pl.BlockSpec((None, pg * per, 2 * KVB_W), lambda b, j, pt: (b, j, 0)),
    )
    return pl.pallas_call(
        _cproj_pages_kernel,
        grid_spec=grid_spec,
        out_shape=jax.ShapeDtypeStruct((nb, npages * per, 2 * KVB_W), F32),
        compiler_params=_cparams(("parallel", "arbitrary")),
        name="cproj_pages",
    )(page_table, *([cache4] * (2 * pg)), wbig)


def _cmp_mlp_kernel(y_ref, pe_ref, w2_ref, o_ref):
    y = y_ref[...]
    n = y.shape[0]
    nxt = pltpu.roll(y[:, KVB_W:], n - 1, 0)
    hsum = y[:, :KVB_W] + nxt + pe_ref[...]
    o_ref[...] = _dot(_gelu(hsum).astype(BF16), w2_ref[...]).astype(o_ref.dtype)


def cmp_mlp(y, pe_term, w2big):
    b, ncp, _ = y.shape
    return pl.pallas_call(
        _cmp_mlp_kernel,
        grid=(b,),
        in_specs=[pl.BlockSpec((None, ncp, 2 * KVB_W), lambda i: (i, 0, 0)),
                  pl.BlockSpec((1, KVB_W), lambda i: (0, 0)),
                  pl.BlockSpec((KVB_W, KVB_W), lambda i: (0, 0))],
        out_specs=pl.BlockSpec((None, ncp, KVB_W), lambda i: (i, 0, 0)),
        out_shape=jax.ShapeDtypeStruct((b, ncp, KVB_W), BF16),
        compiler_params=_cparams(("parallel",)),
        name="cmp_mlp",
    )(y, pe_term, w2big)


def compress_weights(pe, w1, w2):
    w1r = w1.reshape(2, 2, CMP_STRIDE, HEAD_DIM, HEAD_DIM)
    eye = jnp.eye(2, dtype=F32)
    wbig = jnp.einsum('crsdh,cC,gG->scgdrCGh', w1r, eye, eye).reshape(CMP_STRIDE, KVB_W, 2 * KVB_W)
    w2big = jnp.einsum('chd,cC,gG->cghCGd', w2, eye, eye).reshape(KVB_W, KVB_W)
    per = pe.reshape(2, 2, CMP_STRIDE, HEAD_DIM)
    pe_rows = jnp.broadcast_to(jnp.transpose(per, (1, 2, 0, 3))[:, :, :, None, :],
                               (2, CMP_STRIDE, 2, KV_HEADS, HEAD_DIM)).reshape(2 * CMP_STRIDE, KVB_W)
    pe_rows = jnp.pad(pe_rows, ((0, 8 * CMP_STRIDE - 2 * CMP_STRIDE), (0, 0)))
    return wbig.astype(BF16), w2big.astype(BF16), pe_rows


def pe_term_from(pe_rows, wbig):
    ype = cproj_rows(pe_rows, 0, wbig, pe_rows.shape[0])
    return ype[0:1, :KVB_W] + ype[1:2, KVB_W:]


def bias_table(rel_bias):
    n = np.arange(LANE)
    exact = N_BUCKETS // 2
    nf = np.maximum(n, 1).astype(np.float32)
    large = exact + (np.log(nf / np.float32(exact)) / np.float32(math.log(MAX_DISTANCE / exact))
                     * np.float32(N_BUCKETS - exact)).astype(np.int32)
    bucket = np.where(n < exact, n, np.minimum(large, N_BUCKETS - 1))
    assert bucket[LANE - 1] == N_BUCKETS - 1
    return jnp.transpose(rel_bias.astype(F32)[bucket], (1, 0))


def _bias_lookup(tab_row, dist_clipped):
    r, w = dist_clipped.shape
    src = jnp.broadcast_to(tab_row, (r, LANE))
    parts = [jnp.take_along_axis(src, dist_clipped[:, j * LANE:(j + 1) * LANE], axis=1)
             for j in range(w // LANE)]
    return parts[0] if len(parts) == 1 else jnp.concatenate(parts, axis=1)


def _topk_mask(x, k):
    r, n = x.shape
    col = lax.broadcasted_iota(jnp.int32, (r, n), 1)
    sel = jnp.zeros((r, n), F32)
    for _ in range(k):
        m = jnp.max(x, axis=-1, keepdims=True)
        idx = jnp.min(jnp.where(x == m, col, n), axis=-1, keepdims=True)
        hit = col == idx
        sel = jnp.where(hit, 1.0, sel)
        x = jnp.where(hit, NEG_INF, x)
    return sel


def _topk_idx(x, k):
    r, n = x.shape
    col = lax.broadcasted_iota(jnp.int32, (r, n), 1)
    lane = lax.broadcasted_iota(jnp.int32, (r, LANE), 1)
    out = jnp.zeros((r, LANE), jnp.int32)
    for it in range(k):
        m = jnp.max(x, axis=-1, keepdims=True)
        idx = jnp.min(jnp.where(x == m, col, n), axis=-1, keepdims=True)
        out = jnp.where(lane == it, idx, out)
        x = jnp.where(col == idx, NEG_INF, x)
    return out


def _cmp_branch(q, t_pos, ckv_ref, tab_ref, cover_ref, nc, ncols=None):
    r = q.shape[0]
    ncols = ckv_ref.shape[0] if ncols is None else ncols
    n_sel = cover_ref.shape[1]
    c_idx = lax.broadcasted_iota(jnp.int32, (1, ncols), 1)
    dist = t_pos - (c_idx * CMP_STRIDE + CMP_BLOCK - 1)
    valid = (dist >= 0) & (c_idx < nc)
    any_valid = (t_pos >= CMP_BLOCK - 1).astype(F32)
    dcl = jnp.clip(dist, 0, LANE - 1)
    blk = lax.broadcasted_iota(jnp.int32, (1, n_sel), 1)
    cur = t_pos // SEL_BLOCK
    forced = (blk == 0) | (blk == cur) | (blk == cur - 1)
    outs, imps = [], []
    for g in range(KV_HEADS):
        kc = ckv_ref[:ncols, g * HEAD_DIM:(g + 1) * HEAD_DIM]
        vc = ckv_ref[:ncols, (KV_HEADS + g) * HEAD_DIM:(KV_HEADS + g + 1) * HEAD_DIM]
        psum = jnp.zeros((r, ncols), F32)
        for h in range(HPG):
            head = g * HPG + h
            qh = (q[:, head * HEAD_DIM:(head + 1) * HEAD_DIM] * ATT_SCALE).astype(BF16)
            s = _dot_nt(qh, kc) + _bias_lookup(tab_ref[head:head + 1, :], dcl)
            s = jnp.where(valid, s, NEG)
            e = jnp.exp(s - jnp.max(s, axis=-1, keepdims=True))
            p = e * (any_valid / jnp.sum(e, axis=-1, keepdims=True))
            outs.append(_dot(p.astype(BF16), vc))
            psum = psum + p
        p_hi = psum.astype(BF16)
        p_lo = (psum - p_hi.astype(F32)).astype(BF16)
        imp = _dot(p_hi, cover_ref[:ncols]) + _dot(p_lo, cover_ref[:ncols])
        imps.append(jnp.where(forced, BIG, jnp.where(blk <= cur, imp, -BIG)))
    return outs, imps


def _topk_mask_cols(x, k):
    n, t = x.shape
    row = lax.broadcasted_iota(jnp.int32, (n, t), 0)
    sel = jnp.zeros((n, t), F32)
    for _ in range(k):
        m = jnp.max(x, axis=0, keepdims=True)
        idx = jnp.min(jnp.where(x == m, row, n), axis=0, keepdims=True)
        hit = row == idx
        sel = jnp.where(hit, 1.0, sel)
        x = jnp.where(hit, NEG_INF, x)
    return sel


def _nsa_cmp_kernel(q_ref, ckv_ref, tab_ref, cover_ref, oc_ref, mask_ref, *, nc, n_sel):
    i = pl.program_id(0)
    t_pos = i * TQ + lax.broadcasted_iota(jnp.int32, (TQ, 1), 0)
    ncp = ckv_ref.shape[0]
    part = max(ncp // 4, LANE)
    last_c = (i * TQ + TQ - CMP_BLOCK) // CMP_STRIDE
    need = jnp.clip(last_c // part, 0, ncp // part - 1)

    for n_parts in range(1, ncp // part + 1):
        @pl.when(need == n_parts - 1)
        def _(n_parts=n_parts):
            outs, imps = _cmp_branch(q_ref[...], t_pos, ckv_ref, tab_ref, cover_ref, nc, n_parts * part)
            for g in range(KV_HEADS):
                mask_ref[g] = _topk_mask_cols(imps[g].T, min(SEL_TOPN, n_sel))
            oc_ref[...] = jnp.concatenate(outs, axis=1)


def cover_matrix(ncp, n_sel):
    ci = np.arange(ncp)[:, None] * CMP_STRIDE
    sj = np.arange(n_sel)[None, :] * SEL_BLOCK
    return jnp.asarray(((ci < sj + SEL_BLOCK) & (ci + CMP_BLOCK > sj)).astype(np.float32), dtype=BF16)


def nsa_cmp_prompt(hproj, ckv, tab, nc):
    t = hproj.shape[0]
    ncp = ckv.shape[0]
    n_sel = t // SEL_BLOCK
    cover = cover_matrix(ncp, n_sel)
    return pl.pallas_call(
        functools.partial(_nsa_cmp_kernel, nc=nc, n_sel=n_sel),
        grid=(t // TQ,),
        in_specs=[pl.BlockSpec((TQ, ATT_W), lambda i: (i, C_Q // ATT_W)),
                  pl.BlockSpec((ncp, KVB_W), lambda i: (0, 0)),
                  pl.BlockSpec((N_HEADS, LANE), lambda i: (0, 0)),
                  pl.BlockSpec((ncp, n_sel), lambda i: (0, 0))],
        out_specs=[pl.BlockSpec((TQ, ATT_W), lambda i: (i, 0)),
                   pl.BlockSpec((KV_HEADS, n_sel, TQ), lambda i: (0, 0, i))],
        out_shape=[jax.ShapeDtypeStruct((t, ATT_W), F32),
                   jax.ShapeDtypeStruct((KV_HEADS, n_sel, t), F32)],
        compiler_params=_cparams(("parallel",)),
        name="nsa_cmp_prompt",
    )(hproj, ckv, tab, cover)


MASKED = -1e9
NEAR = 2 * TQ
W_KEYS = WINDOW + TQ
BLK_PER_TILE = TK // SEL_BLOCK


VT_ROWS = HEAD_DIM + 16
MASK_ROWS = 16
HT = HPG * TQ


def _online_step_t(carry, st, shift, vt):
    m, acc = carry
    m_new = jnp.maximum(m, jnp.max(st, axis=0, keepdims=True) + shift)
    p = jnp.exp(st - (m_new - shift)).astype(BF16)
    return m_new, jnp.exp(m - m_new) * acc + _dot(vt, p)


def _nsa_sel_kernel(q_ref, ng_ref, oc_ref, mask_ref, ka_ref, vt_ref, tab_ref, ntab_ref, wtab_ref, o_ref, *, n_sel):
    i = pl.program_id(0)
    qt = (q_ref[...] * ATT_SCALE).T
    oct_ = oc_ref[...].T
    gt = _sigmoid(ng_ref[...]).T
    near_blk0 = 2 * i - 2
    n_far = (jnp.maximum(i - 1, 0) * TQ + TK - 1) // TK
    blk8 = lax.broadcasted_iota(jnp.int32, (BLK_PER_TILE, 1), 0)
    init = (jnp.full((1, HT), NEG, F32), jnp.zeros((VT_ROWS, HT), F32))
    zero_shift = jnp.zeros((1, HT), F32)
    chunks_per_tile = TK // LANE

    qtg, w_plain, far_shift = [], [], []
    for g in range(KV_HEADS):
        heads = range(g * HPG, (g + 1) * HPG)
        qg = jnp.concatenate([qt[h * HEAD_DIM:(h + 1) * HEAD_DIM] for h in heads], axis=1).astype(BF16)
        qtg.append(qg)
        w_plain.append(jnp.concatenate([qg, jnp.zeros((LANE - HEAD_DIM, HT), BF16)], axis=0))
        far_shift.append(jnp.concatenate(
            [jnp.broadcast_to(tab_ref[h:h + 1, LANE - 1:LANE], (1, TQ)) for h in heads], axis=1))

    def vt_tile(c0, n, kind):
        return jnp.concatenate([vt_ref[c0 + c, kind] for c in range(n)], axis=1)

    def far_body(kt, carry):
        k0 = pl.multiple_of(WINDOW + kt * TK, TK)
        r0 = pl.multiple_of(kt * BLK_PER_TILE, BLK_PER_TILE)
        c0 = (WINDOW // LANE) + kt * chunks_per_tile
        out = []
        for g in range(KV_HEADS):
            mt = mask_ref[g, pl.ds(r0, BLK_PER_TILE), :]
            keep = (mt > 0.5) & (kt * BLK_PER_TILE + blk8 < near_blk0)
            mneg = jnp.where(keep, 0.0, MASKED)
            m16 = jnp.concatenate([jnp.concatenate([mneg] * HPG, axis=1),
                                   jnp.zeros((MASK_ROWS - BLK_PER_TILE, HT), F32)], axis=0).astype(BF16)
            w = jnp.concatenate([qtg[g], m16, jnp.zeros((LANE - HEAD_DIM - MASK_ROWS, HT), BF16)], axis=0)
            st = _dot(ka_ref[pl.ds(k0, TK), g * LANE:(g + 1) * LANE], w)
            out.append(_online_step_t(carry[g], st, far_shift[g], vt_tile(c0, chunks_per_tile, g)))
        return tuple(out)

    carry = lax.fori_loop(0, n_far, far_body, (init, init))

    kn0 = pl.multiple_of(WINDOW + (i - 1) * TQ, TQ)
    base = pl.multiple_of(jnp.clip((near_blk0 // BLK_PER_TILE) * BLK_PER_TILE, 0, n_sel - MASK_ROWS), BLK_PER_TILE)
    key_blk = near_blk0 + lax.broadcasted_iota(jnp.int32, (NEAR, MASK_ROWS), 0) // SEL_BLOCK
    expand = (key_blk == base + lax.broadcasted_iota(jnp.int32, (NEAR, MASK_ROWS), 1)).astype(BF16)
    kw0 = pl.multiple_of(i * TQ, TQ)
    rows_t = []
    for g in range(KV_HEADS):
        mneg = jnp.where(mask_ref[g, pl.ds(base, MASK_ROWS), :] > 0.5, 0.0, MASKED).astype(BF16)
        sel_add = _dot(expand, mneg)
        sel_add = lax.cond(
            i == 0,
            lambda a: jnp.where(lax.broadcasted_iota(jnp.int32, (NEAR, TQ), 0) >= TQ, a, MASKED),
            lambda a: a, sel_add)
        st = _dot(ka_ref[pl.ds(kn0, NEAR), g * LANE:(g + 1) * LANE], w_plain[g])
        st = st + ntab_ref[g] + jnp.concatenate([sel_add] * HPG, axis=1)
        _, acc = _online_step_t(carry[g], st, zero_shift, vt_tile(kn0 // LANE, NEAR // LANE, g))
        os_t = acc[:HEAD_DIM] / acc[HEAD_DIM:HEAD_DIM + 1]

        st = _dot(ka_ref[pl.ds(kw0, W_KEYS), (KV_HEADS + g) * LANE:(KV_HEADS + g + 1) * LANE], w_plain[g])
        st = st + wtab_ref[g]
        st = lax.cond(
            i < WINDOW // TQ,
            lambda a: jnp.where(lax.broadcasted_iota(jnp.int32, (W_KEYS, HT), 0) >= WINDOW - i * TQ, a, NEG),
            lambda a: a, st)
        _, acc = _online_step_t(init, st, zero_shift, vt_tile(kw0 // LANE, W_KEYS // LANE, KV_HEADS + g))
        ow_t = acc[:HEAD_DIM] / acc[HEAD_DIM:HEAD_DIM + 1]

        for h in range(HPG):
            head = g * HPG + h
            ls = slice(h * TQ, (h + 1) * TQ)
            rows_t.append(gt[head:head + 1] * oct_[head * HEAD_DIM:(head + 1) * HEAD_DIM]
                          + gt[N_HEADS + head:N_HEADS + head + 1] * os_t[:, ls]
                          + gt[2 * N_HEADS + head:2 * N_HEADS + head + 1] * ow_t[:, ls])
    o_ref[...] = jnp.concatenate(rows_t, axis=0).T


def distance_tables(tab):
    r = np.arange(TQ)[:, None]
    dn = TQ + r - np.arange(NEAR)[None, :]
    dw = WINDOW + r - np.arange(W_KEYS)[None, :]
    ntab = jnp.where((dn >= 0)[None], tab[:, np.clip(dn, 0, LANE - 1)], NEG)
    wtab = jnp.where(((dw >= 0) & (dw <= WINDOW))[None], tab[:, np.clip(dw, 0, LANE - 1)], NEG)
    tr = lambda a: jnp.transpose(a.reshape(KV_HEADS, HPG, TQ, -1), (0, 3, 1, 2)).reshape(KV_HEADS, -1, HT)
    return tr(ntab), tr(wtab)


def attention_kv_layout(hproj):
    t = hproj.shape[0]
    grp = lambda c0, g: hproj[:, c0 + g * HEAD_DIM:c0 + (g + 1) * HEAD_DIM]
    e8 = jnp.asarray(np.eye(BLK_PER_TILE, dtype=np.float32)[(np.arange(t) % TK) // SEL_BLOCK])
    z = lambda w: jnp.zeros((t, w), F32)
    ka = [jnp.concatenate([grp(C_KVS, g), e8, z(LANE - HEAD_DIM - BLK_PER_TILE)], axis=1) for g in range(KV_HEADS)]
    ka += [jnp.concatenate([grp(C_KVW, g), z(LANE - HEAD_DIM)], axis=1) for g in range(KV_HEADS)]
    ka = jnp.pad(jnp.concatenate(ka, axis=1).astype(BF16), ((WINDOW, 0), (0, 0)))
    ones = jnp.ones((t, 1), F32)
    vts = []
    for c0 in (C_KVS, C_KVW):
        for g in range(KV_HEADS):
            v = jnp.concatenate([grp(c0 + KV_HEADS * HEAD_DIM, g), ones, z(VT_ROWS - HEAD_DIM - 1)], axis=1)
            v = jnp.pad(v.astype(BF16), ((WINDOW, 0), (0, 0))).reshape(-1, LANE, VT_ROWS)
            vts.append(jnp.swapaxes(v, 1, 2))
    return ka, jnp.stack(vts, axis=1)


def nsa_sel_prompt(hproj, o_c, mask, ka, vt, tab, ntab, wtab):
    t = hproj.shape[0]
    n_sel = mask.shape[1]
    assert t % TK == 0 and n_sel % BLK_PER_TILE == 0 and n_sel >= MASK_ROWS
    const = lambda a: pl.BlockSpec(a.shape, lambda i: (0,) * a.ndim, pipeline_mode=pl.Buffered(1))
    return pl.pallas_call(
        functools.partial(_nsa_sel_kernel, n_sel=n_sel),
        grid=(t // TQ,),
        in_specs=[pl.BlockSpec((TQ, ATT_W), lambda i: (i, C_Q // ATT_W)),
                  pl.BlockSpec((TQ, LANE), lambda i: (i, C_NG // LANE)),
                  pl.BlockSpec((TQ, ATT_W), lambda i: (i, 0)),
                  pl.BlockSpec((KV_HEADS, n_sel, TQ), lambda i: (0, 0, i)),
                  const(ka), const(vt), const(tab), const(ntab), const(wtab)],
        out_specs=pl.BlockSpec((TQ, ATT_W), lambda i: (i, 0)),
        out_shape=jax.ShapeDtypeStruct((t, ATT_W), F32),
        compiler_params=_cparams(("parallel",)),
        name="nsa_sel_prompt",
    )(hproj, hproj, o_c, mask, ka, vt, tab, ntab, wtab)


def _hgrn_gates(hq, hf, lb):
    sig = _sigmoid(hf)
    forget = lb + (1.0 - lb) * sig
    logf = jnp.log(jnp.maximum(forget, TINY))
    k = (1.0 - lb) * (1.0 - sig)
    q = hq * _sigmoid(hq)
    return q, k, logf


def _hgrn_intra(q, k, v, a, sub):
    n = q.shape[0]
    row = lax.broadcasted_iota(jnp.int32, (n, 1), 0) % sub
    outs = [jnp.zeros((n, HG_DV), F32) for _ in range(HG_HEADS)]
    for d in range(sub):
        ks = k if d == 0 else pltpu.roll(k, d, 0)
        a_s = a if d == 0 else pltpu.roll(a, d, 0)
        vs = v if d == 0 else pltpu.roll(v, d, 0)
        msk = row >= d
        w = q * ks * jnp.exp(jnp.where(msk, a - a_s, 0.0)) * msk.astype(F32)
        for h in range(HG_HEADS):
            r = jnp.sum(w[:, h * HG_DK:(h + 1) * HG_DK], axis=-1, keepdims=True)
            outs[h] = outs[h] + r * vs[:, h * HG_DV:(h + 1) * HG_DV]
    return outs


def _hgrn_prompt_kernel(hq_ref, hf_ref, hi_ref, lb_ref, tri_ref, o_ref, st_out_ref, st_ref):
    step = pl.program_id(0)

    @pl.when(step == 0)
    def _():
        st_ref[...] = jnp.zeros(st_ref.shape, F32)

    n = hq_ref.shape[0]
    q, k, logf = _hgrn_gates(hq_ref[...], hf_ref[...], lb_ref[...])
    v = hi_ref[...]
    a = jnp.dot(tri_ref[...], logf, preferred_element_type=F32, precision=lax.Precision.HIGHEST)
    outs = _hgrn_intra(q, k, v, a, SUB)
    rows_out = [[] for _ in range(HG_HEADS)]
    for c in range(n // SUB):
        sl = slice(c * SUB, (c + 1) * SUB)
        a_c = a[sl]
        a_last = a_c[SUB - 1:SUB]
        qe = (q[sl] * jnp.exp(a_c)).astype(BF16)
        kd = (k[sl] * jnp.exp(a_last - a_c)).astype(BF16)
        dec = jnp.exp(a_last)
        for h in range(HG_HEADS):
            hs = slice(h * HG_DK, (h + 1) * HG_DK)
            st = st_ref[h]
            rows_out[h].append(outs[h][sl] + _dot_nt(qe[:, hs], st.astype(BF16)))
            st_ref[h] = dec[:, hs] * st + _dot_tn(v[sl, hs].astype(BF16), kd[:, hs])
    o_ref[...] = jnp.concatenate([jnp.concatenate(r, axis=0) for r in rows_out], axis=1)
    st_out_ref[...] = st_ref[...]


def hgrn_prompt(hproj, lb, tc):
    t = hproj.shape[0]
    tc = min(tc, t)
    r = np.arange(tc)
    tri = jnp.asarray(((r[:, None] >= r[None, :]) & (r[:, None] // SUB == r[None, :] // SUB)).astype(np.float32))
    return pl.pallas_call(
        _hgrn_prompt_kernel,
        grid=(t // tc,),
        in_specs=[pl.BlockSpec((tc, HGK_W), lambda i: (i, C_HQ // HGK_W)),
                  pl.BlockSpec((tc, HGK_W), lambda i: (i, C_HF // HGK_W)),
                  pl.BlockSpec((tc, HGK_W), lambda i: (i, C_HI // HGK_W)),
                  pl.BlockSpec((1, HGK_W), lambda i: (0, 0)),
                  pl.BlockSpec((tc, tc), lambda i: (0, 0))],
        out_specs=[pl.BlockSpec((tc, HGK_W), lambda i: (i, 0)),
                   pl.BlockSpec((HG_HEADS, HG_DV, HG_DK), lambda i: (0, 0, 0))],
        out_shape=[jax.ShapeDtypeStruct((t, HGK_W), F32),
                   jax.ShapeDtypeStruct((HG_HEADS, HG_DV, HG_DK), F32)],
        scratch_shapes=[pltpu.VMEM((HG_HEADS, HG_DV, HG_DK), F32)],
        compiler_params=_cparams(("arbitrary",)),
        name="hgrn_prompt",
    )(hproj, hproj, hproj, lb.reshape(1, HGK_W), tri)


def _merge_kernel(x_ref, att_ref, o_ref, hg_ref, ga_ref, gh_ref, ng_ref, fg_ref, wa_ref, wh_ref, wo_ref,
                  xo_ref, xn_ref):
    o = o_ref[...]
    parts = []
    for h in range(HG_HEADS):
        oh = o[:, h * HG_DV:(h + 1) * HG_DV]
        parts.append(oh * lax.rsqrt(jnp.mean(oh * oh, axis=-1, keepdims=True) + EPS))
    hg = hg_ref[...]
    hgo = jnp.concatenate(parts, axis=1) * ng_ref[...] * (hg * _sigmoid(hg))
    m = (_sigmoid(ga_ref[...]) * _dot(att_ref[...].astype(BF16), wa_ref[...])
         + _sigmoid(gh_ref[...]) * _dot(hgo.astype(BF16), wh_ref[...]))
    x = x_ref[...] + _dot(m.astype(BF16), wo_ref[...])
    xo_ref[...] = x
    y = x * lax.rsqrt(jnp.mean(x * x, axis=-1, keepdims=True) + EPS)
    xn_ref[...] = (y * fg_ref[...]).astype(BF16)


def merge(x, att, o, hproj, hg_norm_g, ffn_g, wa, wh, wo, tm):
    t = x.shape[0]
    tm = min(tm, t)
    row = lambda w, cb: pl.BlockSpec((tm, w), lambda i: (i, cb))
    full = lambda a: pl.BlockSpec(a.shape, lambda i: (0, 0))
    hg_norm_g = hg_norm_g.reshape(1, HGK_W)
    ffn_g = ffn_g.reshape(1, D_MODEL)
    return pl.pallas_call(
        _merge_kernel,
        grid=(t // tm,),
        in_specs=[row(D_MODEL, 0), row(ATT_W, 0), row(HGK_W, 0), row(HGK_W, C_HG // HGK_W),
                  row(D_MODEL, C_GA // D_MODEL), row(D_MODEL, C_GH // D_MODEL),
                  full(hg_norm_g), full(ffn_g), full(wa), full(wh), full(wo)],
        out_specs=[row(D_MODEL, 0), row(D_MODEL, 0)],
        out_shape=[jax.ShapeDtypeStruct((t, D_MODEL), F32), jax.ShapeDtypeStruct((t, D_MODEL), BF16)],
        compiler_params=_cparams(("parallel",)),
        name="merge",
    )(x, att, o, hproj, hproj, hproj, hg_norm_g, ffn_g, wa, wh, wo)


def _topk_cols(x, k, payload=None):
    n, t = x.shape
    row = lax.broadcasted_iota(jnp.int32, (n, t), 0)
    vals, ids, pay = [], [], []
    for _ in range(k):
        m = jnp.max(x, axis=0, keepdims=True)
        idx = jnp.min(jnp.where(x == m, row, n), axis=0, keepdims=True)
        hit = row == idx
        vals.append(m)
        ids.append(idx)
        if payload is not None:
            pay.append(jnp.max(jnp.where(hit, payload, -1), axis=0, keepdims=True))
        x = jnp.where(hit, NEG_INF, x)
    cat = lambda xs: jnp.concatenate(xs, axis=0)
    return cat(vals), cat(ids), (cat(pay) if payload is not None else None)


def _pair_candidates():
    pairs = [(a, b) for a in range(PEER_TOPK) for b in range(PEER_TOPK) if (a + 1) * (b + 1) <= PEER_TOPK]
    rows = -(-len(pairs) // 8) * 8
    sel = np.zeros((2, rows, PEER_TOPK), np.float32)
    for r, (a, b) in enumerate(pairs):
        sel[0, r, a] = 1.0
        sel[1, r, b] = 1.0
    return len(pairs), jnp.asarray(sel)


def _pick_rows(sel, x):
    return jnp.dot(sel, x, preferred_element_type=F32, precision=lax.Precision.HIGHEST)


def _peer_topk_kernel(q_ref, keys_ref, pair_ref, i1_ref, i2_ref, g_ref, *, n_pairs):
    o1, o2, og = [], [], []
    live = lax.broadcasted_iota(jnp.int32, (pair_ref.shape[1], q_ref.shape[0]), 0) < n_pairs
    for h in range(PEER_HEADS):
        top = []
        for side in range(2):
            c0 = (h * 2 + side) * PEER_DH
            s = _dot_nt(keys_ref[side], q_ref[:, c0:c0 + PEER_DH].astype(BF16))
            top.append(_topk_cols(s, PEER_TOPK)[:2])
        (v1, i1), (v2, i2) = top
        cand = jnp.where(live, _pick_rows(pair_ref[0], v1) + _pick_rows(pair_ref[1], v2), NEG_INF)
        cidx = (_pick_rows(pair_ref[0], i1.astype(F32)) * PEER_NKEYS
                + _pick_rows(pair_ref[1], i2.astype(F32))).astype(jnp.int32)
        sv, _, e = _topk_cols(cand, PEER_TOPK, cidx)
        ex = jnp.exp(sv - sv[0:1])
        o1.append(e // PEER_NKEYS)
        o2.append(e % PEER_NKEYS)
        og.append(ex / jnp.sum(ex, axis=0, keepdims=True))
    i1_ref[...] = jnp.concatenate(o1, axis=0).astype(F32).T.astype(jnp.int32)
    i2_ref[...] = jnp.concatenate(o2, axis=0).astype(F32).T.astype(jnp.int32)
    g_ref[...] = jnp.concatenate(og, axis=0).T


def peer_topk(q, keys_bf):
    t = q.shape[0]
    tb = LANE
    nsel = PEER_HEADS * PEER_TOPK
    n_pairs, pair_sel = _pair_candidates()
    out = lambda dt: jax.ShapeDtypeStruct((t, nsel), dt)
    return pl.pallas_call(
        functools.partial(_peer_topk_kernel, n_pairs=n_pairs),
        grid=(t // tb,),
        in_specs=[pl.BlockSpec((tb, q.shape[1]), lambda i: (i, 0)),
                  pl.BlockSpec(keys_bf.shape, lambda i: (0, 0, 0)),
                  pl.BlockSpec(pair_sel.shape, lambda i: (0, 0, 0))],
        out_specs=[pl.BlockSpec((tb, nsel), lambda i: (i, 0))] * 3,
        out_shape=[out(jnp.int32), out(jnp.int32), out(F32)],
        compiler_params=_cparams(("parallel",)),
        name="peer_topk",
    )(q, keys_bf, pair_sel)


GATE_UNROLL = 8


def _peer_dense_kernel(xn_ref, x_ref, i1_ref, i2_ref, g_ref, u_ref, v_ref, o_ref, acc_ref, gate_ref, *, n_a):
    j = pl.program_id(1)
    tm, nsel = i1_ref.shape

    @pl.when(j == 0)
    def _():
        acc_ref[...] = jnp.zeros(acc_ref.shape, F32)
        row = lax.broadcasted_iota(jnp.int32, (PEER_NKEYS, nsel), 0)

        def body(tt, _):
            for k in range(GATE_UNROLL):
                t = tt * GATE_UNROLL + k
                pa = jnp.where(row == i1_ref[pl.ds(t, 1), :], 1.0, 0.0).astype(BF16)
                gb = jnp.where(row == i2_ref[pl.ds(t, 1), :], g_ref[pl.ds(t, 1), :], 0.0).astype(BF16)
                gate_ref[pl.ds(pl.multiple_of(t * PEER_NKEYS, PEER_NKEYS), PEER_NKEYS), :] = _dot_nt(pa, gb)
            return 0

        lax.fori_loop(0, tm // GATE_UNROLL, body, 0)

    s = _dot_nt(xn_ref[...], u_ref[...])
    gate = jnp.concatenate([gate_ref[pl.ds(j * n_a + k, tm, stride=PEER_NKEYS), :] for k in range(n_a)], axis=1)
    acc_ref[...] += _dot((gate * _gelu(s)).astype(BF16), v_ref[...])

    @pl.when(j == pl.num_programs(1) - 1)
    def _():
        o_ref[...] = x_ref[...] + acc_ref[...]


def peer_dense(xn, x, i1, i2, g, u_bf, v_bf, tm, n_a):
    t = x.shape[0]
    tm = min(tm, t)
    te = n_a * PEER_NKEYS
    nsel = i1.shape[1]
    tok = lambda w: pl.BlockSpec((tm, w), lambda i, j: (i, 0))
    return pl.pallas_call(
        functools.partial(_peer_dense_kernel, n_a=n_a),
        grid=(t // tm, PEER_NKEYS // n_a),
        in_specs=[tok(D_MODEL), tok(D_MODEL), tok(nsel), tok(nsel), tok(nsel),
                  pl.BlockSpec((te, D_MODEL), lambda i, j: (j, 0)),
                  pl.BlockSpec((te, D_MODEL), lambda i, j: (j, 0))],
        out_specs=tok(D_MODEL),
        out_shape=jax.ShapeDtypeStruct((t, D_MODEL), F32),
        scratch_shapes=[pltpu.VMEM((tm, D_MODEL), F32), pltpu.VMEM((tm * PEER_NKEYS, PEER_NKEYS), F32)],
        compiler_params=_cparams(("parallel", "arbitrary")),
        name="peer_dense",
    )(xn, x, i1, i2, g, u_bf, v_bf)


def peer(xn_bf, x, wq_bf, keys_bf, u_bf, v_bf, tm):
    t = x.shape[0]
    q = matmul(xn_bf, wq_bf, min(512, t), 1024)
    i1, i2, g = peer_topk(q, keys_bf)
    return peer_dense(xn_bf, x, i1, i2, g, u_bf, v_bf, tm, 16)


ROWS = 8


def _pad_rows(x, rows=ROWS):
    return jnp.concatenate([x, jnp.zeros((rows - x.shape[0], x.shape[1]), x.dtype)], axis=0)


def _nsa_cmp_sample_kernel(q_ref, ckv_ref, tab_ref, cover_ref, oc_ref, sel_ref, *, nc, past, n_sel):
    nt = q_ref.shape[0]
    t_pos = past + lax.broadcasted_iota(jnp.int32, (ROWS, 1), 0)
    outs, imps = _cmp_branch(_pad_rows(q_ref[...]), t_pos, ckv_ref, tab_ref, cover_ref, nc)
    for g in range(KV_HEADS):
        sel_ref[g] = _topk_idx(imps[g], min(SEL_TOPN, n_sel))
    oc_ref[...] = jnp.concatenate(outs, axis=1)[:nt]


def nsa_cmp_sample(hproj3, ckv, tab, nc, past):
    nb, nt, _ = hproj3.shape
    ncp = ckv.shape[1]
    n_sel = -(-(past + nt) // SEL_BLOCK)
    n_sel_pad = -(-n_sel // LANE) * LANE
    cover = cover_matrix(ncp, n_sel_pad)
    return pl.pallas_call(
        functools.partial(_nsa_cmp_sample_kernel, nc=nc, past=past, n_sel=n_sel),
        grid=(nb,),
        in_specs=[pl.BlockSpec((None, nt, ATT_W), lambda b: (b, 0, C_Q // ATT_W)),
                  pl.BlockSpec((None, ncp, KVB_W), lambda b: (b, 0, 0)),
                  pl.BlockSpec((N_HEADS, LANE), lambda b: (0, 0)),
                  pl.BlockSpec((ncp, n_sel_pad), lambda b: (0, 0))],
        out_specs=[pl.BlockSpec((None, nt, ATT_W), lambda b: (b, 0, 0)),
                   pl.BlockSpec((None, KV_HEADS, ROWS, LANE), lambda b: (b, 0, 0, 0))],
        out_shape=[jax.ShapeDtypeStruct((nb, nt, ATT_W), F32),
                   jax.ShapeDtypeStruct((nb, KV_HEADS, ROWS, LANE), jnp.int32)],
        compiler_params=_cparams(("parallel",)),
        name="nsa_cmp_sample",
    )(hproj3, ckv, tab, cover)


def _nsa_sel_sample_kernel(sel_ref, pt_ref, q_ref, ng_ref, oc_ref, kvs_ref, kvw_ref, win_ref, tab_ref, cache_ref,
                           o_ref, buf_ref, sem, *, layer, past, w_len):
    b = pl.program_id(0)
    nt = q_ref.shape[0]
    nslot = SEL_TOPN
    per_page = PAGE_SIZE // SEL_BLOCK

    def slot_copy(t, g, slot):
        blk = sel_ref[((b * nt + t) * KV_HEADS + g) * nslot + slot]
        in_past = blk * SEL_BLOCK < past
        blk_c = jnp.minimum(blk, past // SEL_BLOCK - 1)
        page = pt_ref[b, blk_c // per_page]
        src = cache_ref.at[layer, page, pl.ds((blk_c % per_page) * SEL_BLOCK, SEL_BLOCK), pl.ds(KVB_W, KVB_W)]
        dst = buf_ref.at[t * KV_HEADS + g, pl.ds(slot * SEL_BLOCK, SEL_BLOCK), :]
        return blk, in_past, pltpu.make_async_copy(src, dst, sem)

    for t in range(nt):
        for g in range(KV_HEADS):
            for slot in range(nslot):
                _, in_past, cp = slot_copy(t, g, slot)

                @pl.when(in_past)
                def _():
                    cp.start()

                @pl.when(jnp.logical_not(in_past))
                def _():
                    buf_ref[t * KV_HEADS + g, pl.ds(slot * SEL_BLOCK, SEL_BLOCK), :] = jnp.zeros(
                        (SEL_BLOCK, KVB_W), F32)
    for t in range(nt):
        for g in range(KV_HEADS):
            for slot in range(nslot):
                _, in_past, cp = slot_copy(t, g, slot)

                @pl.when(in_past)
                def _():
                    cp.wait()

    gates = _sigmoid(ng_ref[...])
    nk = nslot * SEL_BLOCK
    lane_slot = lax.broadcasted_iota(jnp.int32, (1, nk), 1) // SEL_BLOCK
    lane_off = lax.broadcasted_iota(jnp.int32, (1, nk), 1) % SEL_BLOCK
    j_new = lax.broadcasted_iota(jnp.int32, (1, LANE), 1)
    kvn = jnp.concatenate([kvs_ref[...], kvw_ref[...]], axis=1)
    kvn_pad = _pad_rows(kvn, LANE).astype(BF16)
    win = win_ref[...].astype(BF16)
    w_idx = lax.broadcasted_iota(jnp.int32, (1, w_len + LANE), 1)
    rows_out = []
    for t in range(nt):
        t_pos = past + t
        heads_out = []
        for g in range(KV_HEADS):
            q4 = jnp.concatenate(
                [q_ref[t:t + 1, (g * HPG + h) * HEAD_DIM:(g * HPG + h + 1) * HEAD_DIM] for h in range(HPG)],
                axis=0)
            q4 = (q4 * ATT_SCALE).astype(BF16)

            def bias_of(dist):
                d = jnp.broadcast_to(jnp.clip(dist, 0, LANE - 1), (ROWS, dist.shape[1]))
                return jnp.concatenate(
                    [_bias_lookup(tab_ref[g * HPG + h:g * HPG + h + 1, :], d)[0:1] for h in range(HPG)], axis=0)

            pos = jnp.zeros((1, nk), jnp.int32)
            has_new = jnp.zeros((1, 1), jnp.int32)
            for slot in range(nslot):
                blk = sel_ref[((b * nt + t) * KV_HEADS + g) * nslot + slot]
                pos = jnp.where(lane_slot == slot, blk * SEL_BLOCK, pos)
                has_new = jnp.maximum(has_new, (blk * SEL_BLOCK >= past).astype(jnp.int32))
            pos = pos + lane_off
            kv_g = buf_ref[t * KV_HEADS + g].astype(BF16)
            k_all = jnp.concatenate([kv_g[:, g * HEAD_DIM:(g + 1) * HEAD_DIM],
                                     kvn_pad[:, g * HEAD_DIM:(g + 1) * HEAD_DIM]], axis=0)
            v_all = jnp.concatenate([kv_g[:, (KV_HEADS + g) * HEAD_DIM:(KV_HEADS + g + 1) * HEAD_DIM],
                                     kvn_pad[:, (KV_HEADS + g) * HEAD_DIM:(KV_HEADS + g + 1) * HEAD_DIM]], axis=0)
            dist = jnp.concatenate([t_pos - pos, t - j_new], axis=1)
            ok = jnp.concatenate([pos < past, (j_new <= t) & (has_new > 0)], axis=1)
            s = jnp.where(ok, _dot_nt(q4, k_all) + bias_of(dist), NEG)
            e = jnp.where(ok, jnp.exp(s - jnp.max(s, axis=-1, keepdims=True)), 0.0)
            o_s = _dot((e / jnp.sum(e, axis=-1, keepdims=True)).astype(BF16), v_all)

            kw = jnp.concatenate([win[:, g * HEAD_DIM:(g + 1) * HEAD_DIM],
                                  kvn_pad[:, (2 * KV_HEADS + g) * HEAD_DIM:(2 * KV_HEADS + g + 1) * HEAD_DIM]], axis=0)
            vw = jnp.concatenate([win[:, (KV_HEADS + g) * HEAD_DIM:(KV_HEADS + g + 1) * HEAD_DIM],
                                  kvn_pad[:, (3 * KV_HEADS + g) * HEAD_DIM:(3 * KV_HEADS + g + 1) * HEAD_DIM]], axis=0)
            dist_w = t_pos - (past - w_len + w_idx)
            ok_w = (dist_w >= 0) & (dist_w <= WINDOW) & (w_idx < w_len + nt)
            s = jnp.where(ok_w, _dot_nt(q4, kw) + bias_of(dist_w), NEG)
            e = jnp.where(ok_w, jnp.exp(s - jnp.max(s, axis=-1, keepdims=True)), 0.0)
            o_w = _dot((e / jnp.sum(e, axis=-1, keepdims=True)).astype(BF16), vw)

            for h in range(HPG):
                head = g * HPG + h
                o_c = oc_ref[t:t + 1, head * HEAD_DIM:(head + 1) * HEAD_DIM]
                heads_out.append(gates[t:t + 1, head:head + 1] * o_c
                                 + gates[t:t + 1, N_HEADS + head:N_HEADS + head + 1] * o_s[h:h + 1]
                                 + gates[t:t + 1, 2 * N_HEADS + head:2 * N_HEADS + head + 1] * o_w[h:h + 1])
        rows_out.append(jnp.concatenate(heads_out, axis=1))
    o_ref[...] = jnp.concatenate(rows_out, axis=0)


def nsa_sel_sample(hproj3, o_c, sel_flat, page_table, cache4, win, tab, layer, past):
    nb, nt, _ = hproj3.shape
    w_len = win.shape[1]
    assert past % SEL_BLOCK == 0 and PAGE_SIZE % SEL_BLOCK == 0 and w_len % LANE == 0 and past >= w_len
    grid_spec = pltpu.PrefetchScalarGridSpec(
        num_scalar_prefetch=2,
        grid=(nb,),
        in_specs=[pl.BlockSpec((None, nt, ATT_W), lambda b, s, p: (b, 0, C_Q // ATT_W)),
                  pl.BlockSpec((None, nt, LANE), lambda b, s, p: (b, 0, C_NG // LANE)),
                  pl.BlockSpec((None, nt, ATT_W), lambda b, s, p: (b, 0, 0)),
                  pl.BlockSpec((None, nt, KVB_W), lambda b, s, p: (b, 0, C_KVS // KVB_W)),
                  pl.BlockSpec((None, nt, KVB_W), lambda b, s, p: (b, 0, C_KVW // KVB_W)),
                  pl.BlockSpec((None, w_len, KVB_W), lambda b, s, p: (b, 0, 0)),
                  pl.BlockSpec((N_HEADS, LANE), lambda b, s, p: (0, 0)),
                  pl.BlockSpec(memory_space=pl.ANY)],
        out_specs=pl.BlockSpec((None, nt, ATT_W), lambda b, s, p: (b, 0, 0)),
        scratch_shapes=[pltpu.VMEM((nt * KV_HEADS, SEL_TOPN * SEL_BLOCK, KVB_W), F32),
                        pltpu.SemaphoreType.DMA(())],
    )
    return pl.pallas_call(
        functools.partial(_nsa_sel_sample_kernel, layer=layer, past=past, w_len=w_len),
        grid_spec=grid_spec,
        out_shape=jax.ShapeDtypeStruct((nb, nt, ATT_W), F32),
        compiler_params=_cparams(("arbitrary",)),
        name="nsa_sel_sample",
    )(sel_flat, page_table, hproj3, hproj3, o_c, hproj3, hproj3, win, tab, cache4)


def _hgrn_sample_kernel(hq_ref, hf_ref, hi_ref, lb_ref, s_ref, o_ref, so_ref):
    nt = hq_ref.shape[0]
    live = (lax.broadcasted_iota(jnp.int32, (ROWS, 1), 0) < nt).astype(F32)
    q, k, logf = _hgrn_gates(_pad_rows(hq_ref[...]), _pad_rows(hf_ref[...]), lb_ref[...])
    q, k, logf = q * live, k * live, logf * live
    v = _pad_rows(hi_ref[...])
    rows = [logf[0:1]]
    for i in range(1, ROWS):
        rows.append(rows[-1] + logf[i:i + 1])
    a = jnp.concatenate(rows, axis=0)
    outs = _hgrn_intra(q, k, v, a, ROWS)
    a_last = a[ROWS - 1:ROWS]
    qe = (q * jnp.exp(a)).astype(BF16)
    kd = (k * jnp.exp(a_last - a)).astype(BF16)
    dec = jnp.exp(a_last)
    o_parts = []
    for h in range(HG_HEADS):
        hs = slice(h * HG_DK, (h + 1) * HG_DK)
        st = s_ref[h].T
        o_parts.append(outs[h] + _dot_nt(qe[:, hs], st.astype(BF16)))
        so_ref[h] = (dec[:, hs] * st + _dot_tn(v[:, hs].astype(BF16), kd[:, hs])).T
    o_ref[...] = jnp.concatenate(o_parts, axis=1)[:nt]


def hgrn_sample(hproj3, lb, state):
    nb, nt, _ = hproj3.shape
    col = lambda c: pl.BlockSpec((None, nt, HGK_W), lambda b: (b, 0, c // HGK_W))
    st_spec = pl.BlockSpec((None, HG_HEADS, HG_DK, HG_DV), lambda b: (b, 0, 0, 0))
    return pl.pallas_call(
        _hgrn_sample_kernel,
        grid=(nb,),
        in_specs=[col(C_HQ), col(C_HF), col(C_HI), pl.BlockSpec((1, HGK_W), lambda b: (0, 0)), st_spec],
        out_specs=[pl.BlockSpec((None, nt, HGK_W), lambda b: (b, 0, 0)), st_spec],
        out_shape=[jax.ShapeDtypeStruct((nb, nt, HGK_W), F32),
                   jax.ShapeDtypeStruct(state.shape, F32)],
        compiler_params=_cparams(("parallel",)),
        name="hgrn_sample",
    )(hproj3, hproj3, hproj3, lb.reshape(1, HGK_W), state)


def reorder_w_in(w):
    o = np.cumsum([0, ATT_W, KVB_W, KVB_W, KVB_W, N_GATE, HGK_W, HGK_W, HGK_W, HGK_W, D_MODEL, D_MODEL])
    q, kvc, kvs, kvw, ng, hq, hf, hi, hg, ga, gh = [w[:, o[i]:o[i + 1]] for i in range(11)]
    pad = jnp.zeros((w.shape[0], N_PROJ - C_NG - N_GATE), w.dtype)
    return jnp.concatenate([q, hq, ga, gh, hf, hi, hg, kvc, kvs, kvw, ng, pad], axis=1).astype(BF16)


def prompt_layer(x, lp, tabs):
    t = x.shape[0]
    tab, ntab, wtab = tabs
    hproj = norm_matmul(x, lp["attn_g"], lp["w_in"], min(512, t), 1408)
    y = cproj_rows(hproj, C_KVC // KVB_W, lp["wbig"], 2048)
    ckv = cmp_mlp(y[None], lp["pe_term"], lp["w2big"])[0]
    o_c, mask = nsa_cmp_prompt(hproj, ckv, tab, t // CMP_STRIDE - 1)
    ka, vt = attention_kv_layout(hproj)
    att = nsa_sel_prompt(hproj, o_c, mask, ka, vt, tab, ntab, wtab)
    o_hg, st = hgrn_prompt(hproj, lp["lb"], 128)
    x, xn = merge(x, att, o_hg, hproj, lp["hg_norm_g"], lp["ffn_g"], lp["wa"], lp["wh"], lp["wo"], 256)
    x = peer(xn, x, lp["peer_wq"], lp["peer_keys"], lp["peer_u"], lp["peer_v"], 256)
    return x, hproj[:, C_KVC:C_KVC + 2 * KVB_W], hproj[:, C_KVW:C_KVW + KVB_W], jnp.swapaxes(st, 1, 2)


def layer_params(l, norm_attn_g, norm_ffn_g, w_in, cmp_pe, cmp_w1, cmp_w2, lower_bounds, hg_norm_g,
                 w_br_attn, w_br_hg, w_out, peer_wq, peer_keys, peer_u, peer_v):
    wbig, w2big, pe_rows = compress_weights(cmp_pe[l], cmp_w1[l], cmp_w2[l])
    return dict(attn_g=norm_attn_g[l], ffn_g=norm_ffn_g[l], w_in=reorder_w_in(w_in[l]),
                wbig=wbig, w2big=w2big, pe_term=pe_term_from(pe_rows, wbig),
                lb=lower_bounds[l], hg_norm_g=hg_norm_g[l],
                wa=w_br_attn[l].astype(BF16), wh=w_br_hg[l].astype(BF16), wo=w_out[l].astype(BF16),
                peer_wq=peer_wq[l].astype(BF16), peer_keys=peer_keys[l].astype(BF16),
                peer_u=peer_u[l].astype(BF16), peer_v=peer_v[l].astype(BF16))


def sample_layer(x, lp, tab, layer, cache4, win, state, page_table, nb, nt):
    past = page_table.shape[1] * PAGE_SIZE
    assert nt < CMP_STRIDE, "new tokens never complete a compression chunk"
    hproj = norm_matmul(x, lp["attn_g"], lp["w_in"], x.shape[0], 1408)
    hproj3 = hproj.reshape(nb, nt, N_PROJ)
    y = cproj_pages(cache4, layer, page_table, lp["wbig"], 16)
    ckv = cmp_mlp(y, lp["pe_term"], lp["w2big"])
    o_c, sel = nsa_cmp_sample(hproj3, ckv, tab, past // CMP_STRIDE - 1, past)
    sel_flat = jnp.transpose(sel[:, :, :nt, :SEL_TOPN], (0, 2, 1, 3)).reshape(-1)
    att = nsa_sel_sample(hproj3, o_c, sel_flat, page_table, cache4, win, tab, layer, past)
    o_hg, st = hgrn_sample(hproj3, lp["lb"], state)
    x, xn = merge(x, att.reshape(nb * nt, ATT_W), o_hg.reshape(nb * nt, HGK_W), hproj, lp["hg_norm_g"],
                  lp["ffn_g"], lp["wa"], lp["wh"], lp["wo"], 256)
    x = peer(xn, x, lp["peer_wq"], lp["peer_keys"], lp["peer_u"], lp["peer_v"], 256)
    return x, hproj[:, C_KVC:C_KVC + 2 * KVB_W], hproj[:, C_KVW:C_KVW + KVB_W], st


def kernel(x_prompt, x_sample, cache_kv, cache_win, state_hgrn, page_table, norm_attn_g, norm_ffn_g, final_norm_g, w_in, cmp_pe, cmp_w1, cmp_w2, rel_bias, hg_lb_logits, hg_norm_g, w_br_attn, w_br_hg, w_out, peer_wq, peer_keys, peer_u, peer_v):
    depth = w_in.shape[0]
    bp, seq, _ = x_prompt.shape
    nb, nt, _ = x_sample.shape
    assert bp == 1
    p_lb = jax.nn.softmax(hg_lb_logits.astype(F32), axis=0)
    lower_bounds = jnp.cumsum(p_lb, axis=0) - p_lb[0]
    tab = bias_table(rel_bias)
    n_pool = cache_kv.shape[1]
    cache4 = cache_kv.reshape(depth, n_pool, PAGE_SIZE, 2 * KVB_W)
    w_buf = cache_win.shape[2]
    xp = x_prompt.reshape(seq, D_MODEL)
    xs = x_sample.reshape(nb * nt, D_MODEL)
    kv_p, win_p, hs_p, kv_s, win_s, hs_s = [], [], [], [], [], []
    for l in range(depth):
        lp = layer_params(l, norm_attn_g, norm_ffn_g, w_in, cmp_pe, cmp_w1, cmp_w2, lower_bounds, hg_norm_g,
                          w_br_attn, w_br_hg, w_out, peer_wq, peer_keys, peer_u, peer_v)
        xp, kv, kw, st = prompt_layer(xp, lp, (tab,) + distance_tables(tab))
        kv_p.append(kv.reshape(bp, seq, 4, KV_HEADS, HEAD_DIM))
        win_p.append(kw[seq - min(WINDOW, seq):].reshape(bp, -1, 2, KV_HEADS, HEAD_DIM))
        hs_p.append(st[None])
        win = cache_win[l].reshape(nb, w_buf, KVB_W)
        xs, kv, kw, st = sample_layer(xs, lp, tab, l, cache4, win, state_hgrn[l], page_table, nb, nt)
        kv_s.append(kv.reshape(nb, nt, 4, KV_HEADS, HEAD_DIM))
        win_s.append(jnp.concatenate([cache_win[l], kw.reshape(nb, nt, 2, KV_HEADS, HEAD_DIM)], axis=1)[:, nt:])
        hs_s.append(st)
    y_prompt = rmsnorm(xp, final_norm_g, min(512, seq)).reshape(bp, seq, D_MODEL)
    y_sample = rmsnorm(xs, final_norm_g, nb * nt).reshape(nb, nt, D_MODEL)
    return (y_prompt, y_sample, jnp.stack(kv_p), jnp.stack(win_p), jnp.stack(hs_p),
            jnp.stack(kv_s), jnp.stack(win_s), jnp.stack(hs_s))
```

```python
import functools
import math

import jax
import jax.numpy as jnp
import numpy as np
from jax import lax
from jax.experimental import pallas as pl
from jax.experimental.pallas import tpu as pltpu

F32 = jnp.float32
BF16 = jnp.bfloat16

D_MODEL = 1024
PAGE_SIZE = 128
N_HEADS = 8
HEAD_DIM = 64
KV_HEADS = 2
HPG = N_HEADS // KV_HEADS
CMP_STRIDE = 16
CMP_BLOCK = 32
SEL_BLOCK = 64
SEL_TOPN = 16
WINDOW = 512
ATT_SCALE = HEAD_DIM ** -0.5
N_BUCKETS = 32
MAX_DISTANCE = 128
HG_HEADS = 4
HG_DK = 128
HG_DV = 128
PEER_HEADS = 8
PEER_NKEYS = 128
PEER_DH = 128
PEER_TOPK = 16
ATT_W = N_HEADS * HEAD_DIM
KVB_W = 2 * KV_HEADS * HEAD_DIM
HGK_W = HG_HEADS * HG_DK
N_GATE = 3 * N_HEADS
EPS = 1e-6
NEG = -1e30
BIG = 1e6
TINY = 1e-30
NEG_INF = float("-inf")

LANE = 128
VMEM_LIMIT = 56 * 1024 * 1024

C_Q, C_HQ, C_GA, C_GH, C_HF, C_HI, C_HG = 0, 512, 1024, 2048, 3072, 3584, 4096
C_KVC, C_KVS, C_KVW, C_NG, N_PROJ = 4608, 4864, 5120, 5376, 5632
SUB = 16
TQ = 128
TK = 512


def _cparams(sem, vmem=VMEM_LIMIT):
    return pltpu.CompilerParams(dimension_semantics=sem, vmem_limit_bytes=vmem)


def _dot(a, b):
    return jnp.dot(a, b, preferred_element_type=F32)


def _dot_nt(a, b):
    return lax.dot_general(a, b, (((1,), (1,)), ((), ())), preferred_element_type=F32)


def _dot_tn(a, b):
    return lax.dot_general(a, b, (((0,), (0,)), ((), ())), preferred_element_type=F32)


def _sigmoid(x):
    return 1.0 / (1.0 + jnp.exp(-x))


def _gelu(x):
    return 0.5 * x * (1.0 + lax.erf(x * (2.0 ** -0.5)))


def _norm_matmul_kernel(x_ref, g_ref, w_ref, o_ref, xn_ref):
    @pl.when(pl.program_id(1) == 0)
    def _():
        x = x_ref[...]
        y = x * lax.rsqrt(jnp.mean(x * x, axis=-1, keepdims=True) + EPS)
        xn_ref[...] = (y * g_ref[...]).astype(BF16)

    o_ref[...] = _dot(xn_ref[...], w_ref[...])


def norm_matmul(x, g, w, tm, tn):
    m, k = x.shape
    n = w.shape[1]
    return pl.pallas_call(
        _norm_matmul_kernel,
        grid=(m // tm, n // tn),
        in_specs=[pl.BlockSpec((tm, k), lambda i, j: (i, 0)),
                  pl.BlockSpec((1, k), lambda i, j: (0, 0)),
                  pl.BlockSpec((k, tn), lambda i, j: (0, j))],
        out_specs=pl.BlockSpec((tm, tn), lambda i, j: (i, j)),
        out_shape=jax.ShapeDtypeStruct((m, n), F32),
        scratch_shapes=[pltpu.VMEM((tm, k), BF16)],
        compiler_params=_cparams(("parallel", "arbitrary")),
        name="norm_matmul",
    )(x, g.reshape(1, k), w)


def _matmul_kernel(x_ref, w_ref, o_ref):
    o_ref[...] = _dot(x_ref[...], w_ref[...])


def matmul(x, w, tm, tn):
    m, k = x.shape
    n = w.shape[1]
    return pl.pallas_call(
        _matmul_kernel,
        grid=(m // tm, n // tn),
        in_specs=[pl.BlockSpec((tm, k), lambda i, j: (i, 0)),
                  pl.BlockSpec((k, tn), lambda i, j: (0, j))],
        out_specs=pl.BlockSpec((tm, tn), lambda i, j: (i, j)),
        out_shape=jax.ShapeDtypeStruct((m, n), F32),
        compiler_params=_cparams(("parallel", "arbitrary")),
        name="matmul",
    )(x, w)


def _rmsnorm_kernel(x_ref, g_ref, o_ref):
    x = x_ref[...]
    o_ref[...] = x * lax.rsqrt(jnp.mean(x * x, axis=-1, keepdims=True) + EPS) * g_ref[...]


def rmsnorm(x, g, tm):
    m, k = x.shape
    return pl.pallas_call(
        _rmsnorm_kernel,
        grid=(m // tm,),
        in_specs=[pl.BlockSpec((tm, k), lambda i: (i, 0)), pl.BlockSpec((1, k), lambda i: (0, 0))],
        out_specs=pl.BlockSpec((tm, k), lambda i: (i, 0)),
        out_shape=jax.ShapeDtypeStruct((m, k), F32),
        compiler_params=_cparams(("parallel",)),
        name="rmsnorm",
    )(x, g.reshape(1, k))


def _cproj_from_pieces(piece_fn, w_ref, y_ref):
    acc = jnp.zeros(y_ref.shape, F32)
    for s in range(CMP_STRIDE):
        acc = acc + _dot(piece_fn(s).astype(BF16), w_ref[s])
    y_ref[...] = acc


def _cproj_rows_kernel(r0_ref, r1_ref, w_ref, y_ref):
    nch = y_ref.shape[0]

    def piece(s):
        return jnp.concatenate([r[pl.ds(s, nch, stride=CMP_STRIDE), :] for r in (r0_ref, r1_ref)], axis=1)

    _cproj_from_pieces(piece, w_ref, y_ref)


def cproj_rows(rows_arr, col_block, wbig, rb):
    t = rows_arr.shape[0]
    rb = min(rb, t)
    half = lambda k: pl.BlockSpec((rb, LANE), lambda i: (i, 2 * col_block + k))
    return pl.pallas_call(
        _cproj_rows_kernel,
        grid=(t // rb,),
        in_specs=[half(0), half(1), pl.BlockSpec(wbig.shape, lambda i: (0, 0, 0))],
        out_specs=pl.BlockSpec((rb // CMP_STRIDE, 2 * KVB_W), lambda i: (i, 0)),
        out_shape=jax.ShapeDtypeStruct((t // CMP_STRIDE, 2 * KVB_W), F32),
        compiler_params=_cparams(("parallel",)),
        name="cproj_rows",
    )(rows_arr, rows_arr, wbig)


def _cproj_pages_kernel(pt_ref, *refs):
    del pt_ref
    pages, w_ref, y_ref = refs[:-2], refs[-2], refs[-1]
    per = PAGE_SIZE // CMP_STRIDE
    npg = len(pages) // 2

    def piece(s):
        halves = [jnp.concatenate([pages[2 * p + k][pl.ds(s, per, stride=CMP_STRIDE), :] for p in range(npg)], axis=0)
                  for k in range(2)]
        return jnp.concatenate(halves, axis=1)

    _cproj_from_pieces(piece, w_ref, y_ref)


def cproj_pages(cache4, layer, page_table, wbig, pg):
    nb, npages = page_table.shape
    per = PAGE_SIZE // CMP_STRIDE

    def page_spec(p, k):
        return pl.BlockSpec((None, None, PAGE_SIZE, LANE),
                            lambda b, j, pt: (layer, pt[b, j * pg + p], 0, k))

    grid_spec = pltpu.PrefetchScalarGridSpec(
        num_scalar_prefetch=1,
        grid=(nb, npages // pg),
        in_specs=[page_spec(p, k) for p in range(pg) for k in range(2)]
        + [pl.BlockSpec(wbig.shape, lambda b, j, pt: (0, 0, 0))],
        out_specs=pl.BlockSpec((None, pg * per, 2 * KVB_W), lambda b, j, pt: (b, j, 0)),
    )
    return pl.pallas_call(
        _cproj_pages_kernel,
        grid_spec=grid_spec,
        out_shape=jax.ShapeDtypeStruct((nb, npages * per, 2 * KVB_W), F32),
        compiler_params=_cparams(("parallel", "arbitrary")),
        name="cproj_pages",
    )(page_table, *([cache4] * (2 * pg)), wbig)


def _cmp_mlp_kernel(y_ref, pe_ref, w2_ref, o_ref):
    y = y_ref[...]
    n = y.shape[0]
    nxt = pltpu.roll(y[:, KVB_W:], n - 1, 0)
    hsum = y[:, :KVB_W] + nxt + pe_ref[...]
    o_ref[...] = _dot(_gelu(hsum).astype(BF16), w2_ref[...]).astype(o_ref.dtype)


def cmp_mlp(y, pe_term, w2big):
    b, ncp, _ = y.shape
    return pl.pallas_call(
        _cmp_mlp_kernel,
        grid=(b,),
        in_specs=[pl.BlockSpec((None, ncp, 2 * KVB_W), lambda i: (i, 0, 0)),
                  pl.BlockSpec((1, KVB_W), lambda i: (0, 0)),
                  pl.BlockSpec((KVB_W, KVB_W), lambda i: (0, 0))],
        out_specs=pl.BlockSpec((None, ncp, KVB_W), lambda i: (i, 0, 0)),
        out_shape=jax.ShapeDtypeStruct((b, ncp, KVB_W), BF16),
        compiler_params=_cparams(("parallel",)),
        name="cmp_mlp",
    )(y, pe_term, w2big)


def compress_weights(pe, w1, w2):
    w1r = w1.reshape(2, 2, CMP_STRIDE, HEAD_DIM, HEAD_DIM)
    eye = jnp.eye(2, dtype=F32)
    wbig = jnp.einsum('crsdh,cC,gG->scgdrCGh', w1r, eye, eye).reshape(CMP_STRIDE, KVB_W, 2 * KVB_W)
    w2big = jnp.einsum('chd,cC,gG->cghCGd', w2, eye, eye).reshape(KVB_W, KVB_W)
    per = pe.reshape(2, 2, CMP_STRIDE, HEAD_DIM)
    pe_rows = jnp.broadcast_to(jnp.transpose(per, (1, 2, 0, 3))[:, :, :, None, :],
                               (2, CMP_STRIDE, 2, KV_HEADS, HEAD_DIM)).reshape(2 * CMP_STRIDE, KVB_W)
    pe_rows = jnp.pad(pe_rows, ((0, 8 * CMP_STRIDE - 2 * CMP_STRIDE), (0, 0)))
    return wbig.astype(BF16), w2big.astype(BF16), pe_rows


def pe_term_from(pe_rows, wbig):
    ype = cproj_rows(pe_rows, 0, wbig, pe_rows.shape[0])
    return ype[0:1, :KVB_W] + ype[1:2, KVB_W:]


def bias_table(rel_bias):
    n = np.arange(LANE)
    exact = N_BUCKETS // 2
    nf = np.maximum(n, 1).astype(np.float32)
    large = exact + (np.log(nf / np.float32(exact)) / np.float32(math.log(MAX_DISTANCE / exact))
                     * np.float32(N_BUCKETS - exact)).astype(np.int32)
    bucket = np.where(n < exact, n, np.minimum(large, N_BUCKETS - 1))
    assert bucket[LANE - 1] == N_BUCKETS - 1
    return jnp.transpose(rel_bias.astype(F32)[bucket], (1, 0))


def _bias_lookup(tab_row, dist_clipped):
    r, w = dist_clipped.shape
    src = jnp.broadcast_to(tab_row, (r, LANE))
    parts = [jnp.take_along_axis(src, dist_clipped[:, j * LANE:(j + 1) * LANE], axis=1)
             for j in range(w // LANE)]
    return parts[0] if len(parts) == 1 else jnp.concatenate(parts, axis=1)


def _topk_mask(x, k):
    r, n = x.shape
    col = lax.broadcasted_iota(jnp.int32, (r, n), 1)
    sel = jnp.zeros((r, n), F32)
    for _ in range(k):
        m = jnp.max(x, axis=-1, keepdims=True)
        idx = jnp.min(jnp.where(x == m, col, n), axis=-1, keepdims=True)
        hit = col == idx
        sel = jnp.where(hit, 1.0, sel)
        x = jnp.where(hit, NEG_INF, x)
    return sel


def _topk_idx(x, k):
    r, n = x.shape
    col = lax.broadcasted_iota(jnp.int32, (r, n), 1)
    lane = lax.broadcasted_iota(jnp.int32, (r, LANE), 1)
    out = jnp.zeros((r, LANE), jnp.int32)
    for it in range(k):
        m = jnp.max(x, axis=-1, keepdims=True)
        idx = jnp.min(jnp.where(x == m, col, n), axis=-1, keepdims=True)
        out = jnp.where(lane == it, idx, out)
        x = jnp.where(col == idx, NEG_INF, x)
    return out


def _cmp_branch(q, t_pos, ckv_ref, tab_ref, cover_ref, nc, ncols=None):
    r = q.shape[0]
    ncols = ckv_ref.shape[0] if ncols is None else ncols
    n_sel = cover_ref.shape[1]
    c_idx = lax.broadcasted_iota(jnp.int32, (1, ncols), 1)
    dist = t_pos - (c_idx * CMP_STRIDE + CMP_BLOCK - 1)
    valid = (dist >= 0) & (c_idx < nc)
    any_valid = (t_pos >= CMP_BLOCK - 1).astype(F32)
    dcl = jnp.clip(dist, 0, LANE - 1)
    blk = lax.broadcasted_iota(jnp.int32, (1, n_sel), 1)
    cur = t_pos // SEL_BLOCK
    forced = (blk == 0) | (blk == cur) | (blk == cur - 1)
    outs, imps = [], []
    for g in range(KV_HEADS):
        kc = ckv_ref[:ncols, g * HEAD_DIM:(g + 1) * HEAD_DIM]
        vc = ckv_ref[:ncols, (KV_HEADS + g) * HEAD_DIM:(KV_HEADS + g + 1) * HEAD_DIM]
        psum = jnp.zeros((r, ncols), F32)
        for h in range(HPG):
            head = g * HPG + h
            qh = (q[:, head * HEAD_DIM:(head + 1) * HEAD_DIM] * ATT_SCALE).astype(BF16)
            s = _dot_nt(qh, kc) + _bias_lookup(tab_ref[head:head + 1, :], dcl)
            s = jnp.where(valid, s, NEG)
            e = jnp.exp(s - jnp.max(s, axis=-1, keepdims=True))
            p = e * (any_valid / jnp.sum(e, axis=-1, keepdims=True))
            outs.append(_dot(p.astype(BF16), vc))
            psum = psum + p
        p_hi = psum.astype(BF16)
        p_lo = (psum - p_hi.astype(F32)).astype(BF16)
        imp = _dot(p_hi, cover_ref[:ncols]) + _dot(p_lo, cover_ref[:ncols])
        imps.append(jnp.where(forced, BIG, jnp.where(blk <= cur, imp, -BIG)))
    return outs, imps


def _topk_mask_cols(x, k):
    n, t = x.shape
    row = lax.broadcasted_iota(jnp.int32, (n, t), 0)
    sel = jnp.zeros((n, t), F32)
    for _ in range(k):
        m = jnp.max(x, axis=0, keepdims=True)
        idx = jnp.min(jnp.where(x == m, row, n), axis=0, keepdims=True)
        hit = row == idx
        sel = jnp.where(hit, 1.0, sel)
        x = jnp.where(hit, NEG_INF, x)
    return sel


def _nsa_cmp_kernel(q_ref, ckv_ref, tab_ref, cover_ref, oc_ref, mask_ref, *, nc, n_sel):
    i = pl.program_id(0)
    t_pos = i * TQ + lax.broadcasted_iota(jnp.int32, (TQ, 1), 0)
    ncp = ckv_ref.shape[0]
    part = max(ncp // 4, LANE)
    last_c = (i * TQ + TQ - CMP_BLOCK) // CMP_STRIDE
    need = jnp.clip(last_c // part, 0, ncp // part - 1)

    for n_parts in range(1, ncp // part + 1):
        @pl.when(need == n_parts - 1)
        def _(n_parts=n_parts):
            outs, imps = _cmp_branch(q_ref[...], t_pos, ckv_ref, tab_ref, cover_ref, nc, n_parts * part)
            for g in range(KV_HEADS):
                mask_ref[g] = _topk_mask_cols(imps[g].T, min(SEL_TOPN, n_sel))
            oc_ref[...] = jnp.concatenate(outs, axis=1)


def cover_matrix(ncp, n_sel):
    ci = np.arange(ncp)[:, None] * CMP_STRIDE
    sj = np.arange(n_sel)[None, :] * SEL_BLOCK
    return jnp.asarray(((ci < sj + SEL_BLOCK) & (ci + CMP_BLOCK > sj)).astype(np.float32), dtype=BF16)


def nsa_cmp_prompt(hproj, ckv, tab, nc):
    t = hproj.shape[0]
    ncp = ckv.shape[0]
    n_sel = t // SEL_BLOCK
    cover = cover_matrix(ncp, n_sel)
    return pl.pallas_call(
        functools.partial(_nsa_cmp_kernel, nc=nc, n_sel=n_sel),
        grid=(t // TQ,),
        in_specs=[pl.BlockSpec((TQ, ATT_W), lambda i: (i, C_Q // ATT_W)),
                  pl.BlockSpec((ncp, KVB_W), lambda i: (0, 0)),
                  pl.BlockSpec((N_HEADS, LANE), lambda i: (0, 0)),
                  pl.BlockSpec((ncp, n_sel), lambda i: (0, 0))],
        out_specs=[pl.BlockSpec((TQ, ATT_W), lambda i: (i, 0)),
                   pl.BlockSpec((KV_HEADS, n_sel, TQ), lambda i: (0, 0, i))],
        out_shape=[jax.ShapeDtypeStruct((t, ATT_W), F32),
                   jax.ShapeDtypeStruct((KV_HEADS, n_sel, t), F32)],
        compiler_params=_cparams(("parallel",)),
        name="nsa_cmp_prompt",
    )(hproj, ckv, tab, cover)


MASKED = -1e9
NEAR = 2 * TQ
W_KEYS = WINDOW + TQ
BLK_PER_TILE = TK // SEL_BLOCK


VT_ROWS = HEAD_DIM + 16
MASK_ROWS = 16
HT = HPG * TQ
FLAG_LANE = HEAD_DIM + BLK_PER_TILE


def _online_step_t(carry, st, shift, vt):
    m, acc = carry
    m_new = jnp.maximum(m, jnp.max(st, axis=0, keepdims=True) + shift)
    p = jnp.exp(st - (m_new - shift)).astype(BF16)
    return m_new, jnp.exp(m - m_new) * acc + _dot(vt, p)


def _nsa_sel_kernel(q_ref, ng_ref, oc_ref, mask_ref, ka_ref, vt_ref, tab_ref, ntab_ref, wtab_ref, o_ref,
                    sta_ref, stb_ref, *, n_sel):
    i = pl.program_id(0)
    qt = (q_ref[...] * ATT_SCALE).T
    oct_ = oc_ref[...].T
    gt = _sigmoid(ng_ref[...]).T
    near_blk0 = 2 * i - 2
    n_far = (jnp.maximum(i - 1, 0) * TQ + TK - 1) // TK
    blk8 = lax.broadcasted_iota(jnp.int32, (BLK_PER_TILE, 1), 0)
    init = (jnp.full((1, HT), NEG, F32), jnp.zeros((VT_ROWS, HT), F32))
    zero_shift = jnp.zeros((1, HT), F32)
    chunks_per_tile = TK // LANE

    before_start = jnp.where(lax.broadcasted_iota(jnp.int32, (MASK_ROWS, HT), 0) == FLAG_LANE - HEAD_DIM,
                             MASKED, 0.0).astype(BF16)
    qtg, w_plain, far_shift = [], [], []
    for g in range(KV_HEADS):
        heads = range(g * HPG, (g + 1) * HPG)
        qg = jnp.concatenate([qt[h * HEAD_DIM:(h + 1) * HEAD_DIM] for h in heads], axis=1).astype(BF16)
        qtg.append(qg)
        w_plain.append(jnp.concatenate([qg, before_start, jnp.zeros((LANE - HEAD_DIM - MASK_ROWS, HT), BF16)], axis=0))
        far_shift.append(jnp.concatenate(
            [jnp.broadcast_to(tab_ref[h:h + 1, LANE - 1:LANE], (1, TQ)) for h in heads], axis=1))

    def vt_tile(c0, n, kind):
        return jnp.concatenate([vt_ref[c0 + c, kind] for c in range(n)], axis=1)

    last_tile = (ka_ref.shape[0] - WINDOW) // TK - 1

    def far_logits(kt, dst_ref):
        kt = jnp.minimum(kt, last_tile)
        k0 = pl.multiple_of(WINDOW + kt * TK, TK)
        r0 = pl.multiple_of(kt * BLK_PER_TILE, BLK_PER_TILE)
        for g in range(KV_HEADS):
            mt = mask_ref[g, pl.ds(r0, BLK_PER_TILE), :]
            keep = (mt > 0.5) & (kt * BLK_PER_TILE + blk8 < near_blk0)
            mneg = jnp.where(keep, 0.0, MASKED)
            m16 = jnp.concatenate([jnp.concatenate([mneg] * HPG, axis=1),
                                   jnp.zeros((MASK_ROWS - BLK_PER_TILE, HT), F32)], axis=0).astype(BF16)
            w = jnp.concatenate([qtg[g], m16, jnp.zeros((LANE - HEAD_DIM - MASK_ROWS, HT), BF16)], axis=0)
            dst_ref[g] = _dot(ka_ref[pl.ds(k0, TK), g * LANE:(g + 1) * LANE], w)

    def far_softmax(kt, src_ref, carry):
        c0 = (WINDOW // LANE) + jnp.minimum(kt, last_tile) * chunks_per_tile
        return tuple(_online_step_t(carry[g], src_ref[g], far_shift[g], vt_tile(c0, chunks_per_tile, g))
                     for g in range(KV_HEADS))

    @pl.when(n_far > 0)
    def _():
        far_logits(0, sta_ref)

    def far_body(k2, carry):
        far_logits(2 * k2 + 1, stb_ref)
        carry = far_softmax(2 * k2, sta_ref, carry)
        far_logits(2 * k2 + 2, sta_ref)
        return far_softmax(2 * k2 + 1, stb_ref, carry)

    carry = lax.fori_loop(0, (n_far + 1) // 2, far_body, (init, init))

    kn0 = pl.multiple_of(WINDOW + (i - 1) * TQ, TQ)
    base = pl.multiple_of(jnp.clip((near_blk0 // BLK_PER_TILE) * BLK_PER_TILE, 0, n_sel - MASK_ROWS), BLK_PER_TILE)
    key_blk = near_blk0 + lax.broadcasted_iota(jnp.int32, (NEAR, MASK_ROWS), 0) // SEL_BLOCK
    expand = (key_blk == base + lax.broadcasted_iota(jnp.int32, (NEAR, MASK_ROWS), 1)).astype(BF16)
    kw0 = pl.multiple_of(i * TQ, TQ)
    rows_t = []
    st_near, st_win = [], []
    for g in range(KV_HEADS):
        mneg = jnp.where(mask_ref[g, pl.ds(base, MASK_ROWS), :] > 0.5, 0.0, MASKED).astype(BF16)
        sel_add = _dot(expand, mneg)
        st = _dot(ka_ref[pl.ds(kn0, NEAR), g * LANE:(g + 1) * LANE], w_plain[g])
        st_near.append(st + ntab_ref[g] + jnp.concatenate([sel_add] * HPG, axis=1))
        st = _dot(ka_ref[pl.ds(kw0, W_KEYS), (KV_HEADS + g) * LANE:(KV_HEADS + g + 1) * LANE], w_plain[g])
        st_win.append(st + wtab_ref[g])
    for g in range(KV_HEADS):
        _, acc = _online_step_t(carry[g], st_near[g], zero_shift, vt_tile(kn0 // LANE, NEAR // LANE, g))
        os_t = acc[:HEAD_DIM] / acc[HEAD_DIM:HEAD_DIM + 1]
        _, acc = _online_step_t(init, st_win[g], zero_shift, vt_tile(kw0 // LANE, W_KEYS // LANE, KV_HEADS + g))
        ow_t = acc[:HEAD_DIM] / acc[HEAD_DIM:HEAD_DIM + 1]

        for h in range(HPG):
            head = g * HPG + h
            ls = slice(h * TQ, (h + 1) * TQ)
            rows_t.append(gt[head:head + 1] * oct_[head * HEAD_DIM:(head + 1) * HEAD_DIM]
                          + gt[N_HEADS + head:N_HEADS + head + 1] * os_t[:, ls]
                          + gt[2 * N_HEADS + head:2 * N_HEADS + head + 1] * ow_t[:, ls])
    o_ref[...] = jnp.concatenate(rows_t, axis=0).T


def _distance_tables_kernel(tab_ref, ntab_ref, wtab_ref):
    head = pl.program_id(0) * HPG + pl.program_id(1)
    row = tab_ref[pl.ds(head, 1), :]
    for out_ref, start, max_dist in ((ntab_ref, TQ, None), (wtab_ref, WINDOW, WINDOW)):
        keys = out_ref.shape[0]
        dist = (start + lax.broadcasted_iota(jnp.int32, (keys, TQ), 1)
                - lax.broadcasted_iota(jnp.int32, (keys, TQ), 0))
        ok = dist >= 0 if max_dist is None else (dist >= 0) & (dist <= max_dist)
        out_ref[...] = jnp.where(ok, _bias_lookup(row, jnp.clip(dist, 0, LANE - 1)), NEG)


def distance_tables(tab):
    return pl.pallas_call(
        _distance_tables_kernel,
        grid=(KV_HEADS, HPG),
        in_specs=[pl.BlockSpec((N_HEADS, LANE), lambda g, h: (0, 0))],
        out_specs=[pl.BlockSpec((None, NEAR, TQ), lambda g, h: (g, 0, h)),
                   pl.BlockSpec((None, W_KEYS, TQ), lambda g, h: (g, 0, h))],
        out_shape=[jax.ShapeDtypeStruct((KV_HEADS, NEAR, HT), F32),
                   jax.ShapeDtypeStruct((KV_HEADS, W_KEYS, HT), F32)],
        compiler_params=_cparams(("parallel", "parallel")),
        name="distance_tables",
    )(tab)


def attention_kv_layout(hproj):
    t = hproj.shape[0]
    grp = lambda c0, g: hproj[:, c0 + g * HEAD_DIM:c0 + (g + 1) * HEAD_DIM]
    e8 = jnp.asarray(np.eye(BLK_PER_TILE, dtype=np.float32)[(np.arange(t) % TK) // SEL_BLOCK])
    z = lambda w: jnp.zeros((t, w), F32)
    ka = [jnp.concatenate([grp(C_KVS, g), e8, z(LANE - HEAD_DIM - BLK_PER_TILE)], axis=1) for g in range(KV_HEADS)]
    ka += [jnp.concatenate([grp(C_KVW, g), z(LANE - HEAD_DIM)], axis=1) for g in range(KV_HEADS)]
    lead = jnp.zeros((WINDOW, LANE), F32).at[:, FLAG_LANE].set(1.0)
    ka = jnp.concatenate([jnp.tile(lead, (1, 2 * KV_HEADS)), jnp.concatenate(ka, axis=1)], axis=0).astype(BF16)
    ones = jnp.ones((t, 1), F32)
    vts = []
    for c0 in (C_KVS, C_KVW):
        for g in range(KV_HEADS):
            v = jnp.concatenate([grp(c0 + KV_HEADS * HEAD_DIM, g), ones, z(VT_ROWS - HEAD_DIM - 1)], axis=1)
            v = jnp.pad(v.astype(BF16), ((WINDOW, 0), (0, 0))).reshape(-1, LANE, VT_ROWS)
            vts.append(jnp.swapaxes(v, 1, 2))
    return ka, jnp.stack(vts, axis=1)


def nsa_sel_prompt(hproj, o_c, mask, ka, vt, tab, ntab, wtab):
    t = hproj.shape[0]
    n_sel = mask.shape[1]
    assert t % TK == 0 and n_sel % BLK_PER_TILE == 0 and n_sel >= MASK_ROWS
    const = lambda a: pl.BlockSpec(a.shape, lambda i: (0,) * a.ndim, pipeline_mode=pl.Buffered(1))
    return pl.pallas_call(
        functools.partial(_nsa_sel_kernel, n_sel=n_sel),
        grid=(t // TQ,),
        in_specs=[pl.BlockSpec((TQ, ATT_W), lambda i: (i, C_Q // ATT_W)),
                  pl.BlockSpec((TQ, LANE), lambda i: (i, C_NG // LANE)),
                  pl.BlockSpec((TQ, ATT_W), lambda i: (i, 0)),
                  pl.BlockSpec((KV_HEADS, n_sel, TQ), lambda i: (0, 0, i)),
                  const(ka), const(vt), const(tab), const(ntab), const(wtab)],
        out_specs=pl.BlockSpec((TQ, ATT_W), lambda i: (i, 0)),
        out_shape=jax.ShapeDtypeStruct((t, ATT_W), F32),
        scratch_shapes=[pltpu.VMEM((KV_HEADS, TK, HT), F32)] * 2,
        compiler_params=_cparams(("parallel",)),
        name="nsa_sel_prompt",
    )(hproj, hproj, o_c, mask, ka, vt, tab, ntab, wtab)


def _hgrn_gates(hq, hf, lb):
    sig = _sigmoid(hf)
    forget = lb + (1.0 - lb) * sig
    logf = jnp.log(jnp.maximum(forget, TINY))
    k = (1.0 - lb) * (1.0 - sig)
    q = hq * _sigmoid(hq)
    return q, k, logf


def _hgrn_intra(q, k, v, a, sub):
    n = q.shape[0]
    row = lax.broadcasted_iota(jnp.int32, (n, 1), 0) % sub
    outs = [jnp.zeros((n, HG_DV), F32) for _ in range(HG_HEADS)]
    for d in range(sub):
        ks = k if d == 0 else pltpu.roll(k, d, 0)
        a_s = a if d == 0 else pltpu.roll(a, d, 0)
        vs = v if d == 0 else pltpu.roll(v, d, 0)
        msk = row >= d
        w = q * ks * jnp.exp(jnp.where(msk, a - a_s, 0.0)) * msk.astype(F32)
        for h in range(HG_HEADS):
            r = jnp.sum(w[:, h * HG_DK:(h + 1) * HG_DK], axis=-1, keepdims=True)
            outs[h] = outs[h] + r * vs[:, h * HG_DV:(h + 1) * HG_DV]
    return outs


def _hgrn_prompt_kernel(hq_ref, hf_ref, hi_ref, lb_ref, tri_ref, o_ref, st_out_ref, st_ref):
    step = pl.program_id(0)

    @pl.when(step == 0)
    def _():
        st_ref[...] = jnp.zeros(st_ref.shape, F32)

    n = hq_ref.shape[0]
    q, k, logf = _hgrn_gates(hq_ref[...], hf_ref[...], lb_ref[...])
    v = hi_ref[...]
    a = jnp.dot(tri_ref[...], logf, preferred_element_type=F32, precision=lax.Precision.HIGHEST)
    outs = _hgrn_intra(q, k, v, a, SUB)
    rows_out = [[] for _ in range(HG_HEADS)]
    for c in range(n // SUB):
        sl = slice(c * SUB, (c + 1) * SUB)
        a_c = a[sl]
        a_last = a_c[SUB - 1:SUB]
        qe = (q[sl] * jnp.exp(a_c)).astype(BF16)
        kd = (k[sl] * jnp.exp(a_last - a_c)).astype(BF16)
        dec = jnp.exp(a_last)
        for h in range(HG_HEADS):
            hs = slice(h * HG_DK, (h + 1) * HG_DK)
            st = st_ref[h]
            rows_out[h].append(outs[h][sl] + _dot_nt(qe[:, hs], st.astype(BF16)))
            st_ref[h] = dec[:, hs] * st + _dot_tn(v[sl, hs].astype(BF16), kd[:, hs])
    o_ref[...] = jnp.concatenate([jnp.concatenate(r, axis=0) for r in rows_out], axis=1)
    st_out_ref[...] = st_ref[...]


def hgrn_prompt(hproj, lb, tc):
    t = hproj.shape[0]
    tc = min(tc, t)
    r = np.arange(tc)
    tri = jnp.asarray(((r[:, None] >= r[None, :]) & (r[:, None] // SUB == r[None, :] // SUB)).astype(np.float32))
    return pl.pallas_call(
        _hgrn_prompt_kernel,
        grid=(t // tc,),
        in_specs=[pl.BlockSpec((tc, HGK_W), lambda i: (i, C_HQ // HGK_W)),
                  pl.BlockSpec((tc, HGK_W), lambda i: (i, C_HF // HGK_W)),
                  pl.BlockSpec((tc, HGK_W), lambda i: (i, C_HI // HGK_W)),
                  pl.BlockSpec((1, HGK_W), lambda i: (0, 0)),
                  pl.BlockSpec((tc, tc), lambda i: (0, 0))],
        out_specs=[pl.BlockSpec((tc, HGK_W), lambda i: (i, 0)),
                   pl.BlockSpec((HG_HEADS, HG_DV, HG_DK), lambda i: (0, 0, 0))],
        out_shape=[jax.ShapeDtypeStruct((t, HGK_W), F32),
                   jax.ShapeDtypeStruct((HG_HEADS, HG_DV, HG_DK), F32)],
        scratch_shapes=[pltpu.VMEM((HG_HEADS, HG_DV, HG_DK), F32)],
        compiler_params=_cparams(("arbitrary",)),
        name="hgrn_prompt",
    )(hproj, hproj, hproj, lb.reshape(1, HGK_W), tri)


def _merge_kernel(x_ref, att_ref, o_ref, hg_ref, ga_ref, gh_ref, ng_ref, fg_ref, wa_ref, wh_ref, wo_ref,
                  xo_ref, xn_ref):
    o = o_ref[...]
    parts = []
    for h in range(HG_HEADS):
        oh = o[:, h * HG_DV:(h + 1) * HG_DV]
        parts.append(oh * lax.rsqrt(jnp.mean(oh * oh, axis=-1, keepdims=True) + EPS))
    hg = hg_ref[...]
    hgo = jnp.concatenate(parts, axis=1) * ng_ref[...] * (hg * _sigmoid(hg))
    m = (_sigmoid(ga_ref[...]) * _dot(att_ref[...].astype(BF16), wa_ref[...])
         + _sigmoid(gh_ref[...]) * _dot(hgo.astype(BF16), wh_ref[...]))
    x = x_ref[...] + _dot(m.astype(BF16), wo_ref[...])
    xo_ref[...] = x
    y = x * lax.rsqrt(jnp.mean(x * x, axis=-1, keepdims=True) + EPS)
    xn_ref[...] = (y * fg_ref[...]).astype(BF16)


def merge(x, att, o, hproj, hg_norm_g, ffn_g, wa, wh, wo, tm):
    t = x.shape[0]
    tm = min(tm, t)
    row = lambda w, cb: pl.BlockSpec((tm, w), lambda i: (i, cb))
    full = lambda a: pl.BlockSpec(a.shape, lambda i: (0, 0))
    hg_norm_g = hg_norm_g.reshape(1, HGK_W)
    ffn_g = ffn_g.reshape(1, D_MODEL)
    return pl.pallas_call(
        _merge_kernel,
        grid=(t // tm,),
        in_specs=[row(D_MODEL, 0), row(ATT_W, 0), row(HGK_W, 0), row(HGK_W, C_HG // HGK_W),
                  row(D_MODEL, C_GA // D_MODEL), row(D_MODEL, C_GH // D_MODEL),
                  full(hg_norm_g), full(ffn_g), full(wa), full(wh), full(wo)],
        out_specs=[row(D_MODEL, 0), row(D_MODEL, 0)],
        out_shape=[jax.ShapeDtypeStruct((t, D_MODEL), F32), jax.ShapeDtypeStruct((t, D_MODEL), BF16)],
        compiler_params=_cparams(("parallel",)),
        name="merge",
    )(x, att, o, hproj, hproj, hproj, hg_norm_g, ffn_g, wa, wh, wo)


def _topk_cols(x, k, payload=None):
    n, t = x.shape
    row = lax.broadcasted_iota(jnp.int32, (n, t), 0)
    vals, ids, pay = [], [], []
    for _ in range(k):
        m = jnp.max(x, axis=0, keepdims=True)
        idx = jnp.min(jnp.where(x == m, row, n), axis=0, keepdims=True)
        hit = row == idx
        vals.append(m)
        ids.append(idx)
        if payload is not None:
            pay.append(jnp.max(jnp.where(hit, payload, -1), axis=0, keepdims=True))
        x = jnp.where(hit, NEG_INF, x)
    cat = lambda xs: jnp.concatenate(xs, axis=0)
    return cat(vals), cat(ids), (cat(pay) if payload is not None else None)


def _pair_candidates():
    pairs = [(a, b) for a in range(PEER_TOPK) for b in range(PEER_TOPK) if (a + 1) * (b + 1) <= PEER_TOPK]
    rows = -(-len(pairs) // 8) * 8
    sel = np.zeros((2, rows, PEER_TOPK), np.float32)
    for r, (a, b) in enumerate(pairs):
        sel[0, r, a] = 1.0
        sel[1, r, b] = 1.0
    return len(pairs), jnp.asarray(sel)


def _pick_rows(sel, x):
    return jnp.dot(sel, x, preferred_element_type=F32, precision=lax.Precision.HIGHEST)


def _peer_topk_kernel(q_ref, keys_ref, pair_ref, i1_ref, i2_ref, g_ref, *, n_pairs):
    o1, o2, og = [], [], []
    live = lax.broadcasted_iota(jnp.int32, (pair_ref.shape[1], q_ref.shape[0]), 0) < n_pairs
    for h in range(PEER_HEADS):
        top = []
        for side in range(2):
            c0 = (h * 2 + side) * PEER_DH
            s = _dot_nt(keys_ref[side], q_ref[:, c0:c0 + PEER_DH].astype(BF16))
            top.append(_topk_cols(s, PEER_TOPK)[:2])
        (v1, i1), (v2, i2) = top
        cand = jnp.where(live, _pick_rows(pair_ref[0], v1) + _pick_rows(pair_ref[1], v2), NEG_INF)
        cidx = (_pick_rows(pair_ref[0], i1.astype(F32)) * PEER_NKEYS
                + _pick_rows(pair_ref[1], i2.astype(F32))).astype(jnp.int32)
        sv, _, e = _topk_cols(cand, PEER_TOPK, cidx)
        ex = jnp.exp(sv - sv[0:1])
        o1.append(e // PEER_NKEYS)
        o2.append(e % PEER_NKEYS)
        og.append(ex / jnp.sum(ex, axis=0, keepdims=True))
    i1_ref[...] = jnp.concatenate(o1, axis=0).astype(F32).T.astype(jnp.int32)
    i2_ref[...] = jnp.concatenate(o2, axis=0).astype(F32).T.astype(jnp.int32)
    g_ref[...] = jnp.concatenate(og, axis=0).T


def peer_topk(q, keys_bf):
    t = q.shape[0]
    tb = LANE
    nsel = PEER_HEADS * PEER_TOPK
    n_pairs, pair_sel = _pair_candidates()
    out = lambda dt: jax.ShapeDtypeStruct((t, nsel), dt)
    return pl.pallas_call(
        functools.partial(_peer_topk_kernel, n_pairs=n_pairs),
        grid=(t // tb,),
        in_specs=[pl.BlockSpec((tb, q.shape[1]), lambda i: (i, 0)),
                  pl.BlockSpec(keys_bf.shape, lambda i: (0, 0, 0)),
                  pl.BlockSpec(pair_sel.shape, lambda i: (0, 0, 0))],
        out_specs=[pl.BlockSpec((tb, nsel), lambda i: (i, 0))] * 3,
        out_shape=[out(jnp.int32), out(jnp.int32), out(F32)],
        compiler_params=_cparams(("parallel",)),
        name="peer_topk",
    )(q, keys_bf, pair_sel)


GATE_UNROLL = 8


def _peer_dense_kernel(xn_ref, x_ref, i1_ref, i2_ref, g_ref, u_ref, v_ref, o_ref, acc_ref, gate_ref, *, n_a):
    j = pl.program_id(1)
    tm, nsel = i1_ref.shape

    @pl.when(j == 0)
    def _():
        acc_ref[...] = jnp.zeros(acc_ref.shape, F32)
        row = lax.broadcasted_iota(jnp.int32, (PEER_NKEYS, nsel), 0)

        def body(tt, _):
            for k in range(GATE_UNROLL):
                t = tt * GATE_UNROLL + k
                pa = jnp.where(row == i1_ref[pl.ds(t, 1), :], 1.0, 0.0).astype(BF16)
                gb = jnp.where(row == i2_ref[pl.ds(t, 1), :], g_ref[pl.ds(t, 1), :], 0.0).astype(BF16)
                gate_ref[pl.ds(pl.multiple_of(t * PEER_NKEYS, PEER_NKEYS), PEER_NKEYS), :] = _dot_nt(pa, gb)
            return 0

        lax.fori_loop(0, tm // GATE_UNROLL, body, 0)

    s = _dot_nt(xn_ref[...], u_ref[...])
    gate = jnp.concatenate([gate_ref[pl.ds(j * n_a + k, tm, stride=PEER_NKEYS), :] for k in range(n_a)], axis=1)
    acc_ref[...] += _dot((gate * _gelu(s)).astype(BF16), v_ref[...])

    @pl.when(j == pl.num_programs(1) - 1)
    def _():
        o_ref[...] = x_ref[...] + acc_ref[...]


def peer_dense(xn, x, i1, i2, g, u_bf, v_bf, tm, n_a):
    t = x.shape[0]
    tm = min(tm, t)
    te = n_a * PEER_NKEYS
    nsel = i1.shape[1]
    tok = lambda w: pl.BlockSpec((tm, w), lambda i, j: (i, 0))
    return pl.pallas_call(
        functools.partial(_peer_dense_kernel, n_a=n_a),
        grid=(t // tm, PEER_NKEYS // n_a),
        in_specs=[tok(D_MODEL), tok(D_MODEL), tok(nsel), tok(nsel), tok(nsel),
                  pl.BlockSpec((te, D_MODEL), lambda i, j: (j, 0)),
                  pl.BlockSpec((te, D_MODEL), lambda i, j: (j, 0))],
        out_specs=tok(D_MODEL),
        out_shape=jax.ShapeDtypeStruct((t, D_MODEL), F32),
        scratch_shapes=[pltpu.VMEM((tm, D_MODEL), F32), pltpu.VMEM((tm * PEER_NKEYS, PEER_NKEYS), F32)],
        compiler_params=_cparams(("parallel", "arbitrary")),
        name="peer_dense",
    )(xn, x, i1, i2, g, u_bf, v_bf)


def peer(xn_bf, x, wq_bf, keys_bf, u_bf, v_bf, tm):
    t = x.shape[0]
    q = matmul(xn_bf, wq_bf, min(512, t), 1024)
    i1, i2, g = peer_topk(q, keys_bf)
    return peer_dense(xn_bf, x, i1, i2, g, u_bf, v_bf, tm, 16)


ROWS = 8


def _pad_rows(x, rows=ROWS):
    return jnp.concatenate([x, jnp.zeros((rows - x.shape[0], x.shape[1]), x.dtype)], axis=0)


def _nsa_cmp_sample_kernel(q_ref, ckv_ref, tab_ref, cover_ref, oc_ref, sel_ref, *, nc, past, n_sel):
    nt = q_ref.shape[0]
    t_pos = past + lax.broadcasted_iota(jnp.int32, (ROWS, 1), 0)
    outs, imps = _cmp_branch(_pad_rows(q_ref[...]), t_pos, ckv_ref, tab_ref, cover_ref, nc)
    for g in range(KV_HEADS):
        sel_ref[g] = _topk_idx(imps[g], min(SEL_TOPN, n_sel))
    oc_ref[...] = jnp.concatenate(outs, axis=1)[:nt]


def nsa_cmp_sample(hproj3, ckv, tab, nc, past):
    nb, nt, _ = hproj3.shape
    ncp = ckv.shape[1]
    n_sel = -(-(past + nt) // SEL_BLOCK)
    n_sel_pad = -(-n_sel // LANE) * LANE
    cover = cover_matrix(ncp, n_sel_pad)
    return pl.pallas_call(
        functools.partial(_nsa_cmp_sample_kernel, nc=nc, past=past, n_sel=n_sel),
        grid=(nb,),
        in_specs=[pl.BlockSpec((None, nt, ATT_W), lambda b: (b, 0, C_Q // ATT_W)),
                  pl.BlockSpec((None, ncp, KVB_W), lambda b: (b, 0, 0)),
                  pl.BlockSpec((N_HEADS, LANE), lambda b: (0, 0)),
                  pl.BlockSpec((ncp, n_sel_pad), lambda b: (0, 0))],
        out_specs=[pl.BlockSpec((None, nt, ATT_W), lambda b: (b, 0, 0)),
                   pl.BlockSpec((None, KV_HEADS, ROWS, LANE), lambda b: (b, 0, 0, 0))],
        out_shape=[jax.ShapeDtypeStruct((nb, nt, ATT_W), F32),
                   jax.ShapeDtypeStruct((nb, KV_HEADS, ROWS, LANE), jnp.int32)],
        compiler_params=_cparams(("parallel",)),
        name="nsa_cmp_sample",
    )(hproj3, ckv, tab, cover)


def _nsa_sel_sample_kernel(sel_ref, pt_ref, q_ref, ng_ref, oc_ref, kvs_ref, kvw_ref, win_ref, tab_ref, cache_ref,
                           o_ref, buf_ref, sem, *, layer, past, w_len):
    b = pl.program_id(0)
    nt = q_ref.shape[0]
    nslot = SEL_TOPN
    per_page = PAGE_SIZE // SEL_BLOCK

    def slot_copy(t, g, slot):
        blk = sel_ref[((b * nt + t) * KV_HEADS + g) * nslot + slot]
        in_past = blk * SEL_BLOCK < past
        blk_c = jnp.minimum(blk, past // SEL_BLOCK - 1)
        page = pt_ref[b, blk_c // per_page]
        src = cache_ref.at[layer, page, pl.ds((blk_c % per_page) * SEL_BLOCK, SEL_BLOCK), pl.ds(KVB_W, KVB_W)]
        dst = buf_ref.at[t * KV_HEADS + g, pl.ds(slot * SEL_BLOCK, SEL_BLOCK), :]
        return blk, in_past, pltpu.make_async_copy(src, dst, sem)

    for t in range(nt):
        for g in range(KV_HEADS):
            for slot in range(nslot):
                _, in_past, cp = slot_copy(t, g, slot)

                @pl.when(in_past)
                def _():
                    cp.start()

                @pl.when(jnp.logical_not(in_past))
                def _():
                    buf_ref[t * KV_HEADS + g, pl.ds(slot * SEL_BLOCK, SEL_BLOCK), :] = jnp.zeros(
                        (SEL_BLOCK, KVB_W), F32)
    for t in range(nt):
        for g in range(KV_HEADS):
            for slot in range(nslot):
                _, in_past, cp = slot_copy(t, g, slot)

                @pl.when(in_past)
                def _():
                    cp.wait()

    gates = _sigmoid(ng_ref[...])
    nk = nslot * SEL_BLOCK
    lane_slot = lax.broadcasted_iota(jnp.int32, (1, nk), 1) // SEL_BLOCK
    lane_off = lax.broadcasted_iota(jnp.int32, (1, nk), 1) % SEL_BLOCK
    j_new = lax.broadcasted_iota(jnp.int32, (1, LANE), 1)
    kvn = jnp.concatenate([kvs_ref[...], kvw_ref[...]], axis=1)
    kvn_pad = _pad_rows(kvn, LANE).astype(BF16)
    win = win_ref[...].astype(BF16)
    w_idx = lax.broadcasted_iota(jnp.int32, (1, w_len + LANE), 1)
    rows_out = []
    for t in range(nt):
        t_pos = past + t
        heads_out = []
        for g in range(KV_HEADS):
            q4 = jnp.concatenate(
                [q_ref[t:t + 1, (g * HPG + h) * HEAD_DIM:(g * HPG + h + 1) * HEAD_DIM] for h in range(HPG)],
                axis=0)
            q4 = (q4 * ATT_SCALE).astype(BF16)

            def bias_of(dist):
                d = jnp.broadcast_to(jnp.clip(dist, 0, LANE - 1), (ROWS, dist.shape[1]))
                return jnp.concatenate(
                    [_bias_lookup(tab_ref[g * HPG + h:g * HPG + h + 1, :], d)[0:1] for h in range(HPG)], axis=0)

            pos = jnp.zeros((1, nk), jnp.int32)
            has_new = jnp.zeros((1, 1), jnp.int32)
            for slot in range(nslot):
                blk = sel_ref[((b * nt + t) * KV_HEADS + g) * nslot + slot]
                pos = jnp.where(lane_slot == slot, blk * SEL_BLOCK, pos)
                has_new = jnp.maximum(has_new, (blk * SEL_BLOCK >= past).astype(jnp.int32))
            pos = pos + lane_off
            kv_g = buf_ref[t * KV_HEADS + g].astype(BF16)
            k_all = jnp.concatenate([kv_g[:, g * HEAD_DIM:(g + 1) * HEAD_DIM],
                                     kvn_pad[:, g * HEAD_DIM:(g + 1) * HEAD_DIM]], axis=0)
            v_all = jnp.concatenate([kv_g[:, (KV_HEADS + g) * HEAD_DIM:(KV_HEADS + g + 1) * HEAD_DIM],
                                     kvn_pad[:, (KV_HEADS + g) * HEAD_DIM:(KV_HEADS + g + 1) * HEAD_DIM]], axis=0)
            dist = jnp.concatenate([t_pos - pos, t - j_new], axis=1)
            ok = jnp.concatenate([pos < past, (j_new <= t) & (has_new > 0)], axis=1)
            s = jnp.where(ok, _dot_nt(q4, k_all) + bias_of(dist), NEG)
            e = jnp.where(ok, jnp.exp(s - jnp.max(s, axis=-1, keepdims=True)), 0.0)
            o_s = _dot((e / jnp.sum(e, axis=-1, keepdims=True)).astype(BF16), v_all)

            kw = jnp.concatenate([win[:, g * HEAD_DIM:(g + 1) * HEAD_DIM],
                                  kvn_pad[:, (2 * KV_HEADS + g) * HEAD_DIM:(2 * KV_HEADS + g + 1) * HEAD_DIM]], axis=0)
            vw = jnp.concatenate([win[:, (KV_HEADS + g) * HEAD_DIM:(KV_HEADS + g + 1) * HEAD_DIM],
                                  kvn_pad[:, (3 * KV_HEADS + g) * HEAD_DIM:(3 * KV_HEADS + g + 1) * HEAD_DIM]], axis=0)
            dist_w = t_pos - (past - w_len + w_idx)
            ok_w = (dist_w >= 0) & (dist_w <= WINDOW) & (w_idx < w_len + nt)
            s = jnp.where(ok_w, _dot_nt(q4, kw) + bias_of(dist_w), NEG)
            e = jnp.where(ok_w, jnp.exp(s - jnp.max(s, axis=-1, keepdims=True)), 0.0)
            o_w = _dot((e / jnp.sum(e, axis=-1, keepdims=True)).astype(BF16), vw)

            for h in range(HPG):
                head = g * HPG + h
                o_c = oc_ref[t:t + 1, head * HEAD_DIM:(head + 1) * HEAD_DIM]
                heads_out.append(gates[t:t + 1, head:head + 1] * o_c
                                 + gates[t:t + 1, N_HEADS + head:N_HEADS + head + 1] * o_s[h:h + 1]
                                 + gates[t:t + 1, 2 * N_HEADS + head:2 * N_HEADS + head + 1] * o_w[h:h + 1])
        rows_out.append(jnp.concatenate(heads_out, axis=1))
    o_ref[...] = jnp.concatenate(rows_out, axis=0)


def nsa_sel_sample(hproj3, o_c, sel_flat, page_table, cache4, win, tab, layer, past):
    nb, nt, _ = hproj3.shape
    w_len = win.shape[1]
    assert past % SEL_BLOCK == 0 and PAGE_SIZE % SEL_BLOCK == 0 and w_len % LANE == 0 and past >= w_len
    grid_spec = pltpu.PrefetchScalarGridSpec(
        num_scalar_prefetch=2,
        grid=(nb,),
        in_specs=[pl.BlockSpec((None, nt, ATT_W), lambda b, s, p: (b, 0, C_Q // ATT_W)),
                  pl.BlockSpec((None, nt, LANE), lambda b, s, p: (b, 0, C_NG // LANE)),
                  pl.BlockSpec((None, nt, ATT_W), lambda b, s, p: (b, 0, 0)),
                  pl.BlockSpec((None, nt, KVB_W), lambda b, s, p: (b, 0, C_KVS // KVB_W)),
                  pl.BlockSpec((None, nt, KVB_W), lambda b, s, p: (b, 0, C_KVW // KVB_W)),
                  pl.BlockSpec((None, w_len, KVB_W), lambda b, s, p: (b, 0, 0)),
                  pl.BlockSpec((N_HEADS, LANE), lambda b, s, p: (0, 0)),
                  pl.BlockSpec(memory_space=pl.ANY)],
        out_specs=pl.BlockSpec((None, nt, ATT_W), lambda b, s, p: (b, 0, 0)),
        scratch_shapes=[pltpu.VMEM((nt * KV_HEADS, SEL_TOPN * SEL_BLOCK, KVB_W), F32),
                        pltpu.SemaphoreType.DMA(())],
    )
    return pl.pallas_call(
        functools.partial(_nsa_sel_sample_kernel, layer=layer, past=past, w_len=w_len),
        grid_spec=grid_spec,
        out_shape=jax.ShapeDtypeStruct((nb, nt, ATT_W), F32),
        compiler_params=_cparams(("arbitrary",)),
        name="nsa_sel_sample",
    )(sel_flat, page_table, hproj3, hproj3, o_c, hproj3, hproj3, win, tab, cache4)


def _hgrn_sample_kernel(hq_ref, hf_ref, hi_ref, lb_ref, s_ref, o_ref, so_ref):
    nt = hq_ref.shape[0]
    live = (lax.broadcasted_iota(jnp.int32, (ROWS, 1), 0) < nt).astype(F32)
    q, k, logf = _hgrn_gates(_pad_rows(hq_ref[...]), _pad_rows(hf_ref[...]), lb_ref[...])
    q, k, logf = q * live, k * live, logf * live
    v = _pad_rows(hi_ref[...])
    rows = [logf[0:1]]
    for i in range(1, ROWS):
        rows.append(rows[-1] + logf[i:i + 1])
    a = jnp.concatenate(rows, axis=0)
    outs = _hgrn_intra(q, k, v, a, ROWS)
    a_last = a[ROWS - 1:ROWS]
    qe = (q * jnp.exp(a)).astype(BF16)
    kd = (k * jnp.exp(a_last - a)).astype(BF16)
    dec = jnp.exp(a_last)
    o_parts = []
    for h in range(HG_HEADS):
        hs = slice(h * HG_DK, (h + 1) * HG_DK)
        st = s_ref[h].T
        o_parts.append(outs[h] + _dot_nt(qe[:, hs], st.astype(BF16)))
        so_ref[h] = (dec[:, hs] * st + _dot_tn(v[:, hs].astype(BF16), kd[:, hs])).T
    o_ref[...] = jnp.concatenate(o_parts, axis=1)[:nt]


def hgrn_sample(hproj3, lb, state):
    nb, nt, _ = hproj3.shape
    col = lambda c: pl.BlockSpec((None, nt, HGK_W), lambda b: (b, 0, c // HGK_W))
    st_spec = pl.BlockSpec((None, HG_HEADS, HG_DK, HG_DV), lambda b: (b, 0, 0, 0))
    return pl.pallas_call(
        _hgrn_sample_kernel,
        grid=(nb,),
        in_specs=[col(C_HQ), col(C_HF), col(C_HI), pl.BlockSpec((1, HGK_W), lambda b: (0, 0)), st_spec],
        out_specs=[pl.BlockSpec((None, nt, HGK_W), lambda b: (b, 0, 0)), st_spec],
        out_shape=[jax.ShapeDtypeStruct((nb, nt, HGK_W), F32),
                   jax.ShapeDtypeStruct(state.shape, F32)],
        compiler_params=_cparams(("parallel",)),
        name="hgrn_sample",
    )(hproj3, hproj3, hproj3, lb.reshape(1, HGK_W), state)


def reorder_w_in(w):
    o = np.cumsum([0, ATT_W, KVB_W, KVB_W, KVB_W, N_GATE, HGK_W, HGK_W, HGK_W, HGK_W, D_MODEL, D_MODEL])
    q, kvc, kvs, kvw, ng, hq, hf, hi, hg, ga, gh = [w[:, o[i]:o[i + 1]] for i in range(11)]
    pad = jnp.zeros((w.shape[0], N_PROJ - C_NG - N_GATE), w.dtype)
    return jnp.concatenate([q, hq, ga, gh, hf, hi, hg, kvc, kvs, kvw, ng, pad], axis=1).astype(BF16)


def prompt_layer(x, lp, tabs):
    t = x.shape[0]
    tab, ntab, wtab = tabs
    hproj = norm_matmul(x, lp["attn_g"], lp["w_in"], min(512, t), 1408)
    y = cproj_rows(hproj, C_KVC // KVB_W, lp["wbig"], 2048)
    ckv = cmp_mlp(y[None], lp["pe_term"], lp["w2big"])[0]
    o_c, mask = nsa_cmp_prompt(hproj, ckv, tab, t // CMP_STRIDE - 1)
    ka, vt = attention_kv_layout(hproj)
    att = nsa_sel_prompt(hproj, o_c, mask, ka, vt, tab, ntab, wtab)
    o_hg, st = hgrn_prompt(hproj, lp["lb"], 128)
    x, xn = merge(x, att, o_hg, hproj, lp["hg_norm_g"], lp["ffn_g"], lp["wa"], lp["wh"], lp["wo"], 256)
    x = peer(xn, x, lp["peer_wq"], lp["peer_keys"], lp["peer_u"], lp["peer_v"], 256)
    return x, hproj[:, C_KVC:C_KVC + 2 * KVB_W], hproj[:, C_KVW:C_KVW + KVB_W], jnp.swapaxes(st, 1, 2)


def layer_params(l, norm_attn_g, norm_ffn_g, w_in, cmp_pe, cmp_w1, cmp_w2, lower_bounds, hg_norm_g,
                 w_br_attn, w_br_hg, w_out, peer_wq, peer_keys, peer_u, peer_v):
    wbig, w2big, pe_rows = compress_weights(cmp_pe[l], cmp_w1[l], cmp_w2[l])
    return dict(attn_g=norm_attn_g[l], ffn_g=norm_ffn_g[l], w_in=reorder_w_in(w_in[l]),
                wbig=wbig, w2big=w2big, pe_term=pe_term_from(pe_rows, wbig),
                lb=lower_bounds[l], hg_norm_g=hg_norm_g[l],
                wa=w_br_attn[l].astype(BF16), wh=w_br_hg[l].astype(BF16), wo=w_out[l].astype(BF16),
                peer_wq=peer_wq[l].astype(BF16), peer_keys=peer_keys[l].astype(BF16),
                peer_u=peer_u[l].astype(BF16), peer_v=peer_v[l].astype(BF16))


def sample_layer(x, lp, tab, layer, cache4, win, state, page_table, nb, nt):
    past = page_table.shape[1] * PAGE_SIZE
    assert nt < CMP_STRIDE, "new tokens never complete a compression chunk"
    hproj = norm_matmul(x, lp["attn_g"], lp["w_in"], x.shape[0], 1408)
    hproj3 = hproj.reshape(nb, nt, N_PROJ)
    y = cproj_pages(cache4, layer, page_table, lp["wbig"], 16)
    ckv = cmp_mlp(y, lp["pe_term"], lp["w2big"])
    o_c, sel = nsa_cmp_sample(hproj3, ckv, tab, past // CMP_STRIDE - 1, past)
    sel_flat = jnp.transpose(sel[:, :, :nt, :SEL_TOPN], (0, 2, 1, 3)).reshape(-1)
    att = nsa_sel_sample(hproj3, o_c, sel_flat, page_table, cache4, win, tab, layer, past)
    o_hg, st = hgrn_sample(hproj3, lp["lb"], state)
    x, xn = merge(x, att.reshape(nb * nt, ATT_W), o_hg.reshape(nb * nt, HGK_W), hproj, lp["hg_norm_g"],
                  lp["ffn_g"], lp["wa"], lp["wh"], lp["wo"], 256)
    x = peer(xn, x, lp["peer_wq"], lp["peer_keys"], lp["peer_u"], lp["peer_v"], 256)
    return x, hproj[:, C_KVC:C_KVC + 2 * KVB_W], hproj[:, C_KVW:C_KVW + KVB_W], st


def kernel(x_prompt, x_sample, cache_kv, cache_win, state_hgrn, page_table, norm_attn_g, norm_ffn_g, final_norm_g, w_in, cmp_pe, cmp_w1, cmp_w2, rel_bias, hg_lb_logits, hg_norm_g, w_br_attn, w_br_hg, w_out, peer_wq, peer_keys, peer_u, peer_v):
    depth = w_in.shape[0]
    bp, seq, _ = x_prompt.shape
    nb, nt, _ = x_sample.shape
    assert bp == 1
    p_lb = jax.nn.softmax(hg_lb_logits.astype(F32), axis=0)
    lower_bounds = jnp.cumsum(p_lb, axis=0) - p_lb[0]
    tab = bias_table(rel_bias)
    tabs = (tab, *distance_tables(tab))
    n_pool = cache_kv.shape[1]
    cache4 = cache_kv.reshape(depth, n_pool, PAGE_SIZE, 2 * KVB_W)
    w_buf = cache_win.shape[2]
    xp = x_prompt.reshape(seq, D_MODEL)
    xs = x_sample.reshape(nb * nt, D_MODEL)
    kv_p, win_p, hs_p, kv_s, win_s, hs_s = [], [], [], [], [], []
    for l in range(depth):
        lp = layer_params(l, norm_attn_g, norm_ffn_g, w_in, cmp_pe, cmp_w1, cmp_w2, lower_bounds, hg_norm_g,
                          w_br_attn, w_br_hg, w_out, peer_wq, peer_keys, peer_u, peer_v)
        xp, kv, kw, st = prompt_layer(xp, lp, tabs)
        kv_p.append(kv.reshape(bp, seq, 4, KV_HEADS, HEAD_DIM))
        win_p.append(kw[seq - min(WINDOW, seq):].reshape(bp, -1, 2, KV_HEADS, HEAD_DIM))
        hs_p.append(st[None])
        win = cache_win[l].reshape(nb, w_buf, KVB_W)
        xs, kv, kw, st = sample_layer(xs, lp, tab, l, cache4, win, state_hgrn[l], page_table, nb, nt)
        kv_s.append(kv.reshape(nb, nt, 4, KV_HEADS, HEAD_DIM))
        win_s.append(jnp.concatenate([cache_win[l], kw.reshape(nb, nt, 2, KV_HEADS, HEAD_DIM)], axis=1)[:, nt:])
        hs_s.append(st)
    y_prompt = rmsnorm(xp, final_norm_g, min(512, seq)).reshape(bp, seq, D_MODEL)
    y_sample = rmsnorm(xs, final_norm_g, nb * nt).reshape(nb, nt, D_MODEL)
    return (y_prompt, y_sample, jnp.stack(kv_p), jnp.stack(win_p), jnp.stack(hs_p),
            jnp.stack(kv_s), jnp.stack(win_s), jnp.stack(hs_s))
```

```python
import functools
import math

import jax
import jax.numpy as jnp
import numpy as np
from jax import lax
from jax.experimental import pallas as pl
from jax.experimental.pallas import tpu as pltpu

F32 = jnp.float32
BF16 = jnp.bfloat16

D_MODEL = 1024
PAGE_SIZE = 128
N_HEADS = 8
HEAD_DIM = 64
KV_HEADS = 2
HPG = N_HEADS // KV_HEADS
CMP_STRIDE = 16
CMP_BLOCK = 32
SEL_BLOCK = 64
SEL_TOPN = 16
WINDOW = 512
ATT_SCALE = HEAD_DIM ** -0.5
N_BUCKETS = 32
MAX_DISTANCE = 128
HG_HEADS = 4
HG_DK = 128
HG_DV = 128
PEER_HEADS = 8
PEER_NKEYS = 128
PEER_DH = 128
PEER_TOPK = 16
ATT_W = N_HEADS * HEAD_DIM
KVB_W = 2 * KV_HEADS * HEAD_DIM
HGK_W = HG_HEADS * HG_DK
N_GATE = 3 * N_HEADS
EPS = 1e-6
NEG = -1e30
BIG = 1e6
TINY = 1e-30
NEG_INF = float("-inf")

LANE = 128
VMEM_LIMIT = 56 * 1024 * 1024

C_Q, C_HQ, C_GA, C_GH, C_HF, C_HI, C_HG = 0, 512, 1024, 2048, 3072, 3584, 4096
C_KVC, C_KVS, C_KVW, C_NG, N_PROJ = 4608, 4864, 5120, 5376, 5632
SUB = 16
TQ = 128
TK = 512


def _cparams(sem, vmem=VMEM_LIMIT):
    return pltpu.CompilerParams(dimension_semantics=sem, vmem_limit_bytes=vmem)


def _dot(a, b):
    return jnp.dot(a, b, preferred_element_type=F32)


def _dot_nt(a, b):
    return lax.dot_general(a, b, (((1,), (1,)), ((), ())), preferred_element_type=F32)


def _dot_tn(a, b):
    return lax.dot_general(a, b, (((0,), (0,)), ((), ())), preferred_element_type=F32)


def _sigmoid(x):
    return 1.0 / (1.0 + jnp.exp(-x))


def _gelu(x):
    return 0.5 * x * (1.0 + lax.erf(x * (2.0 ** -0.5)))


def _norm_matmul_kernel(x_ref, g_ref, w_ref, o_ref, xn_ref):
    @pl.when(pl.program_id(1) == 0)
    def _():
        x = x_ref[...]
        y = x * lax.rsqrt(jnp.mean(x * x, axis=-1, keepdims=True) + EPS)
        xn_ref[...] = (y * g_ref[...]).astype(BF16)

    o_ref[...] = _dot(xn_ref[...], w_ref[...])


def norm_matmul(x, g, w, tm, tn):
    m, k = x.shape
    n = w.shape[1]
    return pl.pallas_call(
        _norm_matmul_kernel,
        grid=(m // tm, n // tn),
        in_specs=[pl.BlockSpec((tm, k), lambda i, j: (i, 0)),
                  pl.BlockSpec((1, k), lambda i, j: (0, 0)),
                  pl.BlockSpec((k, tn), lambda i, j: (0, j))],
        out_specs=pl.BlockSpec((tm, tn), lambda i, j: (i, j)),
        out_shape=jax.ShapeDtypeStruct((m, n), F32),
        scratch_shapes=[pltpu.VMEM((tm, k), BF16)],
        compiler_params=_cparams(("parallel", "arbitrary")),
        name="norm_matmul",
    )(x, g.reshape(1, k), w)


def _matmul_kernel(x_ref, w_ref, o_ref):
    o_ref[...] = _dot(x_ref[...], w_ref[...])


def matmul(x, w, tm, tn):
    m, k = x.shape
    n = w.shape[1]
    return pl.pallas_call(
        _matmul_kernel,
        grid=(m // tm, n // tn),
        in_specs=[pl.BlockSpec((tm, k), lambda i, j: (i, 0)),
                  pl.BlockSpec((k, tn), lambda i, j: (0, j))],
        out_specs=pl.BlockSpec((tm, tn), lambda i, j: (i, j)),
        out_shape=jax.ShapeDtypeStruct((m, n), F32),
        compiler_params=_cparams(("parallel", "arbitrary")),
        name="matmul",
    )(x, w)


def _rmsnorm_kernel(x_ref, g_ref, o_ref):
    x = x_ref[...]
    o_ref[...] = x * lax.rsqrt(jnp.mean(x * x, axis=-1, keepdims=True) + EPS) * g_ref[...]


def rmsnorm(x, g, tm):
    m, k = x.shape
    return pl.pallas_call(
        _rmsnorm_kernel,
        grid=(m // tm,),
        in_specs=[pl.BlockSpec((tm, k), lambda i: (i, 0)), pl.BlockSpec((1, k), lambda i: (0, 0))],
        out_specs=pl.BlockSpec((tm, k), lambda i: (i, 0)),
        out_shape=jax.ShapeDtypeStruct((m, k), F32),
        compiler_params=_cparams(("parallel",)),
        name="rmsnorm",
    )(x, g.reshape(1, k))


def _cproj_from_pieces(piece_fn, w_ref, y_ref):
    acc = jnp.zeros(y_ref.shape, F32)
    for s in range(CMP_STRIDE):
        acc = acc + _dot(piece_fn(s).astype(BF16), w_ref[s])
    y_ref[...] = acc


def _cproj_rows_kernel(r0_ref, r1_ref, w_ref, y_ref):
    nch = y_ref.shape[0]

    def piece(s):
        return jnp.concatenate([r[pl.ds(s, nch, stride=CMP_STRIDE), :] for r in (r0_ref, r1_ref)], axis=1)

    _cproj_from_pieces(piece, w_ref, y_ref)


def cproj_rows(rows_arr, col_block, wbig, rb):
    t = rows_arr.shape[0]
    rb = min(rb, t)
    half = lambda k: pl.BlockSpec((rb, LANE), lambda i: (i, 2 * col_block + k))
    return pl.pallas_call(
        _cproj_rows_kernel,
        grid=(t // rb,),
        in_specs=[half(0), half(1), pl.BlockSpec(wbig.shape, lambda i: (0, 0, 0))],
        out_specs=pl.BlockSpec((rb // CMP_STRIDE, 2 * KVB_W), lambda i: (i, 0)),
        out_shape=jax.ShapeDtypeStruct((t // CMP_STRIDE, 2 * KVB_W), F32),
        compiler_params=_cparams(("parallel",)),
        name="cproj_rows",
    )(rows_arr, rows_arr, wbig)


def _cproj_pages_kernel(pt_ref, *refs):
    del pt_ref
    pages, w_ref, y_ref, lo_ref, hi_ref = refs[:-4], refs[-4], refs[-3], refs[-2], refs[-1]
    nch = y_ref.shape[0]
    for p, page in enumerate(pages):
        rows = pl.ds(p * PAGE_SIZE, PAGE_SIZE)
        lo_ref[rows, :] = page[0:LANE, :].T
        hi_ref[rows, :] = page[LANE:2 * LANE, :].T

    def piece(s):
        return jnp.concatenate([r[pl.ds(s, nch, stride=CMP_STRIDE), :] for r in (lo_ref, hi_ref)], axis=1)

    _cproj_from_pieces(piece, w_ref, y_ref)


def cproj_pages(cache_t, layer, page_table, wbig, pg):
    nb, npages = page_table.shape
    per = PAGE_SIZE // CMP_STRIDE

    def page_spec(p):
        return pl.BlockSpec((None, None, KVB_W, PAGE_SIZE), lambda b, j, pt: (layer, pt[b, j * pg + p], 0, 0))

    grid_spec = pltpu.PrefetchScalarGridSpec(
        num_scalar_prefetch=1,
        grid=(nb, npages // pg),
        in_specs=[page_spec(p) for p in range(pg)] + [pl.BlockSpec(wbig.shape, lambda b, j, pt: (0, 0, 0))],
        out_specs=pl.BlockSpec((None, pg * per, 2 * KVB_W), lambda b, j, pt: (b, j, 0)),
        scratch_shapes=[pltpu.VMEM((pg * PAGE_SIZE, LANE), F32)] * 2,
    )
    return pl.pallas_call(
        _cproj_pages_kernel,
        grid_spec=grid_spec,
        out_shape=jax.ShapeDtypeStruct((nb, npages * per, 2 * KVB_W), F32),
        compiler_params=_cparams(("parallel", "arbitrary")),
        name="cproj_pages",
    )(page_table, *([cache_t] * pg), wbig)


def _cmp_mlp_kernel(y_ref, pe_ref, w2_ref, o_ref):
    y = y_ref[...]
    n = y.shape[0]
    nxt = pltpu.roll(y[:, KVB_W:], n - 1, 0)
    hsum = y[:, :KVB_W] + nxt + pe_ref[...]
    o_ref[...] = _dot(_gelu(hsum).astype(BF16), w2_ref[...]).astype(o_ref.dtype)


def cmp_mlp(y, pe_term, w2big):
    b, ncp, _ = y.shape
    return pl.pallas_call(
        _cmp_mlp_kernel,
        grid=(b,),
        in_specs=[pl.BlockSpec((None, ncp, 2 * KVB_W), lambda i: (i, 0, 0)),
                  pl.BlockSpec((1, KVB_W), lambda i: (0, 0)),
                  pl.BlockSpec((KVB_W, KVB_W), lambda i: (0, 0))],
        out_specs=pl.BlockSpec((None, ncp, KVB_W), lambda i: (i, 0, 0)),
        out_shape=jax.ShapeDtypeStruct((b, ncp, KVB_W), BF16),
        compiler_params=_cparams(("parallel",)),
        name="cmp_mlp",
    )(y, pe_term, w2big)


def compress_weights(pe, w1, w2):
    w1r = w1.reshape(2, 2, CMP_STRIDE, HEAD_DIM, HEAD_DIM)
    eye = jnp.eye(2, dtype=F32)
    wbig = jnp.einsum('crsdh,cC,gG->scgdrCGh', w1r, eye, eye).reshape(CMP_STRIDE, KVB_W, 2 * KVB_W)
    w2big = jnp.einsum('chd,cC,gG->cghCGd', w2, eye, eye).reshape(KVB_W, KVB_W)
    per = pe.reshape(2, 2, CMP_STRIDE, HEAD_DIM)
    pe_rows = jnp.broadcast_to(jnp.transpose(per, (1, 2, 0, 3))[:, :, :, None, :],
                               (2, CMP_STRIDE, 2, KV_HEADS, HEAD_DIM)).reshape(2 * CMP_STRIDE, KVB_W)
    pe_rows = jnp.pad(pe_rows, ((0, 8 * CMP_STRIDE - 2 * CMP_STRIDE), (0, 0)))
    return wbig.astype(BF16), w2big.astype(BF16), pe_rows


def pe_term_from(pe_rows, wbig):
    ype = cproj_rows(pe_rows, 0, wbig, pe_rows.shape[0])
    return ype[0:1, :KVB_W] + ype[1:2, KVB_W:]


def bias_table(rel_bias):
    n = np.arange(LANE)
    exact = N_BUCKETS // 2
    nf = np.maximum(n, 1).astype(np.float32)
    large = exact + (np.log(nf / np.float32(exact)) / np.float32(math.log(MAX_DISTANCE / exact))
                     * np.float32(N_BUCKETS - exact)).astype(np.int32)
    bucket = np.where(n < exact, n, np.minimum(large, N_BUCKETS - 1))
    assert bucket[LANE - 1] == N_BUCKETS - 1
    return jnp.transpose(rel_bias.astype(F32)[bucket], (1, 0))


def _bias_lookup(tab_row, dist_clipped):
    r, w = dist_clipped.shape
    src = jnp.broadcast_to(tab_row, (r, LANE))
    parts = [jnp.take_along_axis(src, dist_clipped[:, j * LANE:(j + 1) * LANE], axis=1)
             for j in range(w // LANE)]
    return parts[0] if len(parts) == 1 else jnp.concatenate(parts, axis=1)


def _topk_mask(x, k):
    r, n = x.shape
    col = lax.broadcasted_iota(jnp.int32, (r, n), 1)
    sel = jnp.zeros((r, n), F32)
    for _ in range(k):
        m = jnp.max(x, axis=-1, keepdims=True)
        idx = jnp.min(jnp.where(x == m, col, n), axis=-1, keepdims=True)
        hit = col == idx
        sel = jnp.where(hit, 1.0, sel)
        x = jnp.where(hit, NEG_INF, x)
    return sel


def _topk_idx(x, k):
    r, n = x.shape
    col = lax.broadcasted_iota(jnp.int32, (r, n), 1)
    lane = lax.broadcasted_iota(jnp.int32, (r, LANE), 1)
    out = jnp.zeros((r, LANE), jnp.int32)
    for it in range(k):
        m = jnp.max(x, axis=-1, keepdims=True)
        idx = jnp.min(jnp.where(x == m, col, n), axis=-1, keepdims=True)
        out = jnp.where(lane == it, idx, out)
        x = jnp.where(col == idx, NEG_INF, x)
    return out


def _cmp_branch(q, t_pos, ckv_ref, tab_ref, cover_ref, nc, ncols=None):
    r = q.shape[0]
    ncols = ckv_ref.shape[0] if ncols is None else ncols
    n_sel = cover_ref.shape[1]
    c_idx = lax.broadcasted_iota(jnp.int32, (1, ncols), 1)
    dist = t_pos - (c_idx * CMP_STRIDE + CMP_BLOCK - 1)
    valid = (dist >= 0) & (c_idx < nc)
    any_valid = (t_pos >= CMP_BLOCK - 1).astype(F32)
    dcl = jnp.clip(dist, 0, LANE - 1)
    blk = lax.broadcasted_iota(jnp.int32, (1, n_sel), 1)
    cur = t_pos // SEL_BLOCK
    forced = (blk == 0) | (blk == cur) | (blk == cur - 1)
    outs, imps = [], []
    for g in range(KV_HEADS):
        kc = ckv_ref[:ncols, g * HEAD_DIM:(g + 1) * HEAD_DIM]
        vc = ckv_ref[:ncols, (KV_HEADS + g) * HEAD_DIM:(KV_HEADS + g + 1) * HEAD_DIM]
        psum = jnp.zeros((r, ncols), F32)
        for h in range(HPG):
            head = g * HPG + h
            qh = (q[:, head * HEAD_DIM:(head + 1) * HEAD_DIM] * ATT_SCALE).astype(BF16)
            s = _dot_nt(qh, kc) + _bias_lookup(tab_ref[head:head + 1, :], dcl)
            s = jnp.where(valid, s, NEG)
            e = jnp.exp(s - jnp.max(s, axis=-1, keepdims=True))
            p = e * (any_valid / jnp.sum(e, axis=-1, keepdims=True))
            outs.append(_dot(p.astype(BF16), vc))
            psum = psum + p
        p_hi = psum.astype(BF16)
        p_lo = (psum - p_hi.astype(F32)).astype(BF16)
        imp = _dot(p_hi, cover_ref[:ncols]) + _dot(p_lo, cover_ref[:ncols])
        imps.append(jnp.where(forced, BIG, jnp.where(blk <= cur, imp, -BIG)))
    return outs, imps


def _topk_mask_cols(x, k):
    n, t = x.shape
    row = lax.broadcasted_iota(jnp.int32, (n, t), 0)
    sel = jnp.zeros((n, t), F32)
    for _ in range(k):
        m = jnp.max(x, axis=0, keepdims=True)
        idx = jnp.min(jnp.where(x == m, row, n), axis=0, keepdims=True)
        hit = row == idx
        sel = jnp.where(hit, 1.0, sel)
        x = jnp.where(hit, NEG_INF, x)
    return sel


def _nsa_cmp_kernel(q_ref, ckv_ref, tab_ref, cover_ref, oc_ref, mask_ref, *, nc, n_sel):
    i = pl.program_id(0)
    t_pos = i * TQ + lax.broadcasted_iota(jnp.int32, (TQ, 1), 0)
    ncp = ckv_ref.shape[0]
    part = max(ncp // 4, LANE)
    last_c = (i * TQ + TQ - CMP_BLOCK) // CMP_STRIDE
    need = jnp.clip(last_c // part, 0, ncp // part - 1)

    for n_parts in range(1, ncp // part + 1):
        @pl.when(need == n_parts - 1)
        def _(n_parts=n_parts):
            outs, imps = _cmp_branch(q_ref[...], t_pos, ckv_ref, tab_ref, cover_ref, nc, n_parts * part)
            for g in range(KV_HEADS):
                mask_ref[g] = _topk_mask_cols(imps[g].T, min(SEL_TOPN, n_sel))
            oc_ref[...] = jnp.concatenate(outs, axis=1)


def cover_matrix(ncp, n_sel):
    ci = np.arange(ncp)[:, None] * CMP_STRIDE
    sj = np.arange(n_sel)[None, :] * SEL_BLOCK
    return jnp.asarray(((ci < sj + SEL_BLOCK) & (ci + CMP_BLOCK > sj)).astype(np.float32), dtype=BF16)


def nsa_cmp_prompt(hproj, ckv, tab, nc):
    t = hproj.shape[0]
    ncp = ckv.shape[0]
    n_sel = t // SEL_BLOCK
    cover = cover_matrix(ncp, n_sel)
    return pl.pallas_call(
        functools.partial(_nsa_cmp_kernel, nc=nc, n_sel=n_sel),
        grid=(t // TQ,),
        in_specs=[pl.BlockSpec((TQ, ATT_W), lambda i: (i, C_Q // ATT_W)),
                  pl.BlockSpec((ncp, KVB_W), lambda i: (0, 0)),
                  pl.BlockSpec((N_HEADS, LANE), lambda i: (0, 0)),
                  pl.BlockSpec((ncp, n_sel), lambda i: (0, 0))],
        out_specs=[pl.BlockSpec((TQ, ATT_W), lambda i: (i, 0)),
                   pl.BlockSpec((KV_HEADS, n_sel, TQ), lambda i: (0, 0, i))],
        out_shape=[jax.ShapeDtypeStruct((t, ATT_W), F32),
                   jax.ShapeDtypeStruct((KV_HEADS, n_sel, t), F32)],
        compiler_params=_cparams(("parallel",)),
        name="nsa_cmp_prompt",
    )(hproj, ckv, tab, cover)


MASKED = -1e9
NEAR = 2 * TQ
W_KEYS = WINDOW + TQ
BLK_PER_TILE = TK // SEL_BLOCK


VT_ROWS = HEAD_DIM + 16
MASK_ROWS = 16
HT = HPG * TQ
FLAG_LANE = HEAD_DIM + BLK_PER_TILE


def _online_step_t(carry, st, shift, vt):
    m, acc = carry
    m_new = jnp.maximum(m, jnp.max(st, axis=0, keepdims=True) + shift)
    p = jnp.exp(st - (m_new - shift)).astype(BF16)
    return m_new, jnp.exp(m - m_new) * acc + _dot(vt, p)


def _nsa_sel_kernel(q_ref, ng_ref, oc_ref, mask_ref, ka_ref, vt_ref, tab_ref, ntab_ref, wtab_ref, o_ref,
                    sta_ref, stb_ref, *, n_sel):
    i = pl.program_id(0)
    qt = (q_ref[...] * ATT_SCALE).T
    oct_ = oc_ref[...].T
    gt = _sigmoid(ng_ref[...]).T
    near_blk0 = 2 * i - 2
    n_far = (jnp.maximum(i - 1, 0) * TQ + TK - 1) // TK
    blk8 = lax.broadcasted_iota(jnp.int32, (BLK_PER_TILE, 1), 0)
    init = (jnp.full((1, HT), NEG, F32), jnp.zeros((VT_ROWS, HT), F32))
    zero_shift = jnp.zeros((1, HT), F32)
    chunks_per_tile = TK // LANE

    before_start = jnp.where(lax.broadcasted_iota(jnp.int32, (MASK_ROWS, HT), 0) == FLAG_LANE - HEAD_DIM,
                             MASKED, 0.0).astype(BF16)
    qtg, w_plain, far_shift = [], [], []
    for g in range(KV_HEADS):
        heads = range(g * HPG, (g + 1) * HPG)
        qg = jnp.concatenate([qt[h * HEAD_DIM:(h + 1) * HEAD_DIM] for h in heads], axis=1).astype(BF16)
        qtg.append(qg)
        w_plain.append(jnp.concatenate([qg, before_start, jnp.zeros((LANE - HEAD_DIM - MASK_ROWS, HT), BF16)], axis=0))
        far_shift.append(jnp.concatenate(
            [jnp.broadcast_to(tab_ref[h:h + 1, LANE - 1:LANE], (1, TQ)) for h in heads], axis=1))

    def vt_tile(c0, n, kind):
        return jnp.concatenate([vt_ref[c0 + c, kind] for c in range(n)], axis=1)

    last_tile = (ka_ref.shape[0] - WINDOW) // TK - 1

    def far_logits(kt, dst_ref):
        kt = jnp.minimum(kt, last_tile)
        k0 = pl.multiple_of(WINDOW + kt * TK, TK)
        r0 = pl.multiple_of(kt * BLK_PER_TILE, BLK_PER_TILE)
        for g in range(KV_HEADS):
            mt = mask_ref[g, pl.ds(r0, BLK_PER_TILE), :]
            keep = (mt > 0.5) & (kt * BLK_PER_TILE + blk8 < near_blk0)
            mneg = jnp.where(keep, 0.0, MASKED)
            m16 = jnp.concatenate([jnp.concatenate([mneg] * HPG, axis=1),
                                   jnp.zeros((MASK_ROWS - BLK_PER_TILE, HT), F32)], axis=0).astype(BF16)
            w = jnp.concatenate([qtg[g], m16, jnp.zeros((LANE - HEAD_DIM - MASK_ROWS, HT), BF16)], axis=0)
            dst_ref[g] = _dot(ka_ref[pl.ds(k0, TK), g * LANE:(g + 1) * LANE], w)

    def far_softmax(kt, src_ref, carry):
        c0 = (WINDOW // LANE) + jnp.minimum(kt, last_tile) * chunks_per_tile
        return tuple(_online_step_t(carry[g], src_ref[g], far_shift[g], vt_tile(c0, chunks_per_tile, g))
                     for g in range(KV_HEADS))

    @pl.when(n_far > 0)
    def _():
        far_logits(0, sta_ref)

    def far_body(k2, carry):
        far_logits(2 * k2 + 1, stb_ref)
        carry = far_softmax(2 * k2, sta_ref, carry)
        far_logits(2 * k2 + 2, sta_ref)
        return far_softmax(2 * k2 + 1, stb_ref, carry)

    carry = lax.fori_loop(0, (n_far + 1) // 2, far_body, (init, init))

    kn0 = pl.multiple_of(WINDOW + (i - 1) * TQ, TQ)
    base = pl.multiple_of(jnp.clip((near_blk0 // BLK_PER_TILE) * BLK_PER_TILE, 0, n_sel - MASK_ROWS), BLK_PER_TILE)
    key_blk = near_blk0 + lax.broadcasted_iota(jnp.int32, (NEAR, MASK_ROWS), 0) // SEL_BLOCK
    expand = (key_blk == base + lax.broadcasted_iota(jnp.int32, (NEAR, MASK_ROWS), 1)).astype(BF16)
    kw0 = pl.multiple_of(i * TQ, TQ)
    rows_t = []
    st_near, st_win = [], []
    for g in range(KV_HEADS):
        mneg = jnp.where(mask_ref[g, pl.ds(base, MASK_ROWS), :] > 0.5, 0.0, MASKED).astype(BF16)
        sel_add = _dot(expand, mneg)
        st = _dot(ka_ref[pl.ds(kn0, NEAR), g * LANE:(g + 1) * LANE], w_plain[g])
        st_near.append(st + ntab_ref[g] + jnp.concatenate([sel_add] * HPG, axis=1))
        st = _dot(ka_ref[pl.ds(kw0, W_KEYS), (KV_HEADS + g) * LANE:(KV_HEADS + g + 1) * LANE], w_plain[g])
        st_win.append(st + wtab_ref[g])
    for g in range(KV_HEADS):
        _, acc = _online_step_t(carry[g], st_near[g], zero_shift, vt_tile(kn0 // LANE, NEAR // LANE, g))
        os_t = acc[:HEAD_DIM] / acc[HEAD_DIM:HEAD_DIM + 1]
        _, acc = _online_step_t(init, st_win[g], zero_shift, vt_tile(kw0 // LANE, W_KEYS // LANE, KV_HEADS + g))
        ow_t = acc[:HEAD_DIM] / acc[HEAD_DIM:HEAD_DIM + 1]

        for h in range(HPG):
            head = g * HPG + h
            ls = slice(h * TQ, (h + 1) * TQ)
            rows_t.append(gt[head:head + 1] * oct_[head * HEAD_DIM:(head + 1) * HEAD_DIM]
                          + gt[N_HEADS + head:N_HEADS + head + 1] * os_t[:, ls]
                          + gt[2 * N_HEADS + head:2 * N_HEADS + head + 1] * ow_t[:, ls])
    o_ref[...] = jnp.concatenate(rows_t, axis=0).T


def _distance_tables_kernel(tab_ref, ntab_ref, wtab_ref):
    head = pl.program_id(0) * HPG + pl.program_id(1)
    row = tab_ref[pl.ds(head, 1), :]
    for out_ref, start, max_dist in ((ntab_ref, TQ, None), (wtab_ref, WINDOW, WINDOW)):
        keys = out_ref.shape[0]
        dist = (start + lax.broadcasted_iota(jnp.int32, (keys, TQ), 1)
                - lax.broadcasted_iota(jnp.int32, (keys, TQ), 0))
        ok = dist >= 0 if max_dist is None else (dist >= 0) & (dist <= max_dist)
        out_ref[...] = jnp.where(ok, _bias_lookup(row, jnp.clip(dist, 0, LANE - 1)), NEG)


def distance_tables(tab):
    return pl.pallas_call(
        _distance_tables_kernel,
        grid=(KV_HEADS, HPG),
        in_specs=[pl.BlockSpec((N_HEADS, LANE), lambda g, h: (0, 0))],
        out_specs=[pl.BlockSpec((None, NEAR, TQ), lambda g, h: (g, 0, h)),
                   pl.BlockSpec((None, W_KEYS, TQ), lambda g, h: (g, 0, h))],
        out_shape=[jax.ShapeDtypeStruct((KV_HEADS, NEAR, HT), F32),
                   jax.ShapeDtypeStruct((KV_HEADS, W_KEYS, HT), F32)],
        compiler_params=_cparams(("parallel", "parallel")),
        name="distance_tables",
    )(tab)


def attention_kv_layout(hproj):
    t = hproj.shape[0]
    grp = lambda c0, g: hproj[:, c0 + g * HEAD_DIM:c0 + (g + 1) * HEAD_DIM]
    e8 = jnp.asarray(np.eye(BLK_PER_TILE, dtype=np.float32)[(np.arange(t) % TK) // SEL_BLOCK])
    z = lambda w: jnp.zeros((t, w), F32)
    ka = [jnp.concatenate([grp(C_KVS, g), e8, z(LANE - HEAD_DIM - BLK_PER_TILE)], axis=1) for g in range(KV_HEADS)]
    ka += [jnp.concatenate([grp(C_KVW, g), z(LANE - HEAD_DIM)], axis=1) for g in range(KV_HEADS)]
    lead = jnp.zeros((WINDOW, LANE), F32).at[:, FLAG_LANE].set(1.0)
    ka = jnp.concatenate([jnp.tile(lead, (1, 2 * KV_HEADS)), jnp.concatenate(ka, axis=1)], axis=0).astype(BF16)
    ones = jnp.ones((t, 1), F32)
    vts = []
    for c0 in (C_KVS, C_KVW):
        for g in range(KV_HEADS):
            v = jnp.concatenate([grp(c0 + KV_HEADS * HEAD_DIM, g), ones, z(VT_ROWS - HEAD_DIM - 1)], axis=1)
            v = jnp.pad(v.astype(BF16), ((WINDOW, 0), (0, 0))).reshape(-1, LANE, VT_ROWS)
            vts.append(jnp.swapaxes(v, 1, 2))
    return ka, jnp.stack(vts, axis=1)


def nsa_sel_prompt(hproj, o_c, mask, ka, vt, tab, ntab, wtab):
    t = hproj.shape[0]
    n_sel = mask.shape[1]
    assert t % TK == 0 and n_sel % BLK_PER_TILE == 0 and n_sel >= MASK_ROWS
    const = lambda a: pl.BlockSpec(a.shape, lambda i: (0,) * a.ndim, pipeline_mode=pl.Buffered(1))
    return pl.pallas_call(
        functools.partial(_nsa_sel_kernel, n_sel=n_sel),
        grid=(t // TQ,),
        in_specs=[pl.BlockSpec((TQ, ATT_W), lambda i: (i, C_Q // ATT_W)),
                  pl.BlockSpec((TQ, LANE), lambda i: (i, C_NG // LANE)),
                  pl.BlockSpec((TQ, ATT_W), lambda i: (i, 0)),
                  pl.BlockSpec((KV_HEADS, n_sel, TQ), lambda i: (0, 0, i)),
                  const(ka), const(vt), const(tab), const(ntab), const(wtab)],
        out_specs=pl.BlockSpec((TQ, ATT_W), lambda i: (i, 0)),
        out_shape=jax.ShapeDtypeStruct((t, ATT_W), F32),
        scratch_shapes=[pltpu.VMEM((KV_HEADS, TK, HT), F32)] * 2,
        compiler_params=_cparams(("parallel",)),
        name="nsa_sel_prompt",
    )(hproj, hproj, o_c, mask, ka, vt, tab, ntab, wtab)


def _hgrn_gates(hq, hf, lb):
    sig = _sigmoid(hf)
    forget = lb + (1.0 - lb) * sig
    logf = jnp.log(jnp.maximum(forget, TINY))
    k = (1.0 - lb) * (1.0 - sig)
    q = hq * _sigmoid(hq)
    return q, k, logf


def _hgrn_intra(q, k, v, a, sub):
    n = q.shape[0]
    row = lax.broadcasted_iota(jnp.int32, (n, 1), 0) % sub
    outs = [jnp.zeros((n, HG_DV), F32) for _ in range(HG_HEADS)]
    for d in range(sub):
        ks = k if d == 0 else pltpu.roll(k, d, 0)
        a_s = a if d == 0 else pltpu.roll(a, d, 0)
        vs = v if d == 0 else pltpu.roll(v, d, 0)
        msk = row >= d
        w = q * ks * jnp.exp(jnp.where(msk, a - a_s, 0.0)) * msk.astype(F32)
        for h in range(HG_HEADS):
            r = jnp.sum(w[:, h * HG_DK:(h + 1) * HG_DK], axis=-1, keepdims=True)
            outs[h] = outs[h] + r * vs[:, h * HG_DV:(h + 1) * HG_DV]
    return outs


def _hgrn_prompt_kernel(hq_ref, hf_ref, hi_ref, lb_ref, tri_ref, o_ref, st_out_ref, st_ref):
    step = pl.program_id(0)

    @pl.when(step == 0)
    def _():
        st_ref[...] = jnp.zeros(st_ref.shape, F32)

    n = hq_ref.shape[0]
    q, k, logf = _hgrn_gates(hq_ref[...], hf_ref[...], lb_ref[...])
    v = hi_ref[...]
    a = jnp.dot(tri_ref[...], logf, preferred_element_type=F32, precision=lax.Precision.HIGHEST)
    outs = _hgrn_intra(q, k, v, a, SUB)
    rows_out = [[] for _ in range(HG_HEADS)]
    for c in range(n // SUB):
        sl = slice(c * SUB, (c + 1) * SUB)
        a_c = a[sl]
        a_last = a_c[SUB - 1:SUB]
        qe = (q[sl] * jnp.exp(a_c)).astype(BF16)
        kd = (k[sl] * jnp.exp(a_last - a_c)).astype(BF16)
        dec = jnp.exp(a_last)
        for h in range(HG_HEADS):
            hs = slice(h * HG_DK, (h + 1) * HG_DK)
            st = st_ref[h]
            rows_out[h].append(outs[h][sl] + _dot_nt(qe[:, hs], st.astype(BF16)))
            st_ref[h] = dec[:, hs] * st + _dot_tn(v[sl, hs].astype(BF16), kd[:, hs])
    o_ref[...] = jnp.concatenate([jnp.concatenate(r, axis=0) for r in rows_out], axis=1)
    st_out_ref[...] = st_ref[...]


def hgrn_prompt(hproj, lb, tc):
    t = hproj.shape[0]
    tc = min(tc, t)
    r = np.arange(tc)
    tri = jnp.asarray(((r[:, None] >= r[None, :]) & (r[:, None] // SUB == r[None, :] // SUB)).astype(np.float32))
    return pl.pallas_call(
        _hgrn_prompt_kernel,
        grid=(t // tc,),
        in_specs=[pl.BlockSpec((tc, HGK_W), lambda i: (i, C_HQ // HGK_W)),
                  pl.BlockSpec((tc, HGK_W), lambda i: (i, C_HF // HGK_W)),
                  pl.BlockSpec((tc, HGK_W), lambda i: (i, C_HI // HGK_W)),
                  pl.BlockSpec((1, HGK_W), lambda i: (0, 0)),
                  pl.BlockSpec((tc, tc), lambda i: (0, 0))],
        out_specs=[pl.BlockSpec((tc, HGK_W), lambda i: (i, 0)),
                   pl.BlockSpec((HG_HEADS, HG_DV, HG_DK), lambda i: (0, 0, 0))],
        out_shape=[jax.ShapeDtypeStruct((t, HGK_W), F32),
                   jax.ShapeDtypeStruct((HG_HEADS, HG_DV, HG_DK), F32)],
        scratch_shapes=[pltpu.VMEM((HG_HEADS, HG_DV, HG_DK), F32)],
        compiler_params=_cparams(("arbitrary",)),
        name="hgrn_prompt",
    )(hproj, hproj, hproj, lb.reshape(1, HGK_W), tri)


def _merge_kernel(x_ref, att_ref, o_ref, hg_ref, ga_ref, gh_ref, ng_ref, fg_ref, wa_ref, wh_ref, wo_ref,
                  xo_ref, xn_ref):
    o = o_ref[...]
    parts = []
    for h in range(HG_HEADS):
        oh = o[:, h * HG_DV:(h + 1) * HG_DV]
        parts.append(oh * lax.rsqrt(jnp.mean(oh * oh, axis=-1, keepdims=True) + EPS))
    hg = hg_ref[...]
    hgo = jnp.concatenate(parts, axis=1) * ng_ref[...] * (hg * _sigmoid(hg))
    m = (_sigmoid(ga_ref[...]) * _dot(att_ref[...].astype(BF16), wa_ref[...])
         + _sigmoid(gh_ref[...]) * _dot(hgo.astype(BF16), wh_ref[...]))
    x = x_ref[...] + _dot(m.astype(BF16), wo_ref[...])
    xo_ref[...] = x
    y = x * lax.rsqrt(jnp.mean(x * x, axis=-1, keepdims=True) + EPS)
    xn_ref[...] = (y * fg_ref[...]).astype(BF16)


def merge(x, att, o, hproj, hg_norm_g, ffn_g, wa, wh, wo, tm):
    t = x.shape[0]
    tm = min(tm, t)
    row = lambda w, cb: pl.BlockSpec((tm, w), lambda i: (i, cb))
    full = lambda a: pl.BlockSpec(a.shape, lambda i: (0, 0))
    hg_norm_g = hg_norm_g.reshape(1, HGK_W)
    ffn_g = ffn_g.reshape(1, D_MODEL)
    return pl.pallas_call(
        _merge_kernel,
        grid=(t // tm,),
        in_specs=[row(D_MODEL, 0), row(ATT_W, 0), row(HGK_W, 0), row(HGK_W, C_HG // HGK_W),
                  row(D_MODEL, C_GA // D_MODEL), row(D_MODEL, C_GH // D_MODEL),
                  full(hg_norm_g), full(ffn_g), full(wa), full(wh), full(wo)],
        out_specs=[row(D_MODEL, 0), row(D_MODEL, 0)],
        out_shape=[jax.ShapeDtypeStruct((t, D_MODEL), F32), jax.ShapeDtypeStruct((t, D_MODEL), BF16)],
        compiler_params=_cparams(("parallel",)),
        name="merge",
    )(x, att, o, hproj, hproj, hproj, hg_norm_g, ffn_g, wa, wh, wo)


def _topk_cols(x, k, payload=None):
    n, t = x.shape
    row = lax.broadcasted_iota(jnp.int32, (n, t), 0)
    vals, ids, pay = [], [], []
    for _ in range(k):
        m = jnp.max(x, axis=0, keepdims=True)
        idx = jnp.min(jnp.where(x == m, row, n), axis=0, keepdims=True)
        hit = row == idx
        vals.append(m)
        ids.append(idx)
        if payload is not None:
            pay.append(jnp.max(jnp.where(hit, payload, -1), axis=0, keepdims=True))
        x = jnp.where(hit, NEG_INF, x)
    cat = lambda xs: jnp.concatenate(xs, axis=0)
    return cat(vals), cat(ids), (cat(pay) if payload is not None else None)


def _pair_candidates():
    pairs = [(a, b) for a in range(PEER_TOPK) for b in range(PEER_TOPK) if (a + 1) * (b + 1) <= PEER_TOPK]
    rows = -(-len(pairs) // 8) * 8
    sel = np.zeros((2, rows, PEER_TOPK), np.float32)
    for r, (a, b) in enumerate(pairs):
        sel[0, r, a] = 1.0
        sel[1, r, b] = 1.0
    return len(pairs), jnp.asarray(sel)


def _pick_rows(sel, x):
    return jnp.dot(sel, x, preferred_element_type=F32, precision=lax.Precision.HIGHEST)


def _peer_topk_kernel(q_ref, keys_ref, pair_ref, i1_ref, i2_ref, g_ref, *, n_pairs):
    o1, o2, og = [], [], []
    live = lax.broadcasted_iota(jnp.int32, (pair_ref.shape[1], q_ref.shape[0]), 0) < n_pairs
    for h in range(PEER_HEADS):
        top = []
        for side in range(2):
            c0 = (h * 2 + side) * PEER_DH
            s = _dot_nt(keys_ref[side], q_ref[:, c0:c0 + PEER_DH].astype(BF16))
            top.append(_topk_cols(s, PEER_TOPK)[:2])
        (v1, i1), (v2, i2) = top
        cand = jnp.where(live, _pick_rows(pair_ref[0], v1) + _pick_rows(pair_ref[1], v2), NEG_INF)
        cidx = (_pick_rows(pair_ref[0], i1.astype(F32)) * PEER_NKEYS
                + _pick_rows(pair_ref[1], i2.astype(F32))).astype(jnp.int32)
        sv, _, e = _topk_cols(cand, PEER_TOPK, cidx)
        ex = jnp.exp(sv - sv[0:1])
        o1.append(e // PEER_NKEYS)
        o2.append(e % PEER_NKEYS)
        og.append(ex / jnp.sum(ex, axis=0, keepdims=True))
    i1_ref[...] = jnp.concatenate(o1, axis=0).astype(F32).T.astype(jnp.int32)
    i2_ref[...] = jnp.concatenate(o2, axis=0).astype(F32).T.astype(jnp.int32)
    g_ref[...] = jnp.concatenate(og, axis=0).T


def peer_topk(q, keys_bf):
    t = q.shape[0]
    tb = LANE
    nsel = PEER_HEADS * PEER_TOPK
    n_pairs, pair_sel = _pair_candidates()
    out = lambda dt: jax.ShapeDtypeStruct((t, nsel), dt)
    return pl.pallas_call(
        functools.partial(_peer_topk_kernel, n_pairs=n_pairs),
        grid=(t // tb,),
        in_specs=[pl.BlockSpec((tb, q.shape[1]), lambda i: (i, 0)),
                  pl.BlockSpec(keys_bf.shape, lambda i: (0, 0, 0)),
                  pl.BlockSpec(pair_sel.shape, lambda i: (0, 0, 0))],
        out_specs=[pl.BlockSpec((tb, nsel), lambda i: (i, 0))] * 3,
        out_shape=[out(jnp.int32), out(jnp.int32), out(F32)],
        compiler_params=_cparams(("parallel",)),
        name="peer_topk",
    )(q, keys_bf, pair_sel)


GATE_UNROLL = 8


def _peer_dense_kernel(xn_ref, x_ref, i1_ref, i2_ref, g_ref, u_ref, v_ref, o_ref, acc_ref, gate_ref, *, n_a):
    j = pl.program_id(1)
    tm, nsel = i1_ref.shape

    @pl.when(j == 0)
    def _():
        acc_ref[...] = jnp.zeros(acc_ref.shape, F32)
        row = lax.broadcasted_iota(jnp.int32, (PEER_NKEYS, nsel), 0)

        def body(tt, _):
            for k in range(GATE_UNROLL):
                t = tt * GATE_UNROLL + k
                pa = jnp.where(row == i1_ref[pl.ds(t, 1), :], 1.0, 0.0).astype(BF16)
                gb = jnp.where(row == i2_ref[pl.ds(t, 1), :], g_ref[pl.ds(t, 1), :], 0.0).astype(BF16)
                gate_ref[pl.ds(pl.multiple_of(t * PEER_NKEYS, PEER_NKEYS), PEER_NKEYS), :] = _dot_nt(pa, gb)
            return 0

        lax.fori_loop(0, tm // GATE_UNROLL, body, 0)

    s = _dot_nt(xn_ref[...], u_ref[...])
    gate = jnp.concatenate([gate_ref[pl.ds(j * n_a + k, tm, stride=PEER_NKEYS), :] for k in range(n_a)], axis=1)
    acc_ref[...] += _dot((gate * _gelu(s)).astype(BF16), v_ref[...])

    @pl.when(j == pl.num_programs(1) - 1)
    def _():
        o_ref[...] = x_ref[...] + acc_ref[...]


def peer_dense(xn, x, i1, i2, g, u_bf, v_bf, tm, n_a):
    t = x.shape[0]
    tm = min(tm, t)
    te = n_a * PEER_NKEYS
    nsel = i1.shape[1]
    tok = lambda w: pl.BlockSpec((tm, w), lambda i, j: (i, 0))
    return pl.pallas_call(
        functools.partial(_peer_dense_kernel, n_a=n_a),
        grid=(t // tm, PEER_NKEYS // n_a),
        in_specs=[tok(D_MODEL), tok(D_MODEL), tok(nsel), tok(nsel), tok(nsel),
                  pl.BlockSpec((te, D_MODEL), lambda i, j: (j, 0)),
                  pl.BlockSpec((te, D_MODEL), lambda i, j: (j, 0))],
        out_specs=tok(D_MODEL),
        out_shape=jax.ShapeDtypeStruct((t, D_MODEL), F32),
        scratch_shapes=[pltpu.VMEM((tm, D_MODEL), F32), pltpu.VMEM((tm * PEER_NKEYS, PEER_NKEYS), F32)],
        compiler_params=_cparams(("parallel", "arbitrary")),
        name="peer_dense",
    )(xn, x, i1, i2, g, u_bf, v_bf)


def peer(xn_bf, x, wq_bf, keys_bf, u_bf, v_bf, tm):
    t = x.shape[0]
    q = matmul(xn_bf, wq_bf, min(512, t), 1024)
    i1, i2, g = peer_topk(q, keys_bf)
    return peer_dense(xn_bf, x, i1, i2, g, u_bf, v_bf, tm, 16)


ROWS = 8


def _pad_rows(x, rows=ROWS):
    return jnp.concatenate([x, jnp.zeros((rows - x.shape[0], x.shape[1]), x.dtype)], axis=0)


def _nsa_cmp_sample_kernel(q_ref, ckv_ref, tab_ref, cover_ref, oc_ref, sel_ref, *, nc, past, n_sel):
    nt = q_ref.shape[0]
    t_pos = past + lax.broadcasted_iota(jnp.int32, (ROWS, 1), 0)
    outs, imps = _cmp_branch(_pad_rows(q_ref[...]), t_pos, ckv_ref, tab_ref, cover_ref, nc)
    for g in range(KV_HEADS):
        sel_ref[g] = _topk_idx(imps[g], min(SEL_TOPN, n_sel))
    oc_ref[...] = jnp.concatenate(outs, axis=1)[:nt]


def nsa_cmp_sample(hproj3, ckv, tab, nc, past):
    nb, nt, _ = hproj3.shape
    ncp = ckv.shape[1]
    n_sel = -(-(past + nt) // SEL_BLOCK)
    n_sel_pad = -(-n_sel // LANE) * LANE
    cover = cover_matrix(ncp, n_sel_pad)
    return pl.pallas_call(
        functools.partial(_nsa_cmp_sample_kernel, nc=nc, past=past, n_sel=n_sel),
        grid=(nb,),
        in_specs=[pl.BlockSpec((None, nt, ATT_W), lambda b: (b, 0, C_Q // ATT_W)),
                  pl.BlockSpec((None, ncp, KVB_W), lambda b: (b, 0, 0)),
                  pl.BlockSpec((N_HEADS, LANE), lambda b: (0, 0)),
                  pl.BlockSpec((ncp, n_sel_pad), lambda b: (0, 0))],
        out_specs=[pl.BlockSpec((None, nt, ATT_W), lambda b: (b, 0, 0)),
                   pl.BlockSpec((None, KV_HEADS, ROWS, LANE), lambda b: (b, 0, 0, 0))],
        out_shape=[jax.ShapeDtypeStruct((nb, nt, ATT_W), F32),
                   jax.ShapeDtypeStruct((nb, KV_HEADS, ROWS, LANE), jnp.int32)],
        compiler_params=_cparams(("parallel",)),
        name="nsa_cmp_sample",
    )(hproj3, ckv, tab, cover)


def _nsa_sel_sample_kernel(sel_ref, pt_ref, q_ref, ng_ref, oc_ref, kvs_ref, kvw_ref, win_ref, tab_ref, cache_ref,
                           o_ref, buf_ref, sem, *, layer, past, w_len):
    b = pl.program_id(0)
    nt = q_ref.shape[0]
    nslot = SEL_TOPN
    per_page = PAGE_SIZE // SEL_BLOCK

    def slot_copy(t, g, slot):
        blk = sel_ref[((b * nt + t) * KV_HEADS + g) * nslot + slot]
        in_past = blk * SEL_BLOCK < past
        blk_c = jnp.minimum(blk, past // SEL_BLOCK - 1)
        page = pt_ref[b, blk_c // per_page]
        src = cache_ref.at[layer, page, pl.ds(KVB_W, KVB_W), :]
        dst = buf_ref.at[t * KV_HEADS + g, :, pl.ds(slot * PAGE_SIZE, PAGE_SIZE)]
        return blk, in_past, pltpu.make_async_copy(src, dst, sem)

    for t in range(nt):
        for g in range(KV_HEADS):
            for slot in range(nslot):
                _, in_past, cp = slot_copy(t, g, slot)

                @pl.when(in_past)
                def _():
                    cp.start()

                @pl.when(jnp.logical_not(in_past))
                def _():
                    buf_ref[t * KV_HEADS + g, :, pl.ds(slot * PAGE_SIZE, PAGE_SIZE)] = jnp.zeros(
                        (KVB_W, PAGE_SIZE), F32)
    for t in range(nt):
        for g in range(KV_HEADS):
            for slot in range(nslot):
                _, in_past, cp = slot_copy(t, g, slot)

                @pl.when(in_past)
                def _():
                    cp.wait()

    gates = _sigmoid(ng_ref[...])
    nk = nslot * PAGE_SIZE
    lane_slot = lax.broadcasted_iota(jnp.int32, (1, nk), 1) // PAGE_SIZE
    lane_off = lax.broadcasted_iota(jnp.int32, (1, nk), 1) % PAGE_SIZE
    j_new = lax.broadcasted_iota(jnp.int32, (1, LANE), 1)
    kvn = jnp.concatenate([kvs_ref[...], kvw_ref[...]], axis=1)
    kvn_pad = _pad_rows(kvn, LANE).astype(BF16)
    win = win_ref[...].astype(BF16)
    w_idx = lax.broadcasted_iota(jnp.int32, (1, w_len + LANE), 1)
    rows_out = []
    for t in range(nt):
        t_pos = past + t
        heads_out = []
        for g in range(KV_HEADS):
            q4 = jnp.concatenate(
                [q_ref[t:t + 1, (g * HPG + h) * HEAD_DIM:(g * HPG + h + 1) * HEAD_DIM] for h in range(HPG)],
                axis=0)
            q4 = (q4 * ATT_SCALE).astype(BF16)

            def bias_of(dist):
                d = jnp.broadcast_to(jnp.clip(dist, 0, LANE - 1), (ROWS, dist.shape[1]))
                return jnp.concatenate(
                    [_bias_lookup(tab_ref[g * HPG + h:g * HPG + h + 1, :], d)[0:1] for h in range(HPG)], axis=0)

            pos = jnp.zeros((1, nk), jnp.int32)
            lane_blk = jnp.zeros((1, nk), jnp.int32)
            has_new = jnp.zeros((1, 1), jnp.int32)
            for slot in range(nslot):
                blk = sel_ref[((b * nt + t) * KV_HEADS + g) * nslot + slot]
                pos = jnp.where(lane_slot == slot, (blk // per_page) * PAGE_SIZE, pos)
                lane_blk = jnp.where(lane_slot == slot, blk, lane_blk)
                has_new = jnp.maximum(has_new, (blk * SEL_BLOCK >= past).astype(jnp.int32))
            pos = pos + lane_off
            in_block = pos // SEL_BLOCK == lane_blk
            kv_g = buf_ref[t * KV_HEADS + g].astype(BF16)
            k_new = kvn_pad[:, g * HEAD_DIM:(g + 1) * HEAD_DIM]
            v_new = kvn_pad[:, (KV_HEADS + g) * HEAD_DIM:(KV_HEADS + g + 1) * HEAD_DIM]
            dist = jnp.concatenate([t_pos - pos, t - j_new], axis=1)
            ok = jnp.concatenate([in_block & (pos < past), (j_new <= t) & (has_new > 0)], axis=1)
            s = jnp.concatenate([_dot(q4, kv_g[g * HEAD_DIM:(g + 1) * HEAD_DIM]), _dot_nt(q4, k_new)], axis=1)
            s = jnp.where(ok, s + bias_of(dist), NEG)
            e = jnp.where(ok, jnp.exp(s - jnp.max(s, axis=-1, keepdims=True)), 0.0)
            p = (e / jnp.sum(e, axis=-1, keepdims=True)).astype(BF16)
            o_s = (_dot_nt(p[:, :nk], kv_g[(KV_HEADS + g) * HEAD_DIM:(KV_HEADS + g + 1) * HEAD_DIM])
                   + _dot(p[:, nk:], v_new))

            kw_new = kvn_pad[:, (2 * KV_HEADS + g) * HEAD_DIM:(2 * KV_HEADS + g + 1) * HEAD_DIM]
            vw_new = kvn_pad[:, (3 * KV_HEADS + g) * HEAD_DIM:(3 * KV_HEADS + g + 1) * HEAD_DIM]
            dist_w = t_pos - (past - w_len + w_idx)
            ok_w = (dist_w >= 0) & (dist_w <= WINDOW) & (w_idx < w_len + nt)
            s = jnp.concatenate([_dot(q4, win[g * HEAD_DIM:(g + 1) * HEAD_DIM]), _dot_nt(q4, kw_new)], axis=1)
            s = jnp.where(ok_w, s + bias_of(dist_w), NEG)
            e = jnp.where(ok_w, jnp.exp(s - jnp.max(s, axis=-1, keepdims=True)), 0.0)
            p = (e / jnp.sum(e, axis=-1, keepdims=True)).astype(BF16)
            o_w = (_dot_nt(p[:, :w_len], win[(KV_HEADS + g) * HEAD_DIM:(KV_HEADS + g + 1) * HEAD_DIM])
                   + _dot(p[:, w_len:], vw_new))

            for h in range(HPG):
                head = g * HPG + h
                o_c = oc_ref[t:t + 1, head * HEAD_DIM:(head + 1) * HEAD_DIM]
                heads_out.append(gates[t:t + 1, head:head + 1] * o_c
                                 + gates[t:t + 1, N_HEADS + head:N_HEADS + head + 1] * o_s[h:h + 1]
                                 + gates[t:t + 1, 2 * N_HEADS + head:2 * N_HEADS + head + 1] * o_w[h:h + 1])
        rows_out.append(jnp.concatenate(heads_out, axis=1))
    o_ref[...] = jnp.concatenate(rows_out, axis=0)


def nsa_sel_sample(hproj3, o_c, sel_flat, page_table, cache_t, win_t, tab, layer, past):
    nb, nt, _ = hproj3.shape
    w_len = win_t.shape[3]
    assert past % SEL_BLOCK == 0 and PAGE_SIZE % SEL_BLOCK == 0 and w_len % LANE == 0 and past >= w_len
    grid_spec = pltpu.PrefetchScalarGridSpec(
        num_scalar_prefetch=2,
        grid=(nb,),
        in_specs=[pl.BlockSpec((None, nt, ATT_W), lambda b, s, p: (b, 0, C_Q // ATT_W)),
                  pl.BlockSpec((None, nt, LANE), lambda b, s, p: (b, 0, C_NG // LANE)),
                  pl.BlockSpec((None, nt, ATT_W), lambda b, s, p: (b, 0, 0)),
                  pl.BlockSpec((None, nt, KVB_W), lambda b, s, p: (b, 0, C_KVS // KVB_W)),
                  pl.BlockSpec((None, nt, KVB_W), lambda b, s, p: (b, 0, C_KVW // KVB_W)),
                  pl.BlockSpec((None, None, KVB_W, w_len), lambda b, s, p: (layer, b, 0, 0)),
                  pl.BlockSpec((N_HEADS, LANE), lambda b, s, p: (0, 0)),
                  pl.BlockSpec(memory_space=pl.ANY)],
        out_specs=pl.BlockSpec((None, nt, ATT_W), lambda b, s, p: (b, 0, 0)),
        scratch_shapes=[pltpu.VMEM((nt * KV_HEADS, KVB_W, SEL_TOPN * PAGE_SIZE), F32),
                        pltpu.SemaphoreType.DMA(())],
    )
    return pl.pallas_call(
        functools.partial(_nsa_sel_sample_kernel, layer=layer, past=past, w_len=w_len),
        grid_spec=grid_spec,
        out_shape=jax.ShapeDtypeStruct((nb, nt, ATT_W), F32),
        compiler_params=_cparams(("arbitrary",)),
        name="nsa_sel_sample",
    )(sel_flat, page_table, hproj3, hproj3, o_c, hproj3, hproj3, win_t, tab, cache_t)


def _hgrn_sample_kernel(hq_ref, hf_ref, hi_ref, lb_ref, s_ref, o_ref, so_ref):
    nt = hq_ref.shape[0]
    live = (lax.broadcasted_iota(jnp.int32, (ROWS, 1), 0) < nt).astype(F32)
    q, k, logf = _hgrn_gates(_pad_rows(hq_ref[...]), _pad_rows(hf_ref[...]), lb_ref[...])
    q, k, logf = q * live, k * live, logf * live
    v = _pad_rows(hi_ref[...])
    rows = [logf[0:1]]
    for i in range(1, ROWS):
        rows.append(rows[-1] + logf[i:i + 1])
    a = jnp.concatenate(rows, axis=0)
    outs = _hgrn_intra(q, k, v, a, ROWS)
    a_last = a[ROWS - 1:ROWS]
    qe = (q * jnp.exp(a)).astype(BF16)
    kd = (k * jnp.exp(a_last - a)).astype(BF16)
    dec = jnp.exp(a_last)
    o_parts = []
    for h in range(HG_HEADS):
        hs = slice(h * HG_DK, (h + 1) * HG_DK)
        st = s_ref[h].T
        o_parts.append(outs[h] + _dot_nt(qe[:, hs], st.astype(BF16)))
        so_ref[h] = (dec[:, hs] * st + _dot_tn(v[:, hs].astype(BF16), kd[:, hs])).T
    o_ref[...] = jnp.concatenate(o_parts, axis=1)[:nt]


def hgrn_sample(hproj3, lb, state):
    nb, nt, _ = hproj3.shape
    col = lambda c: pl.BlockSpec((None, nt, HGK_W), lambda b: (b, 0, c // HGK_W))
    st_spec = pl.BlockSpec((None, HG_HEADS, HG_DK, HG_DV), lambda b: (b, 0, 0, 0))
    return pl.pallas_call(
        _hgrn_sample_kernel,
        grid=(nb,),
        in_specs=[col(C_HQ), col(C_HF), col(C_HI), pl.BlockSpec((1, HGK_W), lambda b: (0, 0)), st_spec],
        out_specs=[pl.BlockSpec((None, nt, HGK_W), lambda b: (b, 0, 0)), st_spec],
        out_shape=[jax.ShapeDtypeStruct((nb, nt, HGK_W), F32),
                   jax.ShapeDtypeStruct(state.shape, F32)],
        compiler_params=_cparams(("parallel",)),
        name="hgrn_sample",
    )(hproj3, hproj3, hproj3, lb.reshape(1, HGK_W), state)


def reorder_w_in(w):
    o = np.cumsum([0, ATT_W, KVB_W, KVB_W, KVB_W, N_GATE, HGK_W, HGK_W, HGK_W, HGK_W, D_MODEL, D_MODEL])
    q, kvc, kvs, kvw, ng, hq, hf, hi, hg, ga, gh = [w[:, o[i]:o[i + 1]] for i in range(11)]
    pad = jnp.zeros((w.shape[0], N_PROJ - C_NG - N_GATE), w.dtype)
    return jnp.concatenate([q, hq, ga, gh, hf, hi, hg, kvc, kvs, kvw, ng, pad], axis=1).astype(BF16)


def prompt_layer(x, lp, tabs):
    t = x.shape[0]
    tab, ntab, wtab = tabs
    hproj = norm_matmul(x, lp["attn_g"], lp["w_in"], min(512, t), 1408)
    y = cproj_rows(hproj, C_KVC // KVB_W, lp["wbig"], 2048)
    ckv = cmp_mlp(y[None], lp["pe_term"], lp["w2big"])[0]
    o_c, mask = nsa_cmp_prompt(hproj, ckv, tab, t // CMP_STRIDE - 1)
    ka, vt = attention_kv_layout(hproj)
    att = nsa_sel_prompt(hproj, o_c, mask, ka, vt, tab, ntab, wtab)
    o_hg, st = hgrn_prompt(hproj, lp["lb"], 128)
    x, xn = merge(x, att, o_hg, hproj, lp["hg_norm_g"], lp["ffn_g"], lp["wa"], lp["wh"], lp["wo"], 256)
    x = peer(xn, x, lp["peer_wq"], lp["peer_keys"], lp["peer_u"], lp["peer_v"], 256)
    return x, hproj[:, C_KVC:C_KVC + 2 * KVB_W], hproj[:, C_KVW:C_KVW + KVB_W], jnp.swapaxes(st, 1, 2)


def layer_params(l, norm_attn_g, norm_ffn_g, w_in, cmp_pe, cmp_w1, cmp_w2, lower_bounds, hg_norm_g,
                 w_br_attn, w_br_hg, w_out, peer_wq, peer_keys, peer_u, peer_v):
    wbig, w2big, pe_rows = compress_weights(cmp_pe[l], cmp_w1[l], cmp_w2[l])
    return dict(attn_g=norm_attn_g[l], ffn_g=norm_ffn_g[l], w_in=reorder_w_in(w_in[l]),
                wbig=wbig, w2big=w2big, pe_term=pe_term_from(pe_rows, wbig),
                lb=lower_bounds[l], hg_norm_g=hg_norm_g[l],
                wa=w_br_attn[l].astype(BF16), wh=w_br_hg[l].astype(BF16), wo=w_out[l].astype(BF16),
                peer_wq=peer_wq[l].astype(BF16), peer_keys=peer_keys[l].astype(BF16),
                peer_u=peer_u[l].astype(BF16), peer_v=peer_v[l].astype(BF16))


def sample_layer(x, lp, tab, layer, cache_t, win_t, state, page_table, nb, nt):
    past = page_table.shape[1] * PAGE_SIZE
    assert nt < CMP_STRIDE, "new tokens never complete a compression chunk"
    hproj = norm_matmul(x, lp["attn_g"], lp["w_in"], x.shape[0], 1408)
    hproj3 = hproj.reshape(nb, nt, N_PROJ)
    y = cproj_pages(cache_t, layer, page_table, lp["wbig"], 16)
    ckv = cmp_mlp(y, lp["pe_term"], lp["w2big"])
    o_c, sel = nsa_cmp_sample(hproj3, ckv, tab, past // CMP_STRIDE - 1, past)
    sel_flat = jnp.transpose(sel[:, :, :nt, :SEL_TOPN], (0, 2, 1, 3)).reshape(-1)
    att = nsa_sel_sample(hproj3, o_c, sel_flat, page_table, cache_t, win_t, tab, layer, past)
    o_hg, st = hgrn_sample(hproj3, lp["lb"], state)
    x, xn = merge(x, att.reshape(nb * nt, ATT_W), o_hg.reshape(nb * nt, HGK_W), hproj, lp["hg_norm_g"],
                  lp["ffn_g"], lp["wa"], lp["wh"], lp["wo"], 256)
    x = peer(xn, x, lp["peer_wq"], lp["peer_keys"], lp["peer_u"], lp["peer_v"], 256)
    return x, hproj[:, C_KVC:C_KVC + 2 * KVB_W], hproj[:, C_KVW:C_KVW + KVB_W], st


def kernel(x_prompt, x_sample, cache_kv, cache_win, state_hgrn, page_table, norm_attn_g, norm_ffn_g, final_norm_g, w_in, cmp_pe, cmp_w1, cmp_w2, rel_bias, hg_lb_logits, hg_norm_g, w_br_attn, w_br_hg, w_out, peer_wq, peer_keys, peer_u, peer_v):
    depth = w_in.shape[0]
    bp, seq, _ = x_prompt.shape
    nb, nt, _ = x_sample.shape
    assert bp == 1
    p_lb = jax.nn.softmax(hg_lb_logits.astype(F32), axis=0)
    lower_bounds = jnp.cumsum(p_lb, axis=0) - p_lb[0]
    tab = bias_table(rel_bias)
    tabs = (tab, *distance_tables(tab))
    n_pool = cache_kv.shape[1]
    w_buf = cache_win.shape[2]
    cache_t = jnp.transpose(cache_kv, (0, 1, 3, 4, 5, 2)).reshape(depth, n_pool, 2 * KVB_W, PAGE_SIZE)
    win_t = jnp.transpose(cache_win, (0, 1, 3, 4, 5, 2)).reshape(depth, nb, KVB_W, w_buf)
    xp = x_prompt.reshape(seq, D_MODEL)
    xs = x_sample.reshape(nb * nt, D_MODEL)
    kv_p, win_p, hs_p, kv_s, win_s, hs_s = [], [], [], [], [], []
    for l in range(depth):
        lp = layer_params(l, norm_attn_g, norm_ffn_g, w_in, cmp_pe, cmp_w1, cmp_w2, lower_bounds, hg_norm_g,
                          w_br_attn, w_br_hg, w_out, peer_wq, peer_keys, peer_u, peer_v)
        xp, kv, kw, st = prompt_layer(xp, lp, tabs)
        kv_p.append(kv.reshape(bp, seq, 4, KV_HEADS, HEAD_DIM))
        win_p.append(kw[seq - min(WINDOW, seq):].reshape(bp, -1, 2, KV_HEADS, HEAD_DIM))
        hs_p.append(st[None])
        xs, kv, kw, st = sample_layer(xs, lp, tab, l, cache_t, win_t, state_hgrn[l], page_table, nb, nt)
        kv_s.append(kv.reshape(nb, nt, 4, KV_HEADS, HEAD_DIM))
        win_s.append(jnp.concatenate([cache_win[l], kw.reshape(nb, nt, 2, KV_HEADS, HEAD_DIM)], axis=1)[:, nt:])
        hs_s.append(st)
    y_prompt = rmsnorm(xp, final_norm_g, min(512, seq)).reshape(bp, seq, D_MODEL)
    y_sample = rmsnorm(xs, final_norm_g, nb * nt).reshape(nb, nt, D_MODEL)
    return (y_prompt, y_sample, jnp.stack(kv_p), jnp.stack(win_p), jnp.stack(hs_p),
            jnp.stack(kv_s), jnp.stack(win_s), jnp.stack(hs_s))
```

```python
import functools
import math

import jax
import jax.numpy as jnp
import numpy as np
from jax import lax
from jax.experimental import pallas as pl
from jax.experimental.pallas import tpu as pltpu

F32 = jnp.float32
BF16 = jnp.bfloat16

D_MODEL = 1024
PAGE_SIZE = 128
N_HEADS = 8
HEAD_DIM = 64
KV_HEADS = 2
HPG = N_HEADS // KV_HEADS
CMP_STRIDE = 16
CMP_BLOCK = 32
SEL_BLOCK = 64
SEL_TOPN = 16
WINDOW = 512
ATT_SCALE = HEAD_DIM ** -0.5
N_BUCKETS = 32
MAX_DISTANCE = 128
HG_HEADS = 4
HG_DK = 128
HG_DV = 128
PEER_HEADS = 8
PEER_NKEYS = 128
PEER_DH = 128
PEER_TOPK = 16
ATT_W = N_HEADS * HEAD_DIM
KVB_W = 2 * KV_HEADS * HEAD_DIM
HGK_W = HG_HEADS * HG_DK
N_GATE = 3 * N_HEADS
EPS = 1e-6
NEG = -1e30
BIG = 1e6
TINY = 1e-30
NEG_INF = float("-inf")

LANE = 128
VMEM_LIMIT = 56 * 1024 * 1024

C_Q, C_HQ, C_GA, C_GH, C_HF, C_HI, C_HG = 0, 512, 1024, 2048, 3072, 3584, 4096
C_KVC, C_KVS, C_KVW, C_NG, N_PROJ = 4608, 4864, 5120, 5376, 5632
SUB = 16
TQ = 128
TK = 512


def _cparams(sem, vmem=VMEM_LIMIT):
    return pltpu.CompilerParams(dimension_semantics=sem, vmem_limit_bytes=vmem)


def _dot(a, b):
    return jnp.dot(a, b, preferred_element_type=F32)


def _dot_nt(a, b):
    return lax.dot_general(a, b, (((1,), (1,)), ((), ())), preferred_element_type=F32)


def _dot_tn(a, b):
    return lax.dot_general(a, b, (((0,), (0,)), ((), ())), preferred_element_type=F32)


def _sigmoid(x):
    return 1.0 / (1.0 + jnp.exp(-x))


def _gelu(x):
    return 0.5 * x * (1.0 + lax.erf(x * (2.0 ** -0.5)))


def _norm_matmul_kernel(x_ref, g_ref, w_ref, o_ref, xn_ref):
    @pl.when(pl.program_id(1) == 0)
    def _():
        x = x_ref[...]
        y = x * lax.rsqrt(jnp.mean(x * x, axis=-1, keepdims=True) + EPS)
        xn_ref[...] = (y * g_ref[...]).astype(BF16)

    o_ref[...] = _dot(xn_ref[...], w_ref[...])


def norm_matmul(x, g, w, tm, tn):
    m, k = x.shape
    n = w.shape[1]
    return pl.pallas_call(
        _norm_matmul_kernel,
        grid=(m // tm, n // tn),
        in_specs=[pl.BlockSpec((tm, k), lambda i, j: (i, 0)),
                  pl.BlockSpec((1, k), lambda i, j: (0, 0)),
                  pl.BlockSpec((k, tn), lambda i, j: (0, j))],
        out_specs=pl.BlockSpec((tm, tn), lambda i, j: (i, j)),
        out_shape=jax.ShapeDtypeStruct((m, n), F32),
        scratch_shapes=[pltpu.VMEM((tm, k), BF16)],
        compiler_params=_cparams(("parallel", "arbitrary")),
        name="norm_matmul",
    )(x, g.reshape(1, k), w)


def _matmul_kernel(x_ref, w_ref, o_ref):
    o_ref[...] = _dot(x_ref[...], w_ref[...])


def matmul(x, w, tm, tn):
    m, k = x.shape
    n = w.shape[1]
    return pl.pallas_call(
        _matmul_kernel,
        grid=(m // tm, n // tn),
        in_specs=[pl.BlockSpec((tm, k), lambda i, j: (i, 0)),
                  pl.BlockSpec((k, tn), lambda i, j: (0, j))],
        out_specs=pl.BlockSpec((tm, tn), lambda i, j: (i, j)),
        out_shape=jax.ShapeDtypeStruct((m, n), F32),
        compiler_params=_cparams(("parallel", "arbitrary")),
        name="matmul",
    )(x, w)


def _rmsnorm_kernel(x_ref, g_ref, o_ref):
    x = x_ref[...]
    o_ref[...] = x * lax.rsqrt(jnp.mean(x * x, axis=-1, keepdims=True) + EPS) * g_ref[...]


def rmsnorm(x, g, tm):
    m, k = x.shape
    return pl.pallas_call(
        _rmsnorm_kernel,
        grid=(m // tm,),
        in_specs=[pl.BlockSpec((tm, k), lambda i: (i, 0)), pl.BlockSpec((1, k), lambda i: (0, 0))],
        out_specs=pl.BlockSpec((tm, k), lambda i: (i, 0)),
        out_shape=jax.ShapeDtypeStruct((m, k), F32),
        compiler_params=_cparams(("parallel",)),
        name="rmsnorm",
    )(x, g.reshape(1, k))


def _cproj_from_pieces(piece_fn, w_ref, y_ref):
    acc = jnp.zeros(y_ref.shape, F32)
    for s in range(CMP_STRIDE):
        acc = acc + _dot(piece_fn(s).astype(BF16), w_ref[s])
    y_ref[...] = acc


def _cproj_rows_kernel(r0_ref, r1_ref, w_ref, y_ref):
    nch = y_ref.shape[0]

    def piece(s):
        return jnp.concatenate([r[pl.ds(s, nch, stride=CMP_STRIDE), :] for r in (r0_ref, r1_ref)], axis=1)

    _cproj_from_pieces(piece, w_ref, y_ref)


def cproj_rows(rows_arr, col_block, wbig, rb):
    t = rows_arr.shape[0]
    rb = min(rb, t)
    half = lambda k: pl.BlockSpec((rb, LANE), lambda i: (i, 2 * col_block + k))
    return pl.pallas_call(
        _cproj_rows_kernel,
        grid=(t // rb,),
        in_specs=[half(0), half(1), pl.BlockSpec(wbig.shape, lambda i: (0, 0, 0))],
        out_specs=pl.BlockSpec((rb // CMP_STRIDE, 2 * KVB_W), lambda i: (i, 0)),
        out_shape=jax.ShapeDtypeStruct((t // CMP_STRIDE, 2 * KVB_W), F32),
        compiler_params=_cparams(("parallel",)),
        name="cproj_rows",
    )(rows_arr, rows_arr, wbig)


def _cproj_pages_kernel(pt_ref, *refs):
    del pt_ref
    pages, w_ref, y_ref, lo_ref, hi_ref = refs[:-4], refs[-4], refs[-3], refs[-2], refs[-1]
    nch = y_ref.shape[0]
    for p, page in enumerate(pages):
        rows = pl.ds(p * PAGE_SIZE, PAGE_SIZE)
        lo_ref[rows, :] = page[0:LANE, :].T
        hi_ref[rows, :] = page[LANE:2 * LANE, :].T

    def piece(s):
        return jnp.concatenate([r[pl.ds(s, nch, stride=CMP_STRIDE), :] for r in (lo_ref, hi_ref)], axis=1)

    _cproj_from_pieces(piece, w_ref, y_ref)


def cproj_pages(cache_t, layer, page_table, wbig, pg):
    nb, npages = page_table.shape
    per = PAGE_SIZE // CMP_STRIDE

    def page_spec(p):
        return pl.BlockSpec((None, None, KVB_W, PAGE_SIZE), lambda b, j, pt: (layer, pt[b, j * pg + p], 0, 0))

    grid_spec = pltpu.PrefetchScalarGridSpec(
        num_scalar_prefetch=1,
        grid=(nb, npages // pg),
        in_specs=[page_spec(p) for p in range(pg)] + [pl.BlockSpec(wbig.shape, lambda b, j, pt: (0, 0, 0))],
        out_specs=pl.BlockSpec((None, pg * per, 2 * KVB_W), lambda b, j, pt: (b, j, 0)),
        scratch_shapes=[pltpu.VMEM((pg * PAGE_SIZE, LANE), F32)] * 2,
    )
    return pl.pallas_call(
        _cproj_pages_kernel,
        grid_spec=grid_spec,
        out_shape=jax.ShapeDtypeStruct((nb, npages * per, 2 * KVB_W), F32),
        compiler_params=_cparams(("parallel", "arbitrary")),
        name="cproj_pages",
    )(page_table, *([cache_t] * pg), wbig)


def _cmp_mlp_kernel(y_ref, pe_ref, w2_ref, o_ref):
    y = y_ref[...]
    n = y.shape[0]
    nxt = pltpu.roll(y[:, KVB_W:], n - 1, 0)
    hsum = y[:, :KVB_W] + nxt + pe_ref[...]
    o_ref[...] = _dot(_gelu(hsum).astype(BF16), w2_ref[...]).astype(o_ref.dtype)


def cmp_mlp(y, pe_term, w2big):
    b, ncp, _ = y.shape
    return pl.pallas_call(
        _cmp_mlp_kernel,
        grid=(b,),
        in_specs=[pl.BlockSpec((None, ncp, 2 * KVB_W), lambda i: (i, 0, 0)),
                  pl.BlockSpec((1, KVB_W), lambda i: (0, 0)),
                  pl.BlockSpec((KVB_W, KVB_W), lambda i: (0, 0))],
        out_specs=pl.BlockSpec((None, ncp, KVB_W), lambda i: (i, 0, 0)),
        out_shape=jax.ShapeDtypeStruct((b, ncp, KVB_W), BF16),
        compiler_params=_cparams(("parallel",)),
        name="cmp_mlp",
    )(y, pe_term, w2big)


def compress_weights(pe, w1, w2):
    w1r = w1.reshape(2, 2, CMP_STRIDE, HEAD_DIM, HEAD_DIM)
    eye = jnp.eye(2, dtype=F32)
    wbig = jnp.einsum('crsdh,cC,gG->scgdrCGh', w1r, eye, eye).reshape(CMP_STRIDE, KVB_W, 2 * KVB_W)
    w2big = jnp.einsum('chd,cC,gG->cghCGd', w2, eye, eye).reshape(KVB_W, KVB_W)
    per = pe.reshape(2, 2, CMP_STRIDE, HEAD_DIM)
    pe_rows = jnp.broadcast_to(jnp.transpose(per, (1, 2, 0, 3))[:, :, :, None, :],
                               (2, CMP_STRIDE, 2, KV_HEADS, HEAD_DIM)).reshape(2 * CMP_STRIDE, KVB_W)
    pe_rows = jnp.pad(pe_rows, ((0, 8 * CMP_STRIDE - 2 * CMP_STRIDE), (0, 0)))
    return wbig.astype(BF16), w2big.astype(BF16), pe_rows


def pe_term_from(pe_rows, wbig):
    ype = cproj_rows(pe_rows, 0, wbig, pe_rows.shape[0])
    return ype[0:1, :KVB_W] + ype[1:2, KVB_W:]


def bias_table(rel_bias):
    n = np.arange(LANE)
    exact = N_BUCKETS // 2
    nf = np.maximum(n, 1).astype(np.float32)
    large = exact + (np.log(nf / np.float32(exact)) / np.float32(math.log(MAX_DISTANCE / exact))
                     * np.float32(N_BUCKETS - exact)).astype(np.int32)
    bucket = np.where(n < exact, n, np.minimum(large, N_BUCKETS - 1))
    assert bucket[LANE - 1] == N_BUCKETS - 1
    return jnp.transpose(rel_bias.astype(F32)[bucket], (1, 0))


def _bias_lookup(tab_row, dist_clipped):
    r, w = dist_clipped.shape
    src = jnp.broadcast_to(tab_row, (r, LANE))
    parts = [jnp.take_along_axis(src, dist_clipped[:, j * LANE:(j + 1) * LANE], axis=1)
             for j in range(w // LANE)]
    return parts[0] if len(parts) == 1 else jnp.concatenate(parts, axis=1)


def _topk_mask(x, k):
    r, n = x.shape
    col = lax.broadcasted_iota(jnp.int32, (r, n), 1)
    sel = jnp.zeros((r, n), F32)
    for _ in range(k):
        m = jnp.max(x, axis=-1, keepdims=True)
        idx = jnp.min(jnp.where(x == m, col, n), axis=-1, keepdims=True)
        hit = col == idx
        sel = jnp.where(hit, 1.0, sel)
        x = jnp.where(hit, NEG_INF, x)
    return sel


def _topk_idx(x, k):
    r, n = x.shape
    col = lax.broadcasted_iota(jnp.int32, (r, n), 1)
    lane = lax.broadcasted_iota(jnp.int32, (r, LANE), 1)
    out = jnp.zeros((r, LANE), jnp.int32)
    for it in range(k):
        m = jnp.max(x, axis=-1, keepdims=True)
        idx = jnp.min(jnp.where(x == m, col, n), axis=-1, keepdims=True)
        out = jnp.where(lane == it, idx, out)
        x = jnp.where(col == idx, NEG_INF, x)
    return out


def _cmp_branch(q, t_pos, ckv_ref, tab_ref, cover_ref, nc, ncols=None):
    r = q.shape[0]
    ncols = ckv_ref.shape[0] if ncols is None else ncols
    n_sel = cover_ref.shape[1]
    c_idx = lax.broadcasted_iota(jnp.int32, (1, ncols), 1)
    dist = t_pos - (c_idx * CMP_STRIDE + CMP_BLOCK - 1)
    valid = (dist >= 0) & (c_idx < nc)
    any_valid = (t_pos >= CMP_BLOCK - 1).astype(F32)
    dcl = jnp.clip(dist, 0, LANE - 1)
    blk = lax.broadcasted_iota(jnp.int32, (1, n_sel), 1)
    cur = t_pos // SEL_BLOCK
    forced = (blk == 0) | (blk == cur) | (blk == cur - 1)
    outs, imps = [], []
    for g in range(KV_HEADS):
        kc = ckv_ref[:ncols, g * HEAD_DIM:(g + 1) * HEAD_DIM]
        vc = ckv_ref[:ncols, (KV_HEADS + g) * HEAD_DIM:(KV_HEADS + g + 1) * HEAD_DIM]
        psum = jnp.zeros((r, ncols), F32)
        for h in range(HPG):
            head = g * HPG + h
            qh = (q[:, head * HEAD_DIM:(head + 1) * HEAD_DIM] * ATT_SCALE).astype(BF16)
            s = _dot_nt(qh, kc) + _bias_lookup(tab_ref[head:head + 1, :], dcl)
            s = jnp.where(valid, s, NEG)
            e = jnp.exp(s - jnp.max(s, axis=-1, keepdims=True))
            p = e * (any_valid / jnp.sum(e, axis=-1, keepdims=True))
            outs.append(_dot(p.astype(BF16), vc))
            psum = psum + p
        p_hi = psum.astype(BF16)
        p_lo = (psum - p_hi.astype(F32)).astype(BF16)
        imp = _dot(p_hi, cover_ref[:ncols]) + _dot(p_lo, cover_ref[:ncols])
        imps.append(jnp.where(forced, BIG, jnp.where(blk <= cur, imp, -BIG)))
    return outs, imps


def _topk_mask_cols(x, k):
    n, t = x.shape
    row = lax.broadcasted_iota(jnp.int32, (n, t), 0)
    forced = x == BIG
    sel = forced.astype(F32)
    x = jnp.where(forced, NEG_INF, x)
    for _ in range(k - 3):
        m = jnp.max(x, axis=0, keepdims=True)
        idx = jnp.min(jnp.where(x == m, row, n), axis=0, keepdims=True)
        hit = row == idx
        sel = jnp.where(hit, 1.0, sel)
        x = jnp.where(hit, NEG_INF, x)
    return sel


def _nsa_cmp_kernel(q_ref, ckv_ref, tab_ref, cover_ref, oc_ref, mask_ref, *, nc, n_sel):
    i = pl.program_id(0)
    t_pos = i * TQ + lax.broadcasted_iota(jnp.int32, (TQ, 1), 0)
    ncp = ckv_ref.shape[0]
    part = max(ncp // 4, LANE)
    last_c = (i * TQ + TQ - CMP_BLOCK) // CMP_STRIDE
    need = jnp.clip(last_c // part, 0, ncp // part - 1)

    for n_parts in range(1, ncp // part + 1):
        @pl.when(need == n_parts - 1)
        def _(n_parts=n_parts):
            outs, imps = _cmp_branch(q_ref[...], t_pos, ckv_ref, tab_ref, cover_ref, nc, n_parts * part)
            for g in range(KV_HEADS):
                mask_ref[g] = _topk_mask_cols(imps[g].T, min(SEL_TOPN, n_sel))
            oc_ref[...] = jnp.concatenate(outs, axis=1)


def cover_matrix(ncp, n_sel):
    ci = np.arange(ncp)[:, None] * CMP_STRIDE
    sj = np.arange(n_sel)[None, :] * SEL_BLOCK
    return jnp.asarray(((ci < sj + SEL_BLOCK) & (ci + CMP_BLOCK > sj)).astype(np.float32), dtype=BF16)


def nsa_cmp_prompt(hproj, ckv, tab, nc):
    t = hproj.shape[0]
    ncp = ckv.shape[0]
    n_sel = t // SEL_BLOCK
    cover = cover_matrix(ncp, n_sel)
    return pl.pallas_call(
        functools.partial(_nsa_cmp_kernel, nc=nc, n_sel=n_sel),
        grid=(t // TQ,),
        in_specs=[pl.BlockSpec((TQ, ATT_W), lambda i: (i, C_Q // ATT_W)),
                  pl.BlockSpec((ncp, KVB_W), lambda i: (0, 0)),
                  pl.BlockSpec((N_HEADS, LANE), lambda i: (0, 0)),
                  pl.BlockSpec((ncp, n_sel), lambda i: (0, 0))],
        out_specs=[pl.BlockSpec((TQ, ATT_W), lambda i: (i, 0)),
                   pl.BlockSpec((KV_HEADS, n_sel, TQ), lambda i: (0, 0, i))],
        out_shape=[jax.ShapeDtypeStruct((t, ATT_W), F32),
                   jax.ShapeDtypeStruct((KV_HEADS, n_sel, t), F32)],
        compiler_params=_cparams(("parallel",)),
        name="nsa_cmp_prompt",
    )(hproj, ckv, tab, cover)


MASKED = -1e9
NEAR = 2 * TQ
W_KEYS = WINDOW + TQ
BLK_PER_TILE = TK // SEL_BLOCK


VT_ROWS = HEAD_DIM + 16
MASK_ROWS = 16
HT = HPG * TQ
FLAG_LANE = HEAD_DIM + BLK_PER_TILE


def _online_step_t(carry, st, shift, vt):
    m, acc = carry
    m_new = jnp.maximum(m, jnp.max(st, axis=0, keepdims=True) + shift)
    p = jnp.exp(st - (m_new - shift)).astype(BF16)
    return m_new, jnp.exp(m - m_new) * acc + _dot(vt, p)


def _nsa_sel_kernel(q_ref, ng_ref, oc_ref, mask_ref, ka_ref, vt_ref, tab_ref, ntab_ref, wtab_ref, o_ref,
                    sta_ref, stb_ref, *, n_sel):
    i = pl.program_id(0)
    qt = (q_ref[...] * ATT_SCALE).T
    oct_ = oc_ref[...].T
    gt = _sigmoid(ng_ref[...]).T
    near_blk0 = 2 * i - 2
    n_far = (jnp.maximum(i - 1, 0) * TQ + TK - 1) // TK
    blk8 = lax.broadcasted_iota(jnp.int32, (BLK_PER_TILE, 1), 0)
    init = (jnp.full((1, HT), NEG, F32), jnp.zeros((VT_ROWS, HT), F32))
    zero_shift = jnp.zeros((1, HT), F32)
    chunks_per_tile = TK // LANE

    before_start = jnp.where(lax.broadcasted_iota(jnp.int32, (MASK_ROWS, HT), 0) == FLAG_LANE - HEAD_DIM,
                             MASKED, 0.0).astype(BF16)
    qtg, w_plain, far_shift = [], [], []
    for g in range(KV_HEADS):
        heads = range(g * HPG, (g + 1) * HPG)
        qg = jnp.concatenate([qt[h * HEAD_DIM:(h + 1) * HEAD_DIM] for h in heads], axis=1).astype(BF16)
        qtg.append(qg)
        w_plain.append(jnp.concatenate([qg, before_start, jnp.zeros((LANE - HEAD_DIM - MASK_ROWS, HT), BF16)], axis=0))
        far_shift.append(jnp.concatenate(
            [jnp.broadcast_to(tab_ref[h:h + 1, LANE - 1:LANE], (1, TQ)) for h in heads], axis=1))

    def vt_tile(c0, n, kind):
        return jnp.concatenate([vt_ref[c0 + c, kind] for c in range(n)], axis=1)

    last_tile = (ka_ref.shape[0] - WINDOW) // TK - 1

    def far_logits(kt, dst_ref):
        kt = jnp.minimum(kt, last_tile)
        k0 = pl.multiple_of(WINDOW + kt * TK, TK)
        r0 = pl.multiple_of(kt * BLK_PER_TILE, BLK_PER_TILE)
        for g in range(KV_HEADS):
            mt = mask_ref[g, pl.ds(r0, BLK_PER_TILE), :]
            keep = (mt > 0.5) & (kt * BLK_PER_TILE + blk8 < near_blk0)
            mneg = jnp.where(keep, 0.0, MASKED)
            m16 = jnp.concatenate([jnp.concatenate([mneg] * HPG, axis=1),
                                   jnp.zeros((MASK_ROWS - BLK_PER_TILE, HT), F32)], axis=0).astype(BF16)
            w = jnp.concatenate([qtg[g], m16, jnp.zeros((LANE - HEAD_DIM - MASK_ROWS, HT), BF16)], axis=0)
            dst_ref[g] = _dot(ka_ref[pl.ds(k0, TK), g * LANE:(g + 1) * LANE], w)

    def far_softmax(kt, src_ref, carry):
        c0 = (WINDOW // LANE) + jnp.minimum(kt, last_tile) * chunks_per_tile
        return tuple(_online_step_t(carry[g], src_ref[g], far_shift[g], vt_tile(c0, chunks_per_tile, g))
                     for g in range(KV_HEADS))

    @pl.when(n_far > 0)
    def _():
        far_logits(0, sta_ref)

    def far_body(k2, carry):
        far_logits(2 * k2 + 1, stb_ref)
        carry = far_softmax(2 * k2, sta_ref, carry)
        far_logits(2 * k2 + 2, sta_ref)
        return far_softmax(2 * k2 + 1, stb_ref, carry)

    carry = lax.fori_loop(0, (n_far + 1) // 2, far_body, (init, init))

    kn0 = pl.multiple_of(WINDOW + (i - 1) * TQ, TQ)
    base = pl.multiple_of(jnp.clip((near_blk0 // BLK_PER_TILE) * BLK_PER_TILE, 0, n_sel - MASK_ROWS), BLK_PER_TILE)
    key_blk = near_blk0 + lax.broadcasted_iota(jnp.int32, (NEAR, MASK_ROWS), 0) // SEL_BLOCK
    expand = (key_blk == base + lax.broadcasted_iota(jnp.int32, (NEAR, MASK_ROWS), 1)).astype(BF16)
    kw0 = pl.multiple_of(i * TQ, TQ)
    rows_t = []
    st_near, st_win = [], []
    for g in range(KV_HEADS):
        mneg = jnp.where(mask_ref[g, pl.ds(base, MASK_ROWS), :] > 0.5, 0.0, MASKED).astype(BF16)
        sel_add = _dot(expand, mneg)
        st = _dot(ka_ref[pl.ds(kn0, NEAR), g * LANE:(g + 1) * LANE], w_plain[g])
        st_near.append(st + ntab_ref[g] + jnp.concatenate([sel_add] * HPG, axis=1))
        st = _dot(ka_ref[pl.ds(kw0, W_KEYS), (KV_HEADS + g) * LANE:(KV_HEADS + g + 1) * LANE], w_plain[g])
        st_win.append(st + wtab_ref[g])
    for g in range(KV_HEADS):
        _, acc = _online_step_t(carry[g], st_near[g], zero_shift, vt_tile(kn0 // LANE, NEAR // LANE, g))
        os_t = acc[:HEAD_DIM] / acc[HEAD_DIM:HEAD_DIM + 1]
        _, acc = _online_step_t(init, st_win[g], zero_shift, vt_tile(kw0 // LANE, W_KEYS // LANE, KV_HEADS + g))
        ow_t = acc[:HEAD_DIM] / acc[HEAD_DIM:HEAD_DIM + 1]

        for h in range(HPG):
            head = g * HPG + h
            ls = slice(h * TQ, (h + 1) * TQ)
            rows_t.append(gt[head:head + 1] * oct_[head * HEAD_DIM:(head + 1) * HEAD_DIM]
                          + gt[N_HEADS + head:N_HEADS + head + 1] * os_t[:, ls]
                          + gt[2 * N_HEADS + head:2 * N_HEADS + head + 1] * ow_t[:, ls])
    o_ref[...] = jnp.concatenate(rows_t, axis=0).T


def _distance_tables_kernel(tab_ref, ntab_ref, wtab_ref):
    head = pl.program_id(0) * HPG + pl.program_id(1)
    row = tab_ref[pl.ds(head, 1), :]
    for out_ref, start, max_dist in ((ntab_ref, TQ, None), (wtab_ref, WINDOW, WINDOW)):
        keys = out_ref.shape[0]
        dist = (start + lax.broadcasted_iota(jnp.int32, (keys, TQ), 1)
                - lax.broadcasted_iota(jnp.int32, (keys, TQ), 0))
        ok = dist >= 0 if max_dist is None else (dist >= 0) & (dist <= max_dist)
        out_ref[...] = jnp.where(ok, _bias_lookup(row, jnp.clip(dist, 0, LANE - 1)), NEG)


def distance_tables(tab):
    return pl.pallas_call(
        _distance_tables_kernel,
        grid=(KV_HEADS, HPG),
        in_specs=[pl.BlockSpec((N_HEADS, LANE), lambda g, h: (0, 0))],
        out_specs=[pl.BlockSpec((None, NEAR, TQ), lambda g, h: (g, 0, h)),
                   pl.BlockSpec((None, W_KEYS, TQ), lambda g, h: (g, 0, h))],
        out_shape=[jax.ShapeDtypeStruct((KV_HEADS, NEAR, HT), F32),
                   jax.ShapeDtypeStruct((KV_HEADS, W_KEYS, HT), F32)],
        compiler_params=_cparams(("parallel", "parallel")),
        name="distance_tables",
    )(tab)


def attention_kv_layout(hproj):
    t = hproj.shape[0]
    grp = lambda c0, g: hproj[:, c0 + g * HEAD_DIM:c0 + (g + 1) * HEAD_DIM]
    e8 = jnp.asarray(np.eye(BLK_PER_TILE, dtype=np.float32)[(np.arange(t) % TK) // SEL_BLOCK])
    z = lambda w: jnp.zeros((t, w), F32)
    ka = [jnp.concatenate([grp(C_KVS, g), e8, z(LANE - HEAD_DIM - BLK_PER_TILE)], axis=1) for g in range(KV_HEADS)]
    ka += [jnp.concatenate([grp(C_KVW, g), z(LANE - HEAD_DIM)], axis=1) for g in range(KV_HEADS)]
    lead = jnp.zeros((WINDOW, LANE), F32).at[:, FLAG_LANE].set(1.0)
    ka = jnp.concatenate([jnp.tile(lead, (1, 2 * KV_HEADS)), jnp.concatenate(ka, axis=1)], axis=0).astype(BF16)
    ones = jnp.ones((t, 1), F32)
    vts = []
    for c0 in (C_KVS, C_KVW):
        for g in range(KV_HEADS):
            v = jnp.concatenate([grp(c0 + KV_HEADS * HEAD_DIM, g), ones, z(VT_ROWS - HEAD_DIM - 1)], axis=1)
            v = jnp.pad(v.astype(BF16), ((WINDOW, 0), (0, 0))).reshape(-1, LANE, VT_ROWS)
            vts.append(jnp.swapaxes(v, 1, 2))
    return ka, jnp.stack(vts, axis=1)


def nsa_sel_prompt(hproj, o_c, mask, ka, vt, tab, ntab, wtab):
    t = hproj.shape[0]
    n_sel = mask.shape[1]
    assert t % TK == 0 and n_sel % BLK_PER_TILE == 0 and n_sel >= MASK_ROWS
    const = lambda a: pl.BlockSpec(a.shape, lambda i: (0,) * a.ndim, pipeline_mode=pl.Buffered(1))
    return pl.pallas_call(
        functools.partial(_nsa_sel_kernel, n_sel=n_sel),
        grid=(t // TQ,),
        in_specs=[pl.BlockSpec((TQ, ATT_W), lambda i: (i, C_Q // ATT_W)),
                  pl.BlockSpec((TQ, LANE), lambda i: (i, C_NG // LANE)),
                  pl.BlockSpec((TQ, ATT_W), lambda i: (i, 0)),
                  pl.BlockSpec((KV_HEADS, n_sel, TQ), lambda i: (0, 0, i)),
                  const(ka), const(vt), const(tab), const(ntab), const(wtab)],
        out_specs=pl.BlockSpec((TQ, ATT_W), lambda i: (i, 0)),
        out_shape=jax.ShapeDtypeStruct((t, ATT_W), F32),
        scratch_shapes=[pltpu.VMEM((KV_HEADS, TK, HT), F32)] * 2,
        compiler_params=_cparams(("parallel",)),
        name="nsa_sel_prompt",
    )(hproj, hproj, o_c, mask, ka, vt, tab, ntab, wtab)


def _hgrn_gates(hq, hf, lb):
    sig = _sigmoid(hf)
    forget = lb + (1.0 - lb) * sig
    logf = jnp.log(jnp.maximum(forget, TINY))
    k = (1.0 - lb) * (1.0 - sig)
    q = hq * _sigmoid(hq)
    return q, k, logf


def _hgrn_intra(q, k, v, a, sub):
    n = q.shape[0]
    row = lax.broadcasted_iota(jnp.int32, (n, 1), 0) % sub
    outs = [jnp.zeros((n, HG_DV), F32) for _ in range(HG_HEADS)]
    for d in range(sub):
        ks = k if d == 0 else pltpu.roll(k, d, 0)
        a_s = a if d == 0 else pltpu.roll(a, d, 0)
        vs = v if d == 0 else pltpu.roll(v, d, 0)
        msk = row >= d
        w = q * ks * jnp.exp(jnp.where(msk, a - a_s, 0.0)) * msk.astype(F32)
        for h in range(HG_HEADS):
            r = jnp.sum(w[:, h * HG_DK:(h + 1) * HG_DK], axis=-1, keepdims=True)
            outs[h] = outs[h] + r * vs[:, h * HG_DV:(h + 1) * HG_DV]
    return outs


def _hgrn_prompt_kernel(hq_ref, hf_ref, hi_ref, lb_ref, tri_ref, o_ref, st_out_ref, st_ref):
    step = pl.program_id(0)

    @pl.when(step == 0)
    def _():
        st_ref[...] = jnp.zeros(st_ref.shape, F32)

    n = hq_ref.shape[0]
    q, k, logf = _hgrn_gates(hq_ref[...], hf_ref[...], lb_ref[...])
    v = hi_ref[...]
    a = jnp.dot(tri_ref[...], logf, preferred_element_type=F32, precision=lax.Precision.HIGHEST)
    outs = _hgrn_intra(q, k, v, a, SUB)
    rows_out = [[] for _ in range(HG_HEADS)]
    for c in range(n // SUB):
        sl = slice(c * SUB, (c + 1) * SUB)
        a_c = a[sl]
        a_last = a_c[SUB - 1:SUB]
        qe = (q[sl] * jnp.exp(a_c)).astype(BF16)
        kd = (k[sl] * jnp.exp(a_last - a_c)).astype(BF16)
        dec = jnp.exp(a_last)
        for h in range(HG_HEADS):
            hs = slice(h * HG_DK, (h + 1) * HG_DK)
            st = st_ref[h]
            rows_out[h].append(outs[h][sl] + _dot_nt(qe[:, hs], st.astype(BF16)))
            st_ref[h] = dec[:, hs] * st + _dot_tn(v[sl, hs].astype(BF16), kd[:, hs])
    o_ref[...] = jnp.concatenate([jnp.concatenate(r, axis=0) for r in rows_out], axis=1)
    st_out_ref[...] = st_ref[...]


def hgrn_prompt(hproj, lb, tc):
    t = hproj.shape[0]
    tc = min(tc, t)
    r = np.arange(tc)
    tri = jnp.asarray(((r[:, None] >= r[None, :]) & (r[:, None] // SUB == r[None, :] // SUB)).astype(np.float32))
    return pl.pallas_call(
        _hgrn_prompt_kernel,
        grid=(t // tc,),
        in_specs=[pl.BlockSpec((tc, HGK_W), lambda i: (i, C_HQ // HGK_W)),
                  pl.BlockSpec((tc, HGK_W), lambda i: (i, C_HF // HGK_W)),
                  pl.BlockSpec((tc, HGK_W), lambda i: (i, C_HI // HGK_W)),
                  pl.BlockSpec((1, HGK_W), lambda i: (0, 0)),
                  pl.BlockSpec((tc, tc), lambda i: (0, 0))],
        out_specs=[pl.BlockSpec((tc, HGK_W), lambda i: (i, 0)),
                   pl.BlockSpec((HG_HEADS, HG_DV, HG_DK), lambda i: (0, 0, 0))],
        out_shape=[jax.ShapeDtypeStruct((t, HGK_W), F32),
                   jax.ShapeDtypeStruct((HG_HEADS, HG_DV, HG_DK), F32)],
        scratch_shapes=[pltpu.VMEM((HG_HEADS, HG_DV, HG_DK), F32)],
        compiler_params=_cparams(("arbitrary",)),
        name="hgrn_prompt",
    )(hproj, hproj, hproj, lb.reshape(1, HGK_W), tri)


def _merge_kernel(x_ref, att_ref, o_ref, hg_ref, ga_ref, gh_ref, ng_ref, fg_ref, wa_ref, wh_ref, wo_ref,
                  xo_ref, xn_ref):
    o = o_ref[...]
    parts = []
    for h in range(HG_HEADS):
        oh = o[:, h * HG_DV:(h + 1) * HG_DV]
        parts.append(oh * lax.rsqrt(jnp.mean(oh * oh, axis=-1, keepdims=True) + EPS))
    hg = hg_ref[...]
    hgo = jnp.concatenate(parts, axis=1) * ng_ref[...] * (hg * _sigmoid(hg))
    m = (_sigmoid(ga_ref[...]) * _dot(att_ref[...].astype(BF16), wa_ref[...])
         + _sigmoid(gh_ref[...]) * _dot(hgo.astype(BF16), wh_ref[...]))
    x = x_ref[...] + _dot(m.astype(BF16), wo_ref[...])
    xo_ref[...] = x
    y = x * lax.rsqrt(jnp.mean(x * x, axis=-1, keepdims=True) + EPS)
    xn_ref[...] = (y * fg_ref[...]).astype(BF16)


def merge(x, att, o, hproj, hg_norm_g, ffn_g, wa, wh, wo, tm):
    t = x.shape[0]
    tm = min(tm, t)
    row = lambda w, cb: pl.BlockSpec((tm, w), lambda i: (i, cb))
    full = lambda a: pl.BlockSpec(a.shape, lambda i: (0, 0))
    hg_norm_g = hg_norm_g.reshape(1, HGK_W)
    ffn_g = ffn_g.reshape(1, D_MODEL)
    return pl.pallas_call(
        _merge_kernel,
        grid=(t // tm,),
        in_specs=[row(D_MODEL, 0), row(ATT_W, 0), row(HGK_W, 0), row(HGK_W, C_HG // HGK_W),
                  row(D_MODEL, C_GA // D_MODEL), row(D_MODEL, C_GH // D_MODEL),
                  full(hg_norm_g), full(ffn_g), full(wa), full(wh), full(wo)],
        out_specs=[row(D_MODEL, 0), row(D_MODEL, 0)],
        out_shape=[jax.ShapeDtypeStruct((t, D_MODEL), F32), jax.ShapeDtypeStruct((t, D_MODEL), BF16)],
        compiler_params=_cparams(("parallel",)),
        name="merge",
    )(x, att, o, hproj, hproj, hproj, hg_norm_g, ffn_g, wa, wh, wo)


def _topk_cols(x, k, payload=None):
    (vals, ids, pay), = _topk_cols_lockstep([x], k, None if payload is None else [payload])
    return vals, ids, pay


def _topk_cols_lockstep(xs, k, payloads=None):
    n, t = xs[0].shape
    row = lax.broadcasted_iota(jnp.int32, (n, t), 0)
    xs = list(xs)
    vals, ids, pay = [[] for _ in xs], [[] for _ in xs], [[] for _ in xs]
    for _ in range(k):
        for j, x in enumerate(xs):
            m = jnp.max(x, axis=0, keepdims=True)
            idx = jnp.min(jnp.where(x == m, row, n), axis=0, keepdims=True)
            hit = row == idx
            vals[j].append(m)
            ids[j].append(idx)
            if payloads is not None:
                pay[j].append(jnp.max(jnp.where(hit, payloads[j], -1), axis=0, keepdims=True))
            xs[j] = jnp.where(hit, NEG_INF, x)
    cat = lambda a: jnp.concatenate(a, axis=0)
    return [(cat(vals[j]), cat(ids[j]), cat(pay[j]) if payloads is not None else None) for j in range(len(xs))]


def _pair_candidates():
    pairs = [(a, b) for a in range(PEER_TOPK) for b in range(PEER_TOPK) if (a + 1) * (b + 1) <= PEER_TOPK]
    rows = -(-len(pairs) // 8) * 8
    sel = np.zeros((2, rows, PEER_TOPK), np.float32)
    for r, (a, b) in enumerate(pairs):
        sel[0, r, a] = 1.0
        sel[1, r, b] = 1.0
    return len(pairs), jnp.asarray(sel)


def _pick_rows(sel, x):
    return jnp.dot(sel, x, preferred_element_type=F32, precision=lax.Precision.HIGHEST)


def _peer_topk_kernel(q_ref, keys_ref, pair_ref, i1_ref, i2_ref, g_ref, *, n_pairs):
    o1, o2, og = [], [], []
    live = lax.broadcasted_iota(jnp.int32, (pair_ref.shape[1], q_ref.shape[0]), 0) < n_pairs
    cands, cidxs = [], []
    for h in range(PEER_HEADS):
        scores = [_dot_nt(keys_ref[side], q_ref[:, (h * 2 + side) * PEER_DH:(h * 2 + side + 1) * PEER_DH].astype(BF16))
                  for side in range(2)]
        (v1, i1, _), (v2, i2, _) = _topk_cols_lockstep(scores, PEER_TOPK)
        cands.append(jnp.where(live, _pick_rows(pair_ref[0], v1) + _pick_rows(pair_ref[1], v2), NEG_INF))
        cidxs.append((_pick_rows(pair_ref[0], i1.astype(F32)) * PEER_NKEYS
                      + _pick_rows(pair_ref[1], i2.astype(F32))).astype(jnp.int32))
    for h0 in range(0, PEER_HEADS, 2):
        for sv, _, e in _topk_cols_lockstep(cands[h0:h0 + 2], PEER_TOPK, cidxs[h0:h0 + 2]):
            ex = jnp.exp(sv - sv[0:1])
            o1.append(e // PEER_NKEYS)
            o2.append(e % PEER_NKEYS)
            og.append(ex / jnp.sum(ex, axis=0, keepdims=True))
    i1_ref[...] = jnp.concatenate(o1, axis=0).astype(F32).T.astype(jnp.int32)
    i2_ref[...] = jnp.concatenate(o2, axis=0).astype(F32).T.astype(jnp.int32)
    g_ref[...] = jnp.concatenate(og, axis=0).T


def peer_topk(q, keys_bf):
    t = q.shape[0]
    tb = LANE
    nsel = PEER_HEADS * PEER_TOPK
    n_pairs, pair_sel = _pair_candidates()
    out = lambda dt: jax.ShapeDtypeStruct((t, nsel), dt)
    return pl.pallas_call(
        functools.partial(_peer_topk_kernel, n_pairs=n_pairs),
        grid=(t // tb,),
        in_specs=[pl.BlockSpec((tb, q.shape[1]), lambda i: (i, 0)),
                  pl.BlockSpec(keys_bf.shape, lambda i: (0, 0, 0)),
                  pl.BlockSpec(pair_sel.shape, lambda i: (0, 0, 0))],
        out_specs=[pl.BlockSpec((tb, nsel), lambda i: (i, 0))] * 3,
        out_shape=[out(jnp.int32), out(jnp.int32), out(F32)],
        compiler_params=_cparams(("parallel",)),
        name="peer_topk",
    )(q, keys_bf, pair_sel)


GATE_UNROLL = 32


def _peer_dense_kernel(xn_ref, x_ref, i1_ref, i2_ref, g_ref, u_ref, v_ref, o_ref, acc_ref, gate_ref, *, n_a):
    j = pl.program_id(1)
    tm, nsel = i1_ref.shape

    @pl.when(j == 0)
    def _():
        acc_ref[...] = jnp.zeros(acc_ref.shape, F32)
        row = lax.broadcasted_iota(jnp.int32, (PEER_NKEYS, nsel), 0)

        def body(tt, _):
            for k in range(GATE_UNROLL):
                t = tt * GATE_UNROLL + k
                pa = jnp.where(row == i1_ref[pl.ds(t, 1), :], 1.0, 0.0).astype(BF16)
                gb = jnp.where(row == i2_ref[pl.ds(t, 1), :], g_ref[pl.ds(t, 1), :], 0.0).astype(BF16)
                gate_ref[pl.ds(pl.multiple_of(t * PEER_NKEYS, PEER_NKEYS), PEER_NKEYS), :] = _dot_nt(pa, gb)
            return 0

        lax.fori_loop(0, tm // GATE_UNROLL, body, 0)

    s = _dot_nt(xn_ref[...], u_ref[...])
    gate = jnp.concatenate([gate_ref[pl.ds(j * n_a + k, tm, stride=PEER_NKEYS), :] for k in range(n_a)], axis=1)
    acc_ref[...] += _dot((gate * _gelu(s)).astype(BF16), v_ref[...])

    @pl.when(j == pl.num_programs(1) - 1)
    def _():
        o_ref[...] = x_ref[...] + acc_ref[...]


def peer_dense(xn, x, i1, i2, g, u_bf, v_bf, tm, n_a):
    t = x.shape[0]
    tm = min(tm, t)
    te = n_a * PEER_NKEYS
    nsel = i1.shape[1]
    tok = lambda w: pl.BlockSpec((tm, w), lambda i, j: (i, 0))
    return pl.pallas_call(
        functools.partial(_peer_dense_kernel, n_a=n_a),
        grid=(t // tm, PEER_NKEYS // n_a),
        in_specs=[tok(D_MODEL), tok(D_MODEL), tok(nsel), tok(nsel), tok(nsel),
                  pl.BlockSpec((te, D_MODEL), lambda i, j: (j, 0)),
                  pl.BlockSpec((te, D_MODEL), lambda i, j: (j, 0))],
        out_specs=tok(D_MODEL),
        out_shape=jax.ShapeDtypeStruct((t, D_MODEL), F32),
        scratch_shapes=[pltpu.VMEM((tm, D_MODEL), F32), pltpu.VMEM((tm * PEER_NKEYS, PEER_NKEYS), F32)],
        compiler_params=_cparams(("parallel", "arbitrary")),
        name="peer_dense",
    )(xn, x, i1, i2, g, u_bf, v_bf)


def peer(xn_bf, x, wq_bf, keys_bf, u_bf, v_bf, tm):
    t = x.shape[0]
    q = matmul(xn_bf, wq_bf, min(512, t), 1024)
    i1, i2, g = peer_topk(q, keys_bf)
    return peer_dense(xn_bf, x, i1, i2, g, u_bf, v_bf, tm, 16)


ROWS = 8


def _pad_rows(x, rows=ROWS):
    return jnp.concatenate([x, jnp.zeros((rows - x.shape[0], x.shape[1]), x.dtype)], axis=0)


def _nsa_cmp_sample_kernel(q_ref, ckv_ref, tab_ref, cover_ref, oc_ref, sel_ref, *, nc, past, n_sel):
    nt = q_ref.shape[0]
    t_pos = past + lax.broadcasted_iota(jnp.int32, (ROWS, 1), 0)
    outs, imps = _cmp_branch(_pad_rows(q_ref[...]), t_pos, ckv_ref, tab_ref, cover_ref, nc)
    for g in range(KV_HEADS):
        sel_ref[g] = _topk_idx(imps[g], min(SEL_TOPN, n_sel))
    oc_ref[...] = jnp.concatenate(outs, axis=1)[:nt]


def nsa_cmp_sample(hproj3, ckv, tab, nc, past):
    nb, nt, _ = hproj3.shape
    ncp = ckv.shape[1]
    n_sel = -(-(past + nt) // SEL_BLOCK)
    n_sel_pad = -(-n_sel // LANE) * LANE
    cover = cover_matrix(ncp, n_sel_pad)
    return pl.pallas_call(
        functools.partial(_nsa_cmp_sample_kernel, nc=nc, past=past, n_sel=n_sel),
        grid=(nb,),
        in_specs=[pl.BlockSpec((None, nt, ATT_W), lambda b: (b, 0, C_Q // ATT_W)),
                  pl.BlockSpec((None, ncp, KVB_W), lambda b: (b, 0, 0)),
                  pl.BlockSpec((N_HEADS, LANE), lambda b: (0, 0)),
                  pl.BlockSpec((ncp, n_sel_pad), lambda b: (0, 0))],
        out_specs=[pl.BlockSpec((None, nt, ATT_W), lambda b: (b, 0, 0)),
                   pl.BlockSpec((None, KV_HEADS, ROWS, LANE), lambda b: (b, 0, 0, 0))],
        out_shape=[jax.ShapeDtypeStruct((nb, nt, ATT_W), F32),
                   jax.ShapeDtypeStruct((nb, KV_HEADS, ROWS, LANE), jnp.int32)],
        compiler_params=_cparams(("parallel",)),
        name="nsa_cmp_sample",
    )(hproj3, ckv, tab, cover)


def _nsa_sel_sample_kernel(sel_ref, pt_ref, q_ref, ng_ref, oc_ref, kvs_ref, kvw_ref, win_ref, tab_ref, cache_ref,
                           o_ref, buf_ref, sem, *, layer, past, w_len):
    b = pl.program_id(0)
    nt = q_ref.shape[0]
    nslot = SEL_TOPN
    per_page = PAGE_SIZE // SEL_BLOCK

    def slot_copy(t, g, slot):
        blk = sel_ref[((b * nt + t) * KV_HEADS + g) * nslot + slot]
        in_past = blk * SEL_BLOCK < past
        blk_c = jnp.minimum(blk, past // SEL_BLOCK - 1)
        page = pt_ref[b, blk_c // per_page]
        src = cache_ref.at[layer, page, pl.ds(KVB_W, KVB_W), :]
        dst = buf_ref.at[t * KV_HEADS + g, :, pl.ds(slot * PAGE_SIZE, PAGE_SIZE)]
        return blk, in_past, pltpu.make_async_copy(src, dst, sem)

    for t in range(nt):
        for g in range(KV_HEADS):
            for slot in range(nslot):
                _, in_past, cp = slot_copy(t, g, slot)

                @pl.when(in_past)
                def _():
                    cp.start()

                @pl.when(jnp.logical_not(in_past))
                def _():
                    buf_ref[t * KV_HEADS + g, :, pl.ds(slot * PAGE_SIZE, PAGE_SIZE)] = jnp.zeros(
                        (KVB_W, PAGE_SIZE), F32)
    for t in range(nt):
        for g in range(KV_HEADS):
            for slot in range(nslot):
                _, in_past, cp = slot_copy(t, g, slot)

                @pl.when(in_past)
                def _():
                    cp.wait()

    gates = _sigmoid(ng_ref[...])
    nk = nslot * PAGE_SIZE
    lane_slot = lax.broadcasted_iota(jnp.int32, (1, nk), 1) // PAGE_SIZE
    lane_off = lax.broadcasted_iota(jnp.int32, (1, nk), 1) % PAGE_SIZE
    j_new = lax.broadcasted_iota(jnp.int32, (1, LANE), 1)
    kvn = jnp.concatenate([kvs_ref[...], kvw_ref[...]], axis=1)
    kvn_pad = _pad_rows(kvn, LANE).astype(BF16)
    win = win_ref[...].astype(BF16)
    w_idx = lax.broadcasted_iota(jnp.int32, (1, w_len + LANE), 1)
    rows_out = []
    for t in range(nt):
        t_pos = past + t
        heads_out = []
        for g in range(KV_HEADS):
            q4 = jnp.concatenate(
                [q_ref[t:t + 1, (g * HPG + h) * HEAD_DIM:(g * HPG + h + 1) * HEAD_DIM] for h in range(HPG)],
                axis=0)
            q4 = (q4 * ATT_SCALE).astype(BF16)

            def bias_of(dist):
                d = jnp.broadcast_to(jnp.clip(dist, 0, LANE - 1), (ROWS, dist.shape[1]))
                return jnp.concatenate(
                    [_bias_lookup(tab_ref[g * HPG + h:g * HPG + h + 1, :], d)[0:1] for h in range(HPG)], axis=0)

            pos = jnp.zeros((1, nk), jnp.int32)
            lane_blk = jnp.zeros((1, nk), jnp.int32)
            has_new = jnp.zeros((1, 1), jnp.int32)
            for slot in range(nslot):
                blk = sel_ref[((b * nt + t) * KV_HEADS + g) * nslot + slot]
                pos = jnp.where(lane_slot == slot, (blk // per_page) * PAGE_SIZE, pos)
                lane_blk = jnp.where(lane_slot == slot, blk, lane_blk)
                has_new = jnp.maximum(has_new, (blk * SEL_BLOCK >= past).astype(jnp.int32))
            pos = pos + lane_off
            in_block = pos // SEL_BLOCK == lane_blk
            kv_g = buf_ref[t * KV_HEADS + g].astype(BF16)
            k_new = kvn_pad[:, g * HEAD_DIM:(g + 1) * HEAD_DIM]
            v_new = kvn_pad[:, (KV_HEADS + g) * HEAD_DIM:(KV_HEADS + g + 1) * HEAD_DIM]
            dist = jnp.concatenate([t_pos - pos, t - j_new], axis=1)
            ok = jnp.concatenate([in_block & (pos < past), (j_new <= t) & (has_new > 0)], axis=1)
            s = jnp.concatenate([_dot(q4, kv_g[g * HEAD_DIM:(g + 1) * HEAD_DIM]), _dot_nt(q4, k_new)], axis=1)
            s = jnp.where(ok, s + bias_of(dist), NEG)
            e = jnp.where(ok, jnp.exp(s - jnp.max(s, axis=-1, keepdims=True)), 0.0)
            p = (e / jnp.sum(e, axis=-1, keepdims=True)).astype(BF16)
            o_s = (_dot_nt(p[:, :nk], kv_g[(KV_HEADS + g) * HEAD_DIM:(KV_HEADS + g + 1) * HEAD_DIM])
                   + _dot(p[:, nk:], v_new))

            kw_new = kvn_pad[:, (2 * KV_HEADS + g) * HEAD_DIM:(2 * KV_HEADS + g + 1) * HEAD_DIM]
            vw_new = kvn_pad[:, (3 * KV_HEADS + g) * HEAD_DIM:(3 * KV_HEADS + g + 1) * HEAD_DIM]
            dist_w = t_pos - (past - w_len + w_idx)
            ok_w = (dist_w >= 0) & (dist_w <= WINDOW) & (w_idx < w_len + nt)
            s = jnp.concatenate([_dot(q4, win[g * HEAD_DIM:(g + 1) * HEAD_DIM]), _dot_nt(q4, kw_new)], axis=1)
            s = jnp.where(ok_w, s + bias_of(dist_w), NEG)
            e = jnp.where(ok_w, jnp.exp(s - jnp.max(s, axis=-1, keepdims=True)), 0.0)
            p = (e / jnp.sum(e, axis=-1, keepdims=True)).astype(BF16)
            o_w = (_dot_nt(p[:, :w_len], win[(KV_HEADS + g) * HEAD_DIM:(KV_HEADS + g + 1) * HEAD_DIM])
                   + _dot(p[:, w_len:], vw_new))

            for h in range(HPG):
                head = g * HPG + h
                o_c = oc_ref[t:t + 1, head * HEAD_DIM:(head + 1) * HEAD_DIM]
                heads_out.append(gates[t:t + 1, head:head + 1] * o_c
                                 + gates[t:t + 1, N_HEADS + head:N_HEADS + head + 1] * o_s[h:h + 1]
                                 + gates[t:t + 1, 2 * N_HEADS + head:2 * N_HEADS + head + 1] * o_w[h:h + 1])
        rows_out.append(jnp.concatenate(heads_out, axis=1))
    o_ref[...] = jnp.concatenate(rows_out, axis=0)


def nsa_sel_sample(hproj3, o_c, sel_flat, page_table, cache_t, win_t, tab, layer, past):
    nb, nt, _ = hproj3.shape
    w_len = win_t.shape[3]
    assert past % SEL_BLOCK == 0 and PAGE_SIZE % SEL_BLOCK == 0 and w_len % LANE == 0 and past >= w_len
    grid_spec = pltpu.PrefetchScalarGridSpec(
        num_scalar_prefetch=2,
        grid=(nb,),
        in_specs=[pl.BlockSpec((None, nt, ATT_W), lambda b, s, p: (b, 0, C_Q // ATT_W)),
                  pl.BlockSpec((None, nt, LANE), lambda b, s, p: (b, 0, C_NG // LANE)),
                  pl.BlockSpec((None, nt, ATT_W), lambda b, s, p: (b, 0, 0)),
                  pl.BlockSpec((None, nt, KVB_W), lambda b, s, p: (b, 0, C_KVS // KVB_W)),
                  pl.BlockSpec((None, nt, KVB_W), lambda b, s, p: (b, 0, C_KVW // KVB_W)),
                  pl.BlockSpec((None, None, KVB_W, w_len), lambda b, s, p: (layer, b, 0, 0)),
                  pl.BlockSpec((N_HEADS, LANE), lambda b, s, p: (0, 0)),
                  pl.BlockSpec(memory_space=pl.ANY)],
        out_specs=pl.BlockSpec((None, nt, ATT_W), lambda b, s, p: (b, 0, 0)),
        scratch_shapes=[pltpu.VMEM((nt * KV_HEADS, KVB_W, SEL_TOPN * PAGE_SIZE), F32),
                        pltpu.SemaphoreType.DMA(())],
    )
    return pl.pallas_call(
        functools.partial(_nsa_sel_sample_kernel, layer=layer, past=past, w_len=w_len),
        grid_spec=grid_spec,
        out_shape=jax.ShapeDtypeStruct((nb, nt, ATT_W), F32),
        compiler_params=_cparams(("arbitrary",)),
        name="nsa_sel_sample",
    )(sel_flat, page_table, hproj3, hproj3, o_c, hproj3, hproj3, win_t, tab, cache_t)


def _hgrn_sample_kernel(hq_ref, hf_ref, hi_ref, lb_ref, s_ref, o_ref, so_ref):
    nt = hq_ref.shape[0]
    live = (lax.broadcasted_iota(jnp.int32, (ROWS, 1), 0) < nt).astype(F32)
    q, k, logf = _hgrn_gates(_pad_rows(hq_ref[...]), _pad_rows(hf_ref[...]), lb_ref[...])
    q, k, logf = q * live, k * live, logf * live
    v = _pad_rows(hi_ref[...])
    rows = [logf[0:1]]
    for i in range(1, ROWS):
        rows.append(rows[-1] + logf[i:i + 1])
    a = jnp.concatenate(rows, axis=0)
    outs = _hgrn_intra(q, k, v, a, ROWS)
    a_last = a[ROWS - 1:ROWS]
    qe = (q * jnp.exp(a)).astype(BF16)
    kd = (k * jnp.exp(a_last - a)).astype(BF16)
    dec = jnp.exp(a_last)
    o_parts = []
    for h in range(HG_HEADS):
        hs = slice(h * HG_DK, (h + 1) * HG_DK)
        st = s_ref[h].T
        o_parts.append(outs[h] + _dot_nt(qe[:, hs], st.astype(BF16)))
        so_ref[h] = (dec[:, hs] * st + _dot_tn(v[:, hs].astype(BF16), kd[:, hs])).T
    o_ref[...] = jnp.concatenate(o_parts, axis=1)[:nt]


def hgrn_sample(hproj3, lb, state):
    nb, nt, _ = hproj3.shape
    col = lambda c: pl.BlockSpec((None, nt, HGK_W), lambda b: (b, 0, c // HGK_W))
    st_spec = pl.BlockSpec((None, HG_HEADS, HG_DK, HG_DV), lambda b: (b, 0, 0, 0))
    return pl.pallas_call(
        _hgrn_sample_kernel,
        grid=(nb,),
        in_specs=[col(C_HQ), col(C_HF), col(C_HI), pl.BlockSpec((1, HGK_W), lambda b: (0, 0)), st_spec],
        out_specs=[pl.BlockSpec((None, nt, HGK_W), lambda b: (b, 0, 0)), st_spec],
        out_shape=[jax.ShapeDtypeStruct((nb, nt, HGK_W), F32),
                   jax.ShapeDtypeStruct(state.shape, F32)],
        compiler_params=_cparams(("parallel",)),
        name="hgrn_sample",
    )(hproj3, hproj3, hproj3, lb.reshape(1, HGK_W), state)


def reorder_w_in(w):
    o = np.cumsum([0, ATT_W, KVB_W, KVB_W, KVB_W, N_GATE, HGK_W, HGK_W, HGK_W, HGK_W, D_MODEL, D_MODEL])
    q, kvc, kvs, kvw, ng, hq, hf, hi, hg, ga, gh = [w[:, o[i]:o[i + 1]] for i in range(11)]
    pad = jnp.zeros((w.shape[0], N_PROJ - C_NG - N_GATE), w.dtype)
    return jnp.concatenate([q, hq, ga, gh, hf, hi, hg, kvc, kvs, kvw, ng, pad], axis=1).astype(BF16)


def prompt_layer(x, lp, tabs):
    t = x.shape[0]
    tab, ntab, wtab = tabs
    hproj = norm_matmul(x, lp["attn_g"], lp["w_in"], min(512, t), 1408)
    y = cproj_rows(hproj, C_KVC // KVB_W, lp["wbig"], 2048)
    ckv = cmp_mlp(y[None], lp["pe_term"], lp["w2big"])[0]
    o_c, mask = nsa_cmp_prompt(hproj, ckv, tab, t // CMP_STRIDE - 1)
    ka, vt = attention_kv_layout(hproj)
    att = nsa_sel_prompt(hproj, o_c, mask, ka, vt, tab, ntab, wtab)
    o_hg, st = hgrn_prompt(hproj, lp["lb"], 128)
    x, xn = merge(x, att, o_hg, hproj, lp["hg_norm_g"], lp["ffn_g"], lp["wa"], lp["wh"], lp["wo"], 256)
    x = peer(xn, x, lp["peer_wq"], lp["peer_keys"], lp["peer_u"], lp["peer_v"], 256)
    return x, hproj[:, C_KVC:C_KVC + 2 * KVB_W], hproj[:, C_KVW:C_KVW + KVB_W], jnp.swapaxes(st, 1, 2)


def layer_params(l, norm_attn_g, norm_ffn_g, w_in, cmp_pe, cmp_w1, cmp_w2, lower_bounds, hg_norm_g,
                 w_br_attn, w_br_hg, w_out, peer_wq, peer_keys, peer_u, peer_v):
    wbig, w2big, pe_rows = compress_weights(cmp_pe[l], cmp_w1[l], cmp_w2[l])
    return dict(attn_g=norm_attn_g[l], ffn_g=norm_ffn_g[l], w_in=reorder_w_in(w_in[l]),
                wbig=wbig, w2big=w2big, pe_term=pe_term_from(pe_rows, wbig),
                lb=lower_bounds[l], hg_norm_g=hg_norm_g[l],
                wa=w_br_attn[l].astype(BF16), wh=w_br_hg[l].astype(BF16), wo=w_out[l].astype(BF16),
                peer_wq=peer_wq[l].astype(BF16), peer_keys=peer_keys[l].astype(BF16),
                peer_u=peer_u[l].astype(BF16), peer_v=peer_v[l].astype(BF16))


def sample_layer(x, lp, tab, layer, cache_t, win_t, state, page_table, nb, nt):
    past = page_table.shape[1] * PAGE_SIZE
    assert nt < CMP_STRIDE, "new tokens never complete a compression chunk"
    hproj = norm_matmul(x, lp["attn_g"], lp["w_in"], x.shape[0], 1408)
    hproj3 = hproj.reshape(nb, nt, N_PROJ)
    y = cproj_pages(cache_t, layer, page_table, lp["wbig"], 16)
    ckv = cmp_mlp(y, lp["pe_term"], lp["w2big"])
    o_c, sel = nsa_cmp_sample(hproj3, ckv, tab, past // CMP_STRIDE - 1, past)
    sel_flat = jnp.transpose(sel[:, :, :nt, :SEL_TOPN], (0, 2, 1, 3)).reshape(-1)
    att = nsa_sel_sample(hproj3, o_c, sel_flat, page_table, cache_t, win_t, tab, layer, past)
    o_hg, st = hgrn_sample(hproj3, lp["lb"], state)
    x, xn = merge(x, att.reshape(nb * nt, ATT_W), o_hg.reshape(nb * nt, HGK_W), hproj, lp["hg_norm_g"],
                  lp["ffn_g"], lp["wa"], lp["wh"], lp["wo"], 256)
    x = peer(xn, x, lp["peer_wq"], lp["peer_keys"], lp["peer_u"], lp["peer_v"], 256)
    return x, hproj[:, C_KVC:C_KVC + 2 * KVB_W], hproj[:, C_KVW:C_KVW + KVB_W], st


def kernel(x_prompt, x_sample, cache_kv, cache_win, state_hgrn, page_table, norm_attn_g, norm_ffn_g, final_norm_g, w_in, cmp_pe, cmp_w1, cmp_w2, rel_bias, hg_lb_logits, hg_norm_g, w_br_attn, w_br_hg, w_out, peer_wq, peer_keys, peer_u, peer_v):
    depth = w_in.shape[0]
    bp, seq, _ = x_prompt.shape
    nb, nt, _ = x_sample.shape
    assert bp == 1
    p_lb = jax.nn.softmax(hg_lb_logits.astype(F32), axis=0)
    lower_bounds = jnp.cumsum(p_lb, axis=0) - p_lb[0]
    tab = bias_table(rel_bias)
    tabs = (tab, *distance_tables(tab))
    n_pool = cache_kv.shape[1]
    w_buf = cache_win.shape[2]
    cache_t = jnp.transpose(cache_kv, (0, 1, 3, 4, 5, 2)).reshape(depth, n_pool, 2 * KVB_W, PAGE_SIZE)
    win_t = jnp.transpose(cache_win, (0, 1, 3, 4, 5, 2)).reshape(depth, nb, KVB_W, w_buf)
    xp = x_prompt.reshape(seq, D_MODEL)
    xs = x_sample.reshape(nb * nt, D_MODEL)
    kv_p, win_p, hs_p, kv_s, win_s, hs_s = [], [], [], [], [], []
    for l in range(depth):
        lp = layer_params(l, norm_attn_g, norm_ffn_g, w_in, cmp_pe, cmp_w1, cmp_w2, lower_bounds, hg_norm_g,
                          w_br_attn, w_br_hg, w_out, peer_wq, peer_keys, peer_u, peer_v)
        xp, kv, kw, st = prompt_layer(xp, lp, tabs)
        kv_p.append(kv.reshape(bp, seq, 4, KV_HEADS, HEAD_DIM))
        win_p.append(kw[seq - min(WINDOW, seq):].reshape(bp, -1, 2, KV_HEADS, HEAD_DIM))
        hs_p.append(st[None])
        xs, kv, kw, st = sample_layer(xs, lp, tab, l, cache_t, win_t, state_hgrn[l], page_table, nb, nt)
        kv_s.append(kv.reshape(nb, nt, 4, KV_HEADS, HEAD_DIM))
        win_s.append(jnp.concatenate([cache_win[l], kw.reshape(nb, nt, 2, KV_HEADS, HEAD_DIM)], axis=1)[:, nt:])
        hs_s.append(st)
    y_prompt = rmsnorm(xp, final_norm_g, min(512, seq)).reshape(bp, seq, D_MODEL)
    y_sample = rmsnorm(xs, final_norm_g, nb * nt).reshape(nb, nt, D_MODEL)
    return (y_prompt, y_sample, jnp.stack(kv_p), jnp.stack(win_p), jnp.stack(hs_p),
            jnp.stack(kv_s), jnp.stack(win_s), jnp.stack(hs_s))
```

```python
import functools
import math

import jax
import jax.numpy as jnp
import numpy as np
from jax import lax
from jax.experimental import pallas as pl
from jax.experimental.pallas import tpu as pltpu

F32 = jnp.float32
BF16 = jnp.bfloat16

D_MODEL = 1024
PAGE_SIZE = 128
N_HEADS = 8
HEAD_DIM = 64
KV_HEADS = 2
HPG = N_HEADS // KV_HEADS
CMP_STRIDE = 16
CMP_BLOCK = 32
SEL_BLOCK = 64
SEL_TOPN = 16
WINDOW = 512
ATT_SCALE = HEAD_DIM ** -0.5
N_BUCKETS = 32
MAX_DISTANCE = 128
HG_HEADS = 4
HG_DK = 128
HG_DV = 128
PEER_HEADS = 8
PEER_NKEYS = 128
PEER_DH = 128
PEER_TOPK = 16
ATT_W = N_HEADS * HEAD_DIM
KVB_W = 2 * KV_HEADS * HEAD_DIM
HGK_W = HG_HEADS * HG_DK
N_GATE = 3 * N_HEADS
EPS = 1e-6
NEG = -1e30
BIG = 1e6
TINY = 1e-30
NEG_INF = float("-inf")

LANE = 128
VMEM_LIMIT = 56 * 1024 * 1024

C_Q, C_HQ, C_GA, C_GH, C_HF, C_HI, C_HG = 0, 512, 1024, 2048, 3072, 3584, 4096
C_KVC, C_KVS, C_KVW, C_NG, N_PROJ = 4608, 4864, 5120, 5376, 5632
SUB = 16
TQ = 128
TK = 512


def _cparams(sem, vmem=VMEM_LIMIT):
    return pltpu.CompilerParams(dimension_semantics=sem, vmem_limit_bytes=vmem)


def _dot(a, b):
    return jnp.dot(a, b, preferred_element_type=F32)


def _dot_nt(a, b):
    return lax.dot_general(a, b, (((1,), (1,)), ((), ())), preferred_element_type=F32)


def _dot_tn(a, b):
    return lax.dot_general(a, b, (((0,), (0,)), ((), ())), preferred_element_type=F32)


def _sigmoid(x):
    return 1.0 / (1.0 + jnp.exp(-x))


def _gelu(x):
    return 0.5 * x * (1.0 + lax.erf(x * (2.0 ** -0.5)))


def _norm_matmul_kernel(x_ref, g_ref, w_ref, o_ref, xn_ref):
    @pl.when(pl.program_id(1) == 0)
    def _():
        x = x_ref[...]
        y = x * lax.rsqrt(jnp.mean(x * x, axis=-1, keepdims=True) + EPS)
        xn_ref[...] = (y * g_ref[...]).astype(BF16)

    o_ref[...] = _dot(xn_ref[...], w_ref[...])


def norm_matmul(x, g, w, tm, tn):
    m, k = x.shape
    n = w.shape[1]
    return pl.pallas_call(
        _norm_matmul_kernel,
        grid=(m // tm, n // tn),
        in_specs=[pl.BlockSpec((tm, k), lambda i, j: (i, 0)),
                  pl.BlockSpec((1, k), lambda i, j: (0, 0)),
                  pl.BlockSpec((k, tn), lambda i, j: (0, j))],
        out_specs=pl.BlockSpec((tm, tn), lambda i, j: (i, j)),
        out_shape=jax.ShapeDtypeStruct((m, n), F32),
        scratch_shapes=[pltpu.VMEM((tm, k), BF16)],
        compiler_params=_cparams(("parallel", "arbitrary")),
        name="norm_matmul",
    )(x, g.reshape(1, k), w)


def _matmul_kernel(x_ref, w_ref, o_ref):
    o_ref[...] = _dot(x_ref[...], w_ref[...])


def matmul(x, w, tm, tn):
    m, k = x.shape
    n = w.shape[1]
    return pl.pallas_call(
        _matmul_kernel,
        grid=(m // tm, n // tn),
        in_specs=[pl.BlockSpec((tm, k), lambda i, j: (i, 0)),
                  pl.BlockSpec((k, tn), lambda i, j: (0, j))],
        out_specs=pl.BlockSpec((tm, tn), lambda i, j: (i, j)),
        out_shape=jax.ShapeDtypeStruct((m, n), F32),
        compiler_params=_cparams(("parallel", "arbitrary")),
        name="matmul",
    )(x, w)


def _rmsnorm_kernel(x_ref, g_ref, o_ref):
    x = x_ref[...]
    o_ref[...] = x * lax.rsqrt(jnp.mean(x * x, axis=-1, keepdims=True) + EPS) * g_ref[...]


def rmsnorm(x, g, tm):
    m, k = x.shape
    return pl.pallas_call(
        _rmsnorm_kernel,
        grid=(m // tm,),
        in_specs=[pl.BlockSpec((tm, k), lambda i: (i, 0)), pl.BlockSpec((1, k), lambda i: (0, 0))],
        out_specs=pl.BlockSpec((tm, k), lambda i: (i, 0)),
        out_shape=jax.ShapeDtypeStruct((m, k), F32),
        compiler_params=_cparams(("parallel",)),
        name="rmsnorm",
    )(x, g.reshape(1, k))


def _cproj_from_pieces(piece_fn, w_ref, y_ref):
    acc = jnp.zeros(y_ref.shape, F32)
    for s in range(CMP_STRIDE):
        acc = acc + _dot(piece_fn(s).astype(BF16), w_ref[s])
    y_ref[...] = acc


def _cproj_rows_kernel(r0_ref, r1_ref, w_ref, y_ref):
    nch = y_ref.shape[0]

    def piece(s):
        return jnp.concatenate([r[pl.ds(s, nch, stride=CMP_STRIDE), :] for r in (r0_ref, r1_ref)], axis=1)

    _cproj_from_pieces(piece, w_ref, y_ref)


def cproj_rows(rows_arr, col_block, wbig, rb):
    t = rows_arr.shape[0]
    rb = min(rb, t)
    half = lambda k: pl.BlockSpec((rb, LANE), lambda i: (i, 2 * col_block + k))
    return pl.pallas_call(
        _cproj_rows_kernel,
        grid=(t // rb,),
        in_specs=[half(0), half(1), pl.BlockSpec(wbig.shape, lambda i: (0, 0, 0))],
        out_specs=pl.BlockSpec((rb // CMP_STRIDE, 2 * KVB_W), lambda i: (i, 0)),
        out_shape=jax.ShapeDtypeStruct((t // CMP_STRIDE, 2 * KVB_W), F32),
        compiler_params=_cparams(("parallel",)),
        name="cproj_rows",
    )(rows_arr, rows_arr, wbig)


def _cproj_pages_kernel(pt_ref, *refs):
    del pt_ref
    pages, w_ref, y_ref, lo_ref, hi_ref = refs[:-4], refs[-4], refs[-3], refs[-2], refs[-1]
    nch = y_ref.shape[0]
    for p, page in enumerate(pages):
        rows = pl.ds(p * PAGE_SIZE, PAGE_SIZE)
        lo_ref[rows, :] = page[0:LANE, :].T
        hi_ref[rows, :] = page[LANE:2 * LANE, :].T

    def piece(s):
        return jnp.concatenate([r[pl.ds(s, nch, stride=CMP_STRIDE), :] for r in (lo_ref, hi_ref)], axis=1)

    _cproj_from_pieces(piece, w_ref, y_ref)


def cproj_pages(cache_t, layer, page_table, wbig, pg):
    nb, npages = page_table.shape
    per = PAGE_SIZE // CMP_STRIDE

    def page_spec(p):
        return pl.BlockSpec((None, None, KVB_W, PAGE_SIZE), lambda b, j, pt: (layer, pt[b, j * pg + p], 0, 0))

    grid_spec = pltpu.PrefetchScalarGridSpec(
        num_scalar_prefetch=1,
        grid=(nb, npages // pg),
        in_specs=[page_spec(p) for p in range(pg)] + [pl.BlockSpec(wbig.shape, lambda b, j, pt: (0, 0, 0))],
        out_specs=pl.BlockSpec((None, pg * per, 2 * KVB_W), lambda b, j, pt: (b, j, 0)),
        scratch_shapes=[pltpu.VMEM((pg * PAGE_SIZE, LANE), F32)] * 2,
    )
    return pl.pallas_call(
        _cproj_pages_kernel,
        grid_spec=grid_spec,
        out_shape=jax.ShapeDtypeStruct((nb, npages * per, 2 * KVB_W), F32),
        compiler_params=_cparams(("parallel", "arbitrary")),
        name="cproj_pages",
    )(page_table, *([cache_t] * pg), wbig)


def _cmp_mlp_kernel(y_ref, pe_ref, w2_ref, o_ref):
    y = y_ref[...]
    n = y.shape[0]
    nxt = pltpu.roll(y[:, KVB_W:], n - 1, 0)
    hsum = y[:, :KVB_W] + nxt + pe_ref[...]
    o_ref[...] = _dot(_gelu(hsum).astype(BF16), w2_ref[...]).astype(o_ref.dtype)


def cmp_mlp(y, pe_term, w2big):
    b, ncp, _ = y.shape
    return pl.pallas_call(
        _cmp_mlp_kernel,
        grid=(b,),
        in_specs=[pl.BlockSpec((None, ncp, 2 * KVB_W), lambda i: (i, 0, 0)),
                  pl.BlockSpec((1, KVB_W), lambda i: (0, 0)),
                  pl.BlockSpec((KVB_W, KVB_W), lambda i: (0, 0))],
        out_specs=pl.BlockSpec((None, ncp, KVB_W), lambda i: (i, 0, 0)),
        out_shape=jax.ShapeDtypeStruct((b, ncp, KVB_W), BF16),
        compiler_params=_cparams(("parallel",)),
        name="cmp_mlp",
    )(y, pe_term, w2big)


def compress_weights(pe, w1, w2):
    w1r = w1.reshape(2, 2, CMP_STRIDE, HEAD_DIM, HEAD_DIM)
    eye = jnp.eye(2, dtype=F32)
    wbig = jnp.einsum('crsdh,cC,gG->scgdrCGh', w1r, eye, eye).reshape(CMP_STRIDE, KVB_W, 2 * KVB_W)
    w2big = jnp.einsum('chd,cC,gG->cghCGd', w2, eye, eye).reshape(KVB_W, KVB_W)
    per = pe.reshape(2, 2, CMP_STRIDE, HEAD_DIM)
    pe_rows = jnp.broadcast_to(jnp.transpose(per, (1, 2, 0, 3))[:, :, :, None, :],
                               (2, CMP_STRIDE, 2, KV_HEADS, HEAD_DIM)).reshape(2 * CMP_STRIDE, KVB_W)
    pe_rows = jnp.pad(pe_rows, ((0, 8 * CMP_STRIDE - 2 * CMP_STRIDE), (0, 0)))
    return wbig.astype(BF16), w2big.astype(BF16), pe_rows


def pe_term_from(pe_rows, wbig):
    ype = cproj_rows(pe_rows, 0, wbig, pe_rows.shape[0])
    return ype[0:1, :KVB_W] + ype[1:2, KVB_W:]


def bias_table(rel_bias):
    n = np.arange(LANE)
    exact = N_BUCKETS // 2
    nf = np.maximum(n, 1).astype(np.float32)
    large = exact + (np.log(nf / np.float32(exact)) / np.float32(math.log(MAX_DISTANCE / exact))
                     * np.float32(N_BUCKETS - exact)).astype(np.int32)
    bucket = np.where(n < exact, n, np.minimum(large, N_BUCKETS - 1))
    assert bucket[LANE - 1] == N_BUCKETS - 1
    return jnp.transpose(rel_bias.astype(F32)[bucket], (1, 0))


def _bias_lookup(tab_row, dist_clipped):
    r, w = dist_clipped.shape
    src = jnp.broadcast_to(tab_row, (r, LANE))
    parts = [jnp.take_along_axis(src, dist_clipped[:, j * LANE:(j + 1) * LANE], axis=1)
             for j in range(w // LANE)]
    return parts[0] if len(parts) == 1 else jnp.concatenate(parts, axis=1)


def _topk_mask(x, k):
    r, n = x.shape
    col = lax.broadcasted_iota(jnp.int32, (r, n), 1)
    sel = jnp.zeros((r, n), F32)
    for _ in range(k):
        m = jnp.max(x, axis=-1, keepdims=True)
        idx = jnp.min(jnp.where(x == m, col, n), axis=-1, keepdims=True)
        hit = col == idx
        sel = jnp.where(hit, 1.0, sel)
        x = jnp.where(hit, NEG_INF, x)
    return sel


def _topk_idx(x, k):
    r, n = x.shape
    col = lax.broadcasted_iota(jnp.int32, (r, n), 1)
    lane = lax.broadcasted_iota(jnp.int32, (r, LANE), 1)
    out = jnp.zeros((r, LANE), jnp.int32)
    for it in range(k):
        m = jnp.max(x, axis=-1, keepdims=True)
        idx = jnp.min(jnp.where(x == m, col, n), axis=-1, keepdims=True)
        out = jnp.where(lane == it, idx, out)
        x = jnp.where(col == idx, NEG_INF, x)
    return out


def _cmp_branch(q, t_pos, ckv_ref, tab_ref, cover_ref, nc, ncols=None):
    r = q.shape[0]
    ncols = ckv_ref.shape[0] if ncols is None else ncols
    n_sel = cover_ref.shape[1]
    c_idx = lax.broadcasted_iota(jnp.int32, (1, ncols), 1)
    dist = t_pos - (c_idx * CMP_STRIDE + CMP_BLOCK - 1)
    valid = (dist >= 0) & (c_idx < nc)
    any_valid = (t_pos >= CMP_BLOCK - 1).astype(F32)
    dcl = jnp.clip(dist, 0, LANE - 1)
    blk = lax.broadcasted_iota(jnp.int32, (1, n_sel), 1)
    cur = t_pos // SEL_BLOCK
    forced = (blk == 0) | (blk == cur) | (blk == cur - 1)
    outs, imps = [], []
    for g in range(KV_HEADS):
        kc = ckv_ref[:ncols, g * HEAD_DIM:(g + 1) * HEAD_DIM]
        vc = ckv_ref[:ncols, (KV_HEADS + g) * HEAD_DIM:(KV_HEADS + g + 1) * HEAD_DIM]
        psum = jnp.zeros((r, ncols), F32)
        for h in range(HPG):
            head = g * HPG + h
            qh = (q[:, head * HEAD_DIM:(head + 1) * HEAD_DIM] * ATT_SCALE).astype(BF16)
            s = _dot_nt(qh, kc) + _bias_lookup(tab_ref[head:head + 1, :], dcl)
            s = jnp.where(valid, s, NEG)
            e = jnp.exp(s - jnp.max(s, axis=-1, keepdims=True))
            p = e * (any_valid / jnp.sum(e, axis=-1, keepdims=True))
            outs.append(_dot(p.astype(BF16), vc))
            psum = psum + p
        p_hi = psum.astype(BF16)
        p_lo = (psum - p_hi.astype(F32)).astype(BF16)
        imp = _dot(p_hi, cover_ref[:ncols]) + _dot(p_lo, cover_ref[:ncols])
        imps.append(jnp.where(forced, BIG, jnp.where(blk <= cur, imp, -BIG)))
    return outs, imps


def _topk_mask_cols(x, k):
    n, t = x.shape
    row = lax.broadcasted_iota(jnp.int32, (n, t), 0)
    forced = x == BIG
    sel = forced.astype(F32)
    x = jnp.where(forced, NEG_INF, x)
    for _ in range(k - 3):
        m = jnp.max(x, axis=0, keepdims=True)
        idx = jnp.min(jnp.where(x == m, row, n), axis=0, keepdims=True)
        hit = row == idx
        sel = jnp.where(hit, 1.0, sel)
        x = jnp.where(hit, NEG_INF, x)
    return sel


def _nsa_cmp_kernel(q_ref, ckv_ref, tab_ref, cover_ref, oc_ref, mask_ref, *, nc, n_sel):
    i = pl.program_id(0)
    t_pos = i * TQ + lax.broadcasted_iota(jnp.int32, (TQ, 1), 0)
    ncp = ckv_ref.shape[0]
    part = max(ncp // 4, LANE)
    last_c = (i * TQ + TQ - CMP_BLOCK) // CMP_STRIDE
    need = jnp.clip(last_c // part, 0, ncp // part - 1)

    for n_parts in range(1, ncp // part + 1):
        @pl.when(need == n_parts - 1)
        def _(n_parts=n_parts):
            outs, imps = _cmp_branch(q_ref[...], t_pos, ckv_ref, tab_ref, cover_ref, nc, n_parts * part)
            for g in range(KV_HEADS):
                mask_ref[g] = _topk_mask_cols(imps[g].T, min(SEL_TOPN, n_sel))
            oc_ref[...] = jnp.concatenate(outs, axis=1)


def cover_matrix(ncp, n_sel):
    ci = np.arange(ncp)[:, None] * CMP_STRIDE
    sj = np.arange(n_sel)[None, :] * SEL_BLOCK
    return jnp.asarray(((ci < sj + SEL_BLOCK) & (ci + CMP_BLOCK > sj)).astype(np.float32), dtype=BF16)


def nsa_cmp_prompt(hproj, ckv, tab, nc):
    t = hproj.shape[0]
    ncp = ckv.shape[0]
    n_sel = t // SEL_BLOCK
    cover = cover_matrix(ncp, n_sel)
    return pl.pallas_call(
        functools.partial(_nsa_cmp_kernel, nc=nc, n_sel=n_sel),
        grid=(t // TQ,),
        in_specs=[pl.BlockSpec((TQ, ATT_W), lambda i: (i, C_Q // ATT_W)),
                  pl.BlockSpec((ncp, KVB_W), lambda i: (0, 0)),
                  pl.BlockSpec((N_HEADS, LANE), lambda i: (0, 0)),
                  pl.BlockSpec((ncp, n_sel), lambda i: (0, 0))],
        out_specs=[pl.BlockSpec((TQ, ATT_W), lambda i: (i, 0)),
                   pl.BlockSpec((KV_HEADS, n_sel, TQ), lambda i: (0, 0, i))],
        out_shape=[jax.ShapeDtypeStruct((t, ATT_W), F32),
                   jax.ShapeDtypeStruct((KV_HEADS, n_sel, t), F32)],
        compiler_params=_cparams(("parallel",)),
        name="nsa_cmp_prompt",
    )(hproj, ckv, tab, cover)


MASKED = -1e9
NEAR = 2 * TQ
W_KEYS = WINDOW + TQ
BLK_PER_TILE = TK // SEL_BLOCK


VT_ROWS = HEAD_DIM + 16
MASK_ROWS = 16
HT = HPG * TQ
FLAG_LANE = HEAD_DIM + BLK_PER_TILE
FAR_UNROLL = 4


def _online_step_t(carry, st, shift, vt):
    m, acc = carry
    m_new = jnp.maximum(m, jnp.max(st, axis=0, keepdims=True) + shift)
    p = jnp.exp(st - (m_new - shift)).astype(BF16)
    return m_new, jnp.exp(m - m_new) * acc + _dot(vt, p)


def _nsa_sel_kernel(q_ref, ng_ref, oc_ref, mask_ref, ka_ref, vt_ref, tab_ref, ntab_ref, wtab_ref, o_ref,
                    sta_ref, stb_ref, *, n_sel):
    i = pl.program_id(0)
    qt = (q_ref[...] * ATT_SCALE).T
    oct_ = oc_ref[...].T
    gt = _sigmoid(ng_ref[...]).T
    near_blk0 = 2 * i - 2
    n_far = (jnp.maximum(i - 1, 0) * TQ + TK - 1) // TK
    blk8 = lax.broadcasted_iota(jnp.int32, (BLK_PER_TILE, 1), 0)
    init = (jnp.full((1, HT), NEG, F32), jnp.zeros((VT_ROWS, HT), F32))
    zero_shift = jnp.zeros((1, HT), F32)
    chunks_per_tile = TK // LANE

    before_start = jnp.where(lax.broadcasted_iota(jnp.int32, (MASK_ROWS, HT), 0) == FLAG_LANE - HEAD_DIM,
                             MASKED, 0.0).astype(BF16)
    qtg, w_plain, far_shift = [], [], []
    for g in range(KV_HEADS):
        heads = range(g * HPG, (g + 1) * HPG)
        qg = jnp.concatenate([qt[h * HEAD_DIM:(h + 1) * HEAD_DIM] for h in heads], axis=1).astype(BF16)
        qtg.append(qg)
        w_plain.append(jnp.concatenate([qg, before_start, jnp.zeros((LANE - HEAD_DIM - MASK_ROWS, HT), BF16)], axis=0))
        far_shift.append(jnp.concatenate(
            [jnp.broadcast_to(tab_ref[h:h + 1, LANE - 1:LANE], (1, TQ)) for h in heads], axis=1))

    def vt_tile(c0, n, kind):
        return jnp.concatenate([vt_ref[c0 + c, kind] for c in range(n)], axis=1)

    last_tile = (ka_ref.shape[0] - WINDOW) // TK - 1

    def far_logits(tile, dst_ref):
        kt = jnp.minimum(tile, last_tile)
        k0 = pl.multiple_of(WINDOW + kt * TK, TK)
        r0 = pl.multiple_of(kt * BLK_PER_TILE, BLK_PER_TILE)
        for g in range(KV_HEADS):
            mt = mask_ref[g, pl.ds(r0, BLK_PER_TILE), :]
            keep = (mt > 0.5) & (tile * BLK_PER_TILE + blk8 < near_blk0)
            mneg = jnp.where(keep, 0.0, MASKED)
            m16 = jnp.concatenate([jnp.concatenate([mneg] * HPG, axis=1),
                                   jnp.zeros((MASK_ROWS - BLK_PER_TILE, HT), F32)], axis=0).astype(BF16)
            w = jnp.concatenate([qtg[g], m16, jnp.zeros((LANE - HEAD_DIM - MASK_ROWS, HT), BF16)], axis=0)
            dst_ref[g] = _dot(ka_ref[pl.ds(k0, TK), g * LANE:(g + 1) * LANE], w)

    def far_softmax(kt, src_ref, carry):
        c0 = (WINDOW // LANE) + jnp.minimum(kt, last_tile) * chunks_per_tile
        return tuple(_online_step_t(carry[g], src_ref[g], far_shift[g], vt_tile(c0, chunks_per_tile, g))
                     for g in range(KV_HEADS))

    @pl.when(n_far > 0)
    def _():
        far_logits(0, sta_ref)

    def far_body(trip, carry):
        bufs = (sta_ref, stb_ref)
        for u in range(FAR_UNROLL):
            tile = trip * FAR_UNROLL + u
            far_logits(tile + 1, bufs[(u + 1) % 2])
            carry = far_softmax(tile, bufs[u % 2], carry)
        return carry

    carry = lax.fori_loop(0, (n_far + FAR_UNROLL - 1) // FAR_UNROLL, far_body, (init, init))

    kn0 = pl.multiple_of(WINDOW + (i - 1) * TQ, TQ)
    base = pl.multiple_of(jnp.clip((near_blk0 // BLK_PER_TILE) * BLK_PER_TILE, 0, n_sel - MASK_ROWS), BLK_PER_TILE)
    key_blk = near_blk0 + lax.broadcasted_iota(jnp.int32, (NEAR, MASK_ROWS), 0) // SEL_BLOCK
    expand = (key_blk == base + lax.broadcasted_iota(jnp.int32, (NEAR, MASK_ROWS), 1)).astype(BF16)
    kw0 = pl.multiple_of(i * TQ, TQ)
    rows_t = []
    st_near, st_win = [], []
    for g in range(KV_HEADS):
        mneg = jnp.where(mask_ref[g, pl.ds(base, MASK_ROWS), :] > 0.5, 0.0, MASKED).astype(BF16)
        sel_add = _dot(expand, mneg)
        st = _dot(ka_ref[pl.ds(kn0, NEAR), g * LANE:(g + 1) * LANE], w_plain[g])
        st_near.append(st + ntab_ref[g] + jnp.concatenate([sel_add] * HPG, axis=1))
        st = _dot(ka_ref[pl.ds(kw0, W_KEYS), (KV_HEADS + g) * LANE:(KV_HEADS + g + 1) * LANE], w_plain[g])
        st_win.append(st + wtab_ref[g])
    for g in range(KV_HEADS):
        _, acc = _online_step_t(carry[g], st_near[g], zero_shift, vt_tile(kn0 // LANE, NEAR // LANE, g))
        os_t = acc[:HEAD_DIM] / acc[HEAD_DIM:HEAD_DIM + 1]
        _, acc = _online_step_t(init, st_win[g], zero_shift, vt_tile(kw0 // LANE, W_KEYS // LANE, KV_HEADS + g))
        ow_t = acc[:HEAD_DIM] / acc[HEAD_DIM:HEAD_DIM + 1]

        for h in range(HPG):
            head = g * HPG + h
            ls = slice(h * TQ, (h + 1) * TQ)
            rows_t.append(gt[head:head + 1] * oct_[head * HEAD_DIM:(head + 1) * HEAD_DIM]
                          + gt[N_HEADS + head:N_HEADS + head + 1] * os_t[:, ls]
                          + gt[2 * N_HEADS + head:2 * N_HEADS + head + 1] * ow_t[:, ls])
    o_ref[...] = jnp.concatenate(rows_t, axis=0).T


def _distance_tables_kernel(tab_ref, ntab_ref, wtab_ref):
    head = pl.program_id(0) * HPG + pl.program_id(1)
    row = tab_ref[pl.ds(head, 1), :]
    for out_ref, start, max_dist in ((ntab_ref, TQ, None), (wtab_ref, WINDOW, WINDOW)):
        keys = out_ref.shape[0]
        dist = (start + lax.broadcasted_iota(jnp.int32, (keys, TQ), 1)
                - lax.broadcasted_iota(jnp.int32, (keys, TQ), 0))
        ok = dist >= 0 if max_dist is None else (dist >= 0) & (dist <= max_dist)
        out_ref[...] = jnp.where(ok, _bias_lookup(row, jnp.clip(dist, 0, LANE - 1)), NEG)


def distance_tables(tab):
    return pl.pallas_call(
        _distance_tables_kernel,
        grid=(KV_HEADS, HPG),
        in_specs=[pl.BlockSpec((N_HEADS, LANE), lambda g, h: (0, 0))],
        out_specs=[pl.BlockSpec((None, NEAR, TQ), lambda g, h: (g, 0, h)),
                   pl.BlockSpec((None, W_KEYS, TQ), lambda g, h: (g, 0, h))],
        out_shape=[jax.ShapeDtypeStruct((KV_HEADS, NEAR, HT), F32),
                   jax.ShapeDtypeStruct((KV_HEADS, W_KEYS, HT), F32)],
        compiler_params=_cparams(("parallel", "parallel")),
        name="distance_tables",
    )(tab)


def attention_kv_layout(hproj):
    t = hproj.shape[0]
    grp = lambda c0, g: hproj[:, c0 + g * HEAD_DIM:c0 + (g + 1) * HEAD_DIM]
    e8 = jnp.asarray(np.eye(BLK_PER_TILE, dtype=np.float32)[(np.arange(t) % TK) // SEL_BLOCK])
    z = lambda w: jnp.zeros((t, w), F32)
    ka = [jnp.concatenate([grp(C_KVS, g), e8, z(LANE - HEAD_DIM - BLK_PER_TILE)], axis=1) for g in range(KV_HEADS)]
    ka += [jnp.concatenate([grp(C_KVW, g), z(LANE - HEAD_DIM)], axis=1) for g in range(KV_HEADS)]
    lead = jnp.zeros((WINDOW, LANE), F32).at[:, FLAG_LANE].set(1.0)
    ka = jnp.concatenate([jnp.tile(lead, (1, 2 * KV_HEADS)), jnp.concatenate(ka, axis=1)], axis=0).astype(BF16)
    ones = jnp.ones((t, 1), F32)
    vts = []
    for c0 in (C_KVS, C_KVW):
        for g in range(KV_HEADS):
            v = jnp.concatenate([grp(c0 + KV_HEADS * HEAD_DIM, g), ones, z(VT_ROWS - HEAD_DIM - 1)], axis=1)
            v = jnp.pad(v.astype(BF16), ((WINDOW, 0), (0, 0))).reshape(-1, LANE, VT_ROWS)
            vts.append(jnp.swapaxes(v, 1, 2))
    return ka, jnp.stack(vts, axis=1)


def nsa_sel_prompt(hproj, o_c, mask, ka, vt, tab, ntab, wtab):
    t = hproj.shape[0]
    n_sel = mask.shape[1]
    assert t % TK == 0 and n_sel % BLK_PER_TILE == 0 and n_sel >= MASK_ROWS
    const = lambda a: pl.BlockSpec(a.shape, lambda i: (0,) * a.ndim, pipeline_mode=pl.Buffered(1))
    return pl.pallas_call(
        functools.partial(_nsa_sel_kernel, n_sel=n_sel),
        grid=(t // TQ,),
        in_specs=[pl.BlockSpec((TQ, ATT_W), lambda i: (i, C_Q // ATT_W)),
                  pl.BlockSpec((TQ, LANE), lambda i: (i, C_NG // LANE)),
                  pl.BlockSpec((TQ, ATT_W), lambda i: (i, 0)),
                  pl.BlockSpec((KV_HEADS, n_sel, TQ), lambda i: (0, 0, i)),
                  const(ka), const(vt), const(tab), const(ntab), const(wtab)],
        out_specs=pl.BlockSpec((TQ, ATT_W), lambda i: (i, 0)),
        out_shape=jax.ShapeDtypeStruct((t, ATT_W), F32),
        scratch_shapes=[pltpu.VMEM((KV_HEADS, TK, HT), F32)] * 2,
        compiler_params=_cparams(("parallel",)),
        name="nsa_sel_prompt",
    )(hproj, hproj, o_c, mask, ka, vt, tab, ntab, wtab)


def _hgrn_gates(hq, hf, lb):
    sig = _sigmoid(hf)
    forget = lb + (1.0 - lb) * sig
    logf = jnp.log(jnp.maximum(forget, TINY))
    k = (1.0 - lb) * (1.0 - sig)
    q = hq * _sigmoid(hq)
    return q, k, logf


def _hgrn_intra(q, k, v, a, sub):
    n = q.shape[0]
    row = lax.broadcasted_iota(jnp.int32, (n, 1), 0) % sub
    outs = [jnp.zeros((n, HG_DV), F32) for _ in range(HG_HEADS)]
    for d in range(sub):
        ks = k if d == 0 else pltpu.roll(k, d, 0)
        a_s = a if d == 0 else pltpu.roll(a, d, 0)
        vs = v if d == 0 else pltpu.roll(v, d, 0)
        msk = row >= d
        w = q * ks * jnp.exp(jnp.where(msk, a - a_s, 0.0)) * msk.astype(F32)
        for h in range(HG_HEADS):
            r = jnp.sum(w[:, h * HG_DK:(h + 1) * HG_DK], axis=-1, keepdims=True)
            outs[h] = outs[h] + r * vs[:, h * HG_DV:(h + 1) * HG_DV]
    return outs


def _hgrn_prompt_kernel(hq_ref, hf_ref, hi_ref, lb_ref, tri_ref, o_ref, st_out_ref, st_ref):
    step = pl.program_id(0)

    @pl.when(step == 0)
    def _():
        st_ref[...] = jnp.zeros(st_ref.shape, F32)

    n = hq_ref.shape[0]
    q, k, logf = _hgrn_gates(hq_ref[...], hf_ref[...], lb_ref[...])
    v = hi_ref[...]
    a = jnp.dot(tri_ref[...], logf, preferred_element_type=F32, precision=lax.Precision.HIGHEST)
    outs = _hgrn_intra(q, k, v, a, SUB)
    rows_out = [[] for _ in range(HG_HEADS)]
    for c in range(n // SUB):
        sl = slice(c * SUB, (c + 1) * SUB)
        a_c = a[sl]
        a_last = a_c[SUB - 1:SUB]
        qe = (q[sl] * jnp.exp(a_c)).astype(BF16)
        kd = (k[sl] * jnp.exp(a_last - a_c)).astype(BF16)
        dec = jnp.exp(a_last)
        for h in range(HG_HEADS):
            hs = slice(h * HG_DK, (h + 1) * HG_DK)
            st = st_ref[h]
            rows_out[h].append(outs[h][sl] + _dot_nt(qe[:, hs], st.astype(BF16)))
            st_ref[h] = dec[:, hs] * st + _dot_tn(v[sl, hs].astype(BF16), kd[:, hs])
    o_ref[...] = jnp.concatenate([jnp.concatenate(r, axis=0) for r in rows_out], axis=1)
    st_out_ref[...] = st_ref[...]


def hgrn_prompt(hproj, lb, tc):
    t = hproj.shape[0]
    tc = min(tc, t)
    r = np.arange(tc)
    tri = jnp.asarray(((r[:, None] >= r[None, :]) & (r[:, None] // SUB == r[None, :] // SUB)).astype(np.float32))
    return pl.pallas_call(
        _hgrn_prompt_kernel,
        grid=(t // tc,),
        in_specs=[pl.BlockSpec((tc, HGK_W), lambda i: (i, C_HQ // HGK_W)),
                  pl.BlockSpec((tc, HGK_W), lambda i: (i, C_HF // HGK_W)),
                  pl.BlockSpec((tc, HGK_W), lambda i: (i, C_HI // HGK_W)),
                  pl.BlockSpec((1, HGK_W), lambda i: (0, 0)),
                  pl.BlockSpec((tc, tc), lambda i: (0, 0))],
        out_specs=[pl.BlockSpec((tc, HGK_W), lambda i: (i, 0)),
                   pl.BlockSpec((HG_HEADS, HG_DV, HG_DK), lambda i: (0, 0, 0))],
        out_shape=[jax.ShapeDtypeStruct((t, HGK_W), F32),
                   jax.ShapeDtypeStruct((HG_HEADS, HG_DV, HG_DK), F32)],
        scratch_shapes=[pltpu.VMEM((HG_HEADS, HG_DV, HG_DK), F32)],
        compiler_params=_cparams(("arbitrary",)),
        name="hgrn_prompt",
    )(hproj, hproj, hproj, lb.reshape(1, HGK_W), tri)


def _merge_kernel(x_ref, att_ref, o_ref, hg_ref, ga_ref, gh_ref, ng_ref, fg_ref, wa_ref, wh_ref, wo_ref,
                  xo_ref, xn_ref):
    o = o_ref[...]
    parts = []
    for h in range(HG_HEADS):
        oh = o[:, h * HG_DV:(h + 1) * HG_DV]
        parts.append(oh * lax.rsqrt(jnp.mean(oh * oh, axis=-1, keepdims=True) + EPS))
    hg = hg_ref[...]
    hgo = jnp.concatenate(parts, axis=1) * ng_ref[...] * (hg * _sigmoid(hg))
    m = (_sigmoid(ga_ref[...]) * _dot(att_ref[...].astype(BF16), wa_ref[...])
         + _sigmoid(gh_ref[...]) * _dot(hgo.astype(BF16), wh_ref[...]))
    x = x_ref[...] + _dot(m.astype(BF16), wo_ref[...])
    xo_ref[...] = x
    y = x * lax.rsqrt(jnp.mean(x * x, axis=-1, keepdims=True) + EPS)
    xn_ref[...] = (y * fg_ref[...]).astype(BF16)


def merge(x, att, o, hproj, hg_norm_g, ffn_g, wa, wh, wo, tm):
    t = x.shape[0]
    tm = min(tm, t)
    row = lambda w, cb: pl.BlockSpec((tm, w), lambda i: (i, cb))
    full = lambda a: pl.BlockSpec(a.shape, lambda i: (0, 0))
    hg_norm_g = hg_norm_g.reshape(1, HGK_W)
    ffn_g = ffn_g.reshape(1, D_MODEL)
    return pl.pallas_call(
        _merge_kernel,
        grid=(t // tm,),
        in_specs=[row(D_MODEL, 0), row(ATT_W, 0), row(HGK_W, 0), row(HGK_W, C_HG // HGK_W),
                  row(D_MODEL, C_GA // D_MODEL), row(D_MODEL, C_GH // D_MODEL),
                  full(hg_norm_g), full(ffn_g), full(wa), full(wh), full(wo)],
        out_specs=[row(D_MODEL, 0), row(D_MODEL, 0)],
        out_shape=[jax.ShapeDtypeStruct((t, D_MODEL), F32), jax.ShapeDtypeStruct((t, D_MODEL), BF16)],
        compiler_params=_cparams(("parallel",)),
        name="merge",
    )(x, att, o, hproj, hproj, hproj, hg_norm_g, ffn_g, wa, wh, wo)


def _topk_cols(x, k, payload=None):
    (vals, ids, pay), = _topk_cols_lockstep([x], k, None if payload is None else [payload])
    return vals, ids, pay


def _topk_cols_lockstep(xs, k, payloads=None):
    n, t = xs[0].shape
    row = lax.broadcasted_iota(jnp.int32, (n, t), 0)
    slot = lax.broadcasted_iota(jnp.int32, (k, t), 0)
    xs = list(xs)
    vals = [jnp.zeros((k, t), F32) for _ in xs]
    ids = [jnp.zeros((k, t), jnp.int32) for _ in xs]
    pay = [jnp.zeros((k, t), jnp.int32) for _ in xs]
    for it in range(k):
        here = slot == it
        for j, x in enumerate(xs):
            m = jnp.max(x, axis=0, keepdims=True)
            idx = jnp.min(jnp.where(x == m, row, n), axis=0, keepdims=True)
            hit = row == idx
            vals[j] = jnp.where(here, m, vals[j])
            ids[j] = jnp.where(here, idx, ids[j])
            if payloads is not None:
                pay[j] = jnp.where(here, jnp.max(jnp.where(hit, payloads[j], -1), axis=0, keepdims=True), pay[j])
            xs[j] = jnp.where(hit, NEG_INF, x)
    return [(vals[j], ids[j], pay[j] if payloads is not None else None) for j in range(len(xs))]


def _pair_candidates():
    pairs = [(a, b) for a in range(PEER_TOPK) for b in range(PEER_TOPK) if (a + 1) * (b + 1) <= PEER_TOPK]
    rows = -(-len(pairs) // 8) * 8
    sel = np.zeros((2, rows, PEER_TOPK), np.float32)
    for r, (a, b) in enumerate(pairs):
        sel[0, r, a] = 1.0
        sel[1, r, b] = 1.0
    return len(pairs), jnp.asarray(sel)


def _pick_rows(sel, x):
    return jnp.dot(sel, x, preferred_element_type=F32, precision=lax.Precision.HIGHEST)


def _peer_topk_kernel(q_ref, keys_ref, pair_ref, i1_ref, i2_ref, g_ref, *, n_pairs):
    o1, o2, og = [], [], []
    live = lax.broadcasted_iota(jnp.int32, (pair_ref.shape[1], q_ref.shape[0]), 0) < n_pairs
    cands, cidxs = [], []
    for h in range(PEER_HEADS):
        scores = [_dot_nt(keys_ref[side], q_ref[:, (h * 2 + side) * PEER_DH:(h * 2 + side + 1) * PEER_DH].astype(BF16))
                  for side in range(2)]
        (v1, i1, _), (v2, i2, _) = _topk_cols_lockstep(scores, PEER_TOPK)
        cands.append(jnp.where(live, _pick_rows(pair_ref[0], v1) + _pick_rows(pair_ref[1], v2), NEG_INF))
        cidxs.append((_pick_rows(pair_ref[0], i1.astype(F32)) * PEER_NKEYS
                      + _pick_rows(pair_ref[1], i2.astype(F32))).astype(jnp.int32))
    for h0 in range(0, PEER_HEADS, 2):
        for sv, _, e in _topk_cols_lockstep(cands[h0:h0 + 2], PEER_TOPK, cidxs[h0:h0 + 2]):
            ex = jnp.exp(sv - sv[0:1])
            o1.append(e // PEER_NKEYS)
            o2.append(e % PEER_NKEYS)
            og.append(ex / jnp.sum(ex, axis=0, keepdims=True))
    i1_ref[...] = jnp.concatenate(o1, axis=0).astype(F32).T.astype(jnp.int32)
    i2_ref[...] = jnp.concatenate(o2, axis=0).astype(F32).T.astype(jnp.int32)
    g_ref[...] = jnp.concatenate(og, axis=0).T


def peer_topk(q, keys_bf):
    t = q.shape[0]
    tb = LANE
    nsel = PEER_HEADS * PEER_TOPK
    n_pairs, pair_sel = _pair_candidates()
    out = lambda dt: jax.ShapeDtypeStruct((t, nsel), dt)
    return pl.pallas_call(
        functools.partial(_peer_topk_kernel, n_pairs=n_pairs),
        grid=(t // tb,),
        in_specs=[pl.BlockSpec((tb, q.shape[1]), lambda i: (i, 0)),
                  pl.BlockSpec(keys_bf.shape, lambda i: (0, 0, 0)),
                  pl.BlockSpec(pair_sel.shape, lambda i: (0, 0, 0))],
        out_specs=[pl.BlockSpec((tb, nsel), lambda i: (i, 0))] * 3,
        out_shape=[out(jnp.int32), out(jnp.int32), out(F32)],
        compiler_params=_cparams(("parallel",)),
        name="peer_topk",
    )(q, keys_bf, pair_sel)


GATE_UNROLL = 32


def _peer_dense_kernel(xn_ref, x_ref, i1_ref, i2_ref, g_ref, u_ref, v_ref, o_ref, acc_ref, gate_ref, *, n_a):
    j = pl.program_id(1)
    tm, nsel = i1_ref.shape

    @pl.when(j == 0)
    def _():
        acc_ref[...] = jnp.zeros(acc_ref.shape, F32)
        row = lax.broadcasted_iota(jnp.int32, (PEER_NKEYS, nsel), 0)

        def body(tt, _):
            for k in range(GATE_UNROLL):
                t = tt * GATE_UNROLL + k
                pa = jnp.where(row == i1_ref[pl.ds(t, 1), :], 1.0, 0.0).astype(BF16)
                gb = jnp.where(row == i2_ref[pl.ds(t, 1), :], g_ref[pl.ds(t, 1), :], 0.0).astype(BF16)
                gate_ref[pl.ds(pl.multiple_of(t * PEER_NKEYS, PEER_NKEYS), PEER_NKEYS), :] = _dot_nt(pa, gb)
            return 0

        lax.fori_loop(0, tm // GATE_UNROLL, body, 0)

    s = _dot_nt(xn_ref[...], u_ref[...])
    gate = jnp.concatenate([gate_ref[pl.ds(j * n_a + k, tm, stride=PEER_NKEYS), :] for k in range(n_a)], axis=1)
    acc_ref[...] += _dot((gate * _gelu(s)).astype(BF16), v_ref[...])

    @pl.when(j == pl.num_programs(1) - 1)
    def _():
        o_ref[...] = x_ref[...] + acc_ref[...]


def peer_dense(xn, x, i1, i2, g, u_bf, v_bf, tm, n_a):
    t = x.shape[0]
    tm = min(tm, t)
    te = n_a * PEER_NKEYS
    nsel = i1.shape[1]
    tok = lambda w: pl.BlockSpec((tm, w), lambda i, j: (i, 0))
    return pl.pallas_call(
        functools.partial(_peer_dense_kernel, n_a=n_a),
        grid=(t // tm, PEER_NKEYS // n_a),
        in_specs=[tok(D_MODEL), tok(D_MODEL), tok(nsel), tok(nsel), tok(nsel),
                  pl.BlockSpec((te, D_MODEL), lambda i, j: (j, 0)),
                  pl.BlockSpec((te, D_MODEL), lambda i, j: (j, 0))],
        out_specs=tok(D_MODEL),
        out_shape=jax.ShapeDtypeStruct((t, D_MODEL), F32),
        scratch_shapes=[pltpu.VMEM((tm, D_MODEL), F32), pltpu.VMEM((tm * PEER_NKEYS, PEER_NKEYS), F32)],
        compiler_params=_cparams(("parallel", "arbitrary")),
        name="peer_dense",
    )(xn, x, i1, i2, g, u_bf, v_bf)


def peer(xn_bf, x, wq_bf, keys_bf, u_bf, v_bf, tm):
    t = x.shape[0]
    q = matmul(xn_bf, wq_bf, min(512, t), 1024)
    i1, i2, g = peer_topk(q, keys_bf)
    return peer_dense(xn_bf, x, i1, i2, g, u_bf, v_bf, tm, 16)


ROWS = 8


def _pad_rows(x, rows=ROWS):
    return jnp.concatenate([x, jnp.zeros((rows - x.shape[0], x.shape[1]), x.dtype)], axis=0)


def _nsa_cmp_sample_kernel(q_ref, ckv_ref, tab_ref, cover_ref, oc_ref, sel_ref, *, nc, past, n_sel):
    nseq, nt = q_ref.shape[:2]
    t_pos = past + lax.broadcasted_iota(jnp.int32, (ROWS, 1), 0)
    for b in range(nseq):
        outs, imps = _cmp_branch(_pad_rows(q_ref[b]), t_pos, ckv_ref.at[b], tab_ref, cover_ref, nc)
        for g in range(KV_HEADS):
            sel_ref[b, g] = _topk_idx(imps[g], min(SEL_TOPN, n_sel))
        oc_ref[b] = jnp.concatenate(outs, axis=1)[:nt]


def nsa_cmp_sample(hproj3, ckv, tab, nc, past):
    nb, nt, _ = hproj3.shape
    ncp = ckv.shape[1]
    n_sel = -(-(past + nt) // SEL_BLOCK)
    n_sel_pad = -(-n_sel // LANE) * LANE
    cover = cover_matrix(ncp, n_sel_pad)
    per = 4 if nb % 4 == 0 else 1
    return pl.pallas_call(
        functools.partial(_nsa_cmp_sample_kernel, nc=nc, past=past, n_sel=n_sel),
        grid=(nb // per,),
        in_specs=[pl.BlockSpec((per, nt, ATT_W), lambda b: (b, 0, C_Q // ATT_W)),
                  pl.BlockSpec((per, ncp, KVB_W), lambda b: (b, 0, 0)),
                  pl.BlockSpec((N_HEADS, LANE), lambda b: (0, 0)),
                  pl.BlockSpec((ncp, n_sel_pad), lambda b: (0, 0))],
        out_specs=[pl.BlockSpec((per, nt, ATT_W), lambda b: (b, 0, 0)),
                   pl.BlockSpec((per, KV_HEADS, ROWS, LANE), lambda b: (b, 0, 0, 0))],
        out_shape=[jax.ShapeDtypeStruct((nb, nt, ATT_W), F32),
                   jax.ShapeDtypeStruct((nb, KV_HEADS, ROWS, LANE), jnp.int32)],
        compiler_params=_cparams(("parallel",)),
        name="nsa_cmp_sample",
    )(hproj3, ckv, tab, cover)


def _nsa_sel_sample_kernel(sel_ref, pt_ref, q_ref, ng_ref, oc_ref, kvs_ref, kvw_ref, win_ref, tab_ref, cache_ref,
                           o_ref, buf_ref, sem, *, layer, past, w_len):
    b = pl.program_id(0)
    nt = q_ref.shape[0]
    nslot = SEL_TOPN
    per_page = PAGE_SIZE // SEL_BLOCK

    def slot_copy(t, g, slot):
        blk = sel_ref[((b * nt + t) * KV_HEADS + g) * nslot + slot]
        in_past = blk * SEL_BLOCK < past
        blk_c = jnp.minimum(blk, past // SEL_BLOCK - 1)
        page = pt_ref[b, blk_c // per_page]
        src = cache_ref.at[layer, page, pl.ds(KVB_W, KVB_W), :]
        dst = buf_ref.at[t * KV_HEADS + g, :, pl.ds(slot * PAGE_SIZE, PAGE_SIZE)]
        return blk, in_past, pltpu.make_async_copy(src, dst, sem)

    for t in range(nt):
        for g in range(KV_HEADS):
            for slot in range(nslot):
                _, in_past, cp = slot_copy(t, g, slot)

                @pl.when(in_past)
                def _():
                    cp.start()

                @pl.when(jnp.logical_not(in_past))
                def _():
                    buf_ref[t * KV_HEADS + g, :, pl.ds(slot * PAGE_SIZE, PAGE_SIZE)] = jnp.zeros(
                        (KVB_W, PAGE_SIZE), F32)
    for t in range(nt):
        for g in range(KV_HEADS):
            for slot in range(nslot):
                _, in_past, cp = slot_copy(t, g, slot)

                @pl.when(in_past)
                def _():
                    cp.wait()

    gates = _sigmoid(ng_ref[...])
    nk = nslot * PAGE_SIZE
    lane_slot = lax.broadcasted_iota(jnp.int32, (1, nk), 1) // PAGE_SIZE
    lane_off = lax.broadcasted_iota(jnp.int32, (1, nk), 1) % PAGE_SIZE
    j_new = lax.broadcasted_iota(jnp.int32, (1, LANE), 1)
    kvn = jnp.concatenate([kvs_ref[...], kvw_ref[...]], axis=1)
    kvn_pad = _pad_rows(kvn, LANE).astype(BF16)
    win = win_ref[...].astype(BF16)
    w_idx = lax.broadcasted_iota(jnp.int32, (1, w_len + LANE), 1)
    rows_out = []
    for t in range(nt):
        t_pos = past + t
        heads_out = []
        for g in range(KV_HEADS):
            q4 = jnp.concatenate(
                [q_ref[t:t + 1, (g * HPG + h) * HEAD_DIM:(g * HPG + h + 1) * HEAD_DIM] for h in range(HPG)],
                axis=0)
            q4 = (q4 * ATT_SCALE).astype(BF16)

            def bias_of(dist):
                d = jnp.broadcast_to(jnp.clip(dist, 0, LANE - 1), (ROWS, dist.shape[1]))
                return jnp.concatenate(
                    [_bias_lookup(tab_ref[g * HPG + h:g * HPG + h + 1, :], d)[0:1] for h in range(HPG)], axis=0)

            pos = jnp.zeros((1, nk), jnp.int32)
            lane_blk = jnp.zeros((1, nk), jnp.int32)
            has_new = jnp.zeros((1, 1), jnp.int32)
            for slot in range(nslot):
                blk = sel_ref[((b * nt + t) * KV_HEADS + g) * nslot + slot]
                pos = jnp.where(lane_slot == slot, (blk // per_page) * PAGE_SIZE, pos)
                lane_blk = jnp.where(lane_slot == slot, blk, lane_blk)
                has_new = jnp.maximum(has_new, (blk * SEL_BLOCK >= past).astype(jnp.int32))
            pos = pos + lane_off
            in_block = pos // SEL_BLOCK == lane_blk
            tg = t * KV_HEADS + g
            k_t = buf_ref[tg, g * HEAD_DIM:(g + 1) * HEAD_DIM, :].astype(BF16)
            v_t = buf_ref[tg, (KV_HEADS + g) * HEAD_DIM:(KV_HEADS + g + 1) * HEAD_DIM, :].astype(BF16)
            k_new = kvn_pad[:, g * HEAD_DIM:(g + 1) * HEAD_DIM]
            v_new = kvn_pad[:, (KV_HEADS + g) * HEAD_DIM:(KV_HEADS + g + 1) * HEAD_DIM]
            dist = jnp.concatenate([t_pos - pos, t - j_new], axis=1)
            ok = jnp.concatenate([in_block & (pos < past), (j_new <= t) & (has_new > 0)], axis=1)
            s = jnp.concatenate([_dot(q4, k_t), _dot_nt(q4, k_new)], axis=1)
            s = jnp.where(ok, s + bias_of(dist), NEG)
            e = jnp.where(ok, jnp.exp(s - jnp.max(s, axis=-1, keepdims=True)), 0.0)
            p = (e / jnp.sum(e, axis=-1, keepdims=True)).astype(BF16)
            o_s = _dot_nt(p[:, :nk], v_t) + _dot(p[:, nk:], v_new)

            kw_new = kvn_pad[:, (2 * KV_HEADS + g) * HEAD_DIM:(2 * KV_HEADS + g + 1) * HEAD_DIM]
            vw_new = kvn_pad[:, (3 * KV_HEADS + g) * HEAD_DIM:(3 * KV_HEADS + g + 1) * HEAD_DIM]
            dist_w = t_pos - (past - w_len + w_idx)
            ok_w = (dist_w >= 0) & (dist_w <= WINDOW) & (w_idx < w_len + nt)
            s = jnp.concatenate([_dot(q4, win[g * HEAD_DIM:(g + 1) * HEAD_DIM]), _dot_nt(q4, kw_new)], axis=1)
            s = jnp.where(ok_w, s + bias_of(dist_w), NEG)
            e = jnp.where(ok_w, jnp.exp(s - jnp.max(s, axis=-1, keepdims=True)), 0.0)
            p = (e / jnp.sum(e, axis=-1, keepdims=True)).astype(BF16)
            o_w = (_dot_nt(p[:, :w_len], win[(KV_HEADS + g) * HEAD_DIM:(KV_HEADS + g + 1) * HEAD_DIM])
                   + _dot(p[:, w_len:], vw_new))

            for h in range(HPG):
                head = g * HPG + h
                o_c = oc_ref[t:t + 1, head * HEAD_DIM:(head + 1) * HEAD_DIM]
                heads_out.append(gates[t:t + 1, head:head + 1] * o_c
                                 + gates[t:t + 1, N_HEADS + head:N_HEADS + head + 1] * o_s[h:h + 1]
                                 + gates[t:t + 1, 2 * N_HEADS + head:2 * N_HEADS + head + 1] * o_w[h:h + 1])
        rows_out.append(jnp.concatenate(heads_out, axis=1))
    o_ref[...] = jnp.concatenate(rows_out, axis=0)


def nsa_sel_sample(hproj3, o_c, sel_flat, page_table, cache_t, win_t, tab, layer, past):
    nb, nt, _ = hproj3.shape
    w_len = win_t.shape[3]
    assert past % SEL_BLOCK == 0 and PAGE_SIZE % SEL_BLOCK == 0 and w_len % LANE == 0 and past >= w_len
    grid_spec = pltpu.PrefetchScalarGridSpec(
        num_scalar_prefetch=2,
        grid=(nb,),
        in_specs=[pl.BlockSpec((None, nt, ATT_W), lambda b, s, p: (b, 0, C_Q // ATT_W)),
                  pl.BlockSpec((None, nt, LANE), lambda b, s, p: (b, 0, C_NG // LANE)),
                  pl.BlockSpec((None, nt, ATT_W), lambda b, s, p: (b, 0, 0)),
                  pl.BlockSpec((None, nt, KVB_W), lambda b, s, p: (b, 0, C_KVS // KVB_W)),
                  pl.BlockSpec((None, nt, KVB_W), lambda b, s, p: (b, 0, C_KVW // KVB_W)),
                  pl.BlockSpec((None, None, KVB_W, w_len), lambda b, s, p: (layer, b, 0, 0)),
                  pl.BlockSpec((N_HEADS, LANE), lambda b, s, p: (0, 0)),
                  pl.BlockSpec(memory_space=pl.ANY)],
        out_specs=pl.BlockSpec((None, nt, ATT_W), lambda b, s, p: (b, 0, 0)),
        scratch_shapes=[pltpu.VMEM((nt * KV_HEADS, KVB_W, SEL_TOPN * PAGE_SIZE), F32),
                        pltpu.SemaphoreType.DMA(())],
    )
    return pl.pallas_call(
        functools.partial(_nsa_sel_sample_kernel, layer=layer, past=past, w_len=w_len),
        grid_spec=grid_spec,
        out_shape=jax.ShapeDtypeStruct((nb, nt, ATT_W), F32),
        compiler_params=_cparams(("arbitrary",)),
        name="nsa_sel_sample",
    )(sel_flat, page_table, hproj3, hproj3, o_c, hproj3, hproj3, win_t, tab, cache_t)


def _hgrn_sample_kernel(hq_ref, hf_ref, hi_ref, lb_ref, s_ref, o_ref, so_ref):
    nt = hq_ref.shape[0]
    live = (lax.broadcasted_iota(jnp.int32, (ROWS, 1), 0) < nt).astype(F32)
    q, k, logf = _hgrn_gates(_pad_rows(hq_ref[...]), _pad_rows(hf_ref[...]), lb_ref[...])
    q, k, logf = q * live, k * live, logf * live
    v = _pad_rows(hi_ref[...])
    rows = [logf[0:1]]
    for i in range(1, ROWS):
        rows.append(rows[-1] + logf[i:i + 1])
    a = jnp.concatenate(rows, axis=0)
    outs = _hgrn_intra(q, k, v, a, ROWS)
    a_last = a[ROWS - 1:ROWS]
    qe = (q * jnp.exp(a)).astype(BF16)
    kd = (k * jnp.exp(a_last - a)).astype(BF16)
    dec = jnp.exp(a_last)
    o_parts = []
    for h in range(HG_HEADS):
        hs = slice(h * HG_DK, (h + 1) * HG_DK)
        st = s_ref[h].T
        o_parts.append(outs[h] + _dot_nt(qe[:, hs], st.astype(BF16)))
        so_ref[h] = (dec[:, hs] * st + _dot_tn(v[:, hs].astype(BF16), kd[:, hs])).T
    o_ref[...] = jnp.concatenate(o_parts, axis=1)[:nt]


def hgrn_sample(hproj3, lb, state):
    nb, nt, _ = hproj3.shape
    col = lambda c: pl.BlockSpec((None, nt, HGK_W), lambda b: (b, 0, c // HGK_W))
    st_spec = pl.BlockSpec((None, HG_HEADS, HG_DK, HG_DV), lambda b: (b, 0, 0, 0))
    return pl.pallas_call(
        _hgrn_sample_kernel,
        grid=(nb,),
        in_specs=[col(C_HQ), col(C_HF), col(C_HI), pl.BlockSpec((1, HGK_W), lambda b: (0, 0)), st_spec],
        out_specs=[pl.BlockSpec((None, nt, HGK_W), lambda b: (b, 0, 0)), st_spec],
        out_shape=[jax.ShapeDtypeStruct((nb, nt, HGK_W), F32),
                   jax.ShapeDtypeStruct(state.shape, F32)],
        compiler_params=_cparams(("parallel",)),
        name="hgrn_sample",
    )(hproj3, hproj3, hproj3, lb.reshape(1, HGK_W), state)


def reorder_w_in(w):
    o = np.cumsum([0, ATT_W, KVB_W, KVB_W, KVB_W, N_GATE, HGK_W, HGK_W, HGK_W, HGK_W, D_MODEL, D_MODEL])
    q, kvc, kvs, kvw, ng, hq, hf, hi, hg, ga, gh = [w[:, o[i]:o[i + 1]] for i in range(11)]
    pad = jnp.zeros((w.shape[0], N_PROJ - C_NG - N_GATE), w.dtype)
    return jnp.concatenate([q, hq, ga, gh, hf, hi, hg, kvc, kvs, kvw, ng, pad], axis=1).astype(BF16)


def prompt_layer(x, lp, tabs):
    t = x.shape[0]
    tab, ntab, wtab = tabs
    hproj = norm_matmul(x, lp["attn_g"], lp["w_in"], min(512, t), 1408)
    y = cproj_rows(hproj, C_KVC // KVB_W, lp["wbig"], 2048)
    ckv = cmp_mlp(y[None], lp["pe_term"], lp["w2big"])[0]
    o_c, mask = nsa_cmp_prompt(hproj, ckv, tab, t // CMP_STRIDE - 1)
    ka, vt = attention_kv_layout(hproj)
    att = nsa_sel_prompt(hproj, o_c, mask, ka, vt, tab, ntab, wtab)
    o_hg, st = hgrn_prompt(hproj, lp["lb"], 128)
    x, xn = merge(x, att, o_hg, hproj, lp["hg_norm_g"], lp["ffn_g"], lp["wa"], lp["wh"], lp["wo"], 256)
    x = peer(xn, x, lp["peer_wq"], lp["peer_keys"], lp["peer_u"], lp["peer_v"], 256)
    return x, hproj[:, C_KVC:C_KVC + 2 * KVB_W], hproj[:, C_KVW:C_KVW + KVB_W], jnp.swapaxes(st, 1, 2)


def layer_params(l, norm_attn_g, norm_ffn_g, w_in, cmp_pe, cmp_w1, cmp_w2, lower_bounds, hg_norm_g,
                 w_br_attn, w_br_hg, w_out, peer_wq, peer_keys, peer_u, peer_v):
    wbig, w2big, pe_rows = compress_weights(cmp_pe[l], cmp_w1[l], cmp_w2[l])
    return dict(attn_g=norm_attn_g[l], ffn_g=norm_ffn_g[l], w_in=reorder_w_in(w_in[l]),
                wbig=wbig, w2big=w2big, pe_term=pe_term_from(pe_rows, wbig),
                lb=lower_bounds[l], hg_norm_g=hg_norm_g[l],
                wa=w_br_attn[l].astype(BF16), wh=w_br_hg[l].astype(BF16), wo=w_out[l].astype(BF16),
                peer_wq=peer_wq[l].astype(BF16), peer_keys=peer_keys[l].astype(BF16),
                peer_u=peer_u[l].astype(BF16), peer_v=peer_v[l].astype(BF16))


def sample_layer(x, lp, tab, layer, cache_t, win_t, state, page_table, nb, nt):
    past = page_table.shape[1] * PAGE_SIZE
    assert nt < CMP_STRIDE, "new tokens never complete a compression chunk"
    hproj = norm_matmul(x, lp["attn_g"], lp["w_in"], x.shape[0], 1408)
    hproj3 = hproj.reshape(nb, nt, N_PROJ)
    y = cproj_pages(cache_t, layer, page_table, lp["wbig"], 16)
    ckv = cmp_mlp(y, lp["pe_term"], lp["w2big"])
    o_c, sel = nsa_cmp_sample(hproj3, ckv, tab, past // CMP_STRIDE - 1, past)
    sel_flat = jnp.transpose(sel[:, :, :nt, :SEL_TOPN], (0, 2, 1, 3)).reshape(-1)
    att = nsa_sel_sample(hproj3, o_c, sel_flat, page_table, cache_t, win_t, tab, layer, past)
    o_hg, st = hgrn_sample(hproj3, lp["lb"], state)
    x, xn = merge(x, att.reshape(nb * nt, ATT_W), o_hg.reshape(nb * nt, HGK_W), hproj, lp["hg_norm_g"],
                  lp["ffn_g"], lp["wa"], lp["wh"], lp["wo"], 256)
    x = peer(xn, x, lp["peer_wq"], lp["peer_keys"], lp["peer_u"], lp["peer_v"], 256)
    return x, hproj[:, C_KVC:C_KVC + 2 * KVB_W], hproj[:, C_KVW:C_KVW + KVB_W], st


def kernel(x_prompt, x_sample, cache_kv, cache_win, state_hgrn, page_table, norm_attn_g, norm_ffn_g, final_norm_g, w_in, cmp_pe, cmp_w1, cmp_w2, rel_bias, hg_lb_logits, hg_norm_g, w_br_attn, w_br_hg, w_out, peer_wq, peer_keys, peer_u, peer_v):
    depth = w_in.shape[0]
    bp, seq, _ = x_prompt.shape
    nb, nt, _ = x_sample.shape
    assert bp == 1
    p_lb = jax.nn.softmax(hg_lb_logits.astype(F32), axis=0)
    lower_bounds = jnp.cumsum(p_lb, axis=0) - p_lb[0]
    tab = bias_table(rel_bias)
    tabs = (tab, *distance_tables(tab))
    n_pool = cache_kv.shape[1]
    w_buf = cache_win.shape[2]
    cache_t = jnp.transpose(cache_kv, (0, 1, 3, 4, 5, 2)).reshape(depth, n_pool, 2 * KVB_W, PAGE_SIZE)
    win_t = jnp.transpose(cache_win, (0, 1, 3, 4, 5, 2)).reshape(depth, nb, KVB_W, w_buf)
    xp = x_prompt.reshape(seq, D_MODEL)
    xs = x_sample.reshape(nb * nt, D_MODEL)
    kv_p, win_p, hs_p, kv_s, win_s, hs_s = [], [], [], [], [], []
    for l in range(depth):
        lp = layer_params(l, norm_attn_g, norm_ffn_g, w_in, cmp_pe, cmp_w1, cmp_w2, lower_bounds, hg_norm_g,
                          w_br_attn, w_br_hg, w_out, peer_wq, peer_keys, peer_u, peer_v)
        xp, kv, kw, st = prompt_layer(xp, lp, tabs)
        kv_p.append(kv.reshape(bp, seq, 4, KV_HEADS, HEAD_DIM))
        win_p.append(kw[seq - min(WINDOW, seq):].reshape(bp, -1, 2, KV_HEADS, HEAD_DIM))
        hs_p.append(st[None])
        xs, kv, kw, st = sample_layer(xs, lp, tab, l, cache_t, win_t, state_hgrn[l], page_table, nb, nt)
        kv_s.append(kv.reshape(nb, nt, 4, KV_HEADS, HEAD_DIM))
        win_s.append(jnp.concatenate([cache_win[l], kw.reshape(nb, nt, 2, KV_HEADS, HEAD_DIM)], axis=1)[:, nt:])
        hs_s.append(st)
    y_prompt = rmsnorm(xp, final_norm_g, min(512, seq)).reshape(bp, seq, D_MODEL)
    y_sample = rmsnorm(xs, final_norm_g, nb * nt).reshape(nb, nt, D_MODEL)
    return (y_prompt, y_sample, jnp.stack(kv_p), jnp.stack(win_p), jnp.stack(hs_p),
            jnp.stack(kv_s), jnp.stack(win_s), jnp.stack(hs_s))
```

```python
import functools
import math

import jax
import jax.numpy as jnp
import numpy as np
from jax import lax
from jax.experimental import pallas as pl
from jax.experimental.pallas import tpu as pltpu

F32 = jnp.float32
BF16 = jnp.bfloat16

D_MODEL = 1024
PAGE_SIZE = 128
N_HEADS = 8
HEAD_DIM = 64
KV_HEADS = 2
HPG = N_HEADS // KV_HEADS
CMP_STRIDE = 16
CMP_BLOCK = 32
SEL_BLOCK = 64
SEL_TOPN = 16
WINDOW = 512
ATT_SCALE = HEAD_DIM ** -0.5
N_BUCKETS = 32
MAX_DISTANCE = 128
HG_HEADS = 4
HG_DK = 128
HG_DV = 128
PEER_HEADS = 8
PEER_NKEYS = 128
PEER_DH = 128
PEER_TOPK = 16
ATT_W = N_HEADS * HEAD_DIM
KVB_W = 2 * KV_HEADS * HEAD_DIM
HGK_W = HG_HEADS * HG_DK
N_GATE = 3 * N_HEADS
EPS = 1e-6
NEG = -1e30
BIG = 1e6
TINY = 1e-30
NEG_INF = float("-inf")

LANE = 128
VMEM_LIMIT = 56 * 1024 * 1024

C_Q, C_HQ, C_GA, C_GH, C_HF, C_HI, C_HG = 0, 512, 1024, 2048, 3072, 3584, 4096
C_KVC, C_KVS, C_KVW, C_NG, N_PROJ = 4608, 4864, 5120, 5376, 5632
SUB = 16
TQ = 128
TK = 512


def _cparams(sem, vmem=VMEM_LIMIT):
    return pltpu.CompilerParams(dimension_semantics=sem, vmem_limit_bytes=vmem)


def _dot(a, b):
    return jnp.dot(a, b, preferred_element_type=F32)


def _dot_nt(a, b):
    return lax.dot_general(a, b, (((1,), (1,)), ((), ())), preferred_element_type=F32)


def _dot_tn(a, b):
    return lax.dot_general(a, b, (((0,), (0,)), ((), ())), preferred_element_type=F32)


def _sigmoid(x):
    return 1.0 / (1.0 + jnp.exp(-x))


def _gelu(x):
    return 0.5 * x * (1.0 + lax.erf(x * (2.0 ** -0.5)))


def _norm_matmul_kernel(x_ref, g_ref, w_ref, o_ref, xn_ref):
    @pl.when(pl.program_id(1) == 0)
    def _():
        x = x_ref[...]
        y = x * lax.rsqrt(jnp.mean(x * x, axis=-1, keepdims=True) + EPS)
        xn_ref[...] = (y * g_ref[...]).astype(BF16)

    o_ref[...] = _dot(xn_ref[...], w_ref[...])


def norm_matmul(x, g, w, tm, tn):
    m, k = x.shape
    n = w.shape[1]
    return pl.pallas_call(
        _norm_matmul_kernel,
        grid=(m // tm, n // tn),
        in_specs=[pl.BlockSpec((tm, k), lambda i, j: (i, 0)),
                  pl.BlockSpec((1, k), lambda i, j: (0, 0)),
                  pl.BlockSpec((k, tn), lambda i, j: (0, j))],
        out_specs=pl.BlockSpec((tm, tn), lambda i, j: (i, j)),
        out_shape=jax.ShapeDtypeStruct((m, n), F32),
        scratch_shapes=[pltpu.VMEM((tm, k), BF16)],
        compiler_params=_cparams(("parallel", "arbitrary")),
        name="norm_matmul",
    )(x, g.reshape(1, k), w)


def _matmul_kernel(x_ref, w_ref, o_ref):
    o_ref[...] = _dot(x_ref[...], w_ref[...])


def matmul(x, w, tm, tn):
    m, k = x.shape
    n = w.shape[1]
    return pl.pallas_call(
        _matmul_kernel,
        grid=(m // tm, n // tn),
        in_specs=[pl.BlockSpec((tm, k), lambda i, j: (i, 0)),
                  pl.BlockSpec((k, tn), lambda i, j: (0, j))],
        out_specs=pl.BlockSpec((tm, tn), lambda i, j: (i, j)),
        out_shape=jax.ShapeDtypeStruct((m, n), F32),
        compiler_params=_cparams(("parallel", "arbitrary")),
        name="matmul",
    )(x, w)


def _rmsnorm_kernel(x_ref, g_ref, o_ref):
    x = x_ref[...]
    o_ref[...] = x * lax.rsqrt(jnp.mean(x * x, axis=-1, keepdims=True) + EPS) * g_ref[...]


def rmsnorm(x, g, tm):
    m, k = x.shape
    return pl.pallas_call(
        _rmsnorm_kernel,
        grid=(m // tm,),
        in_specs=[pl.BlockSpec((tm, k), lambda i: (i, 0)), pl.BlockSpec((1, k), lambda i: (0, 0))],
        out_specs=pl.BlockSpec((tm, k), lambda i: (i, 0)),
        out_shape=jax.ShapeDtypeStruct((m, k), F32),
        compiler_params=_cparams(("parallel",)),
        name="rmsnorm",
    )(x, g.reshape(1, k))


CPROJ_GROUPS = 2


def _cproj_from_pieces(piece_fn, w_ref, y_ref):
    acc = jnp.zeros(y_ref.shape, F32)
    for s in range(CMP_STRIDE):
        acc = acc + _dot(piece_fn(s).astype(BF16), w_ref[s])
    y_ref[...] = acc


def _cproj_rows_kernel(r0_ref, r1_ref, w_ref, y_ref):
    nch = y_ref.shape[0]

    def piece(s):
        return jnp.concatenate([r[pl.ds(s, nch, stride=CMP_STRIDE), :] for r in (r0_ref, r1_ref)], axis=1)

    _cproj_from_pieces(piece, w_ref, y_ref)


def cproj_rows(rows_arr, col_block, wbig, rb):
    t = rows_arr.shape[0]
    rb = min(rb, t)
    half = lambda k: pl.BlockSpec((rb, LANE), lambda i: (i, 2 * col_block + k))
    return pl.pallas_call(
        _cproj_rows_kernel,
        grid=(t // rb,),
        in_specs=[half(0), half(1), pl.BlockSpec(wbig.shape, lambda i: (0, 0, 0))],
        out_specs=pl.BlockSpec((rb // CMP_STRIDE, 2 * KVB_W), lambda i: (i, 0)),
        out_shape=jax.ShapeDtypeStruct((t // CMP_STRIDE, 2 * KVB_W), F32),
        compiler_params=_cparams(("parallel",)),
        name="cproj_rows",
    )(rows_arr, rows_arr, wbig)


def _cproj_pages_kernel(pt_ref, *refs):
    del pt_ref
    n_scr = 2 * CPROJ_GROUPS
    pages, w_ref, y_ref, scr = refs[:-2 - n_scr], refs[-2 - n_scr], refs[-1 - n_scr], refs[-n_scr:]
    per_grp = len(pages) // CPROJ_GROUPS
    nch = y_ref.shape[0] // CPROJ_GROUPS
    for gi in range(CPROJ_GROUPS):
        lo_ref, hi_ref = scr[2 * gi], scr[2 * gi + 1]
        for p in range(per_grp):
            page = pages[gi * per_grp + p]
            rows = pl.ds(p * PAGE_SIZE, PAGE_SIZE)
            lo_ref[rows, :] = page[0:LANE, :].T
            hi_ref[rows, :] = page[LANE:2 * LANE, :].T

        def piece(s, lo_ref=lo_ref, hi_ref=hi_ref):
            return jnp.concatenate([r[pl.ds(s, nch, stride=CMP_STRIDE), :] for r in (lo_ref, hi_ref)], axis=1)

        _cproj_from_pieces(piece, w_ref, y_ref.at[pl.ds(gi * nch, nch)])


def cproj_pages(cache_t, layer, page_table, wbig, pg):
    nb, npages = page_table.shape
    per = PAGE_SIZE // CMP_STRIDE

    def page_spec(p):
        return pl.BlockSpec((None, None, KVB_W, PAGE_SIZE), lambda b, j, pt: (layer, pt[b, j * pg + p], 0, 0))

    grid_spec = pltpu.PrefetchScalarGridSpec(
        num_scalar_prefetch=1,
        grid=(nb, npages // pg),
        in_specs=[page_spec(p) for p in range(pg)] + [pl.BlockSpec(wbig.shape, lambda b, j, pt: (0, 0, 0))],
        out_specs=pl.BlockSpec((None, pg * per, 2 * KVB_W), lambda b, j, pt: (b, j, 0)),
        scratch_shapes=[pltpu.VMEM((pg // CPROJ_GROUPS * PAGE_SIZE, LANE), F32)] * (2 * CPROJ_GROUPS),
    )
    return pl.pallas_call(
        _cproj_pages_kernel,
        grid_spec=grid_spec,
        out_shape=jax.ShapeDtypeStruct((nb, npages * per, 2 * KVB_W), F32),
        compiler_params=_cparams(("parallel", "arbitrary")),
        name="cproj_pages",
    )(page_table, *([cache_t] * pg), wbig)


def _cmp_mlp_kernel(y_ref, pe_ref, w2_ref, o_ref):
    y = y_ref[...]
    n = y.shape[0]
    nxt = pltpu.roll(y[:, KVB_W:], n - 1, 0)
    hsum = y[:, :KVB_W] + nxt + pe_ref[...]
    o_ref[...] = _dot(_gelu(hsum).astype(BF16), w2_ref[...]).astype(o_ref.dtype)


def cmp_mlp(y, pe_term, w2big):
    b, ncp, _ = y.shape
    return pl.pallas_call(
        _cmp_mlp_kernel,
        grid=(b,),
        in_specs=[pl.BlockSpec((None, ncp, 2 * KVB_W), lambda i: (i, 0, 0)),
                  pl.BlockSpec((1, KVB_W), lambda i: (0, 0)),
                  pl.BlockSpec((KVB_W, KVB_W), lambda i: (0, 0))],
        out_specs=pl.BlockSpec((None, ncp, KVB_W), lambda i: (i, 0, 0)),
        out_shape=jax.ShapeDtypeStruct((b, ncp, KVB_W), BF16),
        compiler_params=_cparams(("parallel",)),
        name="cmp_mlp",
    )(y, pe_term, w2big)


def compress_weights(pe, w1, w2):
    w1r = w1.reshape(2, 2, CMP_STRIDE, HEAD_DIM, HEAD_DIM)
    eye = jnp.eye(2, dtype=F32)
    wbig = jnp.einsum('crsdh,cC,gG->scgdrCGh', w1r, eye, eye).reshape(CMP_STRIDE, KVB_W, 2 * KVB_W)
    w2big = jnp.einsum('chd,cC,gG->cghCGd', w2, eye, eye).reshape(KVB_W, KVB_W)
    per = pe.reshape(2, 2, CMP_STRIDE, HEAD_DIM)
    pe_rows = jnp.broadcast_to(jnp.transpose(per, (1, 2, 0, 3))[:, :, :, None, :],
                               (2, CMP_STRIDE, 2, KV_HEADS, HEAD_DIM)).reshape(2 * CMP_STRIDE, KVB_W)
    pe_rows = jnp.pad(pe_rows, ((0, 8 * CMP_STRIDE - 2 * CMP_STRIDE), (0, 0)))
    return wbig.astype(BF16), w2big.astype(BF16), pe_rows


def pe_term_from(pe_rows, wbig):
    ype = cproj_rows(pe_rows, 0, wbig, pe_rows.shape[0])
    return ype[0:1, :KVB_W] + ype[1:2, KVB_W:]


def bias_table(rel_bias):
    n = np.arange(LANE)
    exact = N_BUCKETS // 2
    nf = np.maximum(n, 1).astype(np.float32)
    large = exact + (np.log(nf / np.float32(exact)) / np.float32(math.log(MAX_DISTANCE / exact))
                     * np.float32(N_BUCKETS - exact)).astype(np.int32)
    bucket = np.where(n < exact, n, np.minimum(large, N_BUCKETS - 1))
    assert bucket[LANE - 1] == N_BUCKETS - 1
    return jnp.transpose(rel_bias.astype(F32)[bucket], (1, 0))


def _bias_lookup(tab_row, dist_clipped):
    r, w = dist_clipped.shape
    src = jnp.broadcast_to(tab_row, (r, LANE))
    parts = [jnp.take_along_axis(src, dist_clipped[:, j * LANE:(j + 1) * LANE], axis=1)
             for j in range(w // LANE)]
    return parts[0] if len(parts) == 1 else jnp.concatenate(parts, axis=1)


def _topk_mask(x, k):
    r, n = x.shape
    col = lax.broadcasted_iota(jnp.int32, (r, n), 1)
    sel = jnp.zeros((r, n), F32)
    for _ in range(k):
        m = jnp.max(x, axis=-1, keepdims=True)
        idx = jnp.min(jnp.where(x == m, col, n), axis=-1, keepdims=True)
        hit = col == idx
        sel = jnp.where(hit, 1.0, sel)
        x = jnp.where(hit, NEG_INF, x)
    return sel


def _topk_idx(x, k):
    r, n = x.shape
    col = lax.broadcasted_iota(jnp.int32, (r, n), 1)
    lane = lax.broadcasted_iota(jnp.int32, (r, LANE), 1)
    out = jnp.zeros((r, LANE), jnp.int32)
    for it in range(k):
        m = jnp.max(x, axis=-1, keepdims=True)
        idx = jnp.min(jnp.where(x == m, col, n), axis=-1, keepdims=True)
        out = jnp.where(lane == it, idx, out)
        x = jnp.where(col == idx, NEG_INF, x)
    return out


def _cmp_branch(q, t_pos, ckv_ref, tab_ref, cover_ref, nc, ncols=None):
    r = q.shape[0]
    ncols = ckv_ref.shape[0] if ncols is None else ncols
    n_sel = cover_ref.shape[1]
    c_idx = lax.broadcasted_iota(jnp.int32, (1, ncols), 1)
    dist = t_pos - (c_idx * CMP_STRIDE + CMP_BLOCK - 1)
    valid = (dist >= 0) & (c_idx < nc)
    any_valid = (t_pos >= CMP_BLOCK - 1).astype(F32)
    dcl = jnp.clip(dist, 0, LANE - 1)
    blk = lax.broadcasted_iota(jnp.int32, (1, n_sel), 1)
    cur = t_pos // SEL_BLOCK
    forced = (blk == 0) | (blk == cur) | (blk == cur - 1)
    outs, imps = [], []
    for g in range(KV_HEADS):
        kc = ckv_ref[:ncols, g * HEAD_DIM:(g + 1) * HEAD_DIM]
        vc = ckv_ref[:ncols, (KV_HEADS + g) * HEAD_DIM:(KV_HEADS + g + 1) * HEAD_DIM]
        psum = jnp.zeros((r, ncols), F32)
        for h in range(HPG):
            head = g * HPG + h
            qh = (q[:, head * HEAD_DIM:(head + 1) * HEAD_DIM] * ATT_SCALE).astype(BF16)
            s = _dot_nt(qh, kc) + _bias_lookup(tab_ref[head:head + 1, :], dcl)
            s = jnp.where(valid, s, NEG)
            e = jnp.exp(s - jnp.max(s, axis=-1, keepdims=True))
            p = e * (any_valid / jnp.sum(e, axis=-1, keepdims=True))
            outs.append(_dot(p.astype(BF16), vc))
            psum = psum + p
        p_hi = psum.astype(BF16)
        p_lo = (psum - p_hi.astype(F32)).astype(BF16)
        imp = _dot(p_hi, cover_ref[:ncols]) + _dot(p_lo, cover_ref[:ncols])
        imps.append(jnp.where(forced, BIG, jnp.where(blk <= cur, imp, -BIG)))
    return outs, imps


def _topk_mask_cols(x, k):
    n, t = x.shape
    row = lax.broadcasted_iota(jnp.int32, (n, t), 0)
    forced = x == BIG
    sel = forced.astype(F32)
    x = jnp.where(forced, NEG_INF, x)
    for _ in range(k - 3):
        m = jnp.max(x, axis=0, keepdims=True)
        idx = jnp.min(jnp.where(x == m, row, n), axis=0, keepdims=True)
        hit = row == idx
        sel = jnp.where(hit, 1.0, sel)
        x = jnp.where(hit, NEG_INF, x)
    return sel


def _nsa_cmp_kernel(q_ref, ckv_ref, tab_ref, cover_ref, oc_ref, mask_ref, *, nc, n_sel):
    i = pl.program_id(0)
    t_pos = i * TQ + lax.broadcasted_iota(jnp.int32, (TQ, 1), 0)
    ncp = ckv_ref.shape[0]
    part = max(ncp // 4, LANE)
    last_c = (i * TQ + TQ - CMP_BLOCK) // CMP_STRIDE
    need = jnp.clip(last_c // part, 0, ncp // part - 1)

    for n_parts in range(1, ncp // part + 1):
        @pl.when(need == n_parts - 1)
        def _(n_parts=n_parts):
            outs, imps = _cmp_branch(q_ref[...], t_pos, ckv_ref, tab_ref, cover_ref, nc, n_parts * part)
            for g in range(KV_HEADS):
                mask_ref[g] = _topk_mask_cols(imps[g].T, min(SEL_TOPN, n_sel))
            oc_ref[...] = jnp.concatenate(outs, axis=1)


def cover_matrix(ncp, n_sel):
    ci = np.arange(ncp)[:, None] * CMP_STRIDE
    sj = np.arange(n_sel)[None, :] * SEL_BLOCK
    return jnp.asarray(((ci < sj + SEL_BLOCK) & (ci + CMP_BLOCK > sj)).astype(np.float32), dtype=BF16)


def nsa_cmp_prompt(hproj, ckv, tab, nc):
    t = hproj.shape[0]
    ncp = ckv.shape[0]
    n_sel = t // SEL_BLOCK
    cover = cover_matrix(ncp, n_sel)
    return pl.pallas_call(
        functools.partial(_nsa_cmp_kernel, nc=nc, n_sel=n_sel),
        grid=(t // TQ,),
        in_specs=[pl.BlockSpec((TQ, ATT_W), lambda i: (i, C_Q // ATT_W)),
                  pl.BlockSpec((ncp, KVB_W), lambda i: (0, 0)),
                  pl.BlockSpec((N_HEADS, LANE), lambda i: (0, 0)),
                  pl.BlockSpec((ncp, n_sel), lambda i: (0, 0))],
        out_specs=[pl.BlockSpec((TQ, ATT_W), lambda i: (i, 0)),
                   pl.BlockSpec((KV_HEADS, n_sel, TQ), lambda i: (0, 0, i))],
        out_shape=[jax.ShapeDtypeStruct((t, ATT_W), F32),
                   jax.ShapeDtypeStruct((KV_HEADS, n_sel, t), F32)],
        compiler_params=_cparams(("parallel",)),
        name="nsa_cmp_prompt",
    )(hproj, ckv, tab, cover)


MASKED = -1e9
NEAR = 2 * TQ
W_KEYS = WINDOW + TQ
BLK_PER_TILE = TK // SEL_BLOCK


VT_ROWS = HEAD_DIM + 16
MASK_ROWS = 16
HT = HPG * TQ
FLAG_LANE = HEAD_DIM + BLK_PER_TILE
FAR_UNROLL = 4


def _online_step_t(carry, st, shift, vt):
    m, acc = carry
    m_new = jnp.maximum(m, jnp.max(st, axis=0, keepdims=True) + shift)
    p = jnp.exp(st - (m_new - shift)).astype(BF16)
    return m_new, jnp.exp(m - m_new) * acc + _dot(vt, p)


def _nsa_sel_kernel(q_ref, ng_ref, oc_ref, mask_ref, ka_ref, vt_ref, tab_ref, ntab_ref, wtab_ref, o_ref,
                    sta_ref, stb_ref, *, n_sel):
    i = pl.program_id(0)
    qt = (q_ref[...] * ATT_SCALE).T
    oct_ = oc_ref[...].T
    gt = _sigmoid(ng_ref[...]).T
    near_blk0 = 2 * i - 2
    n_far = (jnp.maximum(i - 1, 0) * TQ + TK - 1) // TK
    blk8 = lax.broadcasted_iota(jnp.int32, (BLK_PER_TILE, 1), 0)
    init = (jnp.full((1, HT), NEG, F32), jnp.zeros((VT_ROWS, HT), F32))
    zero_shift = jnp.zeros((1, HT), F32)
    chunks_per_tile = TK // LANE

    before_start = jnp.where(lax.broadcasted_iota(jnp.int32, (MASK_ROWS, HT), 0) == FLAG_LANE - HEAD_DIM,
                             MASKED, 0.0).astype(BF16)
    qtg, w_plain, far_shift = [], [], []
    for g in range(KV_HEADS):
        heads = range(g * HPG, (g + 1) * HPG)
        qg = jnp.concatenate([qt[h * HEAD_DIM:(h + 1) * HEAD_DIM] for h in heads], axis=1).astype(BF16)
        qtg.append(qg)
        w_plain.append(jnp.concatenate([qg, before_start, jnp.zeros((LANE - HEAD_DIM - MASK_ROWS, HT), BF16)], axis=0))
        far_shift.append(jnp.concatenate(
            [jnp.broadcast_to(tab_ref[h:h + 1, LANE - 1:LANE], (1, TQ)) for h in heads], axis=1))

    def vt_tile(c0, n, kind):
        return jnp.concatenate([vt_ref[c0 + c, kind] for c in range(n)], axis=1)

    last_tile = (ka_ref.shape[0] - WINDOW) // TK - 1

    def far_logits(tile, dst_ref):
        kt = jnp.minimum(tile, last_tile)
        k0 = pl.multiple_of(WINDOW + kt * TK, TK)
        r0 = pl.multiple_of(kt * BLK_PER_TILE, BLK_PER_TILE)
        for g in range(KV_HEADS):
            mt = mask_ref[g, pl.ds(r0, BLK_PER_TILE), :]
            keep = (mt > 0.5) & (tile * BLK_PER_TILE + blk8 < near_blk0)
            mneg = jnp.where(keep, 0.0, MASKED)
            m16 = jnp.concatenate([jnp.concatenate([mneg] * HPG, axis=1),
                                   jnp.zeros((MASK_ROWS - BLK_PER_TILE, HT), F32)], axis=0).astype(BF16)
            w = jnp.concatenate([qtg[g], m16, jnp.zeros((LANE - HEAD_DIM - MASK_ROWS, HT), BF16)], axis=0)
            dst_ref[g] = _dot(ka_ref[pl.ds(k0, TK), g * LANE:(g + 1) * LANE], w)

    def far_softmax(kt, src_ref, carry):
        c0 = (WINDOW // LANE) + jnp.minimum(kt, last_tile) * chunks_per_tile
        return tuple(_online_step_t(carry[g], src_ref[g], far_shift[g], vt_tile(c0, chunks_per_tile, g))
                     for g in range(KV_HEADS))

    @pl.when(n_far > 0)
    def _():
        far_logits(0, sta_ref)

    def far_body(trip, carry):
        bufs = (sta_ref, stb_ref)
        for u in range(FAR_UNROLL):
            tile = trip * FAR_UNROLL + u
            far_logits(tile + 1, bufs[(u + 1) % 2])
            carry = far_softmax(tile, bufs[u % 2], carry)
        return carry

    carry = lax.fori_loop(0, (n_far + FAR_UNROLL - 1) // FAR_UNROLL, far_body, (init, init))

    kn0 = pl.multiple_of(WINDOW + (i - 1) * TQ, TQ)
    base = pl.multiple_of(jnp.clip((near_blk0 // BLK_PER_TILE) * BLK_PER_TILE, 0, n_sel - MASK_ROWS), BLK_PER_TILE)
    key_blk = near_blk0 + lax.broadcasted_iota(jnp.int32, (NEAR, MASK_ROWS), 0) // SEL_BLOCK
    expand = (key_blk == base + lax.broadcasted_iota(jnp.int32, (NEAR, MASK_ROWS), 1)).astype(BF16)
    kw0 = pl.multiple_of(i * TQ, TQ)
    rows_t = []
    st_near, st_win = [], []
    for g in range(KV_HEADS):
        mneg = jnp.where(mask_ref[g, pl.ds(base, MASK_ROWS), :] > 0.5, 0.0, MASKED).astype(BF16)
        sel_add = _dot(expand, mneg)
        st = _dot(ka_ref[pl.ds(kn0, NEAR), g * LANE:(g + 1) * LANE], w_plain[g])
        st_near.append(st + ntab_ref[g] + jnp.concatenate([sel_add] * HPG, axis=1))
        st = _dot(ka_ref[pl.ds(kw0, W_KEYS), (KV_HEADS + g) * LANE:(KV_HEADS + g + 1) * LANE], w_plain[g])
        st_win.append(st + wtab_ref[g])
    for g in range(KV_HEADS):
        _, acc = _online_step_t(carry[g], st_near[g], zero_shift, vt_tile(kn0 // LANE, NEAR // LANE, g))
        os_t = acc[:HEAD_DIM] / acc[HEAD_DIM:HEAD_DIM + 1]
        _, acc = _online_step_t(init, st_win[g], zero_shift, vt_tile(kw0 // LANE, W_KEYS // LANE, KV_HEADS + g))
        ow_t = acc[:HEAD_DIM] / acc[HEAD_DIM:HEAD_DIM + 1]

        for h in range(HPG):
            head = g * HPG + h
            ls = slice(h * TQ, (h + 1) * TQ)
            rows_t.append(gt[head:head + 1] * oct_[head * HEAD_DIM:(head + 1) * HEAD_DIM]
                          + gt[N_HEADS + head:N_HEADS + head + 1] * os_t[:, ls]
                          + gt[2 * N_HEADS + head:2 * N_HEADS + head + 1] * ow_t[:, ls])
    o_ref[...] = jnp.concatenate(rows_t, axis=0).T


def _distance_tables_kernel(tab_ref, ntab_ref, wtab_ref):
    head = pl.program_id(0) * HPG + pl.program_id(1)
    row = tab_ref[pl.ds(head, 1), :]
    for out_ref, start, max_dist in ((ntab_ref, TQ, None), (wtab_ref, WINDOW, WINDOW)):
        keys = out_ref.shape[0]
        dist = (start + lax.broadcasted_iota(jnp.int32, (keys, TQ), 1)
                - lax.broadcasted_iota(jnp.int32, (keys, TQ), 0))
        ok = dist >= 0 if max_dist is None else (dist >= 0) & (dist <= max_dist)
        out_ref[...] = jnp.where(ok, _bias_lookup(row, jnp.clip(dist, 0, LANE - 1)), NEG)


def distance_tables(tab):
    return pl.pallas_call(
        _distance_tables_kernel,
        grid=(KV_HEADS, HPG),
        in_specs=[pl.BlockSpec((N_HEADS, LANE), lambda g, h: (0, 0))],
        out_specs=[pl.BlockSpec((None, NEAR, TQ), lambda g, h: (g, 0, h)),
                   pl.BlockSpec((None, W_KEYS, TQ), lambda g, h: (g, 0, h))],
        out_shape=[jax.ShapeDtypeStruct((KV_HEADS, NEAR, HT), F32),
                   jax.ShapeDtypeStruct((KV_HEADS, W_KEYS, HT), F32)],
        compiler_params=_cparams(("parallel", "parallel")),
        name="distance_tables",
    )(tab)


def attention_kv_layout(hproj):
    t = hproj.shape[0]
    grp = lambda c0, g: hproj[:, c0 + g * HEAD_DIM:c0 + (g + 1) * HEAD_DIM]
    e8 = jnp.asarray(np.eye(BLK_PER_TILE, dtype=np.float32)[(np.arange(t) % TK) // SEL_BLOCK])
    z = lambda w: jnp.zeros((t, w), F32)
    ka = [jnp.concatenate([grp(C_KVS, g), e8, z(LANE - HEAD_DIM - BLK_PER_TILE)], axis=1) for g in range(KV_HEADS)]
    ka += [jnp.concatenate([grp(C_KVW, g), z(LANE - HEAD_DIM)], axis=1) for g in range(KV_HEADS)]
    lead = jnp.zeros((WINDOW, LANE), F32).at[:, FLAG_LANE].set(1.0)
    ka = jnp.concatenate([jnp.tile(lead, (1, 2 * KV_HEADS)), jnp.concatenate(ka, axis=1)], axis=0).astype(BF16)
    ones = jnp.ones((t, 1), F32)
    vts = []
    for c0 in (C_KVS, C_KVW):
        for g in range(KV_HEADS):
            v = jnp.concatenate([grp(c0 + KV_HEADS * HEAD_DIM, g), ones, z(VT_ROWS - HEAD_DIM - 1)], axis=1)
            v = jnp.pad(v.astype(BF16), ((WINDOW, 0), (0, 0))).reshape(-1, LANE, VT_ROWS)
            vts.append(jnp.swapaxes(v, 1, 2))
    return ka, jnp.stack(vts, axis=1)


def nsa_sel_prompt(hproj, o_c, mask, ka, vt, tab, ntab, wtab):
    t = hproj.shape[0]
    n_sel = mask.shape[1]
    assert t % TK == 0 and n_sel % BLK_PER_TILE == 0 and n_sel >= MASK_ROWS
    const = lambda a: pl.BlockSpec(a.shape, lambda i: (0,) * a.ndim, pipeline_mode=pl.Buffered(1))
    return pl.pallas_call(
        functools.partial(_nsa_sel_kernel, n_sel=n_sel),
        grid=(t // TQ,),
        in_specs=[pl.BlockSpec((TQ, ATT_W), lambda i: (i, C_Q // ATT_W)),
                  pl.BlockSpec((TQ, LANE), lambda i: (i, C_NG // LANE)),
                  pl.BlockSpec((TQ, ATT_W), lambda i: (i, 0)),
                  pl.BlockSpec((KV_HEADS, n_sel, TQ), lambda i: (0, 0, i)),
                  const(ka), const(vt), const(tab), const(ntab), const(wtab)],
        out_specs=pl.BlockSpec((TQ, ATT_W), lambda i: (i, 0)),
        out_shape=jax.ShapeDtypeStruct((t, ATT_W), F32),
        scratch_shapes=[pltpu.VMEM((KV_HEADS, TK, HT), F32)] * 2,
        compiler_params=_cparams(("parallel",)),
        name="nsa_sel_prompt",
    )(hproj, hproj, o_c, mask, ka, vt, tab, ntab, wtab)


def _hgrn_gates(hq, hf, lb):
    sig = _sigmoid(hf)
    forget = lb + (1.0 - lb) * sig
    logf = jnp.log(jnp.maximum(forget, TINY))
    k = (1.0 - lb) * (1.0 - sig)
    q = hq * _sigmoid(hq)
    return q, k, logf


def _hgrn_intra(q, k, v, a, sub):
    n = q.shape[0]
    row = lax.broadcasted_iota(jnp.int32, (n, 1), 0) % sub
    outs = [jnp.zeros((n, HG_DV), F32) for _ in range(HG_HEADS)]
    for d in range(sub):
        ks = k if d == 0 else pltpu.roll(k, d, 0)
        a_s = a if d == 0 else pltpu.roll(a, d, 0)
        vs = v if d == 0 else pltpu.roll(v, d, 0)
        msk = row >= d
        w = q * ks * jnp.exp(jnp.where(msk, a - a_s, 0.0)) * msk.astype(F32)
        for h in range(HG_HEADS):
            r = jnp.sum(w[:, h * HG_DK:(h + 1) * HG_DK], axis=-1, keepdims=True)
            outs[h] = outs[h] + r * vs[:, h * HG_DV:(h + 1) * HG_DV]
    return outs


def _hgrn_prompt_kernel(hq_ref, hf_ref, hi_ref, lb_ref, tri_ref, o_ref, st_out_ref, st_ref):
    step = pl.program_id(0)

    @pl.when(step == 0)
    def _():
        st_ref[...] = jnp.zeros(st_ref.shape, F32)

    n = hq_ref.shape[0]
    q, k, logf = _hgrn_gates(hq_ref[...], hf_ref[...], lb_ref[...])
    v = hi_ref[...]
    a = jnp.dot(tri_ref[...], logf, preferred_element_type=F32, precision=lax.Precision.HIGHEST)
    outs = _hgrn_intra(q, k, v, a, SUB)
    rows_out = [[] for _ in range(HG_HEADS)]
    for c in range(n // SUB):
        sl = slice(c * SUB, (c + 1) * SUB)
        a_c = a[sl]
        a_last = a_c[SUB - 1:SUB]
        qe = (q[sl] * jnp.exp(a_c)).astype(BF16)
        kd = (k[sl] * jnp.exp(a_last - a_c)).astype(BF16)
        dec = jnp.exp(a_last)
        for h in range(HG_HEADS):
            hs = slice(h * HG_DK, (h + 1) * HG_DK)
            st = st_ref[h]
            rows_out[h].append(outs[h][sl] + _dot_nt(qe[:, hs], st.astype(BF16)))
            st_ref[h] = dec[:, hs] * st + _dot_tn(v[sl, hs].astype(BF16), kd[:, hs])
    o_ref[...] = jnp.concatenate([jnp.concatenate(r, axis=0) for r in rows_out], axis=1)
    st_out_ref[...] = st_ref[...]


def hgrn_prompt(hproj, lb, tc):
    t = hproj.shape[0]
    tc = min(tc, t)
    r = np.arange(tc)
    tri = jnp.asarray(((r[:, None] >= r[None, :]) & (r[:, None] // SUB == r[None, :] // SUB)).astype(np.float32))
    return pl.pallas_call(
        _hgrn_prompt_kernel,
        grid=(t // tc,),
        in_specs=[pl.BlockSpec((tc, HGK_W), lambda i: (i, C_HQ // HGK_W)),
                  pl.BlockSpec((tc, HGK_W), lambda i: (i, C_HF // HGK_W)),
                  pl.BlockSpec((tc, HGK_W), lambda i: (i, C_HI // HGK_W)),
                  pl.BlockSpec((1, HGK_W), lambda i: (0, 0)),
                  pl.BlockSpec((tc, tc), lambda i: (0, 0))],
        out_specs=[pl.BlockSpec((tc, HGK_W), lambda i: (i, 0)),
                   pl.BlockSpec((HG_HEADS, HG_DV, HG_DK), lambda i: (0, 0, 0))],
        out_shape=[jax.ShapeDtypeStruct((t, HGK_W), F32),
                   jax.ShapeDtypeStruct((HG_HEADS, HG_DV, HG_DK), F32)],
        scratch_shapes=[pltpu.VMEM((HG_HEADS, HG_DV, HG_DK), F32)],
        compiler_params=_cparams(("arbitrary",)),
        name="hgrn_prompt",
    )(hproj, hproj, hproj, lb.reshape(1, HGK_W), tri)


def _merge_kernel(x_ref, att_ref, o_ref, hg_ref, ga_ref, gh_ref, ng_ref, fg_ref, wa_ref, wh_ref, wo_ref,
                  xo_ref, xn_ref):
    o = o_ref[...]
    parts = []
    for h in range(HG_HEADS):
        oh = o[:, h * HG_DV:(h + 1) * HG_DV]
        parts.append(oh * lax.rsqrt(jnp.mean(oh * oh, axis=-1, keepdims=True) + EPS))
    hg = hg_ref[...]
    hgo = jnp.concatenate(parts, axis=1) * ng_ref[...] * (hg * _sigmoid(hg))
    m = (_sigmoid(ga_ref[...]) * _dot(att_ref[...].astype(BF16), wa_ref[...])
         + _sigmoid(gh_ref[...]) * _dot(hgo.astype(BF16), wh_ref[...]))
    x = x_ref[...] + _dot(m.astype(BF16), wo_ref[...])
    xo_ref[...] = x
    y = x * lax.rsqrt(jnp.mean(x * x, axis=-1, keepdims=True) + EPS)
    xn_ref[...] = (y * fg_ref[...]).astype(BF16)


def merge(x, att, o, hproj, hg_norm_g, ffn_g, wa, wh, wo, tm):
    t = x.shape[0]
    tm = min(tm, t)
    row = lambda w, cb: pl.BlockSpec((tm, w), lambda i: (i, cb))
    full = lambda a: pl.BlockSpec(a.shape, lambda i: (0, 0))
    hg_norm_g = hg_norm_g.reshape(1, HGK_W)
    ffn_g = ffn_g.reshape(1, D_MODEL)
    return pl.pallas_call(
        _merge_kernel,
        grid=(t // tm,),
        in_specs=[row(D_MODEL, 0), row(ATT_W, 0), row(HGK_W, 0), row(HGK_W, C_HG // HGK_W),
                  row(D_MODEL, C_GA // D_MODEL), row(D_MODEL, C_GH // D_MODEL),
                  full(hg_norm_g), full(ffn_g), full(wa), full(wh), full(wo)],
        out_specs=[row(D_MODEL, 0), row(D_MODEL, 0)],
        out_shape=[jax.ShapeDtypeStruct((t, D_MODEL), F32), jax.ShapeDtypeStruct((t, D_MODEL), BF16)],
        compiler_params=_cparams(("parallel",)),
        name="merge",
    )(x, att, o, hproj, hproj, hproj, hg_norm_g, ffn_g, wa, wh, wo)


def _topk_cols(x, k, payload=None):
    (vals, ids, pay), = _topk_cols_lockstep([x], k, None if payload is None else [payload])
    return vals, ids, pay


def _topk_cols_lockstep(xs, k, payloads=None):
    n, t = xs[0].shape
    row = lax.broadcasted_iota(jnp.int32, (n, t), 0)
    slot = lax.broadcasted_iota(jnp.int32, (k, t), 0)
    xs = list(xs)
    vals = [jnp.zeros((k, t), F32) for _ in xs]
    ids = [jnp.zeros((k, t), jnp.int32) for _ in xs]
    pay = [jnp.zeros((k, t), jnp.int32) for _ in xs]
    for it in range(k):
        here = slot == it
        for j, x in enumerate(xs):
            m = jnp.max(x, axis=0, keepdims=True)
            idx = jnp.min(jnp.where(x == m, row, n), axis=0, keepdims=True)
            hit = row == idx
            vals[j] = jnp.where(here, m, vals[j])
            ids[j] = jnp.where(here, idx, ids[j])
            if payloads is not None:
                pay[j] = jnp.where(here, jnp.max(jnp.where(hit, payloads[j], -1), axis=0, keepdims=True), pay[j])
            xs[j] = jnp.where(hit, NEG_INF, x)
    return [(vals[j], ids[j], pay[j] if payloads is not None else None) for j in range(len(xs))]


def _pair_candidates():
    pairs = [(a, b) for a in range(PEER_TOPK) for b in range(PEER_TOPK) if (a + 1) * (b + 1) <= PEER_TOPK]
    rows = -(-len(pairs) // 8) * 8
    sel = np.zeros((2, rows, PEER_TOPK), np.float32)
    for r, (a, b) in enumerate(pairs):
        sel[0, r, a] = 1.0
        sel[1, r, b] = 1.0
    return len(pairs), jnp.asarray(sel)


def _pick_rows(sel, x):
    return jnp.dot(sel, x, preferred_element_type=F32, precision=lax.Precision.HIGHEST)


def _peer_topk_kernel(q_ref, keys_ref, pair_ref, i1_ref, i2_ref, g_ref, *, n_pairs):
    o1, o2, og = [], [], []
    live = lax.broadcasted_iota(jnp.int32, (pair_ref.shape[1], q_ref.shape[0]), 0) < n_pairs
    cands, cidxs = [], []
    for h in range(PEER_HEADS):
        scores = [_dot_nt(keys_ref[side], q_ref[:, (h * 2 + side) * PEER_DH:(h * 2 + side + 1) * PEER_DH].astype(BF16))
                  for side in range(2)]
        (v1, i1, _), (v2, i2, _) = _topk_cols_lockstep(scores, PEER_TOPK)
        cands.append(jnp.where(live, _pick_rows(pair_ref[0], v1) + _pick_rows(pair_ref[1], v2), NEG_INF))
        cidxs.append((_pick_rows(pair_ref[0], i1.astype(F32)) * PEER_NKEYS
                      + _pick_rows(pair_ref[1], i2.astype(F32))).astype(jnp.int32))
    for h0 in range(0, PEER_HEADS, 2):
        for sv, _, e in _topk_cols_lockstep(cands[h0:h0 + 2], PEER_TOPK, cidxs[h0:h0 + 2]):
            ex = jnp.exp(sv - sv[0:1])
            o1.append(e // PEER_NKEYS)
            o2.append(e % PEER_NKEYS)
            og.append(ex / jnp.sum(ex, axis=0, keepdims=True))
    i1_ref[...] = jnp.concatenate(o1, axis=0).astype(F32).T.astype(jnp.int32)
    i2_ref[...] = jnp.concatenate(o2, axis=0).astype(F32).T.astype(jnp.int32)
    g_ref[...] = jnp.concatenate(og, axis=0).T


def peer_topk(q, keys_bf):
    t = q.shape[0]
    tb = LANE
    nsel = PEER_HEADS * PEER_TOPK
    n_pairs, pair_sel = _pair_candidates()
    out = lambda dt: jax.ShapeDtypeStruct((t, nsel), dt)
    return pl.pallas_call(
        functools.partial(_peer_topk_kernel, n_pairs=n_pairs),
        grid=(t // tb,),
        in_specs=[pl.BlockSpec((tb, q.shape[1]), lambda i: (i, 0)),
                  pl.BlockSpec(keys_bf.shape, lambda i: (0, 0, 0)),
                  pl.BlockSpec(pair_sel.shape, lambda i: (0, 0, 0))],
        out_specs=[pl.BlockSpec((tb, nsel), lambda i: (i, 0))] * 3,
        out_shape=[out(jnp.int32), out(jnp.int32), out(F32)],
        compiler_params=_cparams(("parallel",)),
        name="peer_topk",
    )(q, keys_bf, pair_sel)


GATE_UNROLL = 32


def _peer_dense_kernel(xn_ref, x_ref, i1_ref, i2_ref, g_ref, u_ref, v_ref, o_ref, acc_ref, gate_ref, *, n_a):
    j = pl.program_id(1)
    tm, nsel = i1_ref.shape

    @pl.when(j == 0)
    def _():
        acc_ref[...] = jnp.zeros(acc_ref.shape, F32)
        row = lax.broadcasted_iota(jnp.int32, (PEER_NKEYS, nsel), 0)

        def body(tt, _):
            for k in range(GATE_UNROLL):
                t = tt * GATE_UNROLL + k
                pa = jnp.where(row == i1_ref[pl.ds(t, 1), :], 1.0, 0.0).astype(BF16)
                gb = jnp.where(row == i2_ref[pl.ds(t, 1), :], g_ref[pl.ds(t, 1), :], 0.0).astype(BF16)
                gate_ref[pl.ds(pl.multiple_of(t * PEER_NKEYS, PEER_NKEYS), PEER_NKEYS), :] = _dot_nt(pa, gb)
            return 0

        lax.fori_loop(0, tm // GATE_UNROLL, body, 0)

    s = _dot_nt(xn_ref[...], u_ref[...])
    gate = jnp.concatenate([gate_ref[pl.ds(j * n_a + k, tm, stride=PEER_NKEYS), :] for k in range(n_a)], axis=1)
    acc_ref[...] += _dot((gate * _gelu(s)).astype(BF16), v_ref[...])

    @pl.when(j == pl.num_programs(1) - 1)
    def _():
        o_ref[...] = x_ref[...] + acc_ref[...]


def peer_dense(xn, x, i1, i2, g, u_bf, v_bf, tm, n_a):
    t = x.shape[0]
    tm = min(tm, t)
    te = n_a * PEER_NKEYS
    nsel = i1.shape[1]
    tok = lambda w: pl.BlockSpec((tm, w), lambda i, j: (i, 0))
    return pl.pallas_call(
        functools.partial(_peer_dense_kernel, n_a=n_a),
        grid=(t // tm, PEER_NKEYS // n_a),
        in_specs=[tok(D_MODEL), tok(D_MODEL), tok(nsel), tok(nsel), tok(nsel),
                  pl.BlockSpec((te, D_MODEL), lambda i, j: (j, 0)),
                  pl.BlockSpec((te, D_MODEL), lambda i, j: (j, 0))],
        out_specs=tok(D_MODEL),
        out_shape=jax.ShapeDtypeStruct((t, D_MODEL), F32),
        scratch_shapes=[pltpu.VMEM((tm, D_MODEL), F32), pltpu.VMEM((tm * PEER_NKEYS, PEER_NKEYS), F32)],
        compiler_params=_cparams(("parallel", "arbitrary")),
        name="peer_dense",
    )(xn, x, i1, i2, g, u_bf, v_bf)


def peer(xn_bf, x, wq_bf, keys_bf, u_bf, v_bf, tm):
    t = x.shape[0]
    q = matmul(xn_bf, wq_bf, min(512, t), 1024)
    i1, i2, g = peer_topk(q, keys_bf)
    return peer_dense(xn_bf, x, i1, i2, g, u_bf, v_bf, tm, 16)


ROWS = 8


def _pad_rows(x, rows=ROWS):
    return jnp.concatenate([x, jnp.zeros((rows - x.shape[0], x.shape[1]), x.dtype)], axis=0)


def _nsa_cmp_sample_kernel(q_ref, ckv_ref, tab_ref, cover_ref, oc_ref, sel_ref, *, nc, past, n_sel):
    nseq, nt = q_ref.shape[:2]
    t_pos = past + lax.broadcasted_iota(jnp.int32, (ROWS, 1), 0)
    for b in range(nseq):
        outs, imps = _cmp_branch(_pad_rows(q_ref[b]), t_pos, ckv_ref.at[b], tab_ref, cover_ref, nc)
        for g in range(KV_HEADS):
            sel_ref[b, g] = _topk_idx(imps[g], min(SEL_TOPN, n_sel))
        oc_ref[b] = jnp.concatenate(outs, axis=1)[:nt]


def nsa_cmp_sample(hproj3, ckv, tab, nc, past):
    nb, nt, _ = hproj3.shape
    ncp = ckv.shape[1]
    n_sel = -(-(past + nt) // SEL_BLOCK)
    n_sel_pad = -(-n_sel // LANE) * LANE
    cover = cover_matrix(ncp, n_sel_pad)
    per = 4 if nb % 4 == 0 else 1
    return pl.pallas_call(
        functools.partial(_nsa_cmp_sample_kernel, nc=nc, past=past, n_sel=n_sel),
        grid=(nb // per,),
        in_specs=[pl.BlockSpec((per, nt, ATT_W), lambda b: (b, 0, C_Q // ATT_W)),
                  pl.BlockSpec((per, ncp, KVB_W), lambda b: (b, 0, 0)),
                  pl.BlockSpec((N_HEADS, LANE), lambda b: (0, 0)),
                  pl.BlockSpec((ncp, n_sel_pad), lambda b: (0, 0))],
        out_specs=[pl.BlockSpec((per, nt, ATT_W), lambda b: (b, 0, 0)),
                   pl.BlockSpec((per, KV_HEADS, ROWS, LANE), lambda b: (b, 0, 0, 0))],
        out_shape=[jax.ShapeDtypeStruct((nb, nt, ATT_W), F32),
                   jax.ShapeDtypeStruct((nb, KV_HEADS, ROWS, LANE), jnp.int32)],
        compiler_params=_cparams(("parallel",)),
        name="nsa_cmp_sample",
    )(hproj3, ckv, tab, cover)


def _nsa_sel_sample_kernel(sel_ref, pt_ref, q_ref, ng_ref, oc_ref, kvs_ref, kvw_ref, win_ref, tab_ref, cache_ref,
                           o_ref, buf_ref, sem, *, layer, past, w_len):
    b = pl.program_id(0)
    nt = q_ref.shape[0]
    nslot = SEL_TOPN
    per_page = PAGE_SIZE // SEL_BLOCK

    def slot_copy(t, g, slot):
        blk = sel_ref[((b * nt + t) * KV_HEADS + g) * nslot + slot]
        in_past = blk * SEL_BLOCK < past
        blk_c = jnp.minimum(blk, past // SEL_BLOCK - 1)
        page = pt_ref[b, blk_c // per_page]
        src = cache_ref.at[layer, page, pl.ds(KVB_W, KVB_W), :]
        dst = buf_ref.at[t * KV_HEADS + g, :, pl.ds(slot * PAGE_SIZE, PAGE_SIZE)]
        return blk, in_past, pltpu.make_async_copy(src, dst, sem)

    for t in range(nt):
        for g in range(KV_HEADS):
            for slot in range(nslot):
                _, in_past, cp = slot_copy(t, g, slot)

                @pl.when(in_past)
                def _():
                    cp.start()

                @pl.when(jnp.logical_not(in_past))
                def _():
                    buf_ref[t * KV_HEADS + g, :, pl.ds(slot * PAGE_SIZE, PAGE_SIZE)] = jnp.zeros(
                        (KVB_W, PAGE_SIZE), F32)
    for t in range(nt):
        for g in range(KV_HEADS):
            for slot in range(nslot):
                _, in_past, cp = slot_copy(t, g, slot)

                @pl.when(in_past)
                def _():
                    cp.wait()

    gates = _sigmoid(ng_ref[...])
    nk = nslot * PAGE_SIZE
    lane_slot = lax.broadcasted_iota(jnp.int32, (1, nk), 1) // PAGE_SIZE
    lane_off = lax.broadcasted_iota(jnp.int32, (1, nk), 1) % PAGE_SIZE
    j_new = lax.broadcasted_iota(jnp.int32, (1, LANE), 1)
    kvn = jnp.concatenate([kvs_ref[...], kvw_ref[...]], axis=1)
    kvn_pad = _pad_rows(kvn, LANE).astype(BF16)
    win = win_ref[...].astype(BF16)
    w_idx = lax.broadcasted_iota(jnp.int32, (1, w_len + LANE), 1)
    rows_out = []
    for t in range(nt):
        t_pos = past + t
        heads_out = []
        for g in range(KV_HEADS):
            q4 = jnp.concatenate(
                [q_ref[t:t + 1, (g * HPG + h) * HEAD_DIM:(g * HPG + h + 1) * HEAD_DIM] for h in range(HPG)],
                axis=0)
            q4 = (q4 * ATT_SCALE).astype(BF16)

            def bias_of(dist):
                d = jnp.broadcast_to(jnp.clip(dist, 0, LANE - 1), (ROWS, dist.shape[1]))
                return jnp.concatenate(
                    [_bias_lookup(tab_ref[g * HPG + h:g * HPG + h + 1, :], d)[0:1] for h in range(HPG)], axis=0)

            pos = jnp.zeros((1, nk), jnp.int32)
            lane_blk = jnp.zeros((1, nk), jnp.int32)
            has_new = jnp.zeros((1, 1), jnp.int32)
            for slot in range(nslot):
                blk = sel_ref[((b * nt + t) * KV_HEADS + g) * nslot + slot]
                pos = jnp.where(lane_slot == slot, (blk // per_page) * PAGE_SIZE, pos)
                lane_blk = jnp.where(lane_slot == slot, blk, lane_blk)
                has_new = jnp.maximum(has_new, (blk * SEL_BLOCK >= past).astype(jnp.int32))
            pos = pos + lane_off
            in_block = pos // SEL_BLOCK == lane_blk
            tg = t * KV_HEADS + g
            k_t = buf_ref[tg, g * HEAD_DIM:(g + 1) * HEAD_DIM, :].astype(BF16)
            v_t = buf_ref[tg, (KV_HEADS + g) * HEAD_DIM:(KV_HEADS + g + 1) * HEAD_DIM, :].astype(BF16)
            k_new = kvn_pad[:, g * HEAD_DIM:(g + 1) * HEAD_DIM]
            v_new = kvn_pad[:, (KV_HEADS + g) * HEAD_DIM:(KV_HEADS + g + 1) * HEAD_DIM]
            dist = jnp.concatenate([t_pos - pos, t - j_new], axis=1)
            ok = jnp.concatenate([in_block & (pos < past), (j_new <= t) & (has_new > 0)], axis=1)
            s = jnp.concatenate([_dot(q4, k_t), _dot_nt(q4, k_new)], axis=1)
            s = jnp.where(ok, s + bias_of(dist), NEG)
            e = jnp.where(ok, jnp.exp(s - jnp.max(s, axis=-1, keepdims=True)), 0.0)
            p = (e / jnp.sum(e, axis=-1, keepdims=True)).astype(BF16)
            o_s = _dot_nt(p[:, :nk], v_t) + _dot(p[:, nk:], v_new)

            kw_new = kvn_pad[:, (2 * KV_HEADS + g) * HEAD_DIM:(2 * KV_HEADS + g + 1) * HEAD_DIM]
            vw_new = kvn_pad[:, (3 * KV_HEADS + g) * HEAD_DIM:(3 * KV_HEADS + g + 1) * HEAD_DIM]
            dist_w = t_pos - (past - w_len + w_idx)
            ok_w = (dist_w >= 0) & (dist_w <= WINDOW) & (w_idx < w_len + nt)
            s = jnp.concatenate([_dot(q4, win[g * HEAD_DIM:(g + 1) * HEAD_DIM]), _dot_nt(q4, kw_new)], axis=1)
            s = jnp.where(ok_w, s + bias_of(dist_w), NEG)
            e = jnp.where(ok_w, jnp.exp(s - jnp.max(s, axis=-1, keepdims=True)), 0.0)
            p = (e / jnp.sum(e, axis=-1, keepdims=True)).astype(BF16)
            o_w = (_dot_nt(p[:, :w_len], win[(KV_HEADS + g) * HEAD_DIM:(KV_HEADS + g + 1) * HEAD_DIM])
                   + _dot(p[:, w_len:], vw_new))

            for h in range(HPG):
                head = g * HPG + h
                o_c = oc_ref[t:t + 1, head * HEAD_DIM:(head + 1) * HEAD_DIM]
                heads_out.append(gates[t:t + 1, head:head + 1] * o_c
                                 + gates[t:t + 1, N_HEADS + head:N_HEADS + head + 1] * o_s[h:h + 1]
                                 + gates[t:t + 1, 2 * N_HEADS + head:2 * N_HEADS + head + 1] * o_w[h:h + 1])
        rows_out.append(jnp.concatenate(heads_out, axis=1))
    o_ref[...] = jnp.concatenate(rows_out, axis=0)


def nsa_sel_sample(hproj3, o_c, sel_flat, page_table, cache_t, win_t, tab, layer, past):
    nb, nt, _ = hproj3.shape
    w_len = win_t.shape[3]
    assert past % SEL_BLOCK == 0 and PAGE_SIZE % SEL_BLOCK == 0 and w_len % LANE == 0 and past >= w_len
    grid_spec = pltpu.PrefetchScalarGridSpec(
        num_scalar_prefetch=2,
        grid=(nb,),
        in_specs=[pl.BlockSpec((None, nt, ATT_W), lambda b, s, p: (b, 0, C_Q // ATT_W)),
                  pl.BlockSpec((None, nt, LANE), lambda b, s, p: (b, 0, C_NG // LANE)),
                  pl.BlockSpec((None, nt, ATT_W), lambda b, s, p: (b, 0, 0)),
                  pl.BlockSpec((None, nt, KVB_W), lambda b, s, p: (b, 0, C_KVS // KVB_W)),
                  pl.BlockSpec((None, nt, KVB_W), lambda b, s, p: (b, 0, C_KVW // KVB_W)),
                  pl.BlockSpec((None, None, KVB_W, w_len), lambda b, s, p: (layer, b, 0, 0)),
                  pl.BlockSpec((N_HEADS, LANE), lambda b, s, p: (0, 0)),
                  pl.BlockSpec(memory_space=pl.ANY)],
        out_specs=pl.BlockSpec((None, nt, ATT_W), lambda b, s, p: (b, 0, 0)),
        scratch_shapes=[pltpu.VMEM((nt * KV_HEADS, KVB_W, SEL_TOPN * PAGE_SIZE), F32),
                        pltpu.SemaphoreType.DMA(())],
    )
    return pl.pallas_call(
        functools.partial(_nsa_sel_sample_kernel, layer=layer, past=past, w_len=w_len),
        grid_spec=grid_spec,
        out_shape=jax.ShapeDtypeStruct((nb, nt, ATT_W), F32),
        compiler_params=_cparams(("arbitrary",)),
        name="nsa_sel_sample",
    )(sel_flat, page_table, hproj3, hproj3, o_c, hproj3, hproj3, win_t, tab, cache_t)


def _hgrn_sample_kernel(hq_ref, hf_ref, hi_ref, lb_ref, s_ref, o_ref, so_ref):
    nt = hq_ref.shape[0]
    live = (lax.broadcasted_iota(jnp.int32, (ROWS, 1), 0) < nt).astype(F32)
    q, k, logf = _hgrn_gates(_pad_rows(hq_ref[...]), _pad_rows(hf_ref[...]), lb_ref[...])
    q, k, logf = q * live, k * live, logf * live
    v = _pad_rows(hi_ref[...])
    rows = [logf[0:1]]
    for i in range(1, ROWS):
        rows.append(rows[-1] + logf[i:i + 1])
    a = jnp.concatenate(rows, axis=0)
    outs = _hgrn_intra(q, k, v, a, ROWS)
    a_last = a[ROWS - 1:ROWS]
    qe = (q * jnp.exp(a)).astype(BF16)
    kd = (k * jnp.exp(a_last - a)).astype(BF16)
    dec = jnp.exp(a_last)
    o_parts = []
    for h in range(HG_HEADS):
        hs = slice(h * HG_DK, (h + 1) * HG_DK)
        st = s_ref[h].T
        o_parts.append(outs[h] + _dot_nt(qe[:, hs], st.astype(BF16)))
        so_ref[h] = (dec[:, hs] * st + _dot_tn(v[:, hs].astype(BF16), kd[:, hs])).T
    o_ref[...] = jnp.concatenate(o_parts, axis=1)[:nt]


def hgrn_sample(hproj3, lb, state):
    nb, nt, _ = hproj3.shape
    col = lambda c: pl.BlockSpec((None, nt, HGK_W), lambda b: (b, 0, c // HGK_W))
    st_spec = pl.BlockSpec((None, HG_HEADS, HG_DK, HG_DV), lambda b: (b, 0, 0, 0))
    return pl.pallas_call(
        _hgrn_sample_kernel,
        grid=(nb,),
        in_specs=[col(C_HQ), col(C_HF), col(C_HI), pl.BlockSpec((1, HGK_W), lambda b: (0, 0)), st_spec],
        out_specs=[pl.BlockSpec((None, nt, HGK_W), lambda b: (b, 0, 0)), st_spec],
        out_shape=[jax.ShapeDtypeStruct((nb, nt, HGK_W), F32),
                   jax.ShapeDtypeStruct(state.shape, F32)],
        compiler_params=_cparams(("parallel",)),
        name="hgrn_sample",
    )(hproj3, hproj3, hproj3, lb.reshape(1, HGK_W), state)


def reorder_w_in(w):
    o = np.cumsum([0, ATT_W, KVB_W, KVB_W, KVB_W, N_GATE, HGK_W, HGK_W, HGK_W, HGK_W, D_MODEL, D_MODEL])
    q, kvc, kvs, kvw, ng, hq, hf, hi, hg, ga, gh = [w[:, o[i]:o[i + 1]] for i in range(11)]
    pad = jnp.zeros((w.shape[0], N_PROJ - C_NG - N_GATE), w.dtype)
    return jnp.concatenate([q, hq, ga, gh, hf, hi, hg, kvc, kvs, kvw, ng, pad], axis=1).astype(BF16)


def prompt_layer(x, lp, tabs):
    t = x.shape[0]
    tab, ntab, wtab = tabs
    hproj = norm_matmul(x, lp["attn_g"], lp["w_in"], min(512, t), 1408)
    y = cproj_rows(hproj, C_KVC // KVB_W, lp["wbig"], 2048)
    ckv = cmp_mlp(y[None], lp["pe_term"], lp["w2big"])[0]
    o_c, mask = nsa_cmp_prompt(hproj, ckv, tab, t // CMP_STRIDE - 1)
    ka, vt = attention_kv_layout(hproj)
    att = nsa_sel_prompt(hproj, o_c, mask, ka, vt, tab, ntab, wtab)
    o_hg, st = hgrn_prompt(hproj, lp["lb"], 128)
    x, xn = merge(x, att, o_hg, hproj, lp["hg_norm_g"], lp["ffn_g"], lp["wa"], lp["wh"], lp["wo"], 256)
    x = peer(xn, x, lp["peer_wq"], lp["peer_keys"], lp["peer_u"], lp["peer_v"], 256)
    return x, hproj[:, C_KVC:C_KVC + 2 * KVB_W], hproj[:, C_KVW:C_KVW + KVB_W], jnp.swapaxes(st, 1, 2)


def layer_params(l, norm_attn_g, norm_ffn_g, w_in, cmp_pe, cmp_w1, cmp_w2, lower_bounds, hg_norm_g,
                 w_br_attn, w_br_hg, w_out, peer_wq, peer_keys, peer_u, peer_v):
    wbig, w2big, pe_rows = compress_weights(cmp_pe[l], cmp_w1[l], cmp_w2[l])
    return dict(attn_g=norm_attn_g[l], ffn_g=norm_ffn_g[l], w_in=reorder_w_in(w_in[l]),
                wbig=wbig, w2big=w2big, pe_term=pe_term_from(pe_rows, wbig),
                lb=lower_bounds[l], hg_norm_g=hg_norm_g[l],
                wa=w_br_attn[l].astype(BF16), wh=w_br_hg[l].astype(BF16), wo=w_out[l].astype(BF16),
                peer_wq=peer_wq[l].astype(BF16), peer_keys=peer_keys[l].astype(BF16),
                peer_u=peer_u[l].astype(BF16), peer_v=peer_v[l].astype(BF16))


def sample_layer(x, lp, tab, layer, cache_t, win_t, state, page_table, nb, nt):
    past = page_table.shape[1] * PAGE_SIZE
    assert nt < CMP_STRIDE, "new tokens never complete a compression chunk"
    hproj = norm_matmul(x, lp["attn_g"], lp["w_in"], x.shape[0], 1408)
    hproj3 = hproj.reshape(nb, nt, N_PROJ)
    y = cproj_pages(cache_t, layer, page_table, lp["wbig"], min(32, page_table.shape[1]))
    ckv = cmp_mlp(y, lp["pe_term"], lp["w2big"])
    o_c, sel = nsa_cmp_sample(hproj3, ckv, tab, past // CMP_STRIDE - 1, past)
    sel_flat = jnp.transpose(sel[:, :, :nt, :SEL_TOPN], (0, 2, 1, 3)).reshape(-1)
    att = nsa_sel_sample(hproj3, o_c, sel_flat, page_table, cache_t, win_t, tab, layer, past)
    o_hg, st = hgrn_sample(hproj3, lp["lb"], state)
    x, xn = merge(x, att.reshape(nb * nt, ATT_W), o_hg.reshape(nb * nt, HGK_W), hproj, lp["hg_norm_g"],
                  lp["ffn_g"], lp["wa"], lp["wh"], lp["wo"], 256)
    x = peer(xn, x, lp["peer_wq"], lp["peer_keys"], lp["peer_u"], lp["peer_v"], 256)
    return x, hproj[:, C_KVC:C_KVC + 2 * KVB_W], hproj[:, C_KVW:C_KVW + KVB_W], st


def kernel(x_prompt, x_sample, cache_kv, cache_win, state_hgrn, page_table, norm_attn_g, norm_ffn_g, final_norm_g, w_in, cmp_pe, cmp_w1, cmp_w2, rel_bias, hg_lb_logits, hg_norm_g, w_br_attn, w_br_hg, w_out, peer_wq, peer_keys, peer_u, peer_v):
    depth = w_in.shape[0]
    bp, seq, _ = x_prompt.shape
    nb, nt, _ = x_sample.shape
    assert bp == 1
    p_lb = jax.nn.softmax(hg_lb_logits.astype(F32), axis=0)
    lower_bounds = jnp.cumsum(p_lb, axis=0) - p_lb[0]
    tab = bias_table(rel_bias)
    tabs = (tab, *distance_tables(tab))
    n_pool = cache_kv.shape[1]
    w_buf = cache_win.shape[2]
    cache_t = jnp.transpose(cache_kv, (0, 1, 3, 4, 5, 2)).reshape(depth, n_pool, 2 * KVB_W, PAGE_SIZE)
    win_t = jnp.transpose(cache_win, (0, 1, 3, 4, 5, 2)).reshape(depth, nb, KVB_W, w_buf)
    xp = x_prompt.reshape(seq, D_MODEL)
    xs = x_sample.reshape(nb * nt, D_MODEL)
    kv_p, win_p, hs_p, kv_s, win_s, hs_s = [], [], [], [], [], []
    for l in range(depth):
        lp = layer_params(l, norm_attn_g, norm_ffn_g, w_in, cmp_pe, cmp_w1, cmp_w2, lower_bounds, hg_norm_g,
                          w_br_attn, w_br_hg, w_out, peer_wq, peer_keys, peer_u, peer_v)
        xp, kv, kw, st = prompt_layer(xp, lp, tabs)
        kv_p.append(kv.reshape(bp, seq, 4, KV_HEADS, HEAD_DIM))
        win_p.append(kw[seq - min(WINDOW, seq):].reshape(bp, -1, 2, KV_HEADS, HEAD_DIM))
        hs_p.append(st[None])
        xs, kv, kw, st = sample_layer(xs, lp, tab, l, cache_t, win_t, state_hgrn[l], page_table, nb, nt)
        kv_s.append(kv.reshape(nb, nt, 4, KV_HEADS, HEAD_DIM))
        win_s.append(jnp.concatenate([cache_win[l], kw.reshape(nb, nt, 2, KV_HEADS, HEAD_DIM)], axis=1)[:, nt:])
        hs_s.append(st)
    y_prompt = rmsnorm(xp, final_norm_g, min(512, seq)).reshape(bp, seq, D_MODEL)
    y_sample = rmsnorm(xs, final_norm_g, nb * nt).reshape(nb, nt, D_MODEL)
    return (y_prompt, y_sample, jnp.stack(kv_p), jnp.stack(win_p), jnp.stack(hs_p),
            jnp.stack(kv_s), jnp.stack(win_s), jnp.stack(hs_s))
```

```python
import functools
import math

import jax
import jax.numpy as jnp
import numpy as np
from jax import lax
from jax.experimental import pallas as pl
from jax.experimental.pallas import tpu as pltpu

F32 = jnp.float32
BF16 = jnp.bfloat16

D_MODEL = 1024
PAGE_SIZE = 128
N_HEADS = 8
HEAD_DIM = 64
KV_HEADS = 2
HPG = N_HEADS // KV_HEADS
CMP_STRIDE = 16
CMP_BLOCK = 32
SEL_BLOCK = 64
SEL_TOPN = 16
WINDOW = 512
ATT_SCALE = HEAD_DIM ** -0.5
N_BUCKETS = 32
MAX_DISTANCE = 128
HG_HEADS = 4
HG_DK = 128
HG_DV = 128
PEER_HEADS = 8
PEER_NKEYS = 128
PEER_DH = 128
PEER_TOPK = 16
ATT_W = N_HEADS * HEAD_DIM
KVB_W = 2 * KV_HEADS * HEAD_DIM
HGK_W = HG_HEADS * HG_DK
N_GATE = 3 * N_HEADS
EPS = 1e-6
NEG = -1e30
BIG = 1e6
TINY = 1e-30
NEG_INF = float("-inf")

LANE = 128
VMEM_LIMIT = 56 * 1024 * 1024

C_Q, C_HQ, C_GA, C_GH, C_HF, C_HI, C_HG = 0, 512, 1024, 2048, 3072, 3584, 4096
C_KVC, C_KVS, C_KVW, C_NG, N_PROJ = 4608, 4864, 5120, 5376, 5632
SUB = 16
TQ = 128
TK = 512


def _cparams(sem, vmem=VMEM_LIMIT):
    return pltpu.CompilerParams(dimension_semantics=sem, vmem_limit_bytes=vmem)


def _dot(a, b):
    return jnp.dot(a, b, preferred_element_type=F32)


def _dot_nt(a, b):
    return lax.dot_general(a, b, (((1,), (1,)), ((), ())), preferred_element_type=F32)


def _dot_tn(a, b):
    return lax.dot_general(a, b, (((0,), (0,)), ((), ())), preferred_element_type=F32)


def _sigmoid(x):
    return 1.0 / (1.0 + jnp.exp(-x))


def _gelu(x):
    return 0.5 * x * (1.0 + lax.erf(x * (2.0 ** -0.5)))


def _norm_matmul_kernel(x_ref, g_ref, w_ref, o_ref, xn_ref):
    @pl.when(pl.program_id(1) == 0)
    def _():
        x = x_ref[...]
        y = x * lax.rsqrt(jnp.mean(x * x, axis=-1, keepdims=True) + EPS)
        xn_ref[...] = (y * g_ref[...]).astype(BF16)

    o_ref[...] = _dot(xn_ref[...], w_ref[...])


def norm_matmul(x, g, w, tm, tn):
    m, k = x.shape
    n = w.shape[1]
    return pl.pallas_call(
        _norm_matmul_kernel,
        grid=(m // tm, n // tn),
        in_specs=[pl.BlockSpec((tm, k), lambda i, j: (i, 0)),
                  pl.BlockSpec((1, k), lambda i, j: (0, 0)),
                  pl.BlockSpec((k, tn), lambda i, j: (0, j))],
        out_specs=pl.BlockSpec((tm, tn), lambda i, j: (i, j)),
        out_shape=jax.ShapeDtypeStruct((m, n), F32),
        scratch_shapes=[pltpu.VMEM((tm, k), BF16)],
        compiler_params=_cparams(("parallel", "arbitrary")),
        name="norm_matmul",
    )(x, g.reshape(1, k), w)


def _matmul_kernel(x_ref, w_ref, o_ref):
    o_ref[...] = _dot(x_ref[...], w_ref[...])


def matmul(x, w, tm, tn):
    m, k = x.shape
    n = w.shape[1]
    return pl.pallas_call(
        _matmul_kernel,
        grid=(m // tm, n // tn),
        in_specs=[pl.BlockSpec((tm, k), lambda i, j: (i, 0)),
                  pl.BlockSpec((k, tn), lambda i, j: (0, j))],
        out_specs=pl.BlockSpec((tm, tn), lambda i, j: (i, j)),
        out_shape=jax.ShapeDtypeStruct((m, n), F32),
        compiler_params=_cparams(("parallel", "arbitrary")),
        name="matmul",
    )(x, w)


def _rmsnorm_kernel(x_ref, g_ref, o_ref):
    x = x_ref[...]
    o_ref[...] = x * lax.rsqrt(jnp.mean(x * x, axis=-1, keepdims=True) + EPS) * g_ref[...]


def rmsnorm(x, g, tm):
    m, k = x.shape
    return pl.pallas_call(
        _rmsnorm_kernel,
        grid=(m // tm,),
        in_specs=[pl.BlockSpec((tm, k), lambda i: (i, 0)), pl.BlockSpec((1, k), lambda i: (0, 0))],
        out_specs=pl.BlockSpec((tm, k), lambda i: (i, 0)),
        out_shape=jax.ShapeDtypeStruct((m, k), F32),
        compiler_params=_cparams(("parallel",)),
        name="rmsnorm",
    )(x, g.reshape(1, k))


CPROJ_GROUPS = 2


def _cproj_from_pieces(piece_fn, w_ref, y_ref):
    acc = jnp.zeros(y_ref.shape, F32)
    for s in range(CMP_STRIDE):
        acc = acc + _dot(piece_fn(s).astype(BF16), w_ref[s])
    y_ref[...] = acc


def _cproj_rows_kernel(r0_ref, r1_ref, w_ref, y_ref):
    nch = y_ref.shape[0]

    def piece(s):
        return jnp.concatenate([r[pl.ds(s, nch, stride=CMP_STRIDE), :] for r in (r0_ref, r1_ref)], axis=1)

    _cproj_from_pieces(piece, w_ref, y_ref)


def cproj_rows(rows_arr, col_block, wbig, rb):
    t = rows_arr.shape[0]
    rb = min(rb, t)
    half = lambda k: pl.BlockSpec((rb, LANE), lambda i: (i, 2 * col_block + k))
    return pl.pallas_call(
        _cproj_rows_kernel,
        grid=(t // rb,),
        in_specs=[half(0), half(1), pl.BlockSpec(wbig.shape, lambda i: (0, 0, 0))],
        out_specs=pl.BlockSpec((rb // CMP_STRIDE, 2 * KVB_W), lambda i: (i, 0)),
        out_shape=jax.ShapeDtypeStruct((t // CMP_STRIDE, 2 * KVB_W), F32),
        compiler_params=_cparams(("parallel",)),
        name="cproj_rows",
    )(rows_arr, rows_arr, wbig)


def _cproj_pages_kernel(pt_ref, *refs):
    del pt_ref
    n_scr = 2 * CPROJ_GROUPS
    pages, w_ref, y_ref, scr = refs[:-2 - n_scr], refs[-2 - n_scr], refs[-1 - n_scr], refs[-n_scr:]
    per_grp = len(pages) // CPROJ_GROUPS
    nch = y_ref.shape[0] // CPROJ_GROUPS
    for gi in range(CPROJ_GROUPS):
        lo_ref, hi_ref = scr[2 * gi], scr[2 * gi + 1]
        for p in range(per_grp):
            page = pages[gi * per_grp + p]
            rows = pl.ds(p * PAGE_SIZE, PAGE_SIZE)
            lo_ref[rows, :] = page[0:LANE, :].T
            hi_ref[rows, :] = page[LANE:2 * LANE, :].T

        def piece(s, lo_ref=lo_ref, hi_ref=hi_ref):
            return jnp.concatenate([r[pl.ds(s, nch, stride=CMP_STRIDE), :] for r in (lo_ref, hi_ref)], axis=1)

        _cproj_from_pieces(piece, w_ref, y_ref.at[pl.ds(gi * nch, nch)])


def cproj_pages(cache_t, layer, page_table, wbig, pg):
    nb, npages = page_table.shape
    per = PAGE_SIZE // CMP_STRIDE

    def page_spec(p):
        return pl.BlockSpec((None, None, KVB_W, PAGE_SIZE), lambda b, j, pt: (layer, pt[b, j * pg + p], 0, 0))

    grid_spec = pltpu.PrefetchScalarGridSpec(
        num_scalar_prefetch=1,
        grid=(nb, npages // pg),
        in_specs=[page_spec(p) for p in range(pg)] + [pl.BlockSpec(wbig.shape, lambda b, j, pt: (0, 0, 0))],
        out_specs=pl.BlockSpec((None, pg * per, 2 * KVB_W), lambda b, j, pt: (b, j, 0)),
        scratch_shapes=[pltpu.VMEM((pg // CPROJ_GROUPS * PAGE_SIZE, LANE), F32)] * (2 * CPROJ_GROUPS),
    )
    return pl.pallas_call(
        _cproj_pages_kernel,
        grid_spec=grid_spec,
        out_shape=jax.ShapeDtypeStruct((nb, npages * per, 2 * KVB_W), F32),
        compiler_params=_cparams(("parallel", "arbitrary")),
        name="cproj_pages",
    )(page_table, *([cache_t] * pg), wbig)


def _cmp_mlp_kernel(y_ref, pe_ref, w2_ref, o_ref):
    y = y_ref[...]
    n = y.shape[0]
    nxt = pltpu.roll(y[:, KVB_W:], n - 1, 0)
    hsum = y[:, :KVB_W] + nxt + pe_ref[...]
    o_ref[...] = _dot(_gelu(hsum).astype(BF16), w2_ref[...]).astype(o_ref.dtype)


def cmp_mlp(y, pe_term, w2big):
    b, ncp, _ = y.shape
    return pl.pallas_call(
        _cmp_mlp_kernel,
        grid=(b,),
        in_specs=[pl.BlockSpec((None, ncp, 2 * KVB_W), lambda i: (i, 0, 0)),
                  pl.BlockSpec((1, KVB_W), lambda i: (0, 0)),
                  pl.BlockSpec((KVB_W, KVB_W), lambda i: (0, 0))],
        out_specs=pl.BlockSpec((None, ncp, KVB_W), lambda i: (i, 0, 0)),
        out_shape=jax.ShapeDtypeStruct((b, ncp, KVB_W), BF16),
        compiler_params=_cparams(("parallel",)),
        name="cmp_mlp",
    )(y, pe_term, w2big)


def compress_weights(pe, w1, w2):
    w1r = w1.reshape(2, 2, CMP_STRIDE, HEAD_DIM, HEAD_DIM)
    eye = jnp.eye(2, dtype=F32)
    wbig = jnp.einsum('crsdh,cC,gG->scgdrCGh', w1r, eye, eye).reshape(CMP_STRIDE, KVB_W, 2 * KVB_W)
    w2big = jnp.einsum('chd,cC,gG->cghCGd', w2, eye, eye).reshape(KVB_W, KVB_W)
    per = pe.reshape(2, 2, CMP_STRIDE, HEAD_DIM)
    pe_rows = jnp.broadcast_to(jnp.transpose(per, (1, 2, 0, 3))[:, :, :, None, :],
                               (2, CMP_STRIDE, 2, KV_HEADS, HEAD_DIM)).reshape(2 * CMP_STRIDE, KVB_W)
    pe_rows = jnp.pad(pe_rows, ((0, 8 * CMP_STRIDE - 2 * CMP_STRIDE), (0, 0)))
    return wbig.astype(BF16), w2big.astype(BF16), pe_rows


def pe_term_from(pe_rows, wbig):
    ype = cproj_rows(pe_rows, 0, wbig, pe_rows.shape[0])
    return ype[0:1, :KVB_W] + ype[1:2, KVB_W:]


def bias_table(rel_bias):
    n = np.arange(LANE)
    exact = N_BUCKETS // 2
    nf = np.maximum(n, 1).astype(np.float32)
    large = exact + (np.log(nf / np.float32(exact)) / np.float32(math.log(MAX_DISTANCE / exact))
                     * np.float32(N_BUCKETS - exact)).astype(np.int32)
    bucket = np.where(n < exact, n, np.minimum(large, N_BUCKETS - 1))
    assert bucket[LANE - 1] == N_BUCKETS - 1
    return jnp.transpose(rel_bias.astype(F32)[bucket], (1, 0))


def _bias_lookup(tab_row, dist_clipped):
    r, w = dist_clipped.shape
    src = jnp.broadcast_to(tab_row, (r, LANE))
    parts = [jnp.take_along_axis(src, dist_clipped[:, j * LANE:(j + 1) * LANE], axis=1)
             for j in range(w // LANE)]
    return parts[0] if len(parts) == 1 else jnp.concatenate(parts, axis=1)


def _topk_idx(x, k):
    r, n = x.shape
    col = lax.broadcasted_iota(jnp.int32, (r, n), 1)
    lane = lax.broadcasted_iota(jnp.int32, (r, LANE), 1)
    out = jnp.zeros((r, LANE), jnp.int32)
    for it in range(k):
        m = jnp.max(x, axis=-1, keepdims=True)
        idx = jnp.min(jnp.where(x == m, col, n), axis=-1, keepdims=True)
        out = jnp.where(lane == it, idx, out)
        x = jnp.where(col == idx, NEG_INF, x)
    return out


def _cmp_branch(q, t_pos, ckv_ref, tab_ref, cover_ref, nc, ncols=None, n_sel=None):
    r = q.shape[0]
    ncols = ckv_ref.shape[0] if ncols is None else ncols
    n_sel = cover_ref.shape[1] if n_sel is None else n_sel
    c_idx = lax.broadcasted_iota(jnp.int32, (1, ncols), 1)
    dist = t_pos - (c_idx * CMP_STRIDE + CMP_BLOCK - 1)
    valid = (dist >= 0) & (c_idx < nc)
    any_valid = (t_pos >= CMP_BLOCK - 1).astype(F32)
    dcl = jnp.clip(dist, 0, LANE - 1)
    blk = lax.broadcasted_iota(jnp.int32, (1, n_sel), 1)
    cur = t_pos // SEL_BLOCK
    forced = (blk == 0) | (blk == cur) | (blk == cur - 1)
    outs, imps = [], []
    for g in range(KV_HEADS):
        kc = ckv_ref[:ncols, g * HEAD_DIM:(g + 1) * HEAD_DIM]
        vc = ckv_ref[:ncols, (KV_HEADS + g) * HEAD_DIM:(KV_HEADS + g + 1) * HEAD_DIM]
        psum = jnp.zeros((r, ncols), F32)
        for h in range(HPG):
            head = g * HPG + h
            qh = (q[:, head * HEAD_DIM:(head + 1) * HEAD_DIM] * ATT_SCALE).astype(BF16)
            s = _dot_nt(qh, kc) + _bias_lookup(tab_ref[head:head + 1, :], dcl)
            s = jnp.where(valid, s, NEG)
            e = jnp.exp(s - jnp.max(s, axis=-1, keepdims=True))
            p = e * (any_valid / jnp.sum(e, axis=-1, keepdims=True))
            outs.append(_dot(p.astype(BF16), vc))
            psum = psum + p
        p_hi = psum.astype(BF16)
        p_lo = (psum - p_hi.astype(F32)).astype(BF16)
        imp = _dot(p_hi, cover_ref[:ncols, :n_sel]) + _dot(p_lo, cover_ref[:ncols, :n_sel])
        imps.append(jnp.where(forced, BIG, jnp.where(blk <= cur, imp, -BIG)))
    return outs, imps


def _topk_mask_cols(x, k):
    n, t = x.shape
    row = lax.broadcasted_iota(jnp.int32, (n, t), 0)
    forced = x == BIG
    sel = forced.astype(F32)
    x = jnp.where(forced, NEG_INF, x)
    for _ in range(k - 3):
        m = jnp.max(x, axis=0, keepdims=True)
        idx = jnp.min(jnp.where(x == m, row, n), axis=0, keepdims=True)
        hit = row == idx
        sel = jnp.where(hit, 1.0, sel)
        x = jnp.where(hit, NEG_INF, x)
    return sel


def _nsa_cmp_kernel(q_ref, ckv_ref, tab_ref, cover_ref, oc_ref, mask_ref, *, nc, n_sel):
    i = pl.program_id(0)
    t_pos = i * TQ + lax.broadcasted_iota(jnp.int32, (TQ, 1), 0)
    ncp = ckv_ref.shape[0]
    part = max(ncp // 8, LANE)
    last_c = (i * TQ + TQ - CMP_BLOCK) // CMP_STRIDE
    need = jnp.clip(last_c // part, 0, ncp // part - 1)

    for n_parts in range(1, ncp // part + 1):
        @pl.when(need == n_parts - 1)
        def _(n_parts=n_parts):
            ncols = n_parts * part
            rows = min(ncols * CMP_STRIDE // SEL_BLOCK, n_sel)
            lanes = min(-(-rows // LANE) * LANE, n_sel)
            outs, imps = _cmp_branch(q_ref[...], t_pos, ckv_ref, tab_ref, cover_ref, nc, ncols, lanes)
            for g in range(KV_HEADS):
                mask_ref[g, :rows] = _topk_mask_cols(imps[g].T[:rows], min(SEL_TOPN, n_sel))
                if rows < n_sel:
                    mask_ref[g, rows:] = jnp.zeros((n_sel - rows, TQ), F32)
            oc_ref[...] = jnp.concatenate(outs, axis=1)


def cover_matrix(ncp, n_sel):
    ci = np.arange(ncp)[:, None] * CMP_STRIDE
    sj = np.arange(n_sel)[None, :] * SEL_BLOCK
    return jnp.asarray(((ci < sj + SEL_BLOCK) & (ci + CMP_BLOCK > sj)).astype(np.float32), dtype=BF16)


def nsa_cmp_prompt(hproj, ckv, tab, nc):
    t = hproj.shape[0]
    ncp = ckv.shape[0]
    n_sel = t // SEL_BLOCK
    cover = cover_matrix(ncp, n_sel)
    return pl.pallas_call(
        functools.partial(_nsa_cmp_kernel, nc=nc, n_sel=n_sel),
        grid=(t // TQ,),
        in_specs=[pl.BlockSpec((TQ, ATT_W), lambda i: (i, C_Q // ATT_W)),
                  pl.BlockSpec((ncp, KVB_W), lambda i: (0, 0)),
                  pl.BlockSpec((N_HEADS, LANE), lambda i: (0, 0)),
                  pl.BlockSpec((ncp, n_sel), lambda i: (0, 0))],
        out_specs=[pl.BlockSpec((TQ, ATT_W), lambda i: (i, 0)),
                   pl.BlockSpec((KV_HEADS, n_sel, TQ), lambda i: (0, 0, i))],
        out_shape=[jax.ShapeDtypeStruct((t, ATT_W), F32),
                   jax.ShapeDtypeStruct((KV_HEADS, n_sel, t), F32)],
        compiler_params=_cparams(("parallel",)),
        name="nsa_cmp_prompt",
    )(hproj, ckv, tab, cover)


MASKED = -1e9
NEAR = 2 * TQ
W_KEYS = WINDOW + TQ
BLK_PER_TILE = TK // SEL_BLOCK


VT_ROWS = HEAD_DIM + 16
MASK_ROWS = 16
HT = HPG * TQ
FLAG_LANE = HEAD_DIM + BLK_PER_TILE
FAR_UNROLL = 4


def _online_step_t(carry, st, shift, vt):
    m, acc = carry
    m_new = jnp.maximum(m, jnp.max(st, axis=0, keepdims=True) + shift)
    p = jnp.exp(st - (m_new - shift)).astype(BF16)
    return m_new, jnp.exp(m - m_new) * acc + _dot(vt, p)


def _nsa_sel_kernel(q_ref, ng_ref, oc_ref, mask_ref, ka_ref, vt_ref, tab_ref, ntab_ref, wtab_ref, o_ref,
                    sta_ref, stb_ref, *, n_sel):
    i = pl.program_id(0)
    qt = (q_ref[...] * ATT_SCALE).T
    oct_ = oc_ref[...].T
    gt = _sigmoid(ng_ref[...]).T
    near_blk0 = 2 * i - 2
    n_far = (jnp.maximum(i - 1, 0) * TQ + TK - 1) // TK
    blk8 = lax.broadcasted_iota(jnp.int32, (BLK_PER_TILE, 1), 0)
    init = (jnp.full((1, HT), NEG, F32), jnp.zeros((VT_ROWS, HT), F32))
    zero_shift = jnp.zeros((1, HT), F32)
    chunks_per_tile = TK // LANE

    before_start = jnp.where(lax.broadcasted_iota(jnp.int32, (MASK_ROWS, HT), 0) == FLAG_LANE - HEAD_DIM,
                             MASKED, 0.0).astype(BF16)
    qtg, w_plain, far_shift = [], [], []
    for g in range(KV_HEADS):
        heads = range(g * HPG, (g + 1) * HPG)
        qg = jnp.concatenate([qt[h * HEAD_DIM:(h + 1) * HEAD_DIM] for h in heads], axis=1).astype(BF16)
        qtg.append(qg)
        w_plain.append(jnp.concatenate([qg, before_start, jnp.zeros((LANE - HEAD_DIM - MASK_ROWS, HT), BF16)], axis=0))
        far_shift.append(jnp.concatenate(
            [jnp.broadcast_to(tab_ref[h:h + 1, LANE - 1:LANE], (1, TQ)) for h in heads], axis=1))

    def vt_tile(c0, n, kind):
        return jnp.concatenate([vt_ref[c0 + c, kind] for c in range(n)], axis=1)

    last_tile = (ka_ref.shape[0] - WINDOW) // TK - 1

    def far_logits(tile, dst_ref):
        kt = jnp.minimum(tile, last_tile)
        k0 = pl.multiple_of(WINDOW + kt * TK, TK)
        r0 = pl.multiple_of(kt * BLK_PER_TILE, BLK_PER_TILE)
        for g in range(KV_HEADS):
            mt = mask_ref[g, pl.ds(r0, BLK_PER_TILE), :]
            keep = (mt > 0.5) & (tile * BLK_PER_TILE + blk8 < near_blk0)
            mneg = jnp.where(keep, 0.0, MASKED)
            m16 = jnp.concatenate([jnp.concatenate([mneg] * HPG, axis=1),
                                   jnp.zeros((MASK_ROWS - BLK_PER_TILE, HT), F32)], axis=0).astype(BF16)
            w = jnp.concatenate([qtg[g], m16, jnp.zeros((LANE - HEAD_DIM - MASK_ROWS, HT), BF16)], axis=0)
            dst_ref[g] = _dot(ka_ref[pl.ds(k0, TK), g * LANE:(g + 1) * LANE], w)

    def far_softmax(kt, src_ref, carry):
        c0 = (WINDOW // LANE) + jnp.minimum(kt, last_tile) * chunks_per_tile
        return tuple(_online_step_t(carry[g], src_ref[g], far_shift[g], vt_tile(c0, chunks_per_tile, g))
                     for g in range(KV_HEADS))

    @pl.when(n_far > 0)
    def _():
        far_logits(0, sta_ref)

    def far_body(trip, carry):
        bufs = (sta_ref, stb_ref)
        for u in range(FAR_UNROLL):
            tile = trip * FAR_UNROLL + u
            far_logits(tile + 1, bufs[(u + 1) % 2])
            carry = far_softmax(tile, bufs[u % 2], carry)
        return carry

    carry = lax.fori_loop(0, (n_far + FAR_UNROLL - 1) // FAR_UNROLL, far_body, (init, init))

    kn0 = pl.multiple_of(WINDOW + (i - 1) * TQ, TQ)
    base = pl.multiple_of(jnp.clip((near_blk0 // BLK_PER_TILE) * BLK_PER_TILE, 0, n_sel - MASK_ROWS), BLK_PER_TILE)
    key_blk = near_blk0 + lax.broadcasted_iota(jnp.int32, (NEAR, MASK_ROWS), 0) // SEL_BLOCK
    expand = (key_blk == base + lax.broadcasted_iota(jnp.int32, (NEAR, MASK_ROWS), 1)).astype(BF16)
    kw0 = pl.multiple_of(i * TQ, TQ)
    rows_t = []
    st_near, st_win = [], []
    for g in range(KV_HEADS):
        mneg = jnp.where(mask_ref[g, pl.ds(base, MASK_ROWS), :] > 0.5, 0.0, MASKED).astype(BF16)
        sel_add = _dot(expand, mneg)
        st = _dot(ka_ref[pl.ds(kn0, NEAR), g * LANE:(g + 1) * LANE], w_plain[g])
        st_near.append(st + ntab_ref[g] + jnp.concatenate([sel_add] * HPG, axis=1))
        st = _dot(ka_ref[pl.ds(kw0, W_KEYS), (KV_HEADS + g) * LANE:(KV_HEADS + g + 1) * LANE], w_plain[g])
        st_win.append(st + wtab_ref[g])
    for g in range(KV_HEADS):
        _, acc = _online_step_t(carry[g], st_near[g], zero_shift, vt_tile(kn0 // LANE, NEAR // LANE, g))
        os_t = acc[:HEAD_DIM] / acc[HEAD_DIM:HEAD_DIM + 1]
        _, acc = _online_step_t(init, st_win[g], zero_shift, vt_tile(kw0 // LANE, W_KEYS // LANE, KV_HEADS + g))
        ow_t = acc[:HEAD_DIM] / acc[HEAD_DIM:HEAD_DIM + 1]

        for h in range(HPG):
            head = g * HPG + h
            ls = slice(h * TQ, (h + 1) * TQ)
            rows_t.append(gt[head:head + 1] * oct_[head * HEAD_DIM:(head + 1) * HEAD_DIM]
                          + gt[N_HEADS + head:N_HEADS + head + 1] * os_t[:, ls]
                          + gt[2 * N_HEADS + head:2 * N_HEADS + head + 1] * ow_t[:, ls])
    o_ref[...] = jnp.concatenate(rows_t, axis=0).T


def _distance_tables_kernel(tab_ref, ntab_ref, wtab_ref):
    head = pl.program_id(0) * HPG + pl.program_id(1)
    row = tab_ref[pl.ds(head, 1), :]
    for out_ref, start, max_dist in ((ntab_ref, TQ, None), (wtab_ref, WINDOW, WINDOW)):
        keys = out_ref.shape[0]
        dist = (start + lax.broadcasted_iota(jnp.int32, (keys, TQ), 1)
                - lax.broadcasted_iota(jnp.int32, (keys, TQ), 0))
        ok = dist >= 0 if max_dist is None else (dist >= 0) & (dist <= max_dist)
        out_ref[...] = jnp.where(ok, _bias_lookup(row, jnp.clip(dist, 0, LANE - 1)), NEG)


def distance_tables(tab):
    return pl.pallas_call(
        _distance_tables_kernel,
        grid=(KV_HEADS, HPG),
        in_specs=[pl.BlockSpec((N_HEADS, LANE), lambda g, h: (0, 0))],
        out_specs=[pl.BlockSpec((None, NEAR, TQ), lambda g, h: (g, 0, h)),
                   pl.BlockSpec((None, W_KEYS, TQ), lambda g, h: (g, 0, h))],
        out_shape=[jax.ShapeDtypeStruct((KV_HEADS, NEAR, HT), F32),
                   jax.ShapeDtypeStruct((KV_HEADS, W_KEYS, HT), F32)],
        compiler_params=_cparams(("parallel", "parallel")),
        name="distance_tables",
    )(tab)


def attention_kv_layout(hproj):
    t = hproj.shape[0]
    grp = lambda c0, g: hproj[:, c0 + g * HEAD_DIM:c0 + (g + 1) * HEAD_DIM]
    e8 = jnp.asarray(np.eye(BLK_PER_TILE, dtype=np.float32)[(np.arange(t) % TK) // SEL_BLOCK])
    z = lambda w: jnp.zeros((t, w), F32)
    ka = [jnp.concatenate([grp(C_KVS, g), e8, z(LANE - HEAD_DIM - BLK_PER_TILE)], axis=1) for g in range(KV_HEADS)]
    ka += [jnp.concatenate([grp(C_KVW, g), z(LANE - HEAD_DIM)], axis=1) for g in range(KV_HEADS)]
    lead = jnp.zeros((WINDOW, LANE), F32).at[:, FLAG_LANE].set(1.0)
    ka = jnp.concatenate([jnp.tile(lead, (1, 2 * KV_HEADS)), jnp.concatenate(ka, axis=1)], axis=0).astype(BF16)
    ones = jnp.ones((t, 1), F32)
    vts = []
    for c0 in (C_KVS, C_KVW):
        for g in range(KV_HEADS):
            v = jnp.concatenate([grp(c0 + KV_HEADS * HEAD_DIM, g), ones, z(VT_ROWS - HEAD_DIM - 1)], axis=1)
            v = jnp.pad(v.astype(BF16), ((WINDOW, 0), (0, 0))).reshape(-1, LANE, VT_ROWS)
            vts.append(jnp.swapaxes(v, 1, 2))
    return ka, jnp.stack(vts, axis=1)


def nsa_sel_prompt(hproj, o_c, mask, ka, vt, tab, ntab, wtab):
    t = hproj.shape[0]
    n_sel = mask.shape[1]
    assert t % TK == 0 and n_sel % BLK_PER_TILE == 0 and n_sel >= MASK_ROWS
    const = lambda a: pl.BlockSpec(a.shape, lambda i: (0,) * a.ndim, pipeline_mode=pl.Buffered(1))
    return pl.pallas_call(
        functools.partial(_nsa_sel_kernel, n_sel=n_sel),
        grid=(t // TQ,),
        in_specs=[pl.BlockSpec((TQ, ATT_W), lambda i: (i, C_Q // ATT_W)),
                  pl.BlockSpec((TQ, LANE), lambda i: (i, C_NG // LANE)),
                  pl.BlockSpec((TQ, ATT_W), lambda i: (i, 0)),
                  pl.BlockSpec((KV_HEADS, n_sel, TQ), lambda i: (0, 0, i)),
                  const(ka), const(vt), const(tab), const(ntab), const(wtab)],
        out_specs=pl.BlockSpec((TQ, ATT_W), lambda i: (i, 0)),
        out_shape=jax.ShapeDtypeStruct((t, ATT_W), F32),
        scratch_shapes=[pltpu.VMEM((KV_HEADS, TK, HT), F32)] * 2,
        compiler_params=_cparams(("parallel",)),
        name="nsa_sel_prompt",
    )(hproj, hproj, o_c, mask, ka, vt, tab, ntab, wtab)


def _hgrn_gates(hq, hf, lb):
    sig = _sigmoid(hf)
    forget = lb + (1.0 - lb) * sig
    logf = jnp.log(jnp.maximum(forget, TINY))
    k = (1.0 - lb) * (1.0 - sig)
    q = hq * _sigmoid(hq)
    return q, k, logf


def _hgrn_intra(q, k, v, a, sub):
    n = q.shape[0]
    row = lax.broadcasted_iota(jnp.int32, (n, 1), 0) % sub
    outs = [jnp.zeros((n, HG_DV), F32) for _ in range(HG_HEADS)]
    for d in range(sub):
        ks = k if d == 0 else pltpu.roll(k, d, 0)
        a_s = a if d == 0 else pltpu.roll(a, d, 0)
        vs = v if d == 0 else pltpu.roll(v, d, 0)
        msk = row >= d
        w = q * ks * jnp.exp(jnp.where(msk, a - a_s, 0.0)) * msk.astype(F32)
        for h in range(HG_HEADS):
            r = jnp.sum(w[:, h * HG_DK:(h + 1) * HG_DK], axis=-1, keepdims=True)
            outs[h] = outs[h] + r * vs[:, h * HG_DV:(h + 1) * HG_DV]
    return outs


def _hgrn_prompt_kernel(hq_ref, hf_ref, hi_ref, lb_ref, tri_ref, o_ref, st_out_ref, st_ref):
    step = pl.program_id(0)

    @pl.when(step == 0)
    def _():
        st_ref[...] = jnp.zeros(st_ref.shape, F32)

    n = hq_ref.shape[0]
    q, k, logf = _hgrn_gates(hq_ref[...], hf_ref[...], lb_ref[...])
    v = hi_ref[...]
    a = jnp.dot(tri_ref[...], logf, preferred_element_type=F32, precision=lax.Precision.HIGHEST)
    outs = _hgrn_intra(q, k, v, a, SUB)
    rows_out = [[] for _ in range(HG_HEADS)]
    for c in range(n // SUB):
        sl = slice(c * SUB, (c + 1) * SUB)
        a_c = a[sl]
        a_last = a_c[SUB - 1:SUB]
        qe = (q[sl] * jnp.exp(a_c)).astype(BF16)
        kd = (k[sl] * jnp.exp(a_last - a_c)).astype(BF16)
        dec = jnp.exp(a_last)
        for h in range(HG_HEADS):
            hs = slice(h * HG_DK, (h + 1) * HG_DK)
            st = st_ref[h]
            rows_out[h].append(outs[h][sl] + _dot_nt(qe[:, hs], st.astype(BF16)))
            st_ref[h] = dec[:, hs] * st + _dot_tn(v[sl, hs].astype(BF16), kd[:, hs])
    o_ref[...] = jnp.concatenate([jnp.concatenate(r, axis=0) for r in rows_out], axis=1)
    st_out_ref[...] = st_ref[...]


def hgrn_prompt(hproj, lb, tc):
    t = hproj.shape[0]
    tc = min(tc, t)
    r = np.arange(tc)
    tri = jnp.asarray(((r[:, None] >= r[None, :]) & (r[:, None] // SUB == r[None, :] // SUB)).astype(np.float32))
    return pl.pallas_call(
        _hgrn_prompt_kernel,
        grid=(t // tc,),
        in_specs=[pl.BlockSpec((tc, HGK_W), lambda i: (i, C_HQ // HGK_W)),
                  pl.BlockSpec((tc, HGK_W), lambda i: (i, C_HF // HGK_W)),
                  pl.BlockSpec((tc, HGK_W), lambda i: (i, C_HI // HGK_W)),
                  pl.BlockSpec((1, HGK_W), lambda i: (0, 0)),
                  pl.BlockSpec((tc, tc), lambda i: (0, 0))],
        out_specs=[pl.BlockSpec((tc, HGK_W), lambda i: (i, 0)),
                   pl.BlockSpec((HG_HEADS, HG_DV, HG_DK), lambda i: (0, 0, 0))],
        out_shape=[jax.ShapeDtypeStruct((t, HGK_W), F32),
                   jax.ShapeDtypeStruct((HG_HEADS, HG_DV, HG_DK), F32)],
        scratch_shapes=[pltpu.VMEM((HG_HEADS, HG_DV, HG_DK), F32)],
        compiler_params=_cparams(("arbitrary",)),
        name="hgrn_prompt",
    )(hproj, hproj, hproj, lb.reshape(1, HGK_W), tri)


def _merge_kernel(x_ref, att_ref, o_ref, hg_ref, ga_ref, gh_ref, ng_ref, fg_ref, wa_ref, wh_ref, wo_ref,
                  xo_ref, xn_ref):
    o = o_ref[...]
    parts = []
    for h in range(HG_HEADS):
        oh = o[:, h * HG_DV:(h + 1) * HG_DV]
        parts.append(oh * lax.rsqrt(jnp.mean(oh * oh, axis=-1, keepdims=True) + EPS))
    hg = hg_ref[...]
    hgo = jnp.concatenate(parts, axis=1) * ng_ref[...] * (hg * _sigmoid(hg))
    m = (_sigmoid(ga_ref[...]) * _dot(att_ref[...].astype(BF16), wa_ref[...])
         + _sigmoid(gh_ref[...]) * _dot(hgo.astype(BF16), wh_ref[...]))
    x = x_ref[...] + _dot(m.astype(BF16), wo_ref[...])
    xo_ref[...] = x
    y = x * lax.rsqrt(jnp.mean(x * x, axis=-1, keepdims=True) + EPS)
    xn_ref[...] = (y * fg_ref[...]).astype(BF16)


def merge(x, att, o, hproj, hg_norm_g, ffn_g, wa, wh, wo, tm):
    t = x.shape[0]
    tm = min(tm, t)
    row = lambda w, cb: pl.BlockSpec((tm, w), lambda i: (i, cb))
    full = lambda a: pl.BlockSpec(a.shape, lambda i: (0, 0))
    hg_norm_g = hg_norm_g.reshape(1, HGK_W)
    ffn_g = ffn_g.reshape(1, D_MODEL)
    return pl.pallas_call(
        _merge_kernel,
        grid=(t // tm,),
        in_specs=[row(D_MODEL, 0), row(ATT_W, 0), row(HGK_W, 0), row(HGK_W, C_HG // HGK_W),
                  row(D_MODEL, C_GA // D_MODEL), row(D_MODEL, C_GH // D_MODEL),
                  full(hg_norm_g), full(ffn_g), full(wa), full(wh), full(wo)],
        out_specs=[row(D_MODEL, 0), row(D_MODEL, 0)],
        out_shape=[jax.ShapeDtypeStruct((t, D_MODEL), F32), jax.ShapeDtypeStruct((t, D_MODEL), BF16)],
        compiler_params=_cparams(("parallel",)),
        name="merge",
    )(x, att, o, hproj, hproj, hproj, hg_norm_g, ffn_g, wa, wh, wo)


def _topk_cols(x, k, payload=None):
    (vals, ids, pay), = _topk_cols_lockstep([x], k, None if payload is None else [payload])
    return vals, ids, pay


def _topk_cols_lockstep(xs, k, payloads=None):
    n, t = xs[0].shape
    row = lax.broadcasted_iota(jnp.int32, (n, t), 0)
    slot = lax.broadcasted_iota(jnp.int32, (k, t), 0)
    xs = list(xs)
    vals = [jnp.zeros((k, t), F32) for _ in xs]
    ids = [jnp.zeros((k, t), jnp.int32) for _ in xs]
    pay = [jnp.zeros((k, t), jnp.int32) for _ in xs]
    for it in range(k):
        here = slot == it
        for j, x in enumerate(xs):
            m = jnp.max(x, axis=0, keepdims=True)
            idx = jnp.min(jnp.where(x == m, row, n), axis=0, keepdims=True)
            hit = row == idx
            vals[j] = jnp.where(here, m, vals[j])
            ids[j] = jnp.where(here, idx, ids[j])
            if payloads is not None:
                pay[j] = jnp.where(here, jnp.max(jnp.where(hit, payloads[j], -1), axis=0, keepdims=True), pay[j])
            xs[j] = jnp.where(hit, NEG_INF, x)
    return [(vals[j], ids[j], pay[j] if payloads is not None else None) for j in range(len(xs))]


def _pair_candidates():
    pairs = [(a, b) for a in range(PEER_TOPK) for b in range(PEER_TOPK) if (a + 1) * (b + 1) <= PEER_TOPK]
    rows = -(-len(pairs) // 8) * 8
    sel = np.zeros((2, rows, PEER_TOPK), np.float32)
    for r, (a, b) in enumerate(pairs):
        sel[0, r, a] = 1.0
        sel[1, r, b] = 1.0
    return len(pairs), jnp.asarray(sel)


def _pick_rows(sel, x):
    return jnp.dot(sel, x, preferred_element_type=F32, precision=lax.Precision.HIGHEST)


def _peer_topk_kernel(q_ref, keys_ref, pair_ref, i1_ref, i2_ref, g_ref, *, n_pairs):
    o1, o2, og = [], [], []
    live = lax.broadcasted_iota(jnp.int32, (pair_ref.shape[1], q_ref.shape[0]), 0) < n_pairs
    cands, cidxs = [], []
    for h in range(PEER_HEADS):
        scores = [_dot_nt(keys_ref[side], q_ref[:, (h * 2 + side) * PEER_DH:(h * 2 + side + 1) * PEER_DH].astype(BF16))
                  for side in range(2)]
        (v1, i1, _), (v2, i2, _) = _topk_cols_lockstep(scores, PEER_TOPK)
        cands.append(jnp.where(live, _pick_rows(pair_ref[0], v1) + _pick_rows(pair_ref[1], v2), NEG_INF))
        cidxs.append((_pick_rows(pair_ref[0], i1.astype(F32)) * PEER_NKEYS
                      + _pick_rows(pair_ref[1], i2.astype(F32))).astype(jnp.int32))
    for h0 in range(0, PEER_HEADS, 2):
        for sv, _, e in _topk_cols_lockstep(cands[h0:h0 + 2], PEER_TOPK, cidxs[h0:h0 + 2]):
            ex = jnp.exp(sv - sv[0:1])
            o1.append(e // PEER_NKEYS)
            o2.append(e % PEER_NKEYS)
            og.append(ex / jnp.sum(ex, axis=0, keepdims=True))
    i1_ref[...] = jnp.concatenate(o1, axis=0).astype(F32).T.astype(jnp.int32)
    i2_ref[...] = jnp.concatenate(o2, axis=0).astype(F32).T.astype(jnp.int32)
    g_ref[...] = jnp.concatenate(og, axis=0).T


def peer_topk(q, keys_bf):
    t = q.shape[0]
    tb = LANE
    nsel = PEER_HEADS * PEER_TOPK
    n_pairs, pair_sel = _pair_candidates()
    out = lambda dt: jax.ShapeDtypeStruct((t, nsel), dt)
    return pl.pallas_call(
        functools.partial(_peer_topk_kernel, n_pairs=n_pairs),
        grid=(t // tb,),
        in_specs=[pl.BlockSpec((tb, q.shape[1]), lambda i: (i, 0)),
                  pl.BlockSpec(keys_bf.shape, lambda i: (0, 0, 0)),
                  pl.BlockSpec(pair_sel.shape, lambda i: (0, 0, 0))],
        out_specs=[pl.BlockSpec((tb, nsel), lambda i: (i, 0))] * 3,
        out_shape=[out(jnp.int32), out(jnp.int32), out(F32)],
        compiler_params=_cparams(("parallel",)),
        name="peer_topk",
    )(q, keys_bf, pair_sel)


GATE_UNROLL = 32


def _peer_dense_kernel(xn_ref, x_ref, i1_ref, i2_ref, g_ref, u_ref, v_ref, o_ref, acc_ref, gate_ref, *, n_a):
    j = pl.program_id(1)
    tm, nsel = i1_ref.shape

    @pl.when(j == 0)
    def _():
        acc_ref[...] = jnp.zeros(acc_ref.shape, F32)
        row = lax.broadcasted_iota(jnp.int32, (PEER_NKEYS, nsel), 0)

        def body(tt, _):
            for k in range(GATE_UNROLL):
                t = tt * GATE_UNROLL + k
                pa = jnp.where(row == i1_ref[pl.ds(t, 1), :], 1.0, 0.0).astype(BF16)
                gb = jnp.where(row == i2_ref[pl.ds(t, 1), :], g_ref[pl.ds(t, 1), :], 0.0).astype(BF16)
                gate_ref[pl.ds(pl.multiple_of(t * PEER_NKEYS, PEER_NKEYS), PEER_NKEYS), :] = _dot_nt(pa, gb)
            return 0

        lax.fori_loop(0, tm // GATE_UNROLL, body, 0)

    s = _dot_nt(xn_ref[...], u_ref[...])
    gate = jnp.concatenate([gate_ref[pl.ds(j * n_a + k, tm, stride=PEER_NKEYS), :] for k in range(n_a)], axis=1)
    acc_ref[...] += _dot((gate * _gelu(s)).astype(BF16), v_ref[...])

    @pl.when(j == pl.num_programs(1) - 1)
    def _():
        o_ref[...] = x_ref[...] + acc_ref[...]


def peer_dense(xn, x, i1, i2, g, u_bf, v_bf, tm, n_a):
    t = x.shape[0]
    tm = min(tm, t)
    te = n_a * PEER_NKEYS
    nsel = i1.shape[1]
    tok = lambda w: pl.BlockSpec((tm, w), lambda i, j: (i, 0))
    return pl.pallas_call(
        functools.partial(_peer_dense_kernel, n_a=n_a),
        grid=(t // tm, PEER_NKEYS // n_a),
        in_specs=[tok(D_MODEL), tok(D_MODEL), tok(nsel), tok(nsel), tok(nsel),
                  pl.BlockSpec((te, D_MODEL), lambda i, j: (j, 0)),
                  pl.BlockSpec((te, D_MODEL), lambda i, j: (j, 0))],
        out_specs=tok(D_MODEL),
        out_shape=jax.ShapeDtypeStruct((t, D_MODEL), F32),
        scratch_shapes=[pltpu.VMEM((tm, D_MODEL), F32), pltpu.VMEM((tm * PEER_NKEYS, PEER_NKEYS), F32)],
        compiler_params=_cparams(("parallel", "arbitrary")),
        name="peer_dense",
    )(xn, x, i1, i2, g, u_bf, v_bf)


def peer(xn_bf, x, wq_bf, keys_bf, u_bf, v_bf, tm):
    t = x.shape[0]
    q = matmul(xn_bf, wq_bf, min(512, t), 1024)
    i1, i2, g = peer_topk(q, keys_bf)
    return peer_dense(xn_bf, x, i1, i2, g, u_bf, v_bf, tm, 16)


ROWS = 8


def _pad_rows(x, rows=ROWS):
    return jnp.concatenate([x, jnp.zeros((rows - x.shape[0], x.shape[1]), x.dtype)], axis=0)


def _nsa_cmp_sample_kernel(q_ref, ckv_ref, tab_ref, cover_ref, oc_ref, sel_ref, *, nc, past, n_sel):
    nseq, nt = q_ref.shape[:2]
    t_pos = past + lax.broadcasted_iota(jnp.int32, (ROWS, 1), 0)
    for b in range(nseq):
        outs, imps = _cmp_branch(_pad_rows(q_ref[b]), t_pos, ckv_ref.at[b], tab_ref, cover_ref, nc)
        for g in range(KV_HEADS):
            sel_ref[b, g] = _topk_idx(imps[g], min(SEL_TOPN, n_sel))
        oc_ref[b] = jnp.concatenate(outs, axis=1)[:nt]


def nsa_cmp_sample(hproj3, ckv, tab, nc, past):
    nb, nt, _ = hproj3.shape
    ncp = ckv.shape[1]
    n_sel = -(-(past + nt) // SEL_BLOCK)
    n_sel_pad = -(-n_sel // LANE) * LANE
    cover = cover_matrix(ncp, n_sel_pad)
    per = 4 if nb % 4 == 0 else 1
    return pl.pallas_call(
        functools.partial(_nsa_cmp_sample_kernel, nc=nc, past=past, n_sel=n_sel),
        grid=(nb // per,),
        in_specs=[pl.BlockSpec((per, nt, ATT_W), lambda b: (b, 0, C_Q // ATT_W)),
                  pl.BlockSpec((per, ncp, KVB_W), lambda b: (b, 0, 0)),
                  pl.BlockSpec((N_HEADS, LANE), lambda b: (0, 0)),
                  pl.BlockSpec((ncp, n_sel_pad), lambda b: (0, 0))],
        out_specs=[pl.BlockSpec((per, nt, ATT_W), lambda b: (b, 0, 0)),
                   pl.BlockSpec((per, KV_HEADS, ROWS, LANE), lambda b: (b, 0, 0, 0))],
        out_shape=[jax.ShapeDtypeStruct((nb, nt, ATT_W), F32),
                   jax.ShapeDtypeStruct((nb, KV_HEADS, ROWS, LANE), jnp.int32)],
        compiler_params=_cparams(("parallel",)),
        name="nsa_cmp_sample",
    )(hproj3, ckv, tab, cover)


def _nsa_sel_sample_kernel(sel_ref, pt_ref, q_ref, ng_ref, oc_ref, kvs_ref, kvw_ref, win_ref, tab_ref, cache_ref,
                           o_ref, buf_ref, sem, *, layer, past, w_len):
    b = pl.program_id(0)
    nt = q_ref.shape[0]
    nslot = SEL_TOPN
    per_page = PAGE_SIZE // SEL_BLOCK

    def slot_copy(t, g, slot):
        blk = sel_ref[((b * nt + t) * KV_HEADS + g) * nslot + slot]
        in_past = blk * SEL_BLOCK < past
        blk_c = jnp.minimum(blk, past // SEL_BLOCK - 1)
        page = pt_ref[b, blk_c // per_page]
        src = cache_ref.at[layer, page, pl.ds(KVB_W, KVB_W), :]
        dst = buf_ref.at[t * KV_HEADS + g, :, pl.ds(slot * PAGE_SIZE, PAGE_SIZE)]
        return blk, in_past, pltpu.make_async_copy(src, dst, sem)

    for t in range(nt):
        for g in range(KV_HEADS):
            for slot in range(nslot):
                _, in_past, cp = slot_copy(t, g, slot)

                @pl.when(in_past)
                def _():
                    cp.start()

                @pl.when(jnp.logical_not(in_past))
                def _():
                    buf_ref[t * KV_HEADS + g, :, pl.ds(slot * PAGE_SIZE, PAGE_SIZE)] = jnp.zeros(
                        (KVB_W, PAGE_SIZE), F32)
    for t in range(nt):
        for g in range(KV_HEADS):
            for slot in range(nslot):
                _, in_past, cp = slot_copy(t, g, slot)

                @pl.when(in_past)
                def _():
                    cp.wait()

    gates = _sigmoid(ng_ref[...])
    nk = nslot * PAGE_SIZE
    lane_slot = lax.broadcasted_iota(jnp.int32, (1, nk), 1) // PAGE_SIZE
    lane_off = lax.broadcasted_iota(jnp.int32, (1, nk), 1) % PAGE_SIZE
    j_new = lax.broadcasted_iota(jnp.int32, (1, LANE), 1)
    kvn = jnp.concatenate([kvs_ref[...], kvw_ref[...]], axis=1)
    kvn_pad = _pad_rows(kvn, LANE).astype(BF16)
    win = win_ref[...].astype(BF16)
    w_idx = lax.broadcasted_iota(jnp.int32, (1, w_len + LANE), 1)
    rows_out = []
    for t in range(nt):
        t_pos = past + t
        heads_out = []
        for g in range(KV_HEADS):
            q4 = jnp.concatenate(
                [q_ref[t:t + 1, (g * HPG + h) * HEAD_DIM:(g * HPG + h + 1) * HEAD_DIM] for h in range(HPG)],
                axis=0)
            q4 = (q4 * ATT_SCALE).astype(BF16)

            def bias_of(dist):
                d = jnp.broadcast_to(jnp.clip(dist, 0, LANE - 1), (ROWS, dist.shape[1]))
                return jnp.concatenate(
                    [_bias_lookup(tab_ref[g * HPG + h:g * HPG + h + 1, :], d)[0:1] for h in range(HPG)], axis=0)

            pos = jnp.zeros((1, nk), jnp.int32)
            lane_blk = jnp.zeros((1, nk), jnp.int32)
            has_new = jnp.zeros((1, 1), jnp.int32)
            for slot in range(nslot):
                blk = sel_ref[((b * nt + t) * KV_HEADS + g) * nslot + slot]
                pos = jnp.where(lane_slot == slot, (blk // per_page) * PAGE_SIZE, pos)
                lane_blk = jnp.where(lane_slot == slot, blk, lane_blk)
                has_new = jnp.maximum(has_new, (blk * SEL_BLOCK >= past).astype(jnp.int32))
            pos = pos + lane_off
            in_block = pos // SEL_BLOCK == lane_blk
            tg = t * KV_HEADS + g
            k_t = buf_ref[tg, g * HEAD_DIM:(g + 1) * HEAD_DIM, :].astype(BF16)
            v_t = buf_ref[tg, (KV_HEADS + g) * HEAD_DIM:(KV_HEADS + g + 1) * HEAD_DIM, :].astype(BF16)
            k_new = kvn_pad[:, g * HEAD_DIM:(g + 1) * HEAD_DIM]
            v_new = kvn_pad[:, (KV_HEADS + g) * HEAD_DIM:(KV_HEADS + g + 1) * HEAD_DIM]
            dist = jnp.concatenate([t_pos - pos, t - j_new], axis=1)
            ok = jnp.concatenate([in_block & (pos < past), (j_new <= t) & (has_new > 0)], axis=1)
            s = jnp.concatenate([_dot(q4, k_t), _dot_nt(q4, k_new)], axis=1)
            s = jnp.where(ok, s + bias_of(dist), NEG)
            e = jnp.where(ok, jnp.exp(s - jnp.max(s, axis=-1, keepdims=True)), 0.0)
            p = (e / jnp.sum(e, axis=-1, keepdims=True)).astype(BF16)
            o_s = _dot_nt(p[:, :nk], v_t) + _dot(p[:, nk:], v_new)

            kw_new = kvn_pad[:, (2 * KV_HEADS + g) * HEAD_DIM:(2 * KV_HEADS + g + 1) * HEAD_DIM]
            vw_new = kvn_pad[:, (3 * KV_HEADS + g) * HEAD_DIM:(3 * KV_HEADS + g + 1) * HEAD_DIM]
            dist_w = t_pos - (past - w_len + w_idx)
            ok_w = (dist_w >= 0) & (dist_w <= WINDOW) & (w_idx < w_len + nt)
            s = jnp.concatenate([_dot(q4, win[g * HEAD_DIM:(g + 1) * HEAD_DIM]), _dot_nt(q4, kw_new)], axis=1)
            s = jnp.where(ok_w, s + bias_of(dist_w), NEG)
            e = jnp.where(ok_w, jnp.exp(s - jnp.max(s, axis=-1, keepdims=True)), 0.0)
            p = (e / jnp.sum(e, axis=-1, keepdims=True)).astype(BF16)
            o_w = (_dot_nt(p[:, :w_len], win[(KV_HEADS + g) * HEAD_DIM:(KV_HEADS + g + 1) * HEAD_DIM])
                   + _dot(p[:, w_len:], vw_new))

            for h in range(HPG):
                head = g * HPG + h
                o_c = oc_ref[t:t + 1, head * HEAD_DIM:(head + 1) * HEAD_DIM]
                heads_out.append(gates[t:t + 1, head:head + 1] * o_c
                                 + gates[t:t + 1, N_HEADS + head:N_HEADS + head + 1] * o_s[h:h + 1]
                                 + gates[t:t + 1, 2 * N_HEADS + head:2 * N_HEADS + head + 1] * o_w[h:h + 1])
        rows_out.append(jnp.concatenate(heads_out, axis=1))
    o_ref[...] = jnp.concatenate(rows_out, axis=0)


def nsa_sel_sample(hproj3, o_c, sel_flat, page_table, cache_t, win_t, tab, layer, past):
    nb, nt, _ = hproj3.shape
    w_len = win_t.shape[3]
    assert past % SEL_BLOCK == 0 and PAGE_SIZE % SEL_BLOCK == 0 and w_len % LANE == 0 and past >= w_len
    grid_spec = pltpu.PrefetchScalarGridSpec(
        num_scalar_prefetch=2,
        grid=(nb,),
        in_specs=[pl.BlockSpec((None, nt, ATT_W), lambda b, s, p: (b, 0, C_Q // ATT_W)),
                  pl.BlockSpec((None, nt, LANE), lambda b, s, p: (b, 0, C_NG // LANE)),
                  pl.BlockSpec((None, nt, ATT_W), lambda b, s, p: (b, 0, 0)),
                  pl.BlockSpec((None, nt, KVB_W), lambda b, s, p: (b, 0, C_KVS // KVB_W)),
                  pl.BlockSpec((None, nt, KVB_W), lambda b, s, p: (b, 0, C_KVW // KVB_W)),
                  pl.BlockSpec((None, None, KVB_W, w_len), lambda b, s, p: (layer, b, 0, 0)),
                  pl.BlockSpec((N_HEADS, LANE), lambda b, s, p: (0, 0)),
                  pl.BlockSpec(memory_space=pl.ANY)],
        out_specs=pl.BlockSpec((None, nt, ATT_W), lambda b, s, p: (b, 0, 0)),
        scratch_shapes=[pltpu.VMEM((nt * KV_HEADS, KVB_W, SEL_TOPN * PAGE_SIZE), F32),
                        pltpu.SemaphoreType.DMA(())],
    )
    return pl.pallas_call(
        functools.partial(_nsa_sel_sample_kernel, layer=layer, past=past, w_len=w_len),
        grid_spec=grid_spec,
        out_shape=jax.ShapeDtypeStruct((nb, nt, ATT_W), F32),
        compiler_params=_cparams(("arbitrary",)),
        name="nsa_sel_sample",
    )(sel_flat, page_table, hproj3, hproj3, o_c, hproj3, hproj3, win_t, tab, cache_t)


def _hgrn_sample_kernel(hq_ref, hf_ref, hi_ref, lb_ref, s_ref, o_ref, so_ref):
    nt = hq_ref.shape[0]
    live = (lax.broadcasted_iota(jnp.int32, (ROWS, 1), 0) < nt).astype(F32)
    q, k, logf = _hgrn_gates(_pad_rows(hq_ref[...]), _pad_rows(hf_ref[...]), lb_ref[...])
    q, k, logf = q * live, k * live, logf * live
    v = _pad_rows(hi_ref[...])
    rows = [logf[0:1]]
    for i in range(1, ROWS):
        rows.append(rows[-1] + logf[i:i + 1])
    a = jnp.concatenate(rows, axis=0)
    outs = _hgrn_intra(q, k, v, a, ROWS)
    a_last = a[ROWS - 1:ROWS]
    qe = (q * jnp.exp(a)).astype(BF16)
    kd = (k * jnp.exp(a_last - a)).astype(BF16)
    dec = jnp.exp(a_last)
    o_parts = []
    for h in range(HG_HEADS):
        hs = slice(h * HG_DK, (h + 1) * HG_DK)
        st = s_ref[h].T
        o_parts.append(outs[h] + _dot_nt(qe[:, hs], st.astype(BF16)))
        so_ref[h] = (dec[:, hs] * st + _dot_tn(v[:, hs].astype(BF16), kd[:, hs])).T
    o_ref[...] = jnp.concatenate(o_parts, axis=1)[:nt]


def hgrn_sample(hproj3, lb, state):
    nb, nt, _ = hproj3.shape
    col = lambda c: pl.BlockSpec((None, nt, HGK_W), lambda b: (b, 0, c // HGK_W))
    st_spec = pl.BlockSpec((None, HG_HEADS, HG_DK, HG_DV), lambda b: (b, 0, 0, 0))
    return pl.pallas_call(
        _hgrn_sample_kernel,
        grid=(nb,),
        in_specs=[col(C_HQ), col(C_HF), col(C_HI), pl.BlockSpec((1, HGK_W), lambda b: (0, 0)), st_spec],
        out_specs=[pl.BlockSpec((None, nt, HGK_W), lambda b: (b, 0, 0)), st_spec],
        out_shape=[jax.ShapeDtypeStruct((nb, nt, HGK_W), F32),
                   jax.ShapeDtypeStruct(state.shape, F32)],
        compiler_params=_cparams(("parallel",)),
        name="hgrn_sample",
    )(hproj3, hproj3, hproj3, lb.reshape(1, HGK_W), state)


def reorder_w_in(w):
    o = np.cumsum([0, ATT_W, KVB_W, KVB_W, KVB_W, N_GATE, HGK_W, HGK_W, HGK_W, HGK_W, D_MODEL, D_MODEL])
    q, kvc, kvs, kvw, ng, hq, hf, hi, hg, ga, gh = [w[:, o[i]:o[i + 1]] for i in range(11)]
    pad = jnp.zeros((w.shape[0], N_PROJ - C_NG - N_GATE), w.dtype)
    return jnp.concatenate([q, hq, ga, gh, hf, hi, hg, kvc, kvs, kvw, ng, pad], axis=1).astype(BF16)


def prompt_layer(x, lp, tabs):
    t = x.shape[0]
    tab, ntab, wtab = tabs
    hproj = norm_matmul(x, lp["attn_g"], lp["w_in"], min(512, t), 1408)
    y = cproj_rows(hproj, C_KVC // KVB_W, lp["wbig"], 2048)
    ckv = cmp_mlp(y[None], lp["pe_term"], lp["w2big"])[0]
    o_c, mask = nsa_cmp_prompt(hproj, ckv, tab, t // CMP_STRIDE - 1)
    ka, vt = attention_kv_layout(hproj)
    att = nsa_sel_prompt(hproj, o_c, mask, ka, vt, tab, ntab, wtab)
    o_hg, st = hgrn_prompt(hproj, lp["lb"], 128)
    x, xn = merge(x, att, o_hg, hproj, lp["hg_norm_g"], lp["ffn_g"], lp["wa"], lp["wh"], lp["wo"], 256)
    x = peer(xn, x, lp["peer_wq"], lp["peer_keys"], lp["peer_u"], lp["peer_v"], 256)
    return x, hproj[:, C_KVC:C_KVC + 2 * KVB_W], hproj[:, C_KVW:C_KVW + KVB_W], jnp.swapaxes(st, 1, 2)


def layer_params(l, norm_attn_g, norm_ffn_g, w_in, cmp_pe, cmp_w1, cmp_w2, lower_bounds, hg_norm_g,
                 w_br_attn, w_br_hg, w_out, peer_wq, peer_keys, peer_u, peer_v):
    wbig, w2big, pe_rows = compress_weights(cmp_pe[l], cmp_w1[l], cmp_w2[l])
    return dict(attn_g=norm_attn_g[l], ffn_g=norm_ffn_g[l], w_in=reorder_w_in(w_in[l]),
                wbig=wbig, w2big=w2big, pe_term=pe_term_from(pe_rows, wbig),
                lb=lower_bounds[l], hg_norm_g=hg_norm_g[l],
                wa=w_br_attn[l].astype(BF16), wh=w_br_hg[l].astype(BF16), wo=w_out[l].astype(BF16),
                peer_wq=peer_wq[l].astype(BF16), peer_keys=peer_keys[l].astype(BF16),
                peer_u=peer_u[l].astype(BF16), peer_v=peer_v[l].astype(BF16))


def sample_layer(x, lp, tab, layer, cache_t, win_t, state, page_table, nb, nt):
    past = page_table.shape[1] * PAGE_SIZE
    assert nt < CMP_STRIDE, "new tokens never complete a compression chunk"
    hproj = norm_matmul(x, lp["attn_g"], lp["w_in"], x.shape[0], 1408)
    hproj3 = hproj.reshape(nb, nt, N_PROJ)
    y = cproj_pages(cache_t, layer, page_table, lp["wbig"], min(32, page_table.shape[1]))
    ckv = cmp_mlp(y, lp["pe_term"], lp["w2big"])
    o_c, sel = nsa_cmp_sample(hproj3, ckv, tab, past // CMP_STRIDE - 1, past)
    sel_flat = jnp.transpose(sel[:, :, :nt, :SEL_TOPN], (0, 2, 1, 3)).reshape(-1)
    att = nsa_sel_sample(hproj3, o_c, sel_flat, page_table, cache_t, win_t, tab, layer, past)
    o_hg, st = hgrn_sample(hproj3, lp["lb"], state)
    x, xn = merge(x, att.reshape(nb * nt, ATT_W), o_hg.reshape(nb * nt, HGK_W), hproj, lp["hg_norm_g"],
                  lp["ffn_g"], lp["wa"], lp["wh"], lp["wo"], 256)
    x = peer(xn, x, lp["peer_wq"], lp["peer_keys"], lp["peer_u"], lp["peer_v"], 256)
    return x, hproj[:, C_KVC:C_KVC + 2 * KVB_W], hproj[:, C_KVW:C_KVW + KVB_W], st


def kernel(x_prompt, x_sample, cache_kv, cache_win, state_hgrn, page_table, norm_attn_g, norm_ffn_g, final_norm_g, w_in, cmp_pe, cmp_w1, cmp_w2, rel_bias, hg_lb_logits, hg_norm_g, w_br_attn, w_br_hg, w_out, peer_wq, peer_keys, peer_u, peer_v):
    depth = w_in.shape[0]
    bp, seq, _ = x_prompt.shape
    nb, nt, _ = x_sample.shape
    assert bp == 1
    p_lb = jax.nn.softmax(hg_lb_logits.astype(F32), axis=0)
    lower_bounds = jnp.cumsum(p_lb, axis=0) - p_lb[0]
    tab = bias_table(rel_bias)
    tabs = (tab, *distance_tables(tab))
    n_pool = cache_kv.shape[1]
    w_buf = cache_win.shape[2]
    cache_t = jnp.transpose(cache_kv, (0, 1, 3, 4, 5, 2)).reshape(depth, n_pool, 2 * KVB_W, PAGE_SIZE)
    win_t = jnp.transpose(cache_win, (0, 1, 3, 4, 5, 2)).reshape(depth, nb, KVB_W, w_buf)
    xp = x_prompt.reshape(seq, D_MODEL)
    xs = x_sample.reshape(nb * nt, D_MODEL)
    kv_p, win_p, hs_p, kv_s, win_s, hs_s = [], [], [], [], [], []
    for l in range(depth):
        lp = layer_params(l, norm_attn_g, norm_ffn_g, w_in, cmp_pe, cmp_w1, cmp_w2, lower_bounds, hg_norm_g,
                          w_br_attn, w_br_hg, w_out, peer_wq, peer_keys, peer_u, peer_v)
        xp, kv, kw, st = prompt_layer(xp, lp, tabs)
        kv_p.append(kv.reshape(bp, seq, 4, KV_HEADS, HEAD_DIM))
        win_p.append(kw[seq - min(WINDOW, seq):].reshape(bp, -1, 2, KV_HEADS, HEAD_DIM))
        hs_p.append(st[None])
        xs, kv, kw, st = sample_layer(xs, lp, tab, l, cache_t, win_t, state_hgrn[l], page_table, nb, nt)
        kv_s.append(kv.reshape(nb, nt, 4, KV_HEADS, HEAD_DIM))
        win_s.append(jnp.concatenate([cache_win[l], kw.reshape(nb, nt, 2, KV_HEADS, HEAD_DIM)], axis=1)[:, nt:])
        hs_s.append(st)
    y_prompt = rmsnorm(xp, final_norm_g, min(512, seq)).reshape(bp, seq, D_MODEL)
    y_sample = rmsnorm(xs, final_norm_g, nb * nt).reshape(nb, nt, D_MODEL)
    return (y_prompt, y_sample, jnp.stack(kv_p), jnp.stack(win_p), jnp.stack(hs_p),
            jnp.stack(kv_s), jnp.stack(win_s), jnp.stack(hs_s))
```

```python
import functools
import math

import jax
import jax.numpy as jnp
import numpy as np
from jax import lax
from jax.experimental import pallas as pl
from jax.experimental.pallas import tpu as pltpu

F32 = jnp.float32
BF16 = jnp.bfloat16

D_MODEL = 1024
PAGE_SIZE = 128
N_HEADS = 8
HEAD_DIM = 64
KV_HEADS = 2
HPG = N_HEADS // KV_HEADS
CMP_STRIDE = 16
CMP_BLOCK = 32
SEL_BLOCK = 64
SEL_TOPN = 16
WINDOW = 512
ATT_SCALE = HEAD_DIM ** -0.5
N_BUCKETS = 32
MAX_DISTANCE = 128
HG_HEADS = 4
HG_DK = 128
HG_DV = 128
PEER_HEADS = 8
PEER_NKEYS = 128
PEER_DH = 128
PEER_TOPK = 16
ATT_W = N_HEADS * HEAD_DIM
KVB_W = 2 * KV_HEADS * HEAD_DIM
HGK_W = HG_HEADS * HG_DK
N_GATE = 3 * N_HEADS
EPS = 1e-6
NEG = -1e30
BIG = 1e6
TINY = 1e-30
NEG_INF = float("-inf")

LANE = 128
VMEM_LIMIT = 56 * 1024 * 1024

C_Q, C_HQ, C_GA, C_GH, C_HF, C_HI, C_HG = 0, 512, 1024, 2048, 3072, 3584, 4096
C_KVC, C_KVS, C_KVW, C_NG, N_PROJ = 4608, 4864, 5120, 5376, 5632
SUB = 16
TQ = 128
TK = 512


def _cparams(sem, vmem=VMEM_LIMIT):
    return pltpu.CompilerParams(dimension_semantics=sem, vmem_limit_bytes=vmem)


def _dot(a, b):
    return jnp.dot(a, b, preferred_element_type=F32)


def _dot_nt(a, b):
    return lax.dot_general(a, b, (((1,), (1,)), ((), ())), preferred_element_type=F32)


def _dot_tn(a, b):
    return lax.dot_general(a, b, (((0,), (0,)), ((), ())), preferred_element_type=F32)


def _sigmoid(x):
    return 1.0 / (1.0 + jnp.exp(-x))


def _gelu(x):
    return 0.5 * x * (1.0 + lax.erf(x * (2.0 ** -0.5)))


def _norm_matmul_kernel(x_ref, g_ref, w_ref, o_ref, xn_ref):
    @pl.when(pl.program_id(1) == 0)
    def _():
        x = x_ref[...]
        y = x * lax.rsqrt(jnp.mean(x * x, axis=-1, keepdims=True) + EPS)
        xn_ref[...] = (y * g_ref[...]).astype(BF16)

    o_ref[...] = _dot(xn_ref[...], w_ref[...])


def norm_matmul(x, g, w, tm, tn):
    m, k = x.shape
    n = w.shape[1]
    return pl.pallas_call(
        _norm_matmul_kernel,
        grid=(m // tm, n // tn),
        in_specs=[pl.BlockSpec((tm, k), lambda i, j: (i, 0)),
                  pl.BlockSpec((1, k), lambda i, j: (0, 0)),
                  pl.BlockSpec((k, tn), lambda i, j: (0, j))],
        out_specs=pl.BlockSpec((tm, tn), lambda i, j: (i, j)),
        out_shape=jax.ShapeDtypeStruct((m, n), F32),
        scratch_shapes=[pltpu.VMEM((tm, k), BF16)],
        compiler_params=_cparams(("parallel", "arbitrary")),
        name="norm_matmul",
    )(x, g.reshape(1, k), w)


def _matmul_kernel(x_ref, w_ref, o_ref):
    o_ref[...] = _dot(x_ref[...], w_ref[...])


def matmul(x, w, tm, tn):
    m, k = x.shape
    n = w.shape[1]
    return pl.pallas_call(
        _matmul_kernel,
        grid=(m // tm, n // tn),
        in_specs=[pl.BlockSpec((tm, k), lambda i, j: (i, 0)),
                  pl.BlockSpec((k, tn), lambda i, j: (0, j))],
        out_specs=pl.BlockSpec((tm, tn), lambda i, j: (i, j)),
        out_shape=jax.ShapeDtypeStruct((m, n), F32),
        compiler_params=_cparams(("parallel", "arbitrary")),
        name="matmul",
    )(x, w)


def _rmsnorm_kernel(x_ref, g_ref, o_ref):
    x = x_ref[...]
    o_ref[...] = x * lax.rsqrt(jnp.mean(x * x, axis=-1, keepdims=True) + EPS) * g_ref[...]


def rmsnorm(x, g, tm):
    m, k = x.shape
    return pl.pallas_call(
        _rmsnorm_kernel,
        grid=(m // tm,),
        in_specs=[pl.BlockSpec((tm, k), lambda i: (i, 0)), pl.BlockSpec((1, k), lambda i: (0, 0))],
        out_specs=pl.BlockSpec((tm, k), lambda i: (i, 0)),
        out_shape=jax.ShapeDtypeStruct((m, k), F32),
        compiler_params=_cparams(("parallel",)),
        name="rmsnorm",
    )(x, g.reshape(1, k))


CPROJ_GROUPS = 2


def _cproj_from_pieces(piece_fn, w_ref, y_ref):
    acc = jnp.zeros(y_ref.shape, F32)
    for s in range(CMP_STRIDE):
        acc = acc + _dot(piece_fn(s).astype(BF16), w_ref[s])
    y_ref[...] = acc


def _cproj_rows_kernel(r0_ref, r1_ref, w_ref, y_ref):
    nch = y_ref.shape[0]

    def piece(s):
        return jnp.concatenate([r[pl.ds(s, nch, stride=CMP_STRIDE), :] for r in (r0_ref, r1_ref)], axis=1)

    _cproj_from_pieces(piece, w_ref, y_ref)


def cproj_rows(rows_arr, col_block, wbig, rb):
    t = rows_arr.shape[0]
    rb = min(rb, t)
    half = lambda k: pl.BlockSpec((rb, LANE), lambda i: (i, 2 * col_block + k))
    return pl.pallas_call(
        _cproj_rows_kernel,
        grid=(t // rb,),
        in_specs=[half(0), half(1), pl.BlockSpec(wbig.shape, lambda i: (0, 0, 0))],
        out_specs=pl.BlockSpec((rb // CMP_STRIDE, 2 * KVB_W), lambda i: (i, 0)),
        out_shape=jax.ShapeDtypeStruct((t // CMP_STRIDE, 2 * KVB_W), F32),
        compiler_params=_cparams(("parallel",)),
        name="cproj_rows",
    )(rows_arr, rows_arr, wbig)


def _cproj_pages_kernel(pt_ref, *refs):
    del pt_ref
    n_scr = 2 * CPROJ_GROUPS
    pages, w_ref, y_ref, scr = refs[:-2 - n_scr], refs[-2 - n_scr], refs[-1 - n_scr], refs[-n_scr:]
    per_grp = len(pages) // CPROJ_GROUPS
    nch = y_ref.shape[0] // CPROJ_GROUPS
    for gi in range(CPROJ_GROUPS):
        lo_ref, hi_ref = scr[2 * gi], scr[2 * gi + 1]
        for p in range(per_grp):
            page = pages[gi * per_grp + p]
            rows = pl.ds(p * PAGE_SIZE, PAGE_SIZE)
            lo_ref[rows, :] = page[0:LANE, :].T
            hi_ref[rows, :] = page[LANE:2 * LANE, :].T

        def piece(s, lo_ref=lo_ref, hi_ref=hi_ref):
            return jnp.concatenate([r[pl.ds(s, nch, stride=CMP_STRIDE), :] for r in (lo_ref, hi_ref)], axis=1)

        _cproj_from_pieces(piece, w_ref, y_ref.at[pl.ds(gi * nch, nch)])


def cproj_pages(cache_t, layer, page_table, wbig, pg):
    nb, npages = page_table.shape
    per = PAGE_SIZE // CMP_STRIDE

    def page_spec(p):
        return pl.BlockSpec((None, None, KVB_W, PAGE_SIZE), lambda b, j, pt: (layer, pt[b, j * pg + p], 0, 0))

    grid_spec = pltpu.PrefetchScalarGridSpec(
        num_scalar_prefetch=1,
        grid=(nb, npages // pg),
        in_specs=[page_spec(p) for p in range(pg)] + [pl.BlockSpec(wbig.shape, lambda b, j, pt: (0, 0, 0))],
        out_specs=pl.BlockSpec((None, pg * per, 2 * KVB_W), lambda b, j, pt: (b, j, 0)),
        scratch_shapes=[pltpu.VMEM((pg // CPROJ_GROUPS * PAGE_SIZE, LANE), F32)] * (2 * CPROJ_GROUPS),
    )
    return pl.pallas_call(
        _cproj_pages_kernel,
        grid_spec=grid_spec,
        out_shape=jax.ShapeDtypeStruct((nb, npages * per, 2 * KVB_W), F32),
        compiler_params=_cparams(("parallel", "arbitrary")),
        name="cproj_pages",
    )(page_table, *([cache_t] * pg), wbig)


def _cmp_mlp_kernel(y_ref, pe_ref, w2_ref, o_ref):
    y = y_ref[...]
    n = y.shape[0]
    nxt = pltpu.roll(y[:, KVB_W:], n - 1, 0)
    hsum = y[:, :KVB_W] + nxt + pe_ref[...]
    o_ref[...] = _dot(_gelu(hsum).astype(BF16), w2_ref[...]).astype(o_ref.dtype)


def cmp_mlp(y, pe_term, w2big):
    b, ncp, _ = y.shape
    return pl.pallas_call(
        _cmp_mlp_kernel,
        grid=(b,),
        in_specs=[pl.BlockSpec((None, ncp, 2 * KVB_W), lambda i: (i, 0, 0)),
                  pl.BlockSpec((1, KVB_W), lambda i: (0, 0)),
                  pl.BlockSpec((KVB_W, KVB_W), lambda i: (0, 0))],
        out_specs=pl.BlockSpec((None, ncp, KVB_W), lambda i: (i, 0, 0)),
        out_shape=jax.ShapeDtypeStruct((b, ncp, KVB_W), BF16),
        compiler_params=_cparams(("parallel",)),
        name="cmp_mlp",
    )(y, pe_term, w2big)


def compress_weights(pe, w1, w2):
    w1r = w1.reshape(2, 2, CMP_STRIDE, HEAD_DIM, HEAD_DIM)
    eye = jnp.eye(2, dtype=F32)
    wbig = jnp.einsum('crsdh,cC,gG->scgdrCGh', w1r, eye, eye).reshape(CMP_STRIDE, KVB_W, 2 * KVB_W)
    w2big = jnp.einsum('chd,cC,gG->cghCGd', w2, eye, eye).reshape(KVB_W, KVB_W)
    per = pe.reshape(2, 2, CMP_STRIDE, HEAD_DIM)
    pe_rows = jnp.broadcast_to(jnp.transpose(per, (1, 2, 0, 3))[:, :, :, None, :],
                               (2, CMP_STRIDE, 2, KV_HEADS, HEAD_DIM)).reshape(2 * CMP_STRIDE, KVB_W)
    pe_rows = jnp.pad(pe_rows, ((0, 8 * CMP_STRIDE - 2 * CMP_STRIDE), (0, 0)))
    return wbig.astype(BF16), w2big.astype(BF16), pe_rows


def pe_term_from(pe_rows, wbig):
    ype = cproj_rows(pe_rows, 0, wbig, pe_rows.shape[0])
    return ype[0:1, :KVB_W] + ype[1:2, KVB_W:]


def bias_table(rel_bias):
    n = np.arange(LANE)
    exact = N_BUCKETS // 2
    nf = np.maximum(n, 1).astype(np.float32)
    large = exact + (np.log(nf / np.float32(exact)) / np.float32(math.log(MAX_DISTANCE / exact))
                     * np.float32(N_BUCKETS - exact)).astype(np.int32)
    bucket = np.where(n < exact, n, np.minimum(large, N_BUCKETS - 1))
    assert bucket[LANE - 1] == N_BUCKETS - 1
    return jnp.transpose(rel_bias.astype(F32)[bucket], (1, 0))


def _bias_lookup(tab_row, dist_clipped):
    r, w = dist_clipped.shape
    src = jnp.broadcast_to(tab_row, (r, LANE))
    parts = [jnp.take_along_axis(src, dist_clipped[:, j * LANE:(j + 1) * LANE], axis=1)
             for j in range(w // LANE)]
    return parts[0] if len(parts) == 1 else jnp.concatenate(parts, axis=1)


def _topk_idx(x, k):
    r, n = x.shape
    col = lax.broadcasted_iota(jnp.int32, (r, n), 1)
    lane = lax.broadcasted_iota(jnp.int32, (r, LANE), 1)
    out = jnp.zeros((r, LANE), jnp.int32)
    for it in range(k):
        m = jnp.max(x, axis=-1, keepdims=True)
        idx = jnp.min(jnp.where(x == m, col, n), axis=-1, keepdims=True)
        out = jnp.where(lane == it, idx, out)
        x = jnp.where(col == idx, NEG_INF, x)
    return out


def _cmp_branch(q, t_pos, ckv_ref, tab_ref, cover_ref, nc, ncols=None, n_sel=None):
    r = q.shape[0]
    ncols = ckv_ref.shape[0] if ncols is None else ncols
    n_sel = cover_ref.shape[1] if n_sel is None else n_sel
    c_idx = lax.broadcasted_iota(jnp.int32, (1, ncols), 1)
    dist = t_pos - (c_idx * CMP_STRIDE + CMP_BLOCK - 1)
    valid = (dist >= 0) & (c_idx < nc)
    any_valid = (t_pos >= CMP_BLOCK - 1).astype(F32)
    dcl = jnp.clip(dist, 0, LANE - 1)
    blk = lax.broadcasted_iota(jnp.int32, (1, n_sel), 1)
    cur = t_pos // SEL_BLOCK
    forced = (blk == 0) | (blk == cur) | (blk == cur - 1)
    outs, imps = [], []
    for g in range(KV_HEADS):
        kc = ckv_ref[:ncols, g * HEAD_DIM:(g + 1) * HEAD_DIM]
        vc = ckv_ref[:ncols, (KV_HEADS + g) * HEAD_DIM:(KV_HEADS + g + 1) * HEAD_DIM]
        psum = jnp.zeros((r, ncols), F32)
        for h in range(HPG):
            head = g * HPG + h
            qh = (q[:, head * HEAD_DIM:(head + 1) * HEAD_DIM] * ATT_SCALE).astype(BF16)
            s = _dot_nt(qh, kc) + _bias_lookup(tab_ref[head:head + 1, :], dcl)
            s = jnp.where(valid, s, NEG)
            e = jnp.exp(s - jnp.max(s, axis=-1, keepdims=True))
            p = e * (any_valid / jnp.sum(e, axis=-1, keepdims=True))
            outs.append(_dot(p.astype(BF16), vc))
            psum = psum + p
        p_hi = psum.astype(BF16)
        p_lo = (psum - p_hi.astype(F32)).astype(BF16)
        imp = _dot(p_hi, cover_ref[:ncols, :n_sel]) + _dot(p_lo, cover_ref[:ncols, :n_sel])
        imps.append(jnp.where(forced, BIG, jnp.where(blk <= cur, imp, -BIG)))
    return outs, imps


def _topk_mask_cols(x, k):
    n, t = x.shape
    row = lax.broadcasted_iota(jnp.int32, (n, t), 0)
    forced = x == BIG
    sel = forced.astype(F32)
    x = jnp.where(forced, NEG_INF, x)
    for _ in range(k - 3):
        m = jnp.max(x, axis=0, keepdims=True)
        idx = jnp.min(jnp.where(x == m, row, n), axis=0, keepdims=True)
        hit = row == idx
        sel = jnp.where(hit, 1.0, sel)
        x = jnp.where(hit, NEG_INF, x)
    return sel


def _nsa_cmp_kernel(q_ref, ckv_ref, tab_ref, cover_ref, oc_ref, mask_ref, *, nc, n_sel):
    i = pl.program_id(0)
    t_pos = i * TQ + lax.broadcasted_iota(jnp.int32, (TQ, 1), 0)
    ncp = ckv_ref.shape[0]
    part = max(ncp // 8, LANE)
    last_c = (i * TQ + TQ - CMP_BLOCK) // CMP_STRIDE
    need = jnp.clip(last_c // part, 0, ncp // part - 1)

    for n_parts in range(1, ncp // part + 1):
        @pl.when(need == n_parts - 1)
        def _(n_parts=n_parts):
            ncols = n_parts * part
            rows = min(ncols * CMP_STRIDE // SEL_BLOCK, n_sel)
            lanes = min(-(-rows // LANE) * LANE, n_sel)
            outs, imps = _cmp_branch(q_ref[...], t_pos, ckv_ref, tab_ref, cover_ref, nc, ncols, lanes)
            for g in range(KV_HEADS):
                mask_ref[g, :rows] = _topk_mask_cols(imps[g].T[:rows], min(SEL_TOPN, n_sel))
                if rows < n_sel:
                    mask_ref[g, rows:] = jnp.zeros((n_sel - rows, TQ), F32)
            oc_ref[...] = jnp.concatenate(outs, axis=1)


def cover_matrix(ncp, n_sel):
    ci = np.arange(ncp)[:, None] * CMP_STRIDE
    sj = np.arange(n_sel)[None, :] * SEL_BLOCK
    return jnp.asarray(((ci < sj + SEL_BLOCK) & (ci + CMP_BLOCK > sj)).astype(np.float32), dtype=BF16)


def nsa_cmp_prompt(hproj, ckv, tab, nc):
    t = hproj.shape[0]
    ncp = ckv.shape[0]
    n_sel = t // SEL_BLOCK
    cover = cover_matrix(ncp, n_sel)
    return pl.pallas_call(
        functools.partial(_nsa_cmp_kernel, nc=nc, n_sel=n_sel),
        grid=(t // TQ,),
        in_specs=[pl.BlockSpec((TQ, ATT_W), lambda i: (i, C_Q // ATT_W)),
                  pl.BlockSpec((ncp, KVB_W), lambda i: (0, 0)),
                  pl.BlockSpec((N_HEADS, LANE), lambda i: (0, 0)),
                  pl.BlockSpec((ncp, n_sel), lambda i: (0, 0))],
        out_specs=[pl.BlockSpec((TQ, ATT_W), lambda i: (i, 0)),
                   pl.BlockSpec((KV_HEADS, n_sel, TQ), lambda i: (0, 0, i))],
        out_shape=[jax.ShapeDtypeStruct((t, ATT_W), F32),
                   jax.ShapeDtypeStruct((KV_HEADS, n_sel, t), F32)],
        compiler_params=_cparams(("parallel",)),
        name="nsa_cmp_prompt",
    )(hproj, ckv, tab, cover)


MASKED = -1e9
NEAR = 2 * TQ
W_KEYS = WINDOW + TQ
BLK_PER_TILE = TK // SEL_BLOCK


VT_ROWS = HEAD_DIM + 16
MASK_ROWS = 16
HT = HPG * TQ
FLAG_LANE = HEAD_DIM + BLK_PER_TILE
FAR_UNROLL = 4


def _online_step_t(carry, st, shift, vt):
    m, acc = carry
    m_new = jnp.maximum(m, jnp.max(st, axis=0, keepdims=True) + shift)
    p = jnp.exp(st - (m_new - shift)).astype(BF16)
    return m_new, jnp.exp(m - m_new) * acc + _dot(vt, p)


def _nsa_sel_kernel(q_ref, ng_ref, oc_ref, mask_ref, ka_ref, vt_ref, tab_ref, ntab_ref, wtab_ref, o_ref,
                    sta_ref, stb_ref, *, n_sel):
    i = pl.program_id(0)
    qt = (q_ref[...] * ATT_SCALE).T
    oct_ = oc_ref[...].T
    gt = _sigmoid(ng_ref[...]).T
    near_blk0 = 2 * i - 2
    n_far = (jnp.maximum(i - 1, 0) * TQ + TK - 1) // TK
    blk8 = lax.broadcasted_iota(jnp.int32, (BLK_PER_TILE, 1), 0)
    init = (jnp.full((1, HT), NEG, F32), jnp.zeros((VT_ROWS, HT), F32))
    zero_shift = jnp.zeros((1, HT), F32)
    chunks_per_tile = TK // LANE

    before_start = jnp.where(lax.broadcasted_iota(jnp.int32, (MASK_ROWS, HT), 0) == FLAG_LANE - HEAD_DIM,
                             MASKED, 0.0).astype(BF16)
    qtg, w_plain, far_shift = [], [], []
    for g in range(KV_HEADS):
        heads = range(g * HPG, (g + 1) * HPG)
        qg = jnp.concatenate([qt[h * HEAD_DIM:(h + 1) * HEAD_DIM] for h in heads], axis=1).astype(BF16)
        qtg.append(qg)
        w_plain.append(jnp.concatenate([qg, before_start, jnp.zeros((LANE - HEAD_DIM - MASK_ROWS, HT), BF16)], axis=0))
        far_shift.append(jnp.concatenate(
            [jnp.broadcast_to(tab_ref[h:h + 1, LANE - 1:LANE], (1, TQ)) for h in heads], axis=1))

    def vt_tile(c0, n, kind):
        return jnp.concatenate([vt_ref[c0 + c, kind] for c in range(n)], axis=1)

    last_tile = (ka_ref.shape[0] - WINDOW) // TK - 1

    def far_logits(tile, dst_ref):
        kt = jnp.minimum(tile, last_tile)
        k0 = pl.multiple_of(WINDOW + kt * TK, TK)
        r0 = pl.multiple_of(kt * BLK_PER_TILE, BLK_PER_TILE)
        for g in range(KV_HEADS):
            mt = mask_ref[g, pl.ds(r0, BLK_PER_TILE), :]
            keep = (mt > 0.5) & (tile * BLK_PER_TILE + blk8 < near_blk0)
            mneg = jnp.where(keep, 0.0, MASKED)
            m16 = jnp.concatenate([jnp.concatenate([mneg] * HPG, axis=1),
                                   jnp.zeros((MASK_ROWS - BLK_PER_TILE, HT), F32)], axis=0).astype(BF16)
            w = jnp.concatenate([qtg[g], m16, jnp.zeros((LANE - HEAD_DIM - MASK_ROWS, HT), BF16)], axis=0)
            dst_ref[g] = _dot(ka_ref[pl.ds(k0, TK), g * LANE:(g + 1) * LANE], w)

    def far_softmax(kt, src_ref, carry):
        c0 = (WINDOW // LANE) + jnp.minimum(kt, last_tile) * chunks_per_tile
        return tuple(_online_step_t(carry[g], src_ref[g], far_shift[g], vt_tile(c0, chunks_per_tile, g))
                     for g in range(KV_HEADS))

    @pl.when(n_far > 0)
    def _():
        far_logits(0, sta_ref)

    def far_body(trip, carry):
        bufs = (sta_ref, stb_ref)
        for u in range(FAR_UNROLL):
            tile = trip * FAR_UNROLL + u
            far_logits(tile + 1, bufs[(u + 1) % 2])
            carry = far_softmax(tile, bufs[u % 2], carry)
        return carry

    carry = lax.fori_loop(0, (n_far + FAR_UNROLL - 1) // FAR_UNROLL, far_body, (init, init))

    kn0 = pl.multiple_of(WINDOW + (i - 1) * TQ, TQ)
    base = pl.multiple_of(jnp.clip((near_blk0 // BLK_PER_TILE) * BLK_PER_TILE, 0, n_sel - MASK_ROWS), BLK_PER_TILE)
    key_blk = near_blk0 + lax.broadcasted_iota(jnp.int32, (NEAR, MASK_ROWS), 0) // SEL_BLOCK
    expand = (key_blk == base + lax.broadcasted_iota(jnp.int32, (NEAR, MASK_ROWS), 1)).astype(BF16)
    kw0 = pl.multiple_of(i * TQ, TQ)
    rows_t = []
    st_near, st_win = [], []
    for g in range(KV_HEADS):
        mneg = jnp.where(mask_ref[g, pl.ds(base, MASK_ROWS), :] > 0.5, 0.0, MASKED).astype(BF16)
        sel_add = _dot(expand, mneg)
        st = _dot(ka_ref[pl.ds(kn0, NEAR), g * LANE:(g + 1) * LANE], w_plain[g])
        st_near.append(st + ntab_ref[g] + jnp.concatenate([sel_add] * HPG, axis=1))
        st = _dot(ka_ref[pl.ds(kw0, W_KEYS), (KV_HEADS + g) * LANE:(KV_HEADS + g + 1) * LANE], w_plain[g])
        st_win.append(st + wtab_ref[g])
    for g in range(KV_HEADS):
        _, acc = _online_step_t(carry[g], st_near[g], zero_shift, vt_tile(kn0 // LANE, NEAR // LANE, g))
        os_t = acc[:HEAD_DIM] / acc[HEAD_DIM:HEAD_DIM + 1]
        _, acc = _online_step_t(init, st_win[g], zero_shift, vt_tile(kw0 // LANE, W_KEYS // LANE, KV_HEADS + g))
        ow_t = acc[:HEAD_DIM] / acc[HEAD_DIM:HEAD_DIM + 1]

        for h in range(HPG):
            head = g * HPG + h
            ls = slice(h * TQ, (h + 1) * TQ)
            rows_t.append(gt[head:head + 1] * oct_[head * HEAD_DIM:(head + 1) * HEAD_DIM]
                          + gt[N_HEADS + head:N_HEADS + head + 1] * os_t[:, ls]
                          + gt[2 * N_HEADS + head:2 * N_HEADS + head + 1] * ow_t[:, ls])
    o_ref[...] = jnp.concatenate(rows_t, axis=0).T


def _distance_tables_kernel(tab_ref, ntab_ref, wtab_ref):
    head = pl.program_id(0) * HPG + pl.program_id(1)
    row = tab_ref[pl.ds(head, 1), :]
    for out_ref, start, max_dist in ((ntab_ref, TQ, None), (wtab_ref, WINDOW, WINDOW)):
        keys = out_ref.shape[0]
        dist = (start + lax.broadcasted_iota(jnp.int32, (keys, TQ), 1)
                - lax.broadcasted_iota(jnp.int32, (keys, TQ), 0))
        ok = dist >= 0 if max_dist is None else (dist >= 0) & (dist <= max_dist)
        out_ref[...] = jnp.where(ok, _bias_lookup(row, jnp.clip(dist, 0, LANE - 1)), NEG)


def distance_tables(tab):
    return pl.pallas_call(
        _distance_tables_kernel,
        grid=(KV_HEADS, HPG),
        in_specs=[pl.BlockSpec((N_HEADS, LANE), lambda g, h: (0, 0))],
        out_specs=[pl.BlockSpec((None, NEAR, TQ), lambda g, h: (g, 0, h)),
                   pl.BlockSpec((None, W_KEYS, TQ), lambda g, h: (g, 0, h))],
        out_shape=[jax.ShapeDtypeStruct((KV_HEADS, NEAR, HT), F32),
                   jax.ShapeDtypeStruct((KV_HEADS, W_KEYS, HT), F32)],
        compiler_params=_cparams(("parallel", "parallel")),
        name="distance_tables",
    )(tab)


def attention_kv_layout(hproj):
    t = hproj.shape[0]
    grp = lambda c0, g: hproj[:, c0 + g * HEAD_DIM:c0 + (g + 1) * HEAD_DIM]
    e8 = jnp.asarray(np.eye(BLK_PER_TILE, dtype=np.float32)[(np.arange(t) % TK) // SEL_BLOCK])
    z = lambda w: jnp.zeros((t, w), F32)
    ka = [jnp.concatenate([grp(C_KVS, g), e8, z(LANE - HEAD_DIM - BLK_PER_TILE)], axis=1) for g in range(KV_HEADS)]
    ka += [jnp.concatenate([grp(C_KVW, g), z(LANE - HEAD_DIM)], axis=1) for g in range(KV_HEADS)]
    lead = jnp.zeros((WINDOW, LANE), F32).at[:, FLAG_LANE].set(1.0)
    ka = jnp.concatenate([jnp.tile(lead, (1, 2 * KV_HEADS)), jnp.concatenate(ka, axis=1)], axis=0).astype(BF16)
    ones = jnp.ones((t, 1), F32)
    vts = []
    for c0 in (C_KVS, C_KVW):
        for g in range(KV_HEADS):
            v = jnp.concatenate([grp(c0 + KV_HEADS * HEAD_DIM, g), ones, z(VT_ROWS - HEAD_DIM - 1)], axis=1)
            v = jnp.pad(v.astype(BF16), ((WINDOW, 0), (0, 0))).reshape(-1, LANE, VT_ROWS)
            vts.append(jnp.swapaxes(v, 1, 2))
    return ka, jnp.stack(vts, axis=1)


def nsa_sel_prompt(hproj, o_c, mask, ka, vt, tab, ntab, wtab):
    t = hproj.shape[0]
    n_sel = mask.shape[1]
    assert t % TK == 0 and n_sel % BLK_PER_TILE == 0 and n_sel >= MASK_ROWS
    const = lambda a: pl.BlockSpec(a.shape, lambda i: (0,) * a.ndim, pipeline_mode=pl.Buffered(1))
    return pl.pallas_call(
        functools.partial(_nsa_sel_kernel, n_sel=n_sel),
        grid=(t // TQ,),
        in_specs=[pl.BlockSpec((TQ, ATT_W), lambda i: (i, C_Q // ATT_W)),
                  pl.BlockSpec((TQ, LANE), lambda i: (i, C_NG // LANE)),
                  pl.BlockSpec((TQ, ATT_W), lambda i: (i, 0)),
                  pl.BlockSpec((KV_HEADS, n_sel, TQ), lambda i: (0, 0, i)),
                  const(ka), const(vt), const(tab), const(ntab), const(wtab)],
        out_specs=pl.BlockSpec((TQ, ATT_W), lambda i: (i, 0)),
        out_shape=jax.ShapeDtypeStruct((t, ATT_W), F32),
        scratch_shapes=[pltpu.VMEM((KV_HEADS, TK, HT), F32)] * 2,
        compiler_params=_cparams(("parallel",)),
        name="nsa_sel_prompt",
    )(hproj, hproj, o_c, mask, ka, vt, tab, ntab, wtab)


def _hgrn_gates(hq, hf, lb):
    sig = _sigmoid(hf)
    forget = lb + (1.0 - lb) * sig
    logf = jnp.log(jnp.maximum(forget, TINY))
    k = (1.0 - lb) * (1.0 - sig)
    q = hq * _sigmoid(hq)
    return q, k, logf


def _hgrn_intra(q, k, v, a, sub):
    n = q.shape[0]
    row = lax.broadcasted_iota(jnp.int32, (n, 1), 0) % sub
    outs = [jnp.zeros((n, HG_DV), F32) for _ in range(HG_HEADS)]
    for d in range(sub):
        ks = k if d == 0 else pltpu.roll(k, d, 0)
        a_s = a if d == 0 else pltpu.roll(a, d, 0)
        vs = v if d == 0 else pltpu.roll(v, d, 0)
        msk = row >= d
        w = q * ks * jnp.exp(jnp.where(msk, a - a_s, 0.0)) * msk.astype(F32)
        for h in range(HG_HEADS):
            r = jnp.sum(w[:, h * HG_DK:(h + 1) * HG_DK], axis=-1, keepdims=True)
            outs[h] = outs[h] + r * vs[:, h * HG_DV:(h + 1) * HG_DV]
    return outs


def _hgrn_prompt_kernel(hq_ref, hf_ref, hi_ref, lb_ref, tri_ref, o_ref, st_out_ref, st_ref):
    step = pl.program_id(0)

    @pl.when(step == 0)
    def _():
        st_ref[...] = jnp.zeros(st_ref.shape, F32)

    n = hq_ref.shape[0]
    q, k, logf = _hgrn_gates(hq_ref[...], hf_ref[...], lb_ref[...])
    v = hi_ref[...]
    a = jnp.dot(tri_ref[...], logf, preferred_element_type=F32, precision=lax.Precision.HIGHEST)
    outs = _hgrn_intra(q, k, v, a, SUB)
    rows_out = [[] for _ in range(HG_HEADS)]
    for c in range(n // SUB):
        sl = slice(c * SUB, (c + 1) * SUB)
        a_c = a[sl]
        a_last = a_c[SUB - 1:SUB]
        qe = (q[sl] * jnp.exp(a_c)).astype(BF16)
        kd = (k[sl] * jnp.exp(a_last - a_c)).astype(BF16)
        dec = jnp.exp(a_last)
        for h in range(HG_HEADS):
            hs = slice(h * HG_DK, (h + 1) * HG_DK)
            st = st_ref[h]
            rows_out[h].append(outs[h][sl] + _dot_nt(qe[:, hs], st.astype(BF16)))
            st_ref[h] = dec[:, hs] * st + _dot_tn(v[sl, hs].astype(BF16), kd[:, hs])
    o_ref[...] = jnp.concatenate([jnp.concatenate(r, axis=0) for r in rows_out], axis=1)
    st_out_ref[...] = st_ref[...]


def hgrn_prompt(hproj, lb, tc):
    t = hproj.shape[0]
    tc = min(tc, t)
    r = np.arange(tc)
    tri = jnp.asarray(((r[:, None] >= r[None, :]) & (r[:, None] // SUB == r[None, :] // SUB)).astype(np.float32))
    return pl.pallas_call(
        _hgrn_prompt_kernel,
        grid=(t // tc,),
        in_specs=[pl.BlockSpec((tc, HGK_W), lambda i: (i, C_HQ // HGK_W)),
                  pl.BlockSpec((tc, HGK_W), lambda i: (i, C_HF // HGK_W)),
                  pl.BlockSpec((tc, HGK_W), lambda i: (i, C_HI // HGK_W)),
                  pl.BlockSpec((1, HGK_W), lambda i: (0, 0)),
                  pl.BlockSpec((tc, tc), lambda i: (0, 0))],
        out_specs=[pl.BlockSpec((tc, HGK_W), lambda i: (i, 0)),
                   pl.BlockSpec((HG_HEADS, HG_DV, HG_DK), lambda i: (0, 0, 0))],
        out_shape=[jax.ShapeDtypeStruct((t, HGK_W), F32),
                   jax.ShapeDtypeStruct((HG_HEADS, HG_DV, HG_DK), F32)],
        scratch_shapes=[pltpu.VMEM((HG_HEADS, HG_DV, HG_DK), F32)],
        compiler_params=_cparams(("arbitrary",)),
        name="hgrn_prompt",
    )(hproj, hproj, hproj, lb.reshape(1, HGK_W), tri)


def _merge_kernel(x_ref, att_ref, o_ref, hg_ref, ga_ref, gh_ref, ng_ref, fg_ref, wa_ref, wh_ref, wo_ref,
                  xo_ref, xn_ref):
    o = o_ref[...]
    parts = []
    for h in range(HG_HEADS):
        oh = o[:, h * HG_DV:(h + 1) * HG_DV]
        parts.append(oh * lax.rsqrt(jnp.mean(oh * oh, axis=-1, keepdims=True) + EPS))
    hg = hg_ref[...]
    hgo = jnp.concatenate(parts, axis=1) * ng_ref[...] * (hg * _sigmoid(hg))
    m = (_sigmoid(ga_ref[...]) * _dot(att_ref[...].astype(BF16), wa_ref[...])
         + _sigmoid(gh_ref[...]) * _dot(hgo.astype(BF16), wh_ref[...]))
    x = x_ref[...] + _dot(m.astype(BF16), wo_ref[...])
    xo_ref[...] = x
    y = x * lax.rsqrt(jnp.mean(x * x, axis=-1, keepdims=True) + EPS)
    xn_ref[...] = (y * fg_ref[...]).astype(BF16)


def merge(x, att, o, hproj, hg_norm_g, ffn_g, wa, wh, wo, tm):
    t = x.shape[0]
    tm = min(tm, t)
    row = lambda w, cb: pl.BlockSpec((tm, w), lambda i: (i, cb))
    full = lambda a: pl.BlockSpec(a.shape, lambda i: (0, 0))
    hg_norm_g = hg_norm_g.reshape(1, HGK_W)
    ffn_g = ffn_g.reshape(1, D_MODEL)
    return pl.pallas_call(
        _merge_kernel,
        grid=(t // tm,),
        in_specs=[row(D_MODEL, 0), row(ATT_W, 0), row(HGK_W, 0), row(HGK_W, C_HG // HGK_W),
                  row(D_MODEL, C_GA // D_MODEL), row(D_MODEL, C_GH // D_MODEL),
                  full(hg_norm_g), full(ffn_g), full(wa), full(wh), full(wo)],
        out_specs=[row(D_MODEL, 0), row(D_MODEL, 0)],
        out_shape=[jax.ShapeDtypeStruct((t, D_MODEL), F32), jax.ShapeDtypeStruct((t, D_MODEL), BF16)],
        compiler_params=_cparams(("parallel",)),
        name="merge",
    )(x, att, o, hproj, hproj, hproj, hg_norm_g, ffn_g, wa, wh, wo)


def _topk_cols(x, k, payload=None):
    (vals, ids, pay), = _topk_cols_lockstep([x], k, None if payload is None else [payload])
    return vals, ids, pay


def _topk_cols_lockstep(xs, k, payloads=None):
    n, t = xs[0].shape
    row = lax.broadcasted_iota(jnp.int32, (n, t), 0)
    slot = lax.broadcasted_iota(jnp.int32, (k, t), 0)
    xs = list(xs)
    vals = [jnp.zeros((k, t), F32) for _ in xs]
    ids = [jnp.zeros((k, t), jnp.int32) for _ in xs]
    pay = [jnp.zeros((k, t), jnp.int32) for _ in xs]
    for it in range(k):
        here = slot == it
        for j, x in enumerate(xs):
            m = jnp.max(x, axis=0, keepdims=True)
            idx = jnp.min(jnp.where(x == m, row, n), axis=0, keepdims=True)
            hit = row == idx
            vals[j] = jnp.where(here, m, vals[j])
            ids[j] = jnp.where(here, idx, ids[j])
            if payloads is not None:
                pay[j] = jnp.where(here, jnp.max(jnp.where(hit, payloads[j], -1), axis=0, keepdims=True), pay[j])
            xs[j] = jnp.where(hit, NEG_INF, x)
    return [(vals[j], ids[j], pay[j] if payloads is not None else None) for j in range(len(xs))]


def _pair_candidates():
    pairs = [(a, b) for a in range(PEER_TOPK) for b in range(PEER_TOPK) if (a + 1) * (b + 1) <= PEER_TOPK]
    rows = -(-len(pairs) // 8) * 8
    sel = np.zeros((2, rows, PEER_TOPK), np.float32)
    for r, (a, b) in enumerate(pairs):
        sel[0, r, a] = 1.0
        sel[1, r, b] = 1.0
    return len(pairs), jnp.asarray(sel)


def _pick_rows(sel, x):
    return jnp.dot(sel, x, preferred_element_type=F32, precision=lax.Precision.HIGHEST)


def _peer_topk_kernel(q_ref, keys_ref, pair_ref, i1_ref, i2_ref, g_ref, *, n_pairs):
    o1, o2, og = [], [], []
    live = lax.broadcasted_iota(jnp.int32, (pair_ref.shape[1], q_ref.shape[0]), 0) < n_pairs
    cands, cidxs = [], []
    for h in range(PEER_HEADS):
        scores = [_dot_nt(keys_ref[side], q_ref[:, (h * 2 + side) * PEER_DH:(h * 2 + side + 1) * PEER_DH].astype(BF16))
                  for side in range(2)]
        (v1, i1, _), (v2, i2, _) = _topk_cols_lockstep(scores, PEER_TOPK)
        cands.append(jnp.where(live, _pick_rows(pair_ref[0], v1) + _pick_rows(pair_ref[1], v2), NEG_INF))
        cidxs.append((_pick_rows(pair_ref[0], i1.astype(F32)) * PEER_NKEYS
                      + _pick_rows(pair_ref[1], i2.astype(F32))).astype(jnp.int32))
    for h0 in range(0, PEER_HEADS, 2):
        for sv, _, e in _topk_cols_lockstep(cands[h0:h0 + 2], PEER_TOPK, cidxs[h0:h0 + 2]):
            ex = jnp.exp(sv - sv[0:1])
            o1.append(e // PEER_NKEYS)
            o2.append(e % PEER_NKEYS)
            og.append(ex / jnp.sum(ex, axis=0, keepdims=True))
    i1_ref[...] = jnp.concatenate(o1, axis=0).astype(F32).T.astype(jnp.int32)
    i2_ref[...] = jnp.concatenate(o2, axis=0).astype(F32).T.astype(jnp.int32)
    g_ref[...] = jnp.concatenate(og, axis=0).T


def peer_topk(q, keys_bf):
    t = q.shape[0]
    tb = LANE
    nsel = PEER_HEADS * PEER_TOPK
    n_pairs, pair_sel = _pair_candidates()
    out = lambda dt: jax.ShapeDtypeStruct((t, nsel), dt)
    return pl.pallas_call(
        functools.partial(_peer_topk_kernel, n_pairs=n_pairs),
        grid=(t // tb,),
        in_specs=[pl.BlockSpec((tb, q.shape[1]), lambda i: (i, 0)),
                  pl.BlockSpec(keys_bf.shape, lambda i: (0, 0, 0)),
                  pl.BlockSpec(pair_sel.shape, lambda i: (0, 0, 0))],
        out_specs=[pl.BlockSpec((tb, nsel), lambda i: (i, 0))] * 3,
        out_shape=[out(jnp.int32), out(jnp.int32), out(F32)],
        compiler_params=_cparams(("parallel",)),
        name="peer_topk",
    )(q, keys_bf, pair_sel)


GATE_UNROLL = 64


def _peer_dense_kernel(xn_ref, x_ref, i1_ref, i2_ref, g_ref, u_ref, v_ref, o_ref, acc_ref, gate_ref, *, n_a):
    j = pl.program_id(1)
    tm, nsel = i1_ref.shape

    @pl.when(j == 0)
    def _():
        acc_ref[...] = jnp.zeros(acc_ref.shape, F32)
        row = lax.broadcasted_iota(jnp.int32, (PEER_NKEYS, nsel), 0)

        def body(tt, _):
            for k in range(GATE_UNROLL):
                t = tt * GATE_UNROLL + k
                pa = jnp.where(row == i1_ref[pl.ds(t, 1), :], 1.0, 0.0).astype(BF16)
                gb = jnp.where(row == i2_ref[pl.ds(t, 1), :], g_ref[pl.ds(t, 1), :], 0.0).astype(BF16)
                gate_ref[pl.ds(pl.multiple_of(t * PEER_NKEYS, PEER_NKEYS), PEER_NKEYS), :] = _dot_nt(pa, gb)
            return 0

        lax.fori_loop(0, tm // GATE_UNROLL, body, 0)

    s = _dot_nt(xn_ref[...], u_ref[...])
    gate = jnp.concatenate([gate_ref[pl.ds(j * n_a + k, tm, stride=PEER_NKEYS), :] for k in range(n_a)], axis=1)
    acc_ref[...] += _dot((gate * _gelu(s)).astype(BF16), v_ref[...])

    @pl.when(j == pl.num_programs(1) - 1)
    def _():
        o_ref[...] = x_ref[...] + acc_ref[...]


def peer_dense(xn, x, i1, i2, g, u_bf, v_bf, tm, n_a):
    t = x.shape[0]
    tm = min(tm, t)
    te = n_a * PEER_NKEYS
    nsel = i1.shape[1]
    tok = lambda w: pl.BlockSpec((tm, w), lambda i, j: (i, 0))
    return pl.pallas_call(
        functools.partial(_peer_dense_kernel, n_a=n_a),
        grid=(t // tm, PEER_NKEYS // n_a),
        in_specs=[tok(D_MODEL), tok(D_MODEL), tok(nsel), tok(nsel), tok(nsel),
                  pl.BlockSpec((te, D_MODEL), lambda i, j: (j, 0)),
                  pl.BlockSpec((te, D_MODEL), lambda i, j: (j, 0))],
        out_specs=tok(D_MODEL),
        out_shape=jax.ShapeDtypeStruct((t, D_MODEL), F32),
        scratch_shapes=[pltpu.VMEM((tm, D_MODEL), F32), pltpu.VMEM((tm * PEER_NKEYS, PEER_NKEYS), F32)],
        compiler_params=_cparams(("parallel", "arbitrary")),
        name="peer_dense",
    )(xn, x, i1, i2, g, u_bf, v_bf)


def peer(xn_bf, x, wq_bf, keys_bf, u_bf, v_bf, tm):
    t = x.shape[0]
    q = matmul(xn_bf, wq_bf, min(512, t), 1024)
    i1, i2, g = peer_topk(q, keys_bf)
    return peer_dense(xn_bf, x, i1, i2, g, u_bf, v_bf, tm, 16)


ROWS = 8


def _pad_rows(x, rows=ROWS):
    return jnp.concatenate([x, jnp.zeros((rows - x.shape[0], x.shape[1]), x.dtype)], axis=0)


def _nsa_cmp_sample_kernel(q_ref, ckv_ref, tab_ref, cover_ref, oc_ref, sel_ref, *, nc, past, n_sel):
    nseq, nt = q_ref.shape[:2]
    t_pos = past + lax.broadcasted_iota(jnp.int32, (ROWS, 1), 0)
    for b in range(nseq):
        outs, imps = _cmp_branch(_pad_rows(q_ref[b]), t_pos, ckv_ref.at[b], tab_ref, cover_ref, nc)
        for g in range(KV_HEADS):
            sel_ref[b, g] = _topk_idx(imps[g], min(SEL_TOPN, n_sel))
        oc_ref[b] = jnp.concatenate(outs, axis=1)[:nt]


def nsa_cmp_sample(hproj3, ckv, tab, nc, past):
    nb, nt, _ = hproj3.shape
    ncp = ckv.shape[1]
    n_sel = -(-(past + nt) // SEL_BLOCK)
    n_sel_pad = -(-n_sel // LANE) * LANE
    cover = cover_matrix(ncp, n_sel_pad)
    per = 8 if nb % 8 == 0 else 1
    return pl.pallas_call(
        functools.partial(_nsa_cmp_sample_kernel, nc=nc, past=past, n_sel=n_sel),
        grid=(nb // per,),
        in_specs=[pl.BlockSpec((per, nt, ATT_W), lambda b: (b, 0, C_Q // ATT_W)),
                  pl.BlockSpec((per, ncp, KVB_W), lambda b: (b, 0, 0)),
                  pl.BlockSpec((N_HEADS, LANE), lambda b: (0, 0)),
                  pl.BlockSpec((ncp, n_sel_pad), lambda b: (0, 0))],
        out_specs=[pl.BlockSpec((per, nt, ATT_W), lambda b: (b, 0, 0)),
                   pl.BlockSpec((per, KV_HEADS, ROWS, LANE), lambda b: (b, 0, 0, 0))],
        out_shape=[jax.ShapeDtypeStruct((nb, nt, ATT_W), F32),
                   jax.ShapeDtypeStruct((nb, KV_HEADS, ROWS, LANE), jnp.int32)],
        compiler_params=_cparams(("parallel",)),
        name="nsa_cmp_sample",
    )(hproj3, ckv, tab, cover)


def _nsa_sel_sample_kernel(sel_ref, pt_ref, q_ref, ng_ref, oc_ref, kvs_ref, kvw_ref, win_ref, tab_ref, cache_ref,
                           o_ref, buf_ref, sem, *, layer, past, w_len):
    b = pl.program_id(0)
    nt = q_ref.shape[0]
    nslot = SEL_TOPN
    per_page = PAGE_SIZE // SEL_BLOCK

    def slot_copy(t, g, slot):
        blk = sel_ref[((b * nt + t) * KV_HEADS + g) * nslot + slot]
        in_past = blk * SEL_BLOCK < past
        blk_c = jnp.minimum(blk, past // SEL_BLOCK - 1)
        page = pt_ref[b, blk_c // per_page]
        src = cache_ref.at[layer, page, pl.ds(KVB_W, KVB_W), :]
        dst = buf_ref.at[t * KV_HEADS + g, :, pl.ds(slot * PAGE_SIZE, PAGE_SIZE)]
        return blk, in_past, pltpu.make_async_copy(src, dst, sem)

    for t in range(nt):
        for g in range(KV_HEADS):
            for slot in range(nslot):
                _, in_past, cp = slot_copy(t, g, slot)

                @pl.when(in_past)
                def _():
                    cp.start()

                @pl.when(jnp.logical_not(in_past))
                def _():
                    buf_ref[t * KV_HEADS + g, :, pl.ds(slot * PAGE_SIZE, PAGE_SIZE)] = jnp.zeros(
                        (KVB_W, PAGE_SIZE), F32)
    for t in range(nt):
        for g in range(KV_HEADS):
            for slot in range(nslot):
                _, in_past, cp = slot_copy(t, g, slot)

                @pl.when(in_past)
                def _():
                    cp.wait()

    gates = _sigmoid(ng_ref[...])
    nk = nslot * PAGE_SIZE
    lane_slot = lax.broadcasted_iota(jnp.int32, (1, nk), 1) // PAGE_SIZE
    lane_off = lax.broadcasted_iota(jnp.int32, (1, nk), 1) % PAGE_SIZE
    j_new = lax.broadcasted_iota(jnp.int32, (1, LANE), 1)
    kvn = jnp.concatenate([kvs_ref[...], kvw_ref[...]], axis=1)
    kvn_pad = _pad_rows(kvn, LANE).astype(BF16)
    win = win_ref[...].astype(BF16)
    w_idx = lax.broadcasted_iota(jnp.int32, (1, w_len + LANE), 1)
    rows_out = []
    for t in range(nt):
        t_pos = past + t
        heads_out = []
        for g in range(KV_HEADS):
            q4 = jnp.concatenate(
                [q_ref[t:t + 1, (g * HPG + h) * HEAD_DIM:(g * HPG + h + 1) * HEAD_DIM] for h in range(HPG)],
                axis=0)
            q4 = (q4 * ATT_SCALE).astype(BF16)

            def bias_of(dist):
                d = jnp.broadcast_to(jnp.clip(dist, 0, LANE - 1), (ROWS, dist.shape[1]))
                return jnp.concatenate(
                    [_bias_lookup(tab_ref[g * HPG + h:g * HPG + h + 1, :], d)[0:1] for h in range(HPG)], axis=0)

            pos = jnp.zeros((1, nk), jnp.int32)
            lane_blk = jnp.zeros((1, nk), jnp.int32)
            has_new = jnp.zeros((1, 1), jnp.int32)
            for slot in range(nslot):
                blk = sel_ref[((b * nt + t) * KV_HEADS + g) * nslot + slot]
                pos = jnp.where(lane_slot == slot, (blk // per_page) * PAGE_SIZE, pos)
                lane_blk = jnp.where(lane_slot == slot, blk, lane_blk)
                has_new = jnp.maximum(has_new, (blk * SEL_BLOCK >= past).astype(jnp.int32))
            pos = pos + lane_off
            in_block = pos // SEL_BLOCK == lane_blk
            tg = t * KV_HEADS + g
            k_t = buf_ref[tg, g * HEAD_DIM:(g + 1) * HEAD_DIM, :].astype(BF16)
            v_t = buf_ref[tg, (KV_HEADS + g) * HEAD_DIM:(KV_HEADS + g + 1) * HEAD_DIM, :].astype(BF16)
            k_new = kvn_pad[:, g * HEAD_DIM:(g + 1) * HEAD_DIM]
            v_new = kvn_pad[:, (KV_HEADS + g) * HEAD_DIM:(KV_HEADS + g + 1) * HEAD_DIM]
            dist = jnp.concatenate([t_pos - pos, t - j_new], axis=1)
            ok = jnp.concatenate([in_block & (pos < past), (j_new <= t) & (has_new > 0)], axis=1)
            s = jnp.concatenate([_dot(q4, k_t), _dot_nt(q4, k_new)], axis=1)
            s = jnp.where(ok, s + bias_of(dist), NEG)
            e = jnp.where(ok, jnp.exp(s - jnp.max(s, axis=-1, keepdims=True)), 0.0)
            p = (e / jnp.sum(e, axis=-1, keepdims=True)).astype(BF16)
            o_s = _dot_nt(p[:, :nk], v_t) + _dot(p[:, nk:], v_new)

            kw_new = kvn_pad[:, (2 * KV_HEADS + g) * HEAD_DIM:(2 * KV_HEADS + g + 1) * HEAD_DIM]
            vw_new = kvn_pad[:, (3 * KV_HEADS + g) * HEAD_DIM:(3 * KV_HEADS + g + 1) * HEAD_DIM]
            dist_w = t_pos - (past - w_len + w_idx)
            ok_w = (dist_w >= 0) & (dist_w <= WINDOW) & (w_idx < w_len + nt)
            s = jnp.concatenate([_dot(q4, win[g * HEAD_DIM:(g + 1) * HEAD_DIM]), _dot_nt(q4, kw_new)], axis=1)
            s = jnp.where(ok_w, s + bias_of(dist_w), NEG)
            e = jnp.where(ok_w, jnp.exp(s - jnp.max(s, axis=-1, keepdims=True)), 0.0)
            p = (e / jnp.sum(e, axis=-1, keepdims=True)).astype(BF16)
            o_w = (_dot_nt(p[:, :w_len], win[(KV_HEADS + g) * HEAD_DIM:(KV_HEADS + g + 1) * HEAD_DIM])
                   + _dot(p[:, w_len:], vw_new))

            for h in range(HPG):
                head = g * HPG + h
                o_c = oc_ref[t:t + 1, head * HEAD_DIM:(head + 1) * HEAD_DIM]
                heads_out.append(gates[t:t + 1, head:head + 1] * o_c
                                 + gates[t:t + 1, N_HEADS + head:N_HEADS + head + 1] * o_s[h:h + 1]
                                 + gates[t:t + 1, 2 * N_HEADS + head:2 * N_HEADS + head + 1] * o_w[h:h + 1])
        rows_out.append(jnp.concatenate(heads_out, axis=1))
    o_ref[...] = jnp.concatenate(rows_out, axis=0)


def nsa_sel_sample(hproj3, o_c, sel_flat, page_table, cache_t, win_t, tab, layer, past):
    nb, nt, _ = hproj3.shape
    w_len = win_t.shape[3]
    assert past % SEL_BLOCK == 0 and PAGE_SIZE % SEL_BLOCK == 0 and w_len % LANE == 0 and past >= w_len
    grid_spec = pltpu.PrefetchScalarGridSpec(
        num_scalar_prefetch=2,
        grid=(nb,),
        in_specs=[pl.BlockSpec((None, nt, ATT_W), lambda b, s, p: (b, 0, C_Q // ATT_W)),
                  pl.BlockSpec((None, nt, LANE), lambda b, s, p: (b, 0, C_NG // LANE)),
                  pl.BlockSpec((None, nt, ATT_W), lambda b, s, p: (b, 0, 0)),
                  pl.BlockSpec((None, nt, KVB_W), lambda b, s, p: (b, 0, C_KVS // KVB_W)),
                  pl.BlockSpec((None, nt, KVB_W), lambda b, s, p: (b, 0, C_KVW // KVB_W)),
                  pl.BlockSpec((None, None, KVB_W, w_len), lambda b, s, p: (layer, b, 0, 0)),
                  pl.BlockSpec((N_HEADS, LANE), lambda b, s, p: (0, 0)),
                  pl.BlockSpec(memory_space=pl.ANY)],
        out_specs=pl.BlockSpec((None, nt, ATT_W), lambda b, s, p: (b, 0, 0)),
        scratch_shapes=[pltpu.VMEM((nt * KV_HEADS, KVB_W, SEL_TOPN * PAGE_SIZE), F32),
                        pltpu.SemaphoreType.DMA(())],
    )
    return pl.pallas_call(
        functools.partial(_nsa_sel_sample_kernel, layer=layer, past=past, w_len=w_len),
        grid_spec=grid_spec,
        out_shape=jax.ShapeDtypeStruct((nb, nt, ATT_W), F32),
        compiler_params=_cparams(("arbitrary",)),
        name="nsa_sel_sample",
    )(sel_flat, page_table, hproj3, hproj3, o_c, hproj3, hproj3, win_t, tab, cache_t)


def _hgrn_sample_kernel(hq_ref, hf_ref, hi_ref, lb_ref, s_ref, o_ref, so_ref):
    nt = hq_ref.shape[0]
    live = (lax.broadcasted_iota(jnp.int32, (ROWS, 1), 0) < nt).astype(F32)
    q, k, logf = _hgrn_gates(_pad_rows(hq_ref[...]), _pad_rows(hf_ref[...]), lb_ref[...])
    q, k, logf = q * live, k * live, logf * live
    v = _pad_rows(hi_ref[...])
    rows = [logf[0:1]]
    for i in range(1, ROWS):
        rows.append(rows[-1] + logf[i:i + 1])
    a = jnp.concatenate(rows, axis=0)
    outs = _hgrn_intra(q, k, v, a, ROWS)
    a_last = a[ROWS - 1:ROWS]
    qe = (q * jnp.exp(a)).astype(BF16)
    kd = (k * jnp.exp(a_last - a)).astype(BF16)
    dec = jnp.exp(a_last)
    o_parts = []
    for h in range(HG_HEADS):
        hs = slice(h * HG_DK, (h + 1) * HG_DK)
        st = s_ref[h].T
        o_parts.append(outs[h] + _dot_nt(qe[:, hs], st.astype(BF16)))
        so_ref[h] = (dec[:, hs] * st + _dot_tn(v[:, hs].astype(BF16), kd[:, hs])).T
    o_ref[...] = jnp.concatenate(o_parts, axis=1)[:nt]


def hgrn_sample(hproj3, lb, state):
    nb, nt, _ = hproj3.shape
    col = lambda c: pl.BlockSpec((None, nt, HGK_W), lambda b: (b, 0, c // HGK_W))
    st_spec = pl.BlockSpec((None, HG_HEADS, HG_DK, HG_DV), lambda b: (b, 0, 0, 0))
    return pl.pallas_call(
        _hgrn_sample_kernel,
        grid=(nb,),
        in_specs=[col(C_HQ), col(C_HF), col(C_HI), pl.BlockSpec((1, HGK_W), lambda b: (0, 0)), st_spec],
        out_specs=[pl.BlockSpec((None, nt, HGK_W), lambda b: (b, 0, 0)), st_spec],
        out_shape=[jax.ShapeDtypeStruct((nb, nt, HGK_W), F32),
                   jax.ShapeDtypeStruct(state.shape, F32)],
        compiler_params=_cparams(("parallel",)),
        name="hgrn_sample",
    )(hproj3, hproj3, hproj3, lb.reshape(1, HGK_W), state)


def reorder_w_in(w):
    o = np.cumsum([0, ATT_W, KVB_W, KVB_W, KVB_W, N_GATE, HGK_W, HGK_W, HGK_W, HGK_W, D_MODEL, D_MODEL])
    q, kvc, kvs, kvw, ng, hq, hf, hi, hg, ga, gh = [w[:, o[i]:o[i + 1]] for i in range(11)]
    pad = jnp.zeros((w.shape[0], N_PROJ - C_NG - N_GATE), w.dtype)
    return jnp.concatenate([q, hq, ga, gh, hf, hi, hg, kvc, kvs, kvw, ng, pad], axis=1).astype(BF16)


def prompt_layer(x, lp, tabs):
    t = x.shape[0]
    tab, ntab, wtab = tabs
    hproj = norm_matmul(x, lp["attn_g"], lp["w_in"], min(512, t), 1408)
    y = cproj_rows(hproj, C_KVC // KVB_W, lp["wbig"], 2048)
    ckv = cmp_mlp(y[None], lp["pe_term"], lp["w2big"])[0]
    o_c, mask = nsa_cmp_prompt(hproj, ckv, tab, t // CMP_STRIDE - 1)
    ka, vt = attention_kv_layout(hproj)
    att = nsa_sel_prompt(hproj, o_c, mask, ka, vt, tab, ntab, wtab)
    o_hg, st = hgrn_prompt(hproj, lp["lb"], 128)
    x, xn = merge(x, att, o_hg, hproj, lp["hg_norm_g"], lp["ffn_g"], lp["wa"], lp["wh"], lp["wo"], 256)
    x = peer(xn, x, lp["peer_wq"], lp["peer_keys"], lp["peer_u"], lp["peer_v"], 256)
    return x, hproj[:, C_KVC:C_KVC + 2 * KVB_W], hproj[:, C_KVW:C_KVW + KVB_W], jnp.swapaxes(st, 1, 2)


def layer_params(l, norm_attn_g, norm_ffn_g, w_in, cmp_pe, cmp_w1, cmp_w2, lower_bounds, hg_norm_g,
                 w_br_attn, w_br_hg, w_out, peer_wq, peer_keys, peer_u, peer_v):
    wbig, w2big, pe_rows = compress_weights(cmp_pe[l], cmp_w1[l], cmp_w2[l])
    return dict(attn_g=norm_attn_g[l], ffn_g=norm_ffn_g[l], w_in=reorder_w_in(w_in[l]),
                wbig=wbig, w2big=w2big, pe_term=pe_term_from(pe_rows, wbig),
                lb=lower_bounds[l], hg_norm_g=hg_norm_g[l],
                wa=w_br_attn[l].astype(BF16), wh=w_br_hg[l].astype(BF16), wo=w_out[l].astype(BF16),
                peer_wq=peer_wq[l].astype(BF16), peer_keys=peer_keys[l].astype(BF16),
                peer_u=peer_u[l].astype(BF16), peer_v=peer_v[l].astype(BF16))


def sample_layer(x, lp, tab, layer, cache_t, win_t, state, page_table, nb, nt):
    past = page_table.shape[1] * PAGE_SIZE
    assert nt < CMP_STRIDE, "new tokens never complete a compression chunk"
    hproj = norm_matmul(x, lp["attn_g"], lp["w_in"], x.shape[0], 1408)
    hproj3 = hproj.reshape(nb, nt, N_PROJ)
    y = cproj_pages(cache_t, layer, page_table, lp["wbig"], min(32, page_table.shape[1]))
    ckv = cmp_mlp(y, lp["pe_term"], lp["w2big"])
    o_c, sel = nsa_cmp_sample(hproj3, ckv, tab, past // CMP_STRIDE - 1, past)
    sel_flat = jnp.transpose(sel[:, :, :nt, :SEL_TOPN], (0, 2, 1, 3)).reshape(-1)
    att = nsa_sel_sample(hproj3, o_c, sel_flat, page_table, cache_t, win_t, tab, layer, past)
    o_hg, st = hgrn_sample(hproj3, lp["lb"], state)
    x, xn = merge(x, att.reshape(nb * nt, ATT_W), o_hg.reshape(nb * nt, HGK_W), hproj, lp["hg_norm_g"],
                  lp["ffn_g"], lp["wa"], lp["wh"], lp["wo"], 256)
    x = peer(xn, x, lp["peer_wq"], lp["peer_keys"], lp["peer_u"], lp["peer_v"], 256)
    return x, hproj[:, C_KVC:C_KVC + 2 * KVB_W], hproj[:, C_KVW:C_KVW + KVB_W], st


def kernel(x_prompt, x_sample, cache_kv, cache_win, state_hgrn, page_table, norm_attn_g, norm_ffn_g, final_norm_g, w_in, cmp_pe, cmp_w1, cmp_w2, rel_bias, hg_lb_logits, hg_norm_g, w_br_attn, w_br_hg, w_out, peer_wq, peer_keys, peer_u, peer_v):
    depth = w_in.shape[0]
    bp, seq, _ = x_prompt.shape
    nb, nt, _ = x_sample.shape
    assert bp == 1
    p_lb = jax.nn.softmax(hg_lb_logits.astype(F32), axis=0)
    lower_bounds = jnp.cumsum(p_lb, axis=0) - p_lb[0]
    tab = bias_table(rel_bias)
    tabs = (tab, *distance_tables(tab))
    n_pool = cache_kv.shape[1]
    w_buf = cache_win.shape[2]
    cache_t = jnp.transpose(cache_kv, (0, 1, 3, 4, 5, 2)).reshape(depth, n_pool, 2 * KVB_W, PAGE_SIZE)
    win_t = jnp.transpose(cache_win, (0, 1, 3, 4, 5, 2)).reshape(depth, nb, KVB_W, w_buf)
    xp = x_prompt.reshape(seq, D_MODEL)
    xs = x_sample.reshape(nb * nt, D_MODEL)
    kv_p, win_p, hs_p, kv_s, win_s, hs_s = [], [], [], [], [], []
    for l in range(depth):
        lp = layer_params(l, norm_attn_g, norm_ffn_g, w_in, cmp_pe, cmp_w1, cmp_w2, lower_bounds, hg_norm_g,
                          w_br_attn, w_br_hg, w_out, peer_wq, peer_keys, peer_u, peer_v)
        xp, kv, kw, st = prompt_layer(xp, lp, tabs)
        kv_p.append(kv.reshape(bp, seq, 4, KV_HEADS, HEAD_DIM))
        win_p.append(kw[seq - min(WINDOW, seq):].reshape(bp, -1, 2, KV_HEADS, HEAD_DIM))
        hs_p.append(st[None])
        xs, kv, kw, st = sample_layer(xs, lp, tab, l, cache_t, win_t, state_hgrn[l], page_table, nb, nt)
        kv_s.append(kv.reshape(nb, nt, 4, KV_HEADS, HEAD_DIM))
        win_s.append(jnp.concatenate([cache_win[l], kw.reshape(nb, nt, 2, KV_HEADS, HEAD_DIM)], axis=1)[:, nt:])
        hs_s.append(st)
    y_prompt = rmsnorm(xp, final_norm_g, min(512, seq)).reshape(bp, seq, D_MODEL)
    y_sample = rmsnorm(xs, final_norm_g, nb * nt).reshape(nb, nt, D_MODEL)
    return (y_prompt, y_sample, jnp.stack(kv_p), jnp.stack(win_p), jnp.stack(hs_p),
            jnp.stack(kv_s), jnp.stack(win_s), jnp.stack(hs_s))
```
